```python
import jax, jax.numpy as jnp
from jax import lax
import numpy as np


D_MODEL = 1024
BATCH = 32
SEQ = 256
DEPTH = 1
DEC_BATCH = 4
DEC_SEQ = 2048
PAST_LEN = 512

GRID_W = 64
BLOCK = 128
WINDOW = 128
A_HEADS = 8
A_KV_HEADS = 2
A_GROUP = A_HEADS // A_KV_HEADS
A_HEAD_DIM = 64
A_Q_W = A_HEADS * A_HEAD_DIM
A_KV_W = A_KV_HEADS * A_HEAD_DIM
B_HEADS = 8
B_NOPE_DIM = 64
B_ROPE_DIM = 32
B_V_DIM = 64
B_QK_DIM = B_NOPE_DIM + B_ROPE_DIM
B_OUT_W = B_HEADS * B_V_DIM
Q_LORA = 256
KV_LORA = 256
D_FF = 2816
N_MOD = 9
ROPE_THETA = 10000.0
EPS = 1e-6
NEG = -1e30
A_SCALE = A_HEAD_DIM ** -0.5
B_SCALE = B_QK_DIM ** -0.5
IN_SIZES = (A_Q_W, A_KV_W, A_KV_W, Q_LORA, KV_LORA, B_ROPE_DIM, D_MODEL, D_MODEL)
IN_WIDTH = A_Q_W + 2 * A_KV_W + Q_LORA + KV_LORA + B_ROPE_DIM + 2 * D_MODEL

kernel_name = 'hybrid_dit_prefix_windowed_gqa_mla_macaron'


def rms_norm(x, g):
    xf = x.astype(jnp.float32)
    y = xf * lax.rsqrt(jnp.mean(xf * xf, axis=-1, keepdims=True) + EPS)
    return (y * g.astype(jnp.float32)).astype(x.dtype)


def modulate(h, shift, scale):
    return h * (1 + scale[:, None, :]) + shift[:, None, :]


def swiglu(h, w_gu, w_down):
    a, u = jnp.split(h @ w_gu, 2, axis=-1)
    return (jax.nn.silu(a) * u) @ w_down


def ada_mods(cvec, ada_w, ada_b):
    m = jax.nn.silu(cvec) @ ada_w + ada_b
    return [m[:, i * D_MODEL:(i + 1) * D_MODEL] for i in range(N_MOD)]


def split_in(z):
    offs = []
    acc = 0
    for s in IN_SIZES[:-1]:
        acc += s
        offs.append(acc)
    return jnp.split(z, offs, axis=-1)


def axial_rope(n, d_rot):
    rows = n // GRID_W
    t_row = jnp.repeat(jnp.arange(rows, dtype=jnp.float32), GRID_W)
    t_col = jnp.tile(jnp.arange(GRID_W, dtype=jnp.float32), rows)
    d_half = d_rot // 2
    inv = 1.0 / (ROPE_THETA ** (jnp.arange(0, d_half, 2, dtype=jnp.float32) / d_half))
    ar = t_row[:, None] * inv[None, :]
    ac = t_col[:, None] * inv[None, :]
    ang = jnp.concatenate([ar, ar, ac, ac], axis=-1)
    return jnp.cos(ang), jnp.sin(ang)


def apply_rope(x, cos, sin):
    shp = (x.shape[1],) + (1,) * (x.ndim - 3) + (x.shape[-1],)
    c = cos.reshape(shp)
    s = sin.reshape(shp)
    xf = x.astype(jnp.float32)
    x1, x2, x3, x4 = jnp.split(xf, 4, axis=-1)
    rot = jnp.concatenate([-x2, x1, -x4, x3], axis=-1)
    return (xf * c + rot * s).astype(x.dtype)


def block_sweep_attention(q, k, v, scale, sink):
    bq, n = q.shape[0], q.shape[1]
    nb = n // BLOCK
    qb = q.astype(jnp.float32).reshape((bq, nb, BLOCK) + q.shape[2:]).swapaxes(0, 1)
    kf = k.astype(jnp.float32)
    vf = v.astype(jnp.float32)

    def one_block(qblk):
        s = jnp.einsum('bqkgd,bskd->bkgqs', qblk, kf) * scale
        if sink is not None:
            col = jnp.broadcast_to(sink.astype(jnp.float32)[None, :, :, None, None], s.shape[:-1] + (1,))
            s = jnp.concatenate([s, col], axis=-1)
        p = jax.nn.softmax(s, axis=-1)
        if sink is not None:
            p = p[..., :-1]
        return jnp.einsum('bkgqs,bskd->bqkgd', p, vf)

    o = lax.map(one_block, qb)
    return o.swapaxes(0, 1).reshape(bq, n, -1).astype(q.dtype)


def windowed_attention(q, k, v, k_ctx, v_ctx, sink, scale):
    b, n = q.shape[0], q.shape[1]
    nb = n // BLOCK
    pad = ((0, 0), (BLOCK, BLOCK), (0, 0), (0, 0))
    kp = jnp.pad(k.astype(jnp.float32), pad).reshape(b, nb + 2, BLOCK, A_KV_HEADS, -1)
    vp = jnp.pad(v.astype(jnp.float32), pad).reshape(b, nb + 2, BLOCK, A_KV_HEADS, -1)
    kband = jnp.concatenate([kp[:, :-2], kp[:, 1:-1], kp[:, 2:]], axis=2)
    vband = jnp.concatenate([vp[:, :-2], vp[:, 1:-1], vp[:, 2:]], axis=2)
    qb = q.astype(jnp.float32).reshape(b, nb, BLOCK, A_KV_HEADS, A_GROUP, -1)
    s_band = jnp.einsum('bnqkgd,bnskd->bnkgqs', qb, kband) * scale
    qi = jnp.arange(BLOCK)
    kj = jnp.arange(3 * BLOCK)
    blk = jnp.arange(nb)
    rel = kj[None, :] - BLOCK - qi[:, None]
    kpos = blk[:, None] * BLOCK - BLOCK + kj[None, :]
    valid = (jnp.abs(rel) <= WINDOW)[None] & ((kpos >= 0) & (kpos < n))[:, None, :]
    s_band = jnp.where(valid[None, :, None, None], s_band, NEG)
    kc = k_ctx.astype(jnp.float32)
    vc = v_ctx.astype(jnp.float32)
    s_ctx = jnp.einsum('bnqkgd,bckd->bnkgqc', qb, kc) * scale
    col = jnp.broadcast_to(sink.astype(jnp.float32)[None, None, :, :, None, None], s_band.shape[:-1] + (1,))
    p = jax.nn.softmax(jnp.concatenate([s_band, s_ctx, col], axis=-1), axis=-1)
    lb = 3 * BLOCK
    lc = k_ctx.shape[1]
    o = (jnp.einsum('bnkgqs,bnskd->bnqkgd', p[..., :lb], vband)
         + jnp.einsum('bnkgqc,bckd->bnqkgd', p[..., lb:lb + lc], vc))
    return o.reshape(b, n, -1).astype(q.dtype)


def mla_queries(q_lat, q_lat_norm, w_uq):
    b, n = q_lat.shape[0], q_lat.shape[1]
    return (rms_norm(q_lat, q_lat_norm) @ w_uq).reshape(b, n, B_HEADS, B_QK_DIM)


def mla_keys(ckv_n, krope, w_ukv):
    b, l = ckv_n.shape[0], ckv_n.shape[1]
    kv = (ckv_n @ w_ukv).reshape(b, l, B_HEADS, B_NOPE_DIM + B_V_DIM)
    k_nope, v = kv[..., :B_NOPE_DIM], kv[..., B_NOPE_DIM:]
    k = jnp.concatenate([k_nope, jnp.broadcast_to(krope[:, :, None, :], (b, l, B_HEADS, B_ROPE_DIM))], axis=-1)
    return k, v


def merge_branches(o_a, o_b, gate_a, gate_b, w_o_a, w_o_b, w_out):
    m = jax.nn.sigmoid(gate_a) * (o_a @ w_o_a) + jax.nn.sigmoid(gate_b) * (o_b @ w_o_b)
    return m @ w_out


def setup_inputs(seed: int = 0) -> dict:
    key = jax.random.key(seed)
    ks = jax.random.split(key, 32)

    def nrm(k, shape, scale=1.0):
        return jax.random.normal(k, shape, jnp.float32) * scale

    def gain(k, shape):
        return 1.0 + nrm(k, shape, 0.05)

    return {
        'x_prompt': nrm(ks[0], (BATCH, SEQ, D_MODEL)),
        'x_sample': nrm(ks[1], (DEC_BATCH, DEC_SEQ, D_MODEL)),
        'cache_attn_k': nrm(ks[2], (DEC_BATCH, DEPTH, PAST_LEN, A_KV_HEADS, A_HEAD_DIM)),
        'cache_attn_v': nrm(ks[3], (DEC_BATCH, DEPTH, PAST_LEN, A_KV_HEADS, A_HEAD_DIM)),
        'cache_mla_ckv': nrm(ks[4], (DEC_BATCH, DEPTH, PAST_LEN, KV_LORA)),
        'cache_mla_krope': nrm(ks[5], (DEC_BATCH, DEPTH, PAST_LEN, B_ROPE_DIM)),
        'c': nrm(ks[6], (DEC_BATCH, D_MODEL)),
        'c_ctx': nrm(ks[7], (D_MODEL,)),
        'ada_w': nrm(ks[8], (DEPTH, D_MODEL, N_MOD * D_MODEL), 0.5 * D_MODEL ** -0.5),
        'ada_b': nrm(ks[9], (DEPTH, N_MOD * D_MODEL), 0.01),
        'ffn1_norm': gain(ks[10], (DEPTH, D_MODEL)),
        'ffn1_w_gu': nrm(ks[11], (DEPTH, D_MODEL, 2 * D_FF), D_MODEL ** -0.5),
        'ffn1_w_down': nrm(ks[12], (DEPTH, D_FF, D_MODEL), D_FF ** -0.5),
        'mix_norm': gain(ks[13], (DEPTH, D_MODEL)),
        'w_in': nrm(ks[14], (DEPTH, D_MODEL, IN_WIDTH), D_MODEL ** -0.5),
        'attn_sink': nrm(ks[15], (DEPTH, A_HEADS), 0.5),
        'q_lat_norm': gain(ks[16], (DEPTH, Q_LORA)),
        'kv_lat_norm': gain(ks[17], (DEPTH, KV_LORA)),
        'w_uq': nrm(ks[18], (DEPTH, Q_LORA, B_HEADS * B_QK_DIM), Q_LORA ** -0.5),
        'w_ukv': nrm(ks[19], (DEPTH, KV_LORA, B_HEADS * (B_NOPE_DIM + B_V_DIM)), KV_LORA ** -0.5),
        'w_o_a': nrm(ks[20], (DEPTH, A_Q_W, D_MODEL), A_Q_W ** -0.5),
        'w_o_b': nrm(ks[21], (DEPTH, B_OUT_W, D_MODEL), B_OUT_W ** -0.5),
        'w_out': nrm(ks[22], (DEPTH, D_MODEL, D_MODEL), D_MODEL ** -0.5),
        'ffn2_norm': gain(ks[23], (DEPTH, D_MODEL)),
        'ffn2_w_gu': nrm(ks[24], (DEPTH, D_MODEL, 2 * D_FF), D_MODEL ** -0.5),
        'ffn2_w_down': nrm(ks[25], (DEPTH, D_FF, D_MODEL), D_FF ** -0.5),
        'final_norm': gain(ks[26], (D_MODEL,)),
    }


def reference(x_prompt, x_sample, cache_attn_k, cache_attn_v, cache_mla_ckv, cache_mla_krope, c, c_ctx,
              ada_w, ada_b, ffn1_norm, ffn1_w_gu, ffn1_w_down, mix_norm, w_in, attn_sink,
              q_lat_norm, kv_lat_norm, w_uq, w_ukv, w_o_a, w_o_b, w_out,
              ffn2_norm, ffn2_w_gu, ffn2_w_down, final_norm):
    xp = x_prompt
    bp, sp = x_prompt.shape[0], x_prompt.shape[1]
    ks_a, vs_a, cs_b, rs_b = [], [], [], []
    for l in range(DEPTH):
        sh1, sc1, g1, sh2, sc2, g2, sh3, sc3, g3 = ada_mods(c_ctx[None, :], ada_w[l], ada_b[l])
        xp = xp + 0.5 * g1[:, None, :] * swiglu(modulate(rms_norm(xp, ffn1_norm[l]), sh1, sc1), ffn1_w_gu[l], ffn1_w_down[l])
        h = modulate(rms_norm(xp, mix_norm[l]), sh2, sc2)
        qa, ka, va, q_lat, ckv, krope, gate_a, gate_b = split_in(h @ w_in[l])
        qa = qa.reshape(bp, sp, A_KV_HEADS, A_GROUP, A_HEAD_DIM)
        ka = ka.reshape(bp, sp, A_KV_HEADS, A_HEAD_DIM)
        va = va.reshape(bp, sp, A_KV_HEADS, A_HEAD_DIM)
        o_a = block_sweep_attention(qa, ka, va, A_SCALE, attn_sink[l].reshape(A_KV_HEADS, A_GROUP))
        q_b = mla_queries(q_lat, q_lat_norm[l], w_uq[l])
        ckv_n = rms_norm(ckv, kv_lat_norm[l])
        k_b, v_b = mla_keys(ckv_n, krope, w_ukv[l])
        o_b = block_sweep_attention(q_b[:, :, :, None, :], k_b, v_b, B_SCALE, None)
        xp = xp + g2[:, None, :] * merge_branches(o_a, o_b, gate_a, gate_b, w_o_a[l], w_o_b[l], w_out[l])
        xp = xp + 0.5 * g3[:, None, :] * swiglu(modulate(rms_norm(xp, ffn2_norm[l]), sh3, sc3), ffn2_w_gu[l], ffn2_w_down[l])
        ks_a.append(ka)
        vs_a.append(va)
        cs_b.append(ckv_n)
        rs_b.append(krope)
    y_prompt = rms_norm(xp, final_norm)
    new_attn_k = jnp.stack(ks_a, axis=1)
    new_attn_v = jnp.stack(vs_a, axis=1)
    new_mla_ckv = jnp.stack(cs_b, axis=1)
    new_mla_krope = jnp.stack(rs_b, axis=1)

    xs = x_sample
    bs, ns = x_sample.shape[0], x_sample.shape[1]
    cos_a, sin_a = axial_rope(ns, A_HEAD_DIM)
    cos_b, sin_b = axial_rope(ns, B_ROPE_DIM)
    for l in range(DEPTH):
        sh1, sc1, g1, sh2, sc2, g2, sh3, sc3, g3 = ada_mods(c, ada_w[l], ada_b[l])
        xs = xs + 0.5 * g1[:, None, :] * swiglu(modulate(rms_norm(xs, ffn1_norm[l]), sh1, sc1), ffn1_w_gu[l], ffn1_w_down[l])
        h = modulate(rms_norm(xs, mix_norm[l]), sh2, sc2)
        qa, ka, va, q_lat, ckv, krope, gate_a, gate_b = split_in(h @ w_in[l])
        qa = apply_rope(qa.reshape(bs, ns, A_KV_HEADS, A_GROUP, A_HEAD_DIM), cos_a, sin_a)
        ka = apply_rope(ka.reshape(bs, ns, A_KV_HEADS, A_HEAD_DIM), cos_a, sin_a)
        va = va.reshape(bs, ns, A_KV_HEADS, A_HEAD_DIM)
        o_a = windowed_attention(qa, ka, va, cache_attn_k[:, l], cache_attn_v[:, l],
                                 attn_sink[l].reshape(A_KV_HEADS, A_GROUP), A_SCALE)
        q_b = mla_queries(q_lat, q_lat_norm[l], w_uq[l])
        q_b = jnp.concatenate([q_b[..., :B_NOPE_DIM], apply_rope(q_b[..., B_NOPE_DIM:], cos_b, sin_b)], axis=-1)
        ckv_n = rms_norm(ckv, kv_lat_norm[l])
        k_lat, v_lat = mla_keys(ckv_n, apply_rope(krope, cos_b, sin_b), w_ukv[l])
        k_ctx, v_ctx = mla_keys(cache_mla_ckv[:, l], cache_mla_krope[:, l], w_ukv[l])
        o_b = block_sweep_attention(q_b[:, :, :, None, :], jnp.concatenate([k_ctx, k_lat], axis=1),
                                    jnp.concatenate([v_ctx, v_lat], axis=1), B_SCALE, None)
        xs = xs + g2[:, None, :] * merge_branches(o_a, o_b, gate_a, gate_b, w_o_a[l], w_o_b[l], w_out[l])
        xs = xs + 0.5 * g3[:, None, :] * swiglu(modulate(rms_norm(xs, ffn2_norm[l]), sh3, sc3), ffn2_w_gu[l], ffn2_w_down[l])
    y_sample = rms_norm(xs, final_norm)

    return (y_prompt, y_sample, new_attn_k, new_attn_v, new_mla_ckv, new_mla_krope)
```

```python
import functools

import jax
import jax.numpy as jnp
from jax import lax
from jax.experimental import pallas as pl
from jax.experimental.pallas import tpu as pltpu

F32 = jnp.float32
BF16 = jnp.bfloat16

D_MODEL = 1024
N_MOD = 9
GRID_W = 64
WINDOW = 128
A_HEADS = 8
A_KV_HEADS = 2
A_HEAD_DIM = 64
A_Q_W = A_HEADS * A_HEAD_DIM
A_KV_W = A_KV_HEADS * A_HEAD_DIM
B_HEADS = 8
B_NOPE_DIM = 64
B_ROPE_DIM = 32
B_V_DIM = 64
B_QK_DIM = B_NOPE_DIM + B_ROPE_DIM
Q_LORA = 256
KV_LORA = 256
D_FF = 2816
ROPE_THETA = 10000.0
EPS = 1e-6
NEG = -1e30
A_SCALE = A_HEAD_DIM ** -0.5
B_SCALE = B_QK_DIM ** -0.5

LANES = 128
HALF = LANES // 2
FF_CHUNK = 256
N_FF_CHUNKS = D_FF // FF_CHUNK
TOKEN_TILE = 256
ADA_TILE = 1024
Q_BLOCK = 128
BAND = 3 * Q_BLOCK
MLA_Q_TILE = 256
MOD_ROWS = 8
VMEM_LIMIT = 56 * 1024 * 1024

IN_A_W = A_Q_W + 2 * A_KV_W + Q_LORA + KV_LORA + LANES
OFF_KA = A_Q_W
OFF_VA = OFF_KA + A_KV_W
OFF_QLAT = OFF_VA + A_KV_W
OFF_CKV = OFF_QLAT + Q_LORA
OFF_KROPE = OFF_CKV + KV_LORA


def _dot(a, b):
    return jnp.dot(a, b, preferred_element_type=F32)


def _dot_nt(a, b):
    return lax.dot_general(a, b, (((1,), (1,)), ((), ())), preferred_element_type=F32)


def _rms(x, g):
    ms = jnp.mean(x * x, axis=-1, keepdims=True)
    return x * lax.rsqrt(ms + EPS) * g


def _mod_norm(x, g, mods_ref, k):
    shift = mods_ref[:, (3 * k) * D_MODEL:(3 * k + 1) * D_MODEL]
    scale = mods_ref[:, (3 * k + 1) * D_MODEL:(3 * k + 2) * D_MODEL]
    return _rms(x, g) * (1.0 + scale) + shift


def _gate(mods_ref, k):
    return mods_ref[:, (3 * k + 2) * D_MODEL:(3 * k + 3) * D_MODEL]


def _ffn(h, wgu_ref, wd_ref):
    acc = None
    for c in range(N_FF_CHUNKS):
        au = _dot(h, wgu_ref[c])
        a = au[:, :FF_CHUNK]
        u = au[:, FF_CHUNK:]
        act = (a * jax.nn.sigmoid(a) * u).astype(BF16)
        d = _dot(act, wd_ref[c])
        acc = d if acc is None else acc + d
    return acc


def _rope(x, cos, sin_signed, dist):
    lane = lax.broadcasted_iota(jnp.int32, x.shape, 1)
    first = (lane & (2 * dist - 1)) < dist
    partner = jnp.where(first, pltpu.roll(x, LANES - dist, axis=1), pltpu.roll(x, dist, axis=1))
    return x * cos + partner * sin_signed


def _attend(scores, values, sink):
    m = None
    for s in scores:
        mx = jnp.max(s, axis=-1, keepdims=True)
        m = mx if m is None else jnp.maximum(m, mx)
    if sink is not None:
        m = jnp.maximum(m, sink)
    l = None
    r = None
    for s, v in zip(scores, values):
        p = jnp.exp(s - m)
        ls = jnp.sum(p, axis=-1, keepdims=True)
        pv = _dot(p.astype(BF16), v)
        l = ls if l is None else l + ls
        r = pv if r is None else r + pv
    if sink is not None:
        l = l + jnp.exp(sink - m)
    return r / l


def _low_lanes(rows):
    return lax.broadcasted_iota(jnp.int32, (rows, LANES), 1) < HALF


def _ada_kernel(c_ref, w_ref, b_ref, o_ref):
    c = c_ref[...]
    s = (c * jax.nn.sigmoid(c)).astype(BF16)
    o_ref[...] = _dot(s, w_ref[...].astype(BF16)) + b_ref[...]


def _ada_mods(cvec, ada_w, ada_b):
    n = ada_w.shape[1]
    return pl.pallas_call(
        _ada_kernel,
        grid=(n // ADA_TILE,),
        in_specs=[
            pl.BlockSpec((MOD_ROWS, D_MODEL), lambda j: (0, 0)),
            pl.BlockSpec((D_MODEL, ADA_TILE), lambda j: (0, j)),
            pl.BlockSpec((1, ADA_TILE), lambda j: (0, j)),
        ],
        out_specs=pl.BlockSpec((MOD_ROWS, ADA_TILE), lambda j: (0, j)),
        out_shape=jax.ShapeDtypeStruct((MOD_ROWS, n), F32),
        compiler_params=pltpu.CompilerParams(dimension_semantics=("parallel",), vmem_limit_bytes=VMEM_LIMIT),
        name="ada_mods",
    )(cvec, ada_w, ada_b)


def _pre_kernel(latent, *refs):
    if latent:
        (x_ref, mods_ref, n1_ref, wgu_ref, wd_ref, nm_ref, win_ref, qn_ref, kvn_ref, wuq_ref, wuk_ref, wuv_ref,
         cosa_ref, sina_ref, cosb_ref, sinb_ref,
         x1_ref, qa_ref, kap_ref, vad_ref, qb_ref, kb_ref, vb_ref) = refs
    else:
        (x_ref, mods_ref, n1_ref, wgu_ref, wd_ref, nm_ref, win_ref, qn_ref, kvn_ref, wuq_ref, wuk_ref, wuv_ref,
         x1_ref, qa_ref, kap_ref, vad_ref, qb_ref, kb_ref, vb_ref,
         knat_ref, vnat_ref, ckvn_ref, krope_ref) = refs

    x = x_ref[...]
    h1 = _mod_norm(x, n1_ref[...], mods_ref, 0).astype(BF16)
    x1 = x + 0.5 * _gate(mods_ref, 0) * _ffn(h1, wgu_ref, wd_ref)
    x1_ref[...] = x1

    h2 = _mod_norm(x1, nm_ref[...], mods_ref, 1).astype(BF16)
    z = _dot(h2, win_ref[...])
    low = _low_lanes(x.shape[0])

    for c in range(A_Q_W // LANES):
        q = z[:, c * LANES:(c + 1) * LANES]
        if latent:
            q = _rope(q, cosa_ref[...], sina_ref[...], A_HEAD_DIM // 4)
        qa_ref[:, c * LANES:(c + 1) * LANES] = (q * A_SCALE).astype(BF16)
    k = z[:, OFF_KA:OFF_KA + LANES]
    v = z[:, OFF_VA:OFF_VA + LANES]
    if latent:
        k = _rope(k, cosa_ref[...], sina_ref[...], A_HEAD_DIM // 4)
    else:
        knat_ref[...] = k
        vnat_ref[...] = v
    k_sw = pltpu.roll(k, HALF, axis=1)
    v_sw = pltpu.roll(v, HALF, axis=1)
    zero = jnp.zeros_like(k)
    kap_ref[:, 0 * LANES:1 * LANES] = jnp.where(low, k, zero).astype(BF16)
    kap_ref[:, 1 * LANES:2 * LANES] = jnp.where(low, zero, k_sw).astype(BF16)
    kap_ref[:, 2 * LANES:3 * LANES] = jnp.where(low, k_sw, zero).astype(BF16)
    kap_ref[:, 3 * LANES:4 * LANES] = jnp.where(low, zero, k).astype(BF16)
    vad_ref[:, 0 * LANES:1 * LANES] = jnp.where(low, v, v_sw).astype(BF16)
    vad_ref[:, 1 * LANES:2 * LANES] = jnp.where(low, v_sw, v).astype(BF16)

    q_lat = _rms(z[:, OFF_QLAT:OFF_QLAT + Q_LORA], qn_ref[...]).astype(BF16)
    qb = _dot(q_lat, wuq_ref[...])
    ckv_n = _rms(z[:, OFF_CKV:OFF_CKV + KV_LORA], kvn_ref[...])
    krp = z[:, OFF_KROPE:OFF_KROPE + LANES]
    if latent:
        krp = _rope(krp, cosb_ref[...], sinb_ref[...], B_ROPE_DIM // 4)
    else:
        ckvn_ref[...] = ckv_n
        krope_ref[...] = pltpu.roll(krp, HALF, axis=1)[:, :B_ROPE_DIM]
    ckv_b = ckv_n.astype(BF16)
    kn = _dot(ckv_b, wuk_ref[...])
    vb_ref[...] = _dot(ckv_b, wuv_ref[...]).astype(BF16)
    for h in range(B_HEADS):
        qh = qb[:, h * LANES:(h + 1) * LANES]
        if latent:
            qh = _rope(qh, cosb_ref[...], sinb_ref[...], B_ROPE_DIM // 4)
        qb_ref[:, h * LANES:(h + 1) * LANES] = (qh * B_SCALE).astype(BF16)
        kb_ref[:, h * LANES:(h + 1) * LANES] = (kn[:, h * LANES:(h + 1) * LANES] + krp).astype(BF16)


def _const_spec(shape):
    nd = len(shape)
    return pl.BlockSpec(shape, lambda i: (0,) * nd, pipeline_mode=pl.Buffered(1))


def _pre(x, mods3, w, latent, tiles_per_batch, rope):
    t = x.shape[0]
    tm = TOKEN_TILE
    n_tiles = t // tm
    if latent:
        mod_row = lambda i: (1 + i // tiles_per_batch, 0, 0)
        pos_blk = lambda i: (i % tiles_per_batch, 0)
    else:
        mod_row = lambda i: (0, 0, 0)
    row_blk = lambda i: (i, 0)

    weights = [w["n1"], w["wgu1"], w["wd1"], w["nm"], w["win_a"], w["qn"], w["kvn"], w["wuq"], w["wuk"], w["wuv"]]
    in_specs = [pl.BlockSpec((tm, D_MODEL), row_blk),
                pl.BlockSpec((None, 1, N_MOD * D_MODEL), mod_row)]
    in_specs += [_const_spec(a.shape) for a in weights]
    args = [x, mods3] + weights
    if latent:
        in_specs += [pl.BlockSpec((tm, LANES), pos_blk)] * 4
        args += list(rope)

    widths = [(D_MODEL, F32), (A_Q_W, BF16), (4 * LANES, BF16), (2 * LANES, BF16),
              (B_HEADS * LANES, BF16), (B_HEADS * LANES, BF16), (B_HEADS * B_V_DIM, BF16)]
    if not latent:
        widths += [(A_KV_W, F32), (A_KV_W, F32), (KV_LORA, F32), (B_ROPE_DIM, F32)]
    out_specs = [pl.BlockSpec((tm, wd), row_blk) for wd, _ in widths]
    out_shape = [jax.ShapeDtypeStruct((t, wd), dt) for wd, dt in widths]

    return pl.pallas_call(
        functools.partial(_pre_kernel, latent),
        grid=(n_tiles,),
        in_specs=in_specs,
        out_specs=out_specs,
        out_shape=out_shape,
        compiler_params=pltpu.CompilerParams(dimension_semantics=("parallel",), vmem_limit_bytes=VMEM_LIMIT),
        name="pre_latent" if latent else "pre_context",
    )(*args)


def _cache_kernel(ckv_ref, krp_ref, wuk_ref, wuv_ref, kb_ref, vb_ref):
    c = ckv_ref[...].astype(BF16)
    kn = _dot(c, wuk_ref[...])
    krp = krp_ref[...]
    for h in range(B_HEADS):
        kb_ref[:, h * LANES:(h + 1) * LANES] = (kn[:, h * LANES:(h + 1) * LANES] + krp).astype(BF16)
    vb_ref[...] = _dot(c, wuv_ref[...]).astype(BF16)


def _cache_kv(ckv, krp, wuk, wuv):
    b, p, _ = ckv.shape
    return pl.pallas_call(
        _cache_kernel,
        grid=(b,),
        in_specs=[pl.BlockSpec((None, p, KV_LORA), lambda i: (i, 0, 0)),
                  pl.BlockSpec((None, p, LANES), lambda i: (i, 0, 0)),
                  pl.BlockSpec(wuk.shape, lambda i: (0, 0)),
                  pl.BlockSpec(wuv.shape, lambda i: (0, 0))],
        out_specs=[pl.BlockSpec((None, p, B_HEADS * LANES), lambda i: (i, 0, 0)),
                   pl.BlockSpec((None, p, B_HEADS * B_V_DIM), lambda i: (i, 0, 0))],
        out_shape=[jax.ShapeDtypeStruct((b, p, B_HEADS * LANES), BF16),
                   jax.ShapeDtypeStruct((b, p, B_HEADS * B_V_DIM), BF16)],
        compiler_params=pltpu.CompilerParams(dimension_semantics=("parallel",), vmem_limit_bytes=VMEM_LIMIT),
        name="cache_mla_kv",
    )(ckv, krp, wuk, wuv)


def _att_ctx_kernel(sink_ref, qa_ref, kap_ref, vad_ref, qb_ref, kb_ref, vb_ref, oa_ref, ob_ref):
    low = _low_lanes(qa_ref.shape[0])
    for kv in range(A_KV_HEADS):
        v = vad_ref[:, kv * LANES:(kv + 1) * LANES]
        for pr in range(2):
            c = 2 * kv + pr
            qp = qa_ref[:, c * LANES:(c + 1) * LANES]
            res = []
            for half in range(2):
                kc = 2 * kv + half
                s = _dot_nt(qp, kap_ref[:, kc * LANES:(kc + 1) * LANES])
                res.append(_attend([s], [v], sink_ref[2 * c + half]))
            oa_ref[:, c * LANES:(c + 1) * LANES] = jnp.where(low, res[0], res[1]).astype(BF16)
    for c in range(B_HEADS // 2):
        v = vb_ref[:, c * LANES:(c + 1) * LANES]
        res = []
        for half in range(2):
            h = 2 * c + half
            s = _dot_nt(qb_ref[:, h * LANES:(h + 1) * LANES], kb_ref[:, h * LANES:(h + 1) * LANES])
            res.append(_attend([s], [v], None))
        ob_ref[:, c * LANES:(c + 1) * LANES] = jnp.where(low, res[0], res[1]).astype(BF16)


def _att_ctx(sink, qa, kap, vad, qb, kb, vb, seq):
    t = qa.shape[0]
    blk = lambda wd: pl.BlockSpec((seq, wd), lambda b: (b, 0))
    return pl.pallas_call(
        _att_ctx_kernel,
        grid=(t // seq,),
        in_specs=[pl.BlockSpec(memory_space=pltpu.SMEM),
                  blk(qa.shape[1]), blk(kap.shape[1]), blk(vad.shape[1]),
                  blk(qb.shape[1]), blk(kb.shape[1]), blk(vb.shape[1])],
        out_specs=[blk(A_Q_W), blk(B_HEADS * B_V_DIM)],
        out_shape=[jax.ShapeDtypeStruct((t, A_Q_W), BF16), jax.ShapeDtypeStruct((t, B_HEADS * B_V_DIM), BF16)],
        compiler_params=pltpu.CompilerParams(dimension_semantics=("parallel",), vmem_limit_bytes=VMEM_LIMIT),
        name="att_context",
    )(sink, qa, kap, vad, qb, kb, vb)


def _att_win_kernel(n_lat, sink_ref, qa_ref, kap_ref, vad_ref, ck_ref, cv_ref, oa_ref):
    i = pl.program_id(1)
    start = pl.multiple_of(jnp.clip(i * Q_BLOCK - Q_BLOCK, 0, n_lat - BAND), Q_BLOCK)
    past = ck_ref.shape[0]
    low_c = _low_lanes(past)
    low_q = _low_lanes(Q_BLOCK)

    ck = ck_ref[...]
    cv = cv_ref[...]
    ck_sw = pltpu.roll(ck, HALF, axis=1)
    cv_sw = pltpu.roll(cv, HALF, axis=1)
    zero = jnp.zeros_like(ck)
    kc_pairs = [[jnp.where(low_c, ck, zero), jnp.where(low_c, zero, ck_sw)],
                [jnp.where(low_c, ck_sw, zero), jnp.where(low_c, zero, ck)]]
    vc_dup = [jnp.where(low_c, cv, cv_sw), jnp.where(low_c, cv_sw, cv)]

    r = lax.broadcasted_iota(jnp.int32, (2 * Q_BLOCK, BAND), 0) & (Q_BLOCK - 1)
    j = lax.broadcasted_iota(jnp.int32, (2 * Q_BLOCK, BAND), 1)
    rel = (start + j) - (i * Q_BLOCK + r)
    valid = jnp.abs(rel) <= WINDOW
    upper = lax.broadcasted_iota(jnp.int32, (2 * Q_BLOCK, 1), 0) >= Q_BLOCK

    for kv in range(A_KV_HEADS):
        q2 = jnp.concatenate([qa_ref[:, (2 * kv) * LANES:(2 * kv + 1) * LANES],
                              qa_ref[:, (2 * kv + 1) * LANES:(2 * kv + 2) * LANES]], axis=0)
        v_band = vad_ref[pl.ds(start, BAND), kv * LANES:(kv + 1) * LANES]
        v_ctx = vc_dup[kv].astype(BF16)
        res = []
        for half in range(2):
            kc = 2 * kv + half
            k_band = kap_ref[pl.ds(start, BAND), kc * LANES:(kc + 1) * LANES]
            s_band = jnp.where(valid, _dot_nt(q2, k_band), NEG)
            s_ctx = _dot_nt(q2, kc_pairs[kv][half].astype(BF16))
            sink = jnp.where(upper, sink_ref[4 * kv + 2 + half], sink_ref[4 * kv + half])
            res.append(_attend([s_band, s_ctx], [v_band, v_ctx], sink))
        for pr in range(2):
            c = 2 * kv + pr
            oa_ref[:, c * LANES:(c + 1) * LANES] = jnp.where(
                low_q, res[0][pr * Q_BLOCK:(pr + 1) * Q_BLOCK], res[1][pr * Q_BLOCK:(pr + 1) * Q_BLOCK]).astype(BF16)


def _att_win(sink, qa, kap, vad, ck, cv):
    b, n, _ = qa.shape
    past = ck.shape[1]
    return pl.pallas_call(
        functools.partial(_att_win_kernel, n),
        grid=(b, n // Q_BLOCK),
        in_specs=[pl.BlockSpec(memory_space=pltpu.SMEM),
                  pl.BlockSpec((None, Q_BLOCK, A_Q_W), lambda bi, i: (bi, i, 0)),
                  pl.BlockSpec((None, n, kap.shape[2]), lambda bi, i: (bi, 0, 0)),
                  pl.BlockSpec((None, n, vad.shape[2]), lambda bi, i: (bi, 0, 0)),
                  pl.BlockSpec((None, past, A_KV_W), lambda bi, i: (bi, 0, 0)),
                  pl.BlockSpec((None, past, A_KV_W), lambda bi, i: (bi, 0, 0))],
        out_specs=pl.BlockSpec((None, Q_BLOCK, A_Q_W), lambda bi, i: (bi, i, 0)),
        out_shape=jax.ShapeDtypeStruct((b, n, A_Q_W), BF16),
        compiler_params=pltpu.CompilerParams(dimension_semantics=("parallel", "parallel"),
                                             vmem_limit_bytes=VMEM_LIMIT),
        name="att_window",
    )(sink, qa, kap, vad, ck, cv)


def _att_mla_kernel(qb_ref, kbl_ref, vbl_ref, kbc_ref, vbc_ref, ob_ref):
    low = _low_lanes(qb_ref.shape[0])
    for c in range(B_HEADS // 2):
        v_ctx = vbc_ref[:, c * LANES:(c + 1) * LANES]
        v_lat = vbl_ref[:, c * LANES:(c + 1) * LANES]
        res = []
        for half in range(2):
            h = 2 * c + half
            q = qb_ref[:, h * LANES:(h + 1) * LANES]
            s_ctx = _dot_nt(q, kbc_ref[:, h * LANES:(h + 1) * LANES])
            s_lat = _dot_nt(q, kbl_ref[:, h * LANES:(h + 1) * LANES])
            res.append(_attend([s_ctx, s_lat], [v_ctx, v_lat], None))
        ob_ref[:, c * LANES:(c + 1) * LANES] = jnp.where(low, res[0], res[1]).astype(BF16)


def _att_mla(qb, kbl, vbl, kbc, vbc):
    b, n, _ = qb.shape
    past = kbc.shape[1]
    tq = MLA_Q_TILE
    return pl.pallas_call(
        _att_mla_kernel,
        grid=(b, n // tq),
        in_specs=[pl.BlockSpec((None, tq, qb.shape[2]), lambda bi, i: (bi, i, 0)),
                  pl.BlockSpec((None, n, kbl.shape[2]), lambda bi, i: (bi, 0, 0)),
                  pl.BlockSpec((None, n, vbl.shape[2]), lambda bi, i: (bi, 0, 0)),
                  pl.BlockSpec((None, past, kbc.shape[2]), lambda bi, i: (bi, 0, 0)),
                  pl.BlockSpec((None, past, vbc.shape[2]), lambda bi, i: (bi, 0, 0))],
        out_specs=pl.BlockSpec((None, tq, B_HEADS * B_V_DIM), lambda bi, i: (bi, i, 0)),
        out_shape=jax.ShapeDtypeStruct((b, n, B_HEADS * B_V_DIM), BF16),
        compiler_params=pltpu.CompilerParams(dimension_semantics=("parallel", "parallel"),
                                             vmem_limit_bytes=VMEM_LIMIT),
        name="att_mla",
    )(qb, kbl, vbl, kbc, vbc)


def _post_kernel(x1_ref, mods_ref, oa_ref, ob_ref, nm_ref, wing_ref, woa_ref, wob_ref, wout_ref,
                 n2_ref, wgu_ref, wd_ref, nf_ref, y_ref):
    x1 = x1_ref[...]
    h2 = _mod_norm(x1, nm_ref[...], mods_ref, 1).astype(BF16)
    g = _dot(h2, wing_ref[...])
    m = (jax.nn.sigmoid(g[:, :D_MODEL]) * _dot(oa_ref[...], woa_ref[...])
         + jax.nn.sigmoid(g[:, D_MODEL:]) * _dot(ob_ref[...], wob_ref[...]))
    x2 = x1 + _gate(mods_ref, 1) * _dot(m.astype(BF16), wout_ref[...])
    h3 = _mod_norm(x2, n2_ref[...], mods_ref, 2).astype(BF16)
    x3 = x2 + 0.5 * _gate(mods_ref, 2) * _ffn(h3, wgu_ref, wd_ref)
    y_ref[...] = _rms(x3, nf_ref[...])


def _post(x1, mods3, oa, ob, w, latent, tiles_per_batch):
    t = x1.shape[0]
    tm = TOKEN_TILE
    if latent:
        mod_row = lambda i: (1 + i // tiles_per_batch, 0, 0)
    else:
        mod_row = lambda i: (0, 0, 0)
    row_blk = lambda i: (i, 0)
    weights = [w["nm"], w["win_g"], w["woa"], w["wob"], w["wout"], w["n2"], w["wgu2"], w["wd2"], w["nf"]]
    in_specs = [pl.BlockSpec((tm, D_MODEL), row_blk),
                pl.BlockSpec((None, 1, N_MOD * D_MODEL), mod_row),
                pl.BlockSpec((tm, A_Q_W), row_blk),
                pl.BlockSpec((tm, B_HEADS * B_V_DIM), row_blk)]
    in_specs += [_const_spec(a.shape) for a in weights]
    return pl.pallas_call(
        _post_kernel,
        grid=(t // tm,),
        in_specs=in_specs,
        out_specs=pl.BlockSpec((tm, D_MODEL), row_blk),
        out_shape=jax.ShapeDtypeStruct((t, D_MODEL), F32),
        compiler_params=pltpu.CompilerParams(dimension_semantics=("parallel",), vmem_limit_bytes=VMEM_LIMIT),
        name="post_latent" if latent else "post_context",
    )(x1, mods3, oa, ob, *weights)


def _rope_tables(n):
    rows = n // GRID_W
    t_row = jnp.repeat(jnp.arange(rows, dtype=F32), GRID_W)
    t_col = jnp.tile(jnp.arange(GRID_W, dtype=F32), rows)

    def angles(d_rot):
        d_half = d_rot // 2
        inv = 1.0 / (ROPE_THETA ** (jnp.arange(0, d_half, 2, dtype=F32) / d_half))
        ar = t_row[:, None] * inv[None, :]
        ac = t_col[:, None] * inv[None, :]
        return jnp.concatenate([ar, ar, ac, ac], axis=-1)

    def signed(sin, d_rot):
        q = d_rot // 4
        sign = jnp.where((jnp.arange(d_rot) % (2 * q)) < q, -1.0, 1.0).astype(F32)
        return sin * sign[None, :]

    ang_a = angles(A_HEAD_DIM)
    cos_a = jnp.tile(jnp.cos(ang_a), (1, LANES // A_HEAD_DIM))
    sin_a = jnp.tile(signed(jnp.sin(ang_a), A_HEAD_DIM), (1, LANES // A_HEAD_DIM))
    ang_b = angles(B_ROPE_DIM)
    pad_l = B_NOPE_DIM
    pad_r = LANES - B_NOPE_DIM - B_ROPE_DIM
    cos_b = jnp.pad(jnp.cos(ang_b), ((0, 0), (pad_l, pad_r)), constant_values=1.0)
    sin_b = jnp.pad(signed(jnp.sin(ang_b), B_ROPE_DIM), ((0, 0), (pad_l, pad_r)))
    return cos_a, sin_a, cos_b, sin_b


def _prep_weights(ffn1_norm, ffn1_w_gu, ffn1_w_down, mix_norm, w_in, q_lat_norm, kv_lat_norm, w_uq, w_ukv,
                  w_o_a, w_o_b, w_out, ffn2_norm, ffn2_w_gu, ffn2_w_down, final_norm):
    def gu(wm):
        a = wm[:, :D_FF].reshape(D_MODEL, N_FF_CHUNKS, FF_CHUNK)
        u = wm[:, D_FF:].reshape(D_MODEL, N_FF_CHUNKS, FF_CHUNK)
        return jnp.concatenate([a, u], axis=2).transpose(1, 0, 2).astype(BF16)

    def down(wm):
        return wm.reshape(N_FF_CHUNKS, FF_CHUNK, D_MODEL).astype(BF16)

    win = w_in[0]
    n_attn = A_Q_W + 2 * A_KV_W + Q_LORA + KV_LORA
    krope_cols = jnp.pad(win[:, n_attn:n_attn + B_ROPE_DIM],
                         ((0, 0), (B_NOPE_DIM, LANES - B_NOPE_DIM - B_ROPE_DIM)))
    win_a = jnp.concatenate([win[:, :n_attn], krope_cols], axis=1).astype(BF16)
    win_g = win[:, n_attn + B_ROPE_DIM:].astype(BF16)

    wuq = jnp.pad(w_uq[0].reshape(Q_LORA, B_HEADS, B_QK_DIM),
                  ((0, 0), (0, 0), (0, LANES - B_QK_DIM))).reshape(Q_LORA, B_HEADS * LANES).astype(BF16)
    wukv = w_ukv[0].reshape(KV_LORA, B_HEADS, B_NOPE_DIM + B_V_DIM)
    wuk = jnp.pad(wukv[:, :, :B_NOPE_DIM],
                  ((0, 0), (0, 0), (0, LANES - B_NOPE_DIM))).reshape(KV_LORA, B_HEADS * LANES).astype(BF16)
    wuv = wukv[:, :, B_NOPE_DIM:].reshape(KV_LORA, B_HEADS * B_V_DIM).astype(BF16)

    return {
        "n1": ffn1_norm, "wgu1": gu(ffn1_w_gu[0]), "wd1": down(ffn1_w_down[0]),
        "nm": mix_norm, "win_a": win_a, "win_g": win_g,
        "qn": q_lat_norm, "kvn": kv_lat_norm, "wuq": wuq, "wuk": wuk, "wuv": wuv,
        "woa": w_o_a[0].astype(BF16), "wob": w_o_b[0].astype(BF16), "wout": w_out[0].astype(BF16),
        "n2": ffn2_norm, "wgu2": gu(ffn2_w_gu[0]), "wd2": down(ffn2_w_down[0]),
        "nf": final_norm.reshape(1, D_MODEL),
    }


def kernel(x_prompt, x_sample, cache_attn_k, cache_attn_v, cache_mla_ckv, cache_mla_krope, c, c_ctx, ada_w, ada_b, ffn1_norm, ffn1_w_gu, ffn1_w_down, mix_norm, w_in, attn_sink, q_lat_norm, kv_lat_norm, w_uq, w_ukv, w_o_a, w_o_b, w_out, ffn2_norm, ffn2_w_gu, ffn2_w_down, final_norm):
    assert ada_w.shape[0] == 1, "single trunk layer"
    bp, sp, d = x_prompt.shape
    bs, ns, _ = x_sample.shape
    past = cache_attn_k.shape[2]
    assert d == D_MODEL and bs + 1 <= MOD_ROWS
    assert sp % TOKEN_TILE == 0 or TOKEN_TILE % sp == 0
    assert ns % TOKEN_TILE == 0 and ns % MLA_Q_TILE == 0 and ns % Q_BLOCK == 0 and ns >= BAND

    w = _prep_weights(ffn1_norm, ffn1_w_gu, ffn1_w_down, mix_norm, w_in, q_lat_norm, kv_lat_norm, w_uq, w_ukv,
                      w_o_a, w_o_b, w_out, ffn2_norm, ffn2_w_gu, ffn2_w_down, final_norm)
    sink = attn_sink[0]

    cvec = jnp.concatenate([c_ctx[None, :], c, jnp.zeros((MOD_ROWS - 1 - bs, d), F32)], axis=0)
    mods = _ada_mods(cvec, ada_w[0], ada_b)
    mods3 = mods.reshape(MOD_ROWS, 1, N_MOD * D_MODEL)

    xp = x_prompt.reshape(bp * sp, d)
    (x1p, qa, kap, vad, qb, kb, vb, k_nat, v_nat, ckv_n, krope) = _pre(xp, mods3, w, False, 1, None)
    oa, ob = _att_ctx(sink, qa, kap, vad, qb, kb, vb, sp)
    y_prompt = _post(x1p, mods3, oa, ob, w, False, 1).reshape(bp, sp, d)

    tiles_per_batch = ns // TOKEN_TILE
    xs = x_sample.reshape(bs * ns, d)
    (x1s, qa, kap, vad, qb, kb, vb) = _pre(xs, mods3, w, True, tiles_per_batch, _rope_tables(ns))
    krp_c = jnp.pad(cache_mla_krope[:, 0], ((0, 0), (0, 0), (B_NOPE_DIM, LANES - B_NOPE_DIM - B_ROPE_DIM)))
    kbc, vbc = _cache_kv(cache_mla_ckv[:, 0], krp_c, w["wuk"], w["wuv"])
    r3 = lambda a: a.reshape(bs, ns, a.shape[1])
    oa = _att_win(sink, r3(qa), r3(kap), r3(vad),
                  cache_attn_k[:, 0].reshape(bs, past, A_KV_W), cache_attn_v[:, 0].reshape(bs, past, A_KV_W))
    ob = _att_mla(r3(qb), r3(kb), r3(vb), kbc, vbc)
    y_sample = _post(x1s, mods3, oa.reshape(bs * ns, A_Q_W), ob.reshape(bs * ns, B_HEADS * B_V_DIM),
                     w, True, tiles_per_batch).reshape(bs, ns, d)

    new_attn_k = k_nat.reshape(bp, 1, sp, A_KV_HEADS, A_HEAD_DIM)
    new_attn_v = v_nat.reshape(bp, 1, sp, A_KV_HEADS, A_HEAD_DIM)
    new_mla_ckv = ckv_n.reshape(bp, 1, sp, KV_LORA)
    new_mla_krope = krope.reshape(bp, 1, sp, B_ROPE_DIM)
    return (y_prompt, y_sample, new_attn_k, new_attn_v, new_mla_ckv, new_mla_krope)
```

```python
import functools

import jax
import jax.numpy as jnp
from jax import lax
from jax.experimental import pallas as pl
from jax.experimental.pallas import tpu as pltpu

F32 = jnp.float32
BF16 = jnp.bfloat16

D_MODEL = 1024
N_MOD = 9
GRID_W = 64
WINDOW = 128
A_HEADS = 8
A_KV_HEADS = 2
A_HEAD_DIM = 64
A_Q_W = A_HEADS * A_HEAD_DIM
A_KV_W = A_KV_HEADS * A_HEAD_DIM
B_HEADS = 8
B_NOPE_DIM = 64
B_ROPE_DIM = 32
B_V_DIM = 64
B_QK_DIM = B_NOPE_DIM + B_ROPE_DIM
Q_LORA = 256
KV_LORA = 256
D_FF = 2816
ROPE_THETA = 10000.0
EPS = 1e-6
NEG = -1e30
A_SCALE = A_HEAD_DIM ** -0.5
B_SCALE = B_QK_DIM ** -0.5

LANES = 128
HALF = LANES // 2
FF_CHUNK = 256
N_FF_CHUNKS = D_FF // FF_CHUNK
TOKEN_TILE = 512
ADA_TILE = 1024
Q_BLOCK = 128
BAND = 3 * Q_BLOCK
MLA_Q_TILE = 256
MOD_ROWS = 8
VMEM_LIMIT = 56 * 1024 * 1024

IN_A_W = A_Q_W + 2 * A_KV_W + Q_LORA + KV_LORA + LANES
OFF_KA = A_Q_W
OFF_VA = OFF_KA + A_KV_W
OFF_QLAT = OFF_VA + A_KV_W
OFF_CKV = OFF_QLAT + Q_LORA
OFF_KROPE = OFF_CKV + KV_LORA


def _dot(a, b):
    return jnp.dot(a, b, preferred_element_type=F32)


def _dot_nt(a, b):
    return lax.dot_general(a, b, (((1,), (1,)), ((), ())), preferred_element_type=F32)


def _rms(x, g):
    ms = jnp.mean(x * x, axis=-1, keepdims=True)
    return x * lax.rsqrt(ms + EPS) * g


def _mod_norm(x, g, mods_ref, k):
    shift = mods_ref[:, (3 * k) * D_MODEL:(3 * k + 1) * D_MODEL]
    scale = mods_ref[:, (3 * k + 1) * D_MODEL:(3 * k + 2) * D_MODEL]
    return _rms(x, g) * (1.0 + scale) + shift


def _gate(mods_ref, k):
    return mods_ref[:, (3 * k + 2) * D_MODEL:(3 * k + 3) * D_MODEL]


def _ffn(h, wgu_ref, wd_ref):
    def gate_up(c):
        a = _dot(h, wgu_ref[:, c * FF_CHUNK:(c + 1) * FF_CHUNK])
        u = _dot(h, wgu_ref[:, D_FF + c * FF_CHUNK:D_FF + (c + 1) * FF_CHUNK])
        return a, u

    acc = None
    nxt = gate_up(0)
    for c in range(N_FF_CHUNKS):
        a, u = nxt
        if c + 1 < N_FF_CHUNKS:
            nxt = gate_up(c + 1)
        act = (a * jax.nn.sigmoid(a) * u).astype(BF16)
        d = _dot(act, wd_ref[c * FF_CHUNK:(c + 1) * FF_CHUNK, :])
        acc = d if acc is None else acc + d
    return acc


def _rope(x, cos, sin_signed, dist):
    lane = lax.broadcasted_iota(jnp.int32, x.shape, 1)
    first = (lane & (2 * dist - 1)) < dist
    partner = jnp.where(first, pltpu.roll(x, LANES - dist, axis=1), pltpu.roll(x, dist, axis=1))
    return x * cos + partner * sin_signed


def _attend(scores, values, sink):
    m = None
    for s in scores:
        mx = jnp.max(s, axis=-1, keepdims=True)
        m = mx if m is None else jnp.maximum(m, mx)
    if sink is not None:
        m = jnp.maximum(m, sink)
    l = None
    r = None
    for s, v in zip(scores, values):
        p = jnp.exp(s - m)
        ls = jnp.sum(p, axis=-1, keepdims=True)
        pv = _dot(p.astype(BF16), v)
        l = ls if l is None else l + ls
        r = pv if r is None else r + pv
    if sink is not None:
        l = l + jnp.exp(sink - m)
    return r / l


def _low_lanes(rows):
    return lax.broadcasted_iota(jnp.int32, (rows, LANES), 1) < HALF


def _ada_kernel(c_ref, w_ref, b_ref, o_ref):
    c = c_ref[...]
    s = (c * jax.nn.sigmoid(c)).astype(BF16)
    o_ref[...] = _dot(s, w_ref[...].astype(BF16)) + b_ref[...]


def _ada_mods(cvec, ada_w, ada_b):
    n = ada_w.shape[1]
    return pl.pallas_call(
        _ada_kernel,
        grid=(n // ADA_TILE,),
        in_specs=[
            pl.BlockSpec((MOD_ROWS, D_MODEL), lambda j: (0, 0)),
            pl.BlockSpec((D_MODEL, ADA_TILE), lambda j: (0, j)),
            pl.BlockSpec((1, ADA_TILE), lambda j: (0, j)),
        ],
        out_specs=pl.BlockSpec((MOD_ROWS, ADA_TILE), lambda j: (0, j)),
        out_shape=jax.ShapeDtypeStruct((MOD_ROWS, n), F32),
        compiler_params=pltpu.CompilerParams(dimension_semantics=("parallel",), vmem_limit_bytes=VMEM_LIMIT),
        name="ada_mods",
    )(cvec, ada_w, ada_b)


def _pre_kernel(latent, *refs):
    if latent:
        (x_ref, mods_ref, n1_ref, wgu_ref, wd_ref, nm_ref, win_ref, qn_ref, kvn_ref, wuq_ref, wuk_ref, wuv_ref,
         cosa_ref, sina_ref, cosb_ref, sinb_ref,
         x1_ref, qa_ref, kap_ref, vad_ref, qb_ref, kb_ref, vb_ref) = refs
    else:
        (x_ref, mods_ref, n1_ref, wgu_ref, wd_ref, nm_ref, win_ref, qn_ref, kvn_ref, wuq_ref, wuk_ref, wuv_ref,
         x1_ref, qa_ref, kap_ref, vad_ref, qb_ref, kb_ref, vb_ref,
         knat_ref, vnat_ref, ckvn_ref, krope_ref) = refs

    x = x_ref[...]
    h1 = _mod_norm(x, n1_ref[...], mods_ref, 0).astype(BF16)
    x1 = x + 0.5 * _gate(mods_ref, 0) * _ffn(h1, wgu_ref, wd_ref)
    x1_ref[...] = x1

    h2 = _mod_norm(x1, nm_ref[...], mods_ref, 1).astype(BF16)
    z = _dot(h2, win_ref[...])
    low = _low_lanes(x.shape[0])

    for c in range(A_Q_W // LANES):
        q = z[:, c * LANES:(c + 1) * LANES]
        if latent:
            q = _rope(q, cosa_ref[...], sina_ref[...], A_HEAD_DIM // 4)
        qa_ref[:, c * LANES:(c + 1) * LANES] = (q * A_SCALE).astype(BF16)
    k = z[:, OFF_KA:OFF_KA + LANES]
    v = z[:, OFF_VA:OFF_VA + LANES]
    if latent:
        k = _rope(k, cosa_ref[...], sina_ref[...], A_HEAD_DIM // 4)
    else:
        knat_ref[...] = k
        vnat_ref[...] = v
    k_sw = pltpu.roll(k, HALF, axis=1)
    v_sw = pltpu.roll(v, HALF, axis=1)
    zero = jnp.zeros_like(k)
    kap_ref[:, 0 * LANES:1 * LANES] = jnp.where(low, k, zero).astype(BF16)
    kap_ref[:, 1 * LANES:2 * LANES] = jnp.where(low, zero, k_sw).astype(BF16)
    kap_ref[:, 2 * LANES:3 * LANES] = jnp.where(low, k_sw, zero).astype(BF16)
    kap_ref[:, 3 * LANES:4 * LANES] = jnp.where(low, zero, k).astype(BF16)
    vad_ref[:, 0 * LANES:1 * LANES] = jnp.where(low, v, v_sw).astype(BF16)
    vad_ref[:, 1 * LANES:2 * LANES] = jnp.where(low, v_sw, v).astype(BF16)

    q_lat = _rms(z[:, OFF_QLAT:OFF_QLAT + Q_LORA], qn_ref[...]).astype(BF16)
    qb = _dot(q_lat, wuq_ref[...])
    ckv_n = _rms(z[:, OFF_CKV:OFF_CKV + KV_LORA], kvn_ref[...])
    krp = z[:, OFF_KROPE:OFF_KROPE + LANES]
    if latent:
        krp = _rope(krp, cosb_ref[...], sinb_ref[...], B_ROPE_DIM // 4)
    else:
        ckvn_ref[...] = ckv_n
        krope_ref[...] = pltpu.roll(krp, HALF, axis=1)[:, :B_ROPE_DIM]
    ckv_b = ckv_n.astype(BF16)
    kn = _dot(ckv_b, wuk_ref[...])
    vb_ref[...] = _dot(ckv_b, wuv_ref[...]).astype(BF16)
    for h in range(B_HEADS):
        qh = qb[:, h * LANES:(h + 1) * LANES]
        if latent:
            qh = _rope(qh, cosb_ref[...], sinb_ref[...], B_ROPE_DIM // 4)
        qb_ref[:, h * LANES:(h + 1) * LANES] = (qh * B_SCALE).astype(BF16)
        kb_ref[:, h * LANES:(h + 1) * LANES] = (kn[:, h * LANES:(h + 1) * LANES] + krp).astype(BF16)


def _const_spec(shape):
    nd = len(shape)
    return pl.BlockSpec(shape, lambda i: (0,) * nd, pipeline_mode=pl.Buffered(1))


def _pre(x, mods3, w, latent, tiles_per_batch, rope):
    t = x.shape[0]
    tm = TOKEN_TILE
    n_tiles = t // tm
    if latent:
        mod_row = lambda i: (1 + i // tiles_per_batch, 0, 0)
        pos_blk = lambda i: (i % tiles_per_batch, 0)
    else:
        mod_row = lambda i: (0, 0, 0)
    row_blk = lambda i: (i, 0)

    weights = [w["n1"], w["wgu1"], w["wd1"], w["nm"], w["win_a"], w["qn"], w["kvn"], w["wuq"], w["wuk"], w["wuv"]]
    in_specs = [pl.BlockSpec((tm, D_MODEL), row_blk),
                pl.BlockSpec((None, 1, N_MOD * D_MODEL), mod_row)]
    in_specs += [_const_spec(a.shape) for a in weights]
    args = [x, mods3] + weights
    if latent:
        in_specs += [pl.BlockSpec((tm, LANES), pos_blk)] * 4
        args += list(rope)

    widths = [(D_MODEL, F32), (A_Q_W, BF16), (4 * LANES, BF16), (2 * LANES, BF16),
              (B_HEADS * LANES, BF16), (B_HEADS * LANES, BF16), (B_HEADS * B_V_DIM, BF16)]
    if not latent:
        widths += [(A_KV_W, F32), (A_KV_W, F32), (KV_LORA, F32), (B_ROPE_DIM, F32)]
    out_specs = [pl.BlockSpec((tm, wd), row_blk) for wd, _ in widths]
    out_shape = [jax.ShapeDtypeStruct((t, wd), dt) for wd, dt in widths]

    return pl.pallas_call(
        functools.partial(_pre_kernel, latent),
        grid=(n_tiles,),
        in_specs=in_specs,
        out_specs=out_specs,
        out_shape=out_shape,
        compiler_params=pltpu.CompilerParams(dimension_semantics=("parallel",), vmem_limit_bytes=VMEM_LIMIT),
        name="pre_latent" if latent else "pre_context",
    )(*args)


def _cache_kernel(ckv_ref, krp_ref, wuk_ref, wuv_ref, kb_ref, vb_ref):
    c = ckv_ref[...].astype(BF16)
    kn = _dot(c, wuk_ref[...])
    krp = krp_ref[...]
    for h in range(B_HEADS):
        kb_ref[:, h * LANES:(h + 1) * LANES] = (kn[:, h * LANES:(h + 1) * LANES] + krp).astype(BF16)
    vb_ref[...] = _dot(c, wuv_ref[...]).astype(BF16)


def _cache_kv(ckv, krp, wuk, wuv):
    b, p, _ = ckv.shape
    return pl.pallas_call(
        _cache_kernel,
        grid=(b,),
        in_specs=[pl.BlockSpec((None, p, KV_LORA), lambda i: (i, 0, 0)),
                  pl.BlockSpec((None, p, LANES), lambda i: (i, 0, 0)),
                  pl.BlockSpec(wuk.shape, lambda i: (0, 0)),
                  pl.BlockSpec(wuv.shape, lambda i: (0, 0))],
        out_specs=[pl.BlockSpec((None, p, B_HEADS * LANES), lambda i: (i, 0, 0)),
                   pl.BlockSpec((None, p, B_HEADS * B_V_DIM), lambda i: (i, 0, 0))],
        out_shape=[jax.ShapeDtypeStruct((b, p, B_HEADS * LANES), BF16),
                   jax.ShapeDtypeStruct((b, p, B_HEADS * B_V_DIM), BF16)],
        compiler_params=pltpu.CompilerParams(dimension_semantics=("parallel",), vmem_limit_bytes=VMEM_LIMIT),
        name="cache_mla_kv",
    )(ckv, krp, wuk, wuv)


def _att_ctx_kernel(sink_ref, qa_ref, kap_ref, vad_ref, qb_ref, kb_ref, vb_ref, oa_ref, ob_ref):
    low = _low_lanes(qa_ref.shape[0])
    for kv in range(A_KV_HEADS):
        v = vad_ref[:, kv * LANES:(kv + 1) * LANES]
        for pr in range(2):
            c = 2 * kv + pr
            qp = qa_ref[:, c * LANES:(c + 1) * LANES]
            res = []
            for half in range(2):
                kc = 2 * kv + half
                s = _dot_nt(qp, kap_ref[:, kc * LANES:(kc + 1) * LANES])
                res.append(_attend([s], [v], sink_ref[2 * c + half]))
            oa_ref[:, c * LANES:(c + 1) * LANES] = jnp.where(low, res[0], res[1]).astype(BF16)
    for c in range(B_HEADS // 2):
        v = vb_ref[:, c * LANES:(c + 1) * LANES]
        res = []
        for half in range(2):
            h = 2 * c + half
            s = _dot_nt(qb_ref[:, h * LANES:(h + 1) * LANES], kb_ref[:, h * LANES:(h + 1) * LANES])
            res.append(_attend([s], [v], None))
        ob_ref[:, c * LANES:(c + 1) * LANES] = jnp.where(low, res[0], res[1]).astype(BF16)


def _att_ctx(sink, qa, kap, vad, qb, kb, vb, seq):
    t = qa.shape[0]
    blk = lambda wd: pl.BlockSpec((seq, wd), lambda b: (b, 0))
    return pl.pallas_call(
        _att_ctx_kernel,
        grid=(t // seq,),
        in_specs=[pl.BlockSpec(memory_space=pltpu.SMEM),
                  blk(qa.shape[1]), blk(kap.shape[1]), blk(vad.shape[1]),
                  blk(qb.shape[1]), blk(kb.shape[1]), blk(vb.shape[1])],
        out_specs=[blk(A_Q_W), blk(B_HEADS * B_V_DIM)],
        out_shape=[jax.ShapeDtypeStruct((t, A_Q_W), BF16), jax.ShapeDtypeStruct((t, B_HEADS * B_V_DIM), BF16)],
        compiler_params=pltpu.CompilerParams(dimension_semantics=("parallel",), vmem_limit_bytes=VMEM_LIMIT),
        name="att_context",
    )(sink, qa, kap, vad, qb, kb, vb)


def _att_win_kernel(n_lat, sink_ref, qa_ref, kap_ref, vad_ref, ck_ref, cv_ref, oa_ref):
    i = pl.program_id(1)
    start = pl.multiple_of(jnp.clip(i * Q_BLOCK - Q_BLOCK, 0, n_lat - BAND), Q_BLOCK)
    past = ck_ref.shape[0]
    low_c = _low_lanes(past)
    low_q = _low_lanes(Q_BLOCK)

    ck = ck_ref[...]
    cv = cv_ref[...]
    ck_sw = pltpu.roll(ck, HALF, axis=1)
    cv_sw = pltpu.roll(cv, HALF, axis=1)
    zero = jnp.zeros_like(ck)
    kc_pairs = [[jnp.where(low_c, ck, zero), jnp.where(low_c, zero, ck_sw)],
                [jnp.where(low_c, ck_sw, zero), jnp.where(low_c, zero, ck)]]
    vc_dup = [jnp.where(low_c, cv, cv_sw), jnp.where(low_c, cv_sw, cv)]

    r = lax.broadcasted_iota(jnp.int32, (2 * Q_BLOCK, BAND), 0) & (Q_BLOCK - 1)
    j = lax.broadcasted_iota(jnp.int32, (2 * Q_BLOCK, BAND), 1)
    rel = (start + j) - (i * Q_BLOCK + r)
    valid = jnp.abs(rel) <= WINDOW
    upper = lax.broadcasted_iota(jnp.int32, (2 * Q_BLOCK, 1), 0) >= Q_BLOCK

    for kv in range(A_KV_HEADS):
        q2 = jnp.concatenate([qa_ref[:, (2 * kv) * LANES:(2 * kv + 1) * LANES],
                              qa_ref[:, (2 * kv + 1) * LANES:(2 * kv + 2) * LANES]], axis=0)
        v_band = vad_ref[pl.ds(start, BAND), kv * LANES:(kv + 1) * LANES]
        v_ctx = vc_dup[kv].astype(BF16)
        res = []
        for half in range(2):
            kc = 2 * kv + half
            k_band = kap_ref[pl.ds(start, BAND), kc * LANES:(kc + 1) * LANES]
            s_band = jnp.where(valid, _dot_nt(q2, k_band), NEG)
            s_ctx = _dot_nt(q2, kc_pairs[kv][half].astype(BF16))
            sink = jnp.where(upper, sink_ref[4 * kv + 2 + half], sink_ref[4 * kv + half])
            res.append(_attend([s_band, s_ctx], [v_band, v_ctx], sink))
        for pr in range(2):
            c = 2 * kv + pr
            oa_ref[:, c * LANES:(c + 1) * LANES] = jnp.where(
                low_q, res[0][pr * Q_BLOCK:(pr + 1) * Q_BLOCK], res[1][pr * Q_BLOCK:(pr + 1) * Q_BLOCK]).astype(BF16)


def _att_win(sink, qa, kap, vad, ck, cv):
    b, n, _ = qa.shape
    past = ck.shape[1]
    return pl.pallas_call(
        functools.partial(_att_win_kernel, n),
        grid=(b, n // Q_BLOCK),
        in_specs=[pl.BlockSpec(memory_space=pltpu.SMEM),
                  pl.BlockSpec((None, Q_BLOCK, A_Q_W), lambda bi, i: (bi, i, 0)),
                  pl.BlockSpec((None, n, kap.shape[2]), lambda bi, i: (bi, 0, 0)),
                  pl.BlockSpec((None, n, vad.shape[2]), lambda bi, i: (bi, 0, 0)),
                  pl.BlockSpec((None, past, A_KV_W), lambda bi, i: (bi, 0, 0)),
                  pl.BlockSpec((None, past, A_KV_W), lambda bi, i: (bi, 0, 0))],
        out_specs=pl.BlockSpec((None, Q_BLOCK, A_Q_W), lambda bi, i: (bi, i, 0)),
        out_shape=jax.ShapeDtypeStruct((b, n, A_Q_W), BF16),
        compiler_params=pltpu.CompilerParams(dimension_semantics=("parallel", "parallel"),
                                             vmem_limit_bytes=VMEM_LIMIT),
        name="att_window",
    )(sink, qa, kap, vad, ck, cv)


def _att_mla_kernel(qb_ref, kbl_ref, vbl_ref, kbc_ref, vbc_ref, ob_ref):
    low = _low_lanes(qb_ref.shape[0])
    for c in range(B_HEADS // 2):
        v_ctx = vbc_ref[:, c * LANES:(c + 1) * LANES]
        v_lat = vbl_ref[:, c * LANES:(c + 1) * LANES]
        res = []
        for half in range(2):
            h = 2 * c + half
            q = qb_ref[:, h * LANES:(h + 1) * LANES]
            s_ctx = _dot_nt(q, kbc_ref[:, h * LANES:(h + 1) * LANES])
            s_lat = _dot_nt(q, kbl_ref[:, h * LANES:(h + 1) * LANES])
            res.append(_attend([s_ctx, s_lat], [v_ctx, v_lat], None))
        ob_ref[:, c * LANES:(c + 1) * LANES] = jnp.where(low, res[0], res[1]).astype(BF16)


def _att_mla(qb, kbl, vbl, kbc, vbc):
    b, n, _ = qb.shape
    past = kbc.shape[1]
    tq = MLA_Q_TILE
    return pl.pallas_call(
        _att_mla_kernel,
        grid=(b, n // tq),
        in_specs=[pl.BlockSpec((None, tq, qb.shape[2]), lambda bi, i: (bi, i, 0)),
                  pl.BlockSpec((None, n, kbl.shape[2]), lambda bi, i: (bi, 0, 0)),
                  pl.BlockSpec((None, n, vbl.shape[2]), lambda bi, i: (bi, 0, 0)),
                  pl.BlockSpec((None, past, kbc.shape[2]), lambda bi, i: (bi, 0, 0)),
                  pl.BlockSpec((None, past, vbc.shape[2]), lambda bi, i: (bi, 0, 0))],
        out_specs=pl.BlockSpec((None, tq, B_HEADS * B_V_DIM), lambda bi, i: (bi, i, 0)),
        out_shape=jax.ShapeDtypeStruct((b, n, B_HEADS * B_V_DIM), BF16),
        compiler_params=pltpu.CompilerParams(dimension_semantics=("parallel", "parallel"),
                                             vmem_limit_bytes=VMEM_LIMIT),
        name="att_mla",
    )(qb, kbl, vbl, kbc, vbc)


def _post_kernel(x1_ref, mods_ref, oa_ref, ob_ref, nm_ref, wing_ref, woa_ref, wob_ref, wout_ref,
                 n2_ref, wgu_ref, wd_ref, nf_ref, y_ref):
    x1 = x1_ref[...]
    h2 = _mod_norm(x1, nm_ref[...], mods_ref, 1).astype(BF16)
    g = _dot(h2, wing_ref[...])
    m = (jax.nn.sigmoid(g[:, :D_MODEL]) * _dot(oa_ref[...], woa_ref[...])
         + jax.nn.sigmoid(g[:, D_MODEL:]) * _dot(ob_ref[...], wob_ref[...]))
    x2 = x1 + _gate(mods_ref, 1) * _dot(m.astype(BF16), wout_ref[...])
    h3 = _mod_norm(x2, n2_ref[...], mods_ref, 2).astype(BF16)
    x3 = x2 + 0.5 * _gate(mods_ref, 2) * _ffn(h3, wgu_ref, wd_ref)
    y_ref[...] = _rms(x3, nf_ref[...])


def _post(x1, mods3, oa, ob, w, latent, tiles_per_batch):
    t = x1.shape[0]
    tm = TOKEN_TILE
    if latent:
        mod_row = lambda i: (1 + i // tiles_per_batch, 0, 0)
    else:
        mod_row = lambda i: (0, 0, 0)
    row_blk = lambda i: (i, 0)
    weights = [w["nm"], w["win_g"], w["woa"], w["wob"], w["wout"], w["n2"], w["wgu2"], w["wd2"], w["nf"]]
    in_specs = [pl.BlockSpec((tm, D_MODEL), row_blk),
                pl.BlockSpec((None, 1, N_MOD * D_MODEL), mod_row),
                pl.BlockSpec((tm, A_Q_W), row_blk),
                pl.BlockSpec((tm, B_HEADS * B_V_DIM), row_blk)]
    in_specs += [_const_spec(a.shape) for a in weights]
    return pl.pallas_call(
        _post_kernel,
        grid=(t // tm,),
        in_specs=in_specs,
        out_specs=pl.BlockSpec((tm, D_MODEL), row_blk),
        out_shape=jax.ShapeDtypeStruct((t, D_MODEL), F32),
        compiler_params=pltpu.CompilerParams(dimension_semantics=("parallel",), vmem_limit_bytes=VMEM_LIMIT),
        name="post_latent" if latent else "post_context",
    )(x1, mods3, oa, ob, *weights)


def _rope_tables(n):
    rows = n // GRID_W
    t_row = jnp.repeat(jnp.arange(rows, dtype=F32), GRID_W)
    t_col = jnp.tile(jnp.arange(GRID_W, dtype=F32), rows)

    def angles(d_rot):
        d_half = d_rot // 2
        inv = 1.0 / (ROPE_THETA ** (jnp.arange(0, d_half, 2, dtype=F32) / d_half))
        ar = t_row[:, None] * inv[None, :]
        ac = t_col[:, None] * inv[None, :]
        return jnp.concatenate([ar, ar, ac, ac], axis=-1)

    def signed(sin, d_rot):
        q = d_rot // 4
        sign = jnp.where((jnp.arange(d_rot) % (2 * q)) < q, -1.0, 1.0).astype(F32)
        return sin * sign[None, :]

    ang_a = angles(A_HEAD_DIM)
    cos_a = jnp.tile(jnp.cos(ang_a), (1, LANES // A_HEAD_DIM))
    sin_a = jnp.tile(signed(jnp.sin(ang_a), A_HEAD_DIM), (1, LANES // A_HEAD_DIM))
    ang_b = angles(B_ROPE_DIM)
    pad_l = B_NOPE_DIM
    pad_r = LANES - B_NOPE_DIM - B_ROPE_DIM
    cos_b = jnp.pad(jnp.cos(ang_b), ((0, 0), (pad_l, pad_r)), constant_values=1.0)
    sin_b = jnp.pad(signed(jnp.sin(ang_b), B_ROPE_DIM), ((0, 0), (pad_l, pad_r)))
    return cos_a, sin_a, cos_b, sin_b


def _prep_weights(ffn1_norm, ffn1_w_gu, ffn1_w_down, mix_norm, w_in, q_lat_norm, kv_lat_norm, w_uq, w_ukv,
                  w_o_a, w_o_b, w_out, ffn2_norm, ffn2_w_gu, ffn2_w_down, final_norm):
    win = w_in[0]
    n_attn = A_Q_W + 2 * A_KV_W + Q_LORA + KV_LORA
    krope_cols = jnp.pad(win[:, n_attn:n_attn + B_ROPE_DIM],
                         ((0, 0), (B_NOPE_DIM, LANES - B_NOPE_DIM - B_ROPE_DIM)))
    win_a = jnp.concatenate([win[:, :n_attn], krope_cols], axis=1).astype(BF16)
    win_g = win[:, n_attn + B_ROPE_DIM:].astype(BF16)

    wuq = jnp.pad(w_uq[0].reshape(Q_LORA, B_HEADS, B_QK_DIM),
                  ((0, 0), (0, 0), (0, LANES - B_QK_DIM))).reshape(Q_LORA, B_HEADS * LANES).astype(BF16)
    wukv = w_ukv[0].reshape(KV_LORA, B_HEADS, B_NOPE_DIM + B_V_DIM)
    wuk = jnp.pad(wukv[:, :, :B_NOPE_DIM],
                  ((0, 0), (0, 0), (0, LANES - B_NOPE_DIM))).reshape(KV_LORA, B_HEADS * LANES).astype(BF16)
    wuv = wukv[:, :, B_NOPE_DIM:].reshape(KV_LORA, B_HEADS * B_V_DIM).astype(BF16)

    return {
        "n1": ffn1_norm, "wgu1": ffn1_w_gu[0].astype(BF16), "wd1": ffn1_w_down[0].astype(BF16),
        "nm": mix_norm, "win_a": win_a, "win_g": win_g,
        "qn": q_lat_norm, "kvn": kv_lat_norm, "wuq": wuq, "wuk": wuk, "wuv": wuv,
        "woa": w_o_a[0].astype(BF16), "wob": w_o_b[0].astype(BF16), "wout": w_out[0].astype(BF16),
        "n2": ffn2_norm, "wgu2": ffn2_w_gu[0].astype(BF16), "wd2": ffn2_w_down[0].astype(BF16),
        "nf": final_norm.reshape(1, D_MODEL),
    }


def kernel(x_prompt, x_sample, cache_attn_k, cache_attn_v, cache_mla_ckv, cache_mla_krope, c, c_ctx, ada_w, ada_b, ffn1_norm, ffn1_w_gu, ffn1_w_down, mix_norm, w_in, attn_sink, q_lat_norm, kv_lat_norm, w_uq, w_ukv, w_o_a, w_o_b, w_out, ffn2_norm, ffn2_w_gu, ffn2_w_down, final_norm):
    assert ada_w.shape[0] == 1, "single trunk layer"
    bp, sp, d = x_prompt.shape
    bs, ns, _ = x_sample.shape
    past = cache_attn_k.shape[2]
    assert d == D_MODEL and bs + 1 <= MOD_ROWS
    assert sp % TOKEN_TILE == 0 or TOKEN_TILE % sp == 0
    assert ns % TOKEN_TILE == 0 and ns % MLA_Q_TILE == 0 and ns % Q_BLOCK == 0 and ns >= BAND

    w = _prep_weights(ffn1_norm, ffn1_w_gu, ffn1_w_down, mix_norm, w_in, q_lat_norm, kv_lat_norm, w_uq, w_ukv,
                      w_o_a, w_o_b, w_out, ffn2_norm, ffn2_w_gu, ffn2_w_down, final_norm)
    sink = attn_sink[0]

    cvec = jnp.concatenate([c_ctx[None, :], c, jnp.zeros((MOD_ROWS - 1 - bs, d), F32)], axis=0)
    mods = _ada_mods(cvec, ada_w[0], ada_b)
    mods3 = mods.reshape(MOD_ROWS, 1, N_MOD * D_MODEL)

    xp = x_prompt.reshape(bp * sp, d)
    (x1p, qa, kap, vad, qb, kb, vb, k_nat, v_nat, ckv_n, krope) = _pre(xp, mods3, w, False, 1, None)
    oa, ob = _att_ctx(sink, qa, kap, vad, qb, kb, vb, sp)
    y_prompt = _post(x1p, mods3, oa, ob, w, False, 1).reshape(bp, sp, d)

    tiles_per_batch = ns // TOKEN_TILE
    xs = x_sample.reshape(bs * ns, d)
    (x1s, qa, kap, vad, qb, kb, vb) = _pre(xs, mods3, w, True, tiles_per_batch, _rope_tables(ns))
    krp_c = jnp.pad(cache_mla_krope[:, 0], ((0, 0), (0, 0), (B_NOPE_DIM, LANES - B_NOPE_DIM - B_ROPE_DIM)))
    kbc, vbc = _cache_kv(cache_mla_ckv[:, 0], krp_c, w["wuk"], w["wuv"])
    r3 = lambda a: a.reshape(bs, ns, a.shape[1])
    oa = _att_win(sink, r3(qa), r3(kap), r3(vad),
                  cache_attn_k[:, 0].reshape(bs, past, A_KV_W), cache_attn_v[:, 0].reshape(bs, past, A_KV_W))
    ob = _att_mla(r3(qb), r3(kb), r3(vb), kbc, vbc)
    y_sample = _post(x1s, mods3, oa.reshape(bs * ns, A_Q_W), ob.reshape(bs * ns, B_HEADS * B_V_DIM),
                     w, True, tiles_per_batch).reshape(bs, ns, d)

    new_attn_k = k_nat.reshape(bp, 1, sp, A_KV_HEADS, A_HEAD_DIM)
    new_attn_v = v_nat.reshape(bp, 1, sp, A_KV_HEADS, A_HEAD_DIM)
    new_mla_ckv = ckv_n.reshape(bp, 1, sp, KV_LORA)
    new_mla_krope = krope.reshape(bp, 1, sp, B_ROPE_DIM)
    return (y_prompt, y_sample, new_attn_k, new_attn_v, new_mla_ckv, new_mla_krope)
```

```python
import functools

import jax
import jax.numpy as jnp
from jax import lax
from jax.experimental import pallas as pl
from jax.experimental.pallas import tpu as pltpu

F32 = jnp.float32
BF16 = jnp.bfloat16

D_MODEL = 1024
N_MOD = 9
GRID_W = 64
WINDOW = 128
A_HEADS = 8
A_KV_HEADS = 2
A_HEAD_DIM = 64
A_Q_W = A_HEADS * A_HEAD_DIM
A_KV_W = A_KV_HEADS * A_HEAD_DIM
B_HEADS = 8
B_NOPE_DIM = 64
B_ROPE_DIM = 32
B_V_DIM = 64
B_QK_DIM = B_NOPE_DIM + B_ROPE_DIM
Q_LORA = 256
KV_LORA = 256
D_FF = 2816
ROPE_THETA = 10000.0
EPS = 1e-6
NEG = -1e30
A_SCALE = A_HEAD_DIM ** -0.5
B_SCALE = B_QK_DIM ** -0.5
LOG2E = 1.4426950408889634

LANES = 128
HALF = LANES // 2
FF_CHUNK = 256
N_FF_CHUNKS = D_FF // FF_CHUNK
TOKEN_TILE = 512
ADA_TILE = 1024
Q_BLOCK = 128
BAND = 3 * Q_BLOCK
MLA_Q_TILE = 256
MOD_ROWS = 8
VMEM_LIMIT = 56 * 1024 * 1024

IN_A_W = A_Q_W + 2 * A_KV_W + Q_LORA + KV_LORA + LANES
OFF_KA = A_Q_W
OFF_VA = OFF_KA + A_KV_W
OFF_QLAT = OFF_VA + A_KV_W
OFF_CKV = OFF_QLAT + Q_LORA
OFF_KROPE = OFF_CKV + KV_LORA


def _dot(a, b):
    return jnp.dot(a, b, preferred_element_type=F32)


def _dot_nt(a, b):
    return lax.dot_general(a, b, (((1,), (1,)), ((), ())), preferred_element_type=F32)


def _rms(x, g):
    ms = jnp.mean(x * x, axis=-1, keepdims=True)
    return x * lax.rsqrt(ms + EPS) * g


def _mod_norm(x, g, mods_ref, k):
    shift = mods_ref[:, (3 * k) * D_MODEL:(3 * k + 1) * D_MODEL]
    scale = mods_ref[:, (3 * k + 1) * D_MODEL:(3 * k + 2) * D_MODEL]
    return _rms(x, g) * (1.0 + scale) + shift


def _gate(mods_ref, k):
    return mods_ref[:, (3 * k + 2) * D_MODEL:(3 * k + 3) * D_MODEL]


def _ffn(h, wgu_ref, wd_ref):
    def gate_up(c):
        a = _dot(h, wgu_ref[:, c * FF_CHUNK:(c + 1) * FF_CHUNK])
        u = _dot(h, wgu_ref[:, D_FF + c * FF_CHUNK:D_FF + (c + 1) * FF_CHUNK])
        return a, u

    acc = None
    nxt = gate_up(0)
    for c in range(N_FF_CHUNKS):
        a, u = nxt
        if c + 1 < N_FF_CHUNKS:
            nxt = gate_up(c + 1)
        act = (a * jax.nn.sigmoid(a) * u).astype(BF16)
        d = _dot(act, wd_ref[c * FF_CHUNK:(c + 1) * FF_CHUNK, :])
        acc = d if acc is None else acc + d
    return acc


def _rope(x, cos, sin_signed, dist):
    lane = lax.broadcasted_iota(jnp.int32, x.shape, 1)
    first = (lane & (2 * dist - 1)) < dist
    partner = jnp.where(first, pltpu.roll(x, LANES - dist, axis=1), pltpu.roll(x, dist, axis=1))
    return x * cos + partner * sin_signed


def _store_gqa_kv(k, v, low, kap_ref, vad_ref):
    k_sw = pltpu.roll(k, HALF, axis=1)
    v_sw = pltpu.roll(v, HALF, axis=1)
    zero = jnp.zeros_like(k)
    kap_ref[:, 0 * LANES:1 * LANES] = jnp.where(low, k, zero).astype(BF16)
    kap_ref[:, 1 * LANES:2 * LANES] = jnp.where(low, zero, k_sw).astype(BF16)
    kap_ref[:, 2 * LANES:3 * LANES] = jnp.where(low, k_sw, zero).astype(BF16)
    kap_ref[:, 3 * LANES:4 * LANES] = jnp.where(low, zero, k).astype(BF16)
    vad_ref[:, 0 * LANES:1 * LANES] = jnp.where(low, v, v_sw).astype(BF16)
    vad_ref[:, 1 * LANES:2 * LANES] = jnp.where(low, v_sw, v).astype(BF16)


def _store_seq_minor(out_ref, val):
    nb, feat, seq = out_ref.shape
    for bi in range(nb):
        out_ref[bi] = val[bi * seq:(bi + 1) * seq, :].T[:feat, :]


def _attend(scores, values, sink):
    m = None
    for s in scores:
        mx = jnp.max(s, axis=-1, keepdims=True)
        m = mx if m is None else jnp.maximum(m, mx)
    if sink is not None:
        m = jnp.maximum(m, sink)
    l = None
    r = None
    for s, v in zip(scores, values):
        p = jnp.exp2(s - m)
        ls = jnp.sum(p, axis=-1, keepdims=True)
        pv = _dot(p.astype(BF16), v)
        l = ls if l is None else l + ls
        r = pv if r is None else r + pv
    if sink is not None:
        l = l + jnp.exp2(sink - m)
    return r / l


def _low_lanes(rows):
    return lax.broadcasted_iota(jnp.int32, (rows, LANES), 1) < HALF


def _ada_kernel(c_ref, w_ref, b_ref, o_ref):
    c = c_ref[...]
    s = (c * jax.nn.sigmoid(c)).astype(BF16)
    o_ref[...] = _dot(s, w_ref[...].astype(BF16)) + b_ref[...]


def _ada_mods(cvec, ada_w, ada_b):
    n = ada_w.shape[1]
    return pl.pallas_call(
        _ada_kernel,
        grid=(n // ADA_TILE,),
        in_specs=[
            pl.BlockSpec((MOD_ROWS, D_MODEL), lambda j: (0, 0)),
            pl.BlockSpec((D_MODEL, ADA_TILE), lambda j: (0, j)),
            pl.BlockSpec((1, ADA_TILE), lambda j: (0, j)),
        ],
        out_specs=pl.BlockSpec((MOD_ROWS, ADA_TILE), lambda j: (0, j)),
        out_shape=jax.ShapeDtypeStruct((MOD_ROWS, n), F32),
        compiler_params=pltpu.CompilerParams(dimension_semantics=("parallel",), vmem_limit_bytes=VMEM_LIMIT),
        name="ada_mods",
    )(cvec, ada_w, ada_b)


def _pre_kernel(latent, *refs):
    if latent:
        (x_ref, mods_ref, n1_ref, wgu_ref, wd_ref, nm_ref, win_ref, qn_ref, kvn_ref, wuq_ref, wuk_ref, wuv_ref,
         cosa_ref, sina_ref, cosb_ref, sinb_ref,
         x1_ref, qa_ref, kap_ref, vad_ref, qb_ref, kb_ref, vb_ref) = refs
    else:
        (x_ref, mods_ref, n1_ref, wgu_ref, wd_ref, nm_ref, win_ref, qn_ref, kvn_ref, wuq_ref, wuk_ref, wuv_ref,
         x1_ref, qa_ref, kap_ref, vad_ref, qb_ref, kb_ref, vb_ref,
         knat_ref, vnat_ref, ckvn_ref, krope_ref) = refs

    x = x_ref[...]
    h1 = _mod_norm(x, n1_ref[...], mods_ref, 0).astype(BF16)
    x1 = x + 0.5 * _gate(mods_ref, 0) * _ffn(h1, wgu_ref, wd_ref)
    x1_ref[...] = x1

    h2 = _mod_norm(x1, nm_ref[...], mods_ref, 1).astype(BF16)
    z = _dot(h2, win_ref[...])
    low = _low_lanes(x.shape[0])

    for c in range(A_Q_W // LANES):
        q = z[:, c * LANES:(c + 1) * LANES]
        if latent:
            q = _rope(q, cosa_ref[...], sina_ref[...], A_HEAD_DIM // 4)
        qa_ref[:, c * LANES:(c + 1) * LANES] = (q * (A_SCALE * LOG2E)).astype(BF16)
    k = z[:, OFF_KA:OFF_KA + LANES]
    v = z[:, OFF_VA:OFF_VA + LANES]
    if latent:
        k = _rope(k, cosa_ref[...], sina_ref[...], A_HEAD_DIM // 4)
    else:
        _store_seq_minor(knat_ref, k)
        _store_seq_minor(vnat_ref, v)
    _store_gqa_kv(k, v, low, kap_ref, vad_ref)

    q_lat = _rms(z[:, OFF_QLAT:OFF_QLAT + Q_LORA], qn_ref[...]).astype(BF16)
    qb = _dot(q_lat, wuq_ref[...])
    ckv_n = _rms(z[:, OFF_CKV:OFF_CKV + KV_LORA], kvn_ref[...])
    krp = z[:, OFF_KROPE:OFF_KROPE + LANES]
    if latent:
        krp = _rope(krp, cosb_ref[...], sinb_ref[...], B_ROPE_DIM // 4)
    else:
        ckvn_ref[...] = ckv_n
        _store_seq_minor(krope_ref, pltpu.roll(krp, HALF, axis=1))
    ckv_b = ckv_n.astype(BF16)
    kn = _dot(ckv_b, wuk_ref[...])
    vb_ref[...] = _dot(ckv_b, wuv_ref[...]).astype(BF16)
    for h in range(B_HEADS):
        qh = qb[:, h * LANES:(h + 1) * LANES]
        if latent:
            qh = _rope(qh, cosb_ref[...], sinb_ref[...], B_ROPE_DIM // 4)
        qb_ref[:, h * LANES:(h + 1) * LANES] = (qh * (B_SCALE * LOG2E)).astype(BF16)
        kb_ref[:, h * LANES:(h + 1) * LANES] = (kn[:, h * LANES:(h + 1) * LANES] + krp).astype(BF16)


def _const_spec(shape):
    nd = len(shape)
    return pl.BlockSpec(shape, lambda i: (0,) * nd, pipeline_mode=pl.Buffered(1))


def _pre(x, mods3, w, latent, seq, rope):
    t = x.shape[0]
    tiles_per_batch = max(seq // TOKEN_TILE, 1)
    tm = TOKEN_TILE
    n_tiles = t // tm
    if latent:
        mod_row = lambda i: (1 + i // tiles_per_batch, 0, 0)
        pos_blk = lambda i: (i % tiles_per_batch, 0)
    else:
        mod_row = lambda i: (0, 0, 0)
    row_blk = lambda i: (i, 0)

    weights = [w["n1"], w["wgu1"], w["wd1"], w["nm"], w["win_a"], w["qn"], w["kvn"], w["wuq"], w["wuk"], w["wuv"]]
    in_specs = [pl.BlockSpec((tm, D_MODEL), row_blk),
                pl.BlockSpec((None, 1, N_MOD * D_MODEL), mod_row)]
    in_specs += [_const_spec(a.shape) for a in weights]
    args = [x, mods3] + weights
    if latent:
        in_specs += [pl.BlockSpec((tm, LANES), pos_blk)] * 4
        args += list(rope)

    widths = [(D_MODEL, F32), (A_Q_W, BF16), (4 * LANES, BF16), (2 * LANES, BF16),
              (B_HEADS * LANES, BF16), (B_HEADS * LANES, BF16), (B_HEADS * B_V_DIM, BF16)]
    out_specs = [pl.BlockSpec((tm, wd), row_blk) for wd, _ in widths]
    out_shape = [jax.ShapeDtypeStruct((t, wd), dt) for wd, dt in widths]
    if not latent:
        nb = tm // seq
        for feat in (A_KV_W, A_KV_W):
            out_specs.append(pl.BlockSpec((nb, feat, seq), lambda i: (i, 0, 0)))
            out_shape.append(jax.ShapeDtypeStruct((t // seq, feat, seq), F32))
        out_specs.append(pl.BlockSpec((tm, KV_LORA), row_blk))
        out_shape.append(jax.ShapeDtypeStruct((t, KV_LORA), F32))
        out_specs.append(pl.BlockSpec((nb, B_ROPE_DIM, seq), lambda i: (i, 0, 0)))
        out_shape.append(jax.ShapeDtypeStruct((t // seq, B_ROPE_DIM, seq), F32))

    return pl.pallas_call(
        functools.partial(_pre_kernel, latent),
        grid=(n_tiles,),
        in_specs=in_specs,
        out_specs=out_specs,
        out_shape=out_shape,
        compiler_params=pltpu.CompilerParams(dimension_semantics=("parallel",), vmem_limit_bytes=VMEM_LIMIT),
        name="pre_latent" if latent else "pre_context",
    )(*args)


def _cache_kernel(ckv_ref, krp_ref, ck_ref, cv_ref, wuk_ref, wuv_ref, kb_ref, vb_ref, kap_ref, vad_ref):
    c = ckv_ref[...].astype(BF16)
    kn = _dot(c, wuk_ref[...])
    krp = krp_ref[...]
    for h in range(B_HEADS):
        kb_ref[:, h * LANES:(h + 1) * LANES] = (kn[:, h * LANES:(h + 1) * LANES] + krp).astype(BF16)
    vb_ref[...] = _dot(c, wuv_ref[...]).astype(BF16)
    _store_gqa_kv(ck_ref[...], cv_ref[...], _low_lanes(ck_ref.shape[0]), kap_ref, vad_ref)


def _cache_kv(ckv, krp, ck, cv, wuk, wuv):
    b, p, _ = ckv.shape
    blk = lambda wd: pl.BlockSpec((None, p, wd), lambda i: (i, 0, 0))
    widths = [B_HEADS * LANES, B_HEADS * B_V_DIM, 4 * LANES, 2 * LANES]
    return pl.pallas_call(
        _cache_kernel,
        grid=(b,),
        in_specs=[blk(KV_LORA), blk(LANES), blk(A_KV_W), blk(A_KV_W),
                  pl.BlockSpec(wuk.shape, lambda i: (0, 0)),
                  pl.BlockSpec(wuv.shape, lambda i: (0, 0))],
        out_specs=[blk(wd) for wd in widths],
        out_shape=[jax.ShapeDtypeStruct((b, p, wd), BF16) for wd in widths],
        compiler_params=pltpu.CompilerParams(dimension_semantics=("parallel",), vmem_limit_bytes=VMEM_LIMIT),
        name="cache_kv",
    )(ckv, krp, ck, cv, wuk, wuv)


def _att_ctx_kernel(sink_ref, qa_ref, kap_ref, vad_ref, qb_ref, kb_ref, vb_ref, oa_ref, ob_ref):
    low = _low_lanes(qa_ref.shape[0])
    n_a = A_HEADS

    def scores(u):
        if u < n_a:
            c, half = divmod(u, 2)
            kc = 2 * (c // 2) + half
            return _dot_nt(qa_ref[:, c * LANES:(c + 1) * LANES], kap_ref[:, kc * LANES:(kc + 1) * LANES])
        h = u - n_a
        return _dot_nt(qb_ref[:, h * LANES:(h + 1) * LANES], kb_ref[:, h * LANES:(h + 1) * LANES])

    nxt = scores(0)
    res = []
    for u in range(n_a + B_HEADS):
        s = nxt
        if u + 1 < n_a + B_HEADS:
            nxt = scores(u + 1)
        if u < n_a:
            kv = u // 4
            res.append(_attend([s], [vad_ref[:, kv * LANES:(kv + 1) * LANES]], sink_ref[u] * LOG2E))
            out_ref, c = oa_ref, u // 2
        else:
            c = (u - n_a) // 2
            res.append(_attend([s], [vb_ref[:, c * LANES:(c + 1) * LANES]], None))
            out_ref = ob_ref
        if u % 2 == 1:
            out_ref[:, c * LANES:(c + 1) * LANES] = jnp.where(low, res[u - 1], res[u]).astype(BF16)


def _att_ctx(sink, qa, kap, vad, qb, kb, vb, seq):
    t = qa.shape[0]
    blk = lambda wd: pl.BlockSpec((seq, wd), lambda b: (b, 0))
    return pl.pallas_call(
        _att_ctx_kernel,
        grid=(t // seq,),
        in_specs=[pl.BlockSpec(memory_space=pltpu.SMEM),
                  blk(qa.shape[1]), blk(kap.shape[1]), blk(vad.shape[1]),
                  blk(qb.shape[1]), blk(kb.shape[1]), blk(vb.shape[1])],
        out_specs=[blk(A_Q_W), blk(B_HEADS * B_V_DIM)],
        out_shape=[jax.ShapeDtypeStruct((t, A_Q_W), BF16), jax.ShapeDtypeStruct((t, B_HEADS * B_V_DIM), BF16)],
        compiler_params=pltpu.CompilerParams(dimension_semantics=("parallel",), vmem_limit_bytes=VMEM_LIMIT),
        name="att_context",
    )(sink, qa, kap, vad, qb, kb, vb)


def _att_win_kernel(n_lat, sink_ref, qa_ref, kap_ref, vad_ref, kapc_ref, vadc_ref, oa_ref):
    i = pl.program_id(1)
    start = pl.multiple_of(jnp.clip(i * Q_BLOCK - Q_BLOCK, 0, n_lat - BAND), Q_BLOCK)
    low_q = _low_lanes(Q_BLOCK)

    r = lax.broadcasted_iota(jnp.int32, (2 * Q_BLOCK, BAND), 0) & (Q_BLOCK - 1)
    j = lax.broadcasted_iota(jnp.int32, (2 * Q_BLOCK, BAND), 1)
    rel = (j - r) + (start - i * Q_BLOCK)
    valid = jnp.abs(rel) <= WINDOW
    upper = lax.broadcasted_iota(jnp.int32, (2 * Q_BLOCK, 1), 0) >= Q_BLOCK

    def scores(u):
        kv, half = divmod(u, 2)
        q2 = jnp.concatenate([qa_ref[:, (2 * kv) * LANES:(2 * kv + 1) * LANES],
                              qa_ref[:, (2 * kv + 1) * LANES:(2 * kv + 2) * LANES]], axis=0)
        s_band = _dot_nt(q2, kap_ref[pl.ds(start, BAND), u * LANES:(u + 1) * LANES])
        s_ctx = _dot_nt(q2, kapc_ref[:, u * LANES:(u + 1) * LANES])
        return [jnp.where(valid, s_band, NEG), s_ctx]

    nxt = scores(0)
    res = []
    for u in range(2 * A_KV_HEADS):
        kv, half = divmod(u, 2)
        s = nxt
        if u + 1 < 2 * A_KV_HEADS:
            nxt = scores(u + 1)
        sink = jnp.where(upper, sink_ref[4 * kv + 2 + half], sink_ref[4 * kv + half]) * LOG2E
        values = [vad_ref[pl.ds(start, BAND), kv * LANES:(kv + 1) * LANES], vadc_ref[:, kv * LANES:(kv + 1) * LANES]]
        res.append(_attend(s, values, sink))
        if half == 1:
            for pr in range(2):
                c = 2 * kv + pr
                rows = slice(pr * Q_BLOCK, (pr + 1) * Q_BLOCK)
                oa_ref[:, c * LANES:(c + 1) * LANES] = jnp.where(low_q, res[u - 1][rows], res[u][rows]).astype(BF16)


def _att_win(sink, qa, kap, vad, kapc, vadc):
    b, n, _ = qa.shape
    past = kapc.shape[1]
    return pl.pallas_call(
        functools.partial(_att_win_kernel, n),
        grid=(b, n // Q_BLOCK),
        in_specs=[pl.BlockSpec(memory_space=pltpu.SMEM),
                  pl.BlockSpec((None, Q_BLOCK, A_Q_W), lambda bi, i: (bi, i, 0)),
                  pl.BlockSpec((None, n, kap.shape[2]), lambda bi, i: (bi, 0, 0)),
                  pl.BlockSpec((None, n, vad.shape[2]), lambda bi, i: (bi, 0, 0)),
                  pl.BlockSpec((None, past, kapc.shape[2]), lambda bi, i: (bi, 0, 0)),
                  pl.BlockSpec((None, past, vadc.shape[2]), lambda bi, i: (bi, 0, 0))],
        out_specs=pl.BlockSpec((None, Q_BLOCK, A_Q_W), lambda bi, i: (bi, i, 0)),
        out_shape=jax.ShapeDtypeStruct((b, n, A_Q_W), BF16),
        compiler_params=pltpu.CompilerParams(dimension_semantics=("parallel", "parallel"),
                                             vmem_limit_bytes=VMEM_LIMIT),
        name="att_window",
    )(sink, qa, kap, vad, kapc, vadc)


def _att_mla_kernel(qb_ref, kbl_ref, vbl_ref, kbc_ref, vbc_ref, ob_ref):
    low = _low_lanes(qb_ref.shape[0])

    def scores(h):
        q = qb_ref[:, h * LANES:(h + 1) * LANES]
        return [_dot_nt(q, kbc_ref[:, h * LANES:(h + 1) * LANES]),
                _dot_nt(q, kbl_ref[:, h * LANES:(h + 1) * LANES])]

    nxt = scores(0)
    res = []
    for h in range(B_HEADS):
        s = nxt
        if h + 1 < B_HEADS:
            nxt = scores(h + 1)
        c = h // 2
        res.append(_attend(s, [vbc_ref[:, c * LANES:(c + 1) * LANES], vbl_ref[:, c * LANES:(c + 1) * LANES]], None))
        if h % 2 == 1:
            ob_ref[:, c * LANES:(c + 1) * LANES] = jnp.where(low, res[h - 1], res[h]).astype(BF16)


def _att_mla(qb, kbl, vbl, kbc, vbc):
    b, n, _ = qb.shape
    past = kbc.shape[1]
    tq = MLA_Q_TILE
    return pl.pallas_call(
        _att_mla_kernel,
        grid=(b, n // tq),
        in_specs=[pl.BlockSpec((None, tq, qb.shape[2]), lambda bi, i: (bi, i, 0)),
                  pl.BlockSpec((None, n, kbl.shape[2]), lambda bi, i: (bi, 0, 0)),
                  pl.BlockSpec((None, n, vbl.shape[2]), lambda bi, i: (bi, 0, 0)),
                  pl.BlockSpec((None, past, kbc.shape[2]), lambda bi, i: (bi, 0, 0)),
                  pl.BlockSpec((None, past, vbc.shape[2]), lambda bi, i: (bi, 0, 0))],
        out_specs=pl.BlockSpec((None, tq, B_HEADS * B_V_DIM), lambda bi, i: (bi, i, 0)),
        out_shape=jax.ShapeDtypeStruct((b, n, B_HEADS * B_V_DIM), BF16),
        compiler_params=pltpu.CompilerParams(dimension_semantics=("parallel", "parallel"),
                                             vmem_limit_bytes=VMEM_LIMIT),
        name="att_mla",
    )(qb, kbl, vbl, kbc, vbc)


def _post_kernel(x1_ref, mods_ref, oa_ref, ob_ref, nm_ref, wing_ref, woa_ref, wob_ref, wout_ref,
                 n2_ref, wgu_ref, wd_ref, nf_ref, y_ref):
    x1 = x1_ref[...]
    h2 = _mod_norm(x1, nm_ref[...], mods_ref, 1).astype(BF16)
    g = _dot(h2, wing_ref[...])
    m = (jax.nn.sigmoid(g[:, :D_MODEL]) * _dot(oa_ref[...], woa_ref[...])
         + jax.nn.sigmoid(g[:, D_MODEL:]) * _dot(ob_ref[...], wob_ref[...]))
    x2 = x1 + _gate(mods_ref, 1) * _dot(m.astype(BF16), wout_ref[...])
    h3 = _mod_norm(x2, n2_ref[...], mods_ref, 2).astype(BF16)
    x3 = x2 + 0.5 * _gate(mods_ref, 2) * _ffn(h3, wgu_ref, wd_ref)
    y_ref[...] = _rms(x3, nf_ref[...])


def _post(x1, mods3, oa, ob, w, latent, tiles_per_batch):
    t = x1.shape[0]
    tm = TOKEN_TILE
    if latent:
        mod_row = lambda i: (1 + i // tiles_per_batch, 0, 0)
    else:
        mod_row = lambda i: (0, 0, 0)
    row_blk = lambda i: (i, 0)
    weights = [w["nm"], w["win_g"], w["woa"], w["wob"], w["wout"], w["n2"], w["wgu2"], w["wd2"], w["nf"]]
    in_specs = [pl.BlockSpec((tm, D_MODEL), row_blk),
                pl.BlockSpec((None, 1, N_MOD * D_MODEL), mod_row),
                pl.BlockSpec((tm, A_Q_W), row_blk),
                pl.BlockSpec((tm, B_HEADS * B_V_DIM), row_blk)]
    in_specs += [_const_spec(a.shape) for a in weights]
    return pl.pallas_call(
        _post_kernel,
        grid=(t // tm,),
        in_specs=in_specs,
        out_specs=pl.BlockSpec((tm, D_MODEL), row_blk),
        out_shape=jax.ShapeDtypeStruct((t, D_MODEL), F32),
        compiler_params=pltpu.CompilerParams(dimension_semantics=("parallel",), vmem_limit_bytes=VMEM_LIMIT),
        name="post_latent" if latent else "post_context",
    )(x1, mods3, oa, ob, *weights)


def _rope_tables(n):
    rows = n // GRID_W
    t_row = jnp.repeat(jnp.arange(rows, dtype=F32), GRID_W)
    t_col = jnp.tile(jnp.arange(GRID_W, dtype=F32), rows)

    def angles(d_rot):
        d_half = d_rot // 2
        inv = 1.0 / (ROPE_THETA ** (jnp.arange(0, d_half, 2, dtype=F32) / d_half))
        ar = t_row[:, None] * inv[None, :]
        ac = t_col[:, None] * inv[None, :]
        return jnp.concatenate([ar, ar, ac, ac], axis=-1)

    def signed(sin, d_rot):
        q = d_rot // 4
        sign = jnp.where((jnp.arange(d_rot) % (2 * q)) < q, -1.0, 1.0).astype(F32)
        return sin * sign[None, :]

    ang_a = angles(A_HEAD_DIM)
    cos_a = jnp.tile(jnp.cos(ang_a), (1, LANES // A_HEAD_DIM))
    sin_a = jnp.tile(signed(jnp.sin(ang_a), A_HEAD_DIM), (1, LANES // A_HEAD_DIM))
    ang_b = angles(B_ROPE_DIM)
    pad_l = B_NOPE_DIM
    pad_r = LANES - B_NOPE_DIM - B_ROPE_DIM
    cos_b = jnp.pad(jnp.cos(ang_b), ((0, 0), (pad_l, pad_r)), constant_values=1.0)
    sin_b = jnp.pad(signed(jnp.sin(ang_b), B_ROPE_DIM), ((0, 0), (pad_l, pad_r)))
    return cos_a, sin_a, cos_b, sin_b


def _prep_weights(ffn1_norm, ffn1_w_gu, ffn1_w_down, mix_norm, w_in, q_lat_norm, kv_lat_norm, w_uq, w_ukv,
                  w_o_a, w_o_b, w_out, ffn2_norm, ffn2_w_gu, ffn2_w_down, final_norm):
    win = w_in[0]
    n_attn = A_Q_W + 2 * A_KV_W + Q_LORA + KV_LORA
    krope_cols = jnp.pad(win[:, n_attn:n_attn + B_ROPE_DIM],
                         ((0, 0), (B_NOPE_DIM, LANES - B_NOPE_DIM - B_ROPE_DIM)))
    win_a = jnp.concatenate([win[:, :n_attn], krope_cols], axis=1).astype(BF16)
    win_g = win[:, n_attn + B_ROPE_DIM:].astype(BF16)

    wuq = jnp.pad(w_uq[0].reshape(Q_LORA, B_HEADS, B_QK_DIM),
                  ((0, 0), (0, 0), (0, LANES - B_QK_DIM))).reshape(Q_LORA, B_HEADS * LANES).astype(BF16)
    wukv = w_ukv[0].reshape(KV_LORA, B_HEADS, B_NOPE_DIM + B_V_DIM)
    wuk = jnp.pad(wukv[:, :, :B_NOPE_DIM],
                  ((0, 0), (0, 0), (0, LANES - B_NOPE_DIM))).reshape(KV_LORA, B_HEADS * LANES).astype(BF16)
    wuv = wukv[:, :, B_NOPE_DIM:].reshape(KV_LORA, B_HEADS * B_V_DIM).astype(BF16)

    return {
        "n1": ffn1_norm, "wgu1": ffn1_w_gu[0].astype(BF16), "wd1": ffn1_w_down[0].astype(BF16),
        "nm": mix_norm, "win_a": win_a, "win_g": win_g,
        "qn": q_lat_norm, "kvn": kv_lat_norm, "wuq": wuq, "wuk": wuk, "wuv": wuv,
        "woa": w_o_a[0].astype(BF16), "wob": w_o_b[0].astype(BF16), "wout": w_out[0].astype(BF16),
        "n2": ffn2_norm, "wgu2": ffn2_w_gu[0].astype(BF16), "wd2": ffn2_w_down[0].astype(BF16),
        "nf": final_norm.reshape(1, D_MODEL),
    }


def kernel(x_prompt, x_sample, cache_attn_k, cache_attn_v, cache_mla_ckv, cache_mla_krope, c, c_ctx, ada_w, ada_b, ffn1_norm, ffn1_w_gu, ffn1_w_down, mix_norm, w_in, attn_sink, q_lat_norm, kv_lat_norm, w_uq, w_ukv, w_o_a, w_o_b, w_out, ffn2_norm, ffn2_w_gu, ffn2_w_down, final_norm):
    assert ada_w.shape[0] == 1, "single trunk layer"
    bp, sp, d = x_prompt.shape
    bs, ns, _ = x_sample.shape
    past = cache_attn_k.shape[2]
    assert d == D_MODEL and bs + 1 <= MOD_ROWS
    assert TOKEN_TILE % sp == 0 and (bp * sp) % TOKEN_TILE == 0
    assert ns % TOKEN_TILE == 0 and ns % MLA_Q_TILE == 0 and ns % Q_BLOCK == 0 and ns >= BAND

    w = _prep_weights(ffn1_norm, ffn1_w_gu, ffn1_w_down, mix_norm, w_in, q_lat_norm, kv_lat_norm, w_uq, w_ukv,
                      w_o_a, w_o_b, w_out, ffn2_norm, ffn2_w_gu, ffn2_w_down, final_norm)
    sink = attn_sink[0]

    cvec = jnp.concatenate([c_ctx[None, :], c, jnp.zeros((MOD_ROWS - 1 - bs, d), F32)], axis=0)
    mods = _ada_mods(cvec, ada_w[0], ada_b)
    mods3 = mods.reshape(MOD_ROWS, 1, N_MOD * D_MODEL)

    xp = x_prompt.reshape(bp * sp, d)
    (x1p, qa, kap, vad, qb, kb, vb, k_t, v_t, ckv_n, krope_t) = _pre(xp, mods3, w, False, sp, None)
    oa, ob = _att_ctx(sink, qa, kap, vad, qb, kb, vb, sp)
    y_prompt = _post(x1p, mods3, oa, ob, w, False, 1).reshape(bp, sp, d)

    tiles_per_batch = ns // TOKEN_TILE
    xs = x_sample.reshape(bs * ns, d)
    (x1s, qa, kap, vad, qb, kb, vb) = _pre(xs, mods3, w, True, ns, _rope_tables(ns))
    krp_c = jnp.pad(cache_mla_krope[:, 0], ((0, 0), (0, 0), (B_NOPE_DIM, LANES - B_NOPE_DIM - B_ROPE_DIM)))
    kbc, vbc, kapc, vadc = _cache_kv(
        cache_mla_ckv[:, 0], krp_c,
        cache_attn_k[:, 0].reshape(bs, past, A_KV_W), cache_attn_v[:, 0].reshape(bs, past, A_KV_W),
        w["wuk"], w["wuv"])
    r3 = lambda a: a.reshape(bs, ns, a.shape[1])
    oa = _att_win(sink, r3(qa), r3(kap), r3(vad), kapc, vadc)
    ob = _att_mla(r3(qb), r3(kb), r3(vb), kbc, vbc)
    y_sample = _post(x1s, mods3, oa.reshape(bs * ns, A_Q_W), ob.reshape(bs * ns, B_HEADS * B_V_DIM),
                     w, True, tiles_per_batch).reshape(bs, ns, d)

    new_attn_k = k_t.reshape(bp, 1, A_KV_HEADS, A_HEAD_DIM, sp).transpose(0, 1, 4, 2, 3)
    new_attn_v = v_t.reshape(bp, 1, A_KV_HEADS, A_HEAD_DIM, sp).transpose(0, 1, 4, 2, 3)
    new_mla_ckv = ckv_n.reshape(bp, 1, sp, KV_LORA)
    new_mla_krope = krope_t.reshape(bp, 1, B_ROPE_DIM, sp).transpose(0, 1, 3, 2)
    return (y_prompt, y_sample, new_attn_k, new_attn_v, new_mla_ckv, new_mla_krope)
```

```python
import functools

import jax
import jax.numpy as jnp
from jax import lax
from jax.experimental import pallas as pl
from jax.experimental.pallas import tpu as pltpu

F32 = jnp.float32
BF16 = jnp.bfloat16

D_MODEL = 1024
N_MOD = 9
GRID_W = 64
WINDOW = 128
A_HEADS = 8
A_KV_HEADS = 2
A_HEAD_DIM = 64
A_Q_W = A_HEADS * A_HEAD_DIM
A_KV_W = A_KV_HEADS * A_HEAD_DIM
B_HEADS = 8
B_NOPE_DIM = 64
B_ROPE_DIM = 32
B_V_DIM = 64
B_QK_DIM = B_NOPE_DIM + B_ROPE_DIM
Q_LORA = 256
KV_LORA = 256
D_FF = 2816
ROPE_THETA = 10000.0
EPS = 1e-6
NEG = -1e30
A_SCALE = A_HEAD_DIM ** -0.5
B_SCALE = B_QK_DIM ** -0.5
LOG2E = 1.4426950408889634

LANES = 128
HALF = LANES // 2
FF_CHUNK = 256
N_FF_CHUNKS = D_FF // FF_CHUNK
TOKEN_TILE = 512
ADA_TILE = 1024
Q_BLOCK = 128
BAND = 3 * Q_BLOCK
SUM_ROWS = 16
MLA_Q_TILE = 512
MLA_KEY_BLOCK = 512
MOD_ROWS = 8
VMEM_LIMIT = 56 * 1024 * 1024

IN_A_W = A_Q_W + 2 * A_KV_W + Q_LORA + KV_LORA + LANES
OFF_KA = A_Q_W
OFF_VA = OFF_KA + A_KV_W
OFF_QLAT = OFF_VA + A_KV_W
OFF_CKV = OFF_QLAT + Q_LORA
OFF_KROPE = OFF_CKV + KV_LORA


def _dot(a, b):
    return jnp.dot(a, b, preferred_element_type=F32)


def _dot_nt(a, b):
    return lax.dot_general(a, b, (((1,), (1,)), ((), ())), preferred_element_type=F32)


def _rms(x, g):
    ms = jnp.mean(x * x, axis=-1, keepdims=True)
    return x * lax.rsqrt(ms + EPS) * g


def _mod_norm(x, g, mods_ref, k):
    shift = mods_ref[:, (3 * k) * D_MODEL:(3 * k + 1) * D_MODEL]
    scale = mods_ref[:, (3 * k + 1) * D_MODEL:(3 * k + 2) * D_MODEL]
    return _rms(x, g) * (1.0 + scale) + shift


def _gate(mods_ref, k):
    return mods_ref[:, (3 * k + 2) * D_MODEL:(3 * k + 3) * D_MODEL]


def _ffn(h, wgu_ref, wd_ref):
    def gate_up(c):
        a = _dot(h, wgu_ref[:, c * FF_CHUNK:(c + 1) * FF_CHUNK])
        u = _dot(h, wgu_ref[:, D_FF + c * FF_CHUNK:D_FF + (c + 1) * FF_CHUNK])
        return a, u

    acc = None
    nxt = gate_up(0)
    for c in range(N_FF_CHUNKS):
        a, u = nxt
        if c + 1 < N_FF_CHUNKS:
            nxt = gate_up(c + 1)
        act = (a * jax.nn.sigmoid(a) * u).astype(BF16)
        d = _dot(act, wd_ref[c * FF_CHUNK:(c + 1) * FF_CHUNK, :])
        acc = d if acc is None else acc + d
    return acc


def _rope(x, cos, sin_signed, dist):
    lane = lax.broadcasted_iota(jnp.int32, x.shape, 1)
    first = (lane & (2 * dist - 1)) < dist
    partner = jnp.where(first, pltpu.roll(x, LANES - dist, axis=1), pltpu.roll(x, dist, axis=1))
    return x * cos + partner * sin_signed


def _store_gqa_kv(k, v, low, kap_ref, vad_ref):
    k_sw = pltpu.roll(k, HALF, axis=1)
    v_sw = pltpu.roll(v, HALF, axis=1)
    zero = jnp.zeros_like(k)
    kap_ref[:, 0 * LANES:1 * LANES] = jnp.where(low, k, zero).astype(BF16)
    kap_ref[:, 1 * LANES:2 * LANES] = jnp.where(low, zero, k_sw).astype(BF16)
    kap_ref[:, 2 * LANES:3 * LANES] = jnp.where(low, k_sw, zero).astype(BF16)
    kap_ref[:, 3 * LANES:4 * LANES] = jnp.where(low, zero, k).astype(BF16)
    vad_ref[:, 0 * LANES:1 * LANES] = jnp.where(low, v, v_sw).astype(BF16)
    vad_ref[:, 1 * LANES:2 * LANES] = jnp.where(low, v_sw, v).astype(BF16)


def _store_seq_minor(out_ref, val):
    nb, feat, seq = out_ref.shape
    for bi in range(nb):
        out_ref[bi] = val[bi * seq:(bi + 1) * seq, :].T[:feat, :]


def _attend(scores, values, sink):
    m = None
    for s in scores:
        mx = jnp.max(s, axis=-1, keepdims=True)
        m = mx if m is None else jnp.maximum(m, mx)
    if sink is not None:
        m = jnp.maximum(m, sink)
    l = None
    r = None
    for s, v in zip(scores, values):
        p = jnp.exp2(s - m)
        ls = jnp.sum(p, axis=-1, keepdims=True)
        pv = _dot(p.astype(BF16), v)
        l = ls if l is None else l + ls
        r = pv if r is None else r + pv
    if sink is not None:
        l = l + jnp.exp2(sink - m)
    return r / l


def _low_lanes(rows):
    return lax.broadcasted_iota(jnp.int32, (rows, LANES), 1) < HALF


def _ada_kernel(c_ref, w_ref, b_ref, o_ref):
    c = c_ref[...]
    s = (c * jax.nn.sigmoid(c)).astype(BF16)
    o_ref[...] = _dot(s, w_ref[...].astype(BF16)) + b_ref[...]


def _ada_mods(cvec, ada_w, ada_b):
    n = ada_w.shape[1]
    return pl.pallas_call(
        _ada_kernel,
        grid=(n // ADA_TILE,),
        in_specs=[
            pl.BlockSpec((MOD_ROWS, D_MODEL), lambda j: (0, 0)),
            pl.BlockSpec((D_MODEL, ADA_TILE), lambda j: (0, j)),
            pl.BlockSpec((1, ADA_TILE), lambda j: (0, j)),
        ],
        out_specs=pl.BlockSpec((MOD_ROWS, ADA_TILE), lambda j: (0, j)),
        out_shape=jax.ShapeDtypeStruct((MOD_ROWS, n), F32),
        compiler_params=pltpu.CompilerParams(dimension_semantics=("parallel",), vmem_limit_bytes=VMEM_LIMIT),
        name="ada_mods",
    )(cvec, ada_w, ada_b)


def _pre_kernel(latent, *refs):
    if latent:
        (x_ref, mods_ref, n1_ref, wgu_ref, wd_ref, nm_ref, win_ref, qn_ref, kvn_ref, wuq_ref, wuk_ref, wuv_ref,
         cosa_ref, sina_ref, cosb_ref, sinb_ref,
         x1_ref, qa_ref, kap_ref, vad_ref, qb_ref, kb_ref, vb_ref) = refs
    else:
        (x_ref, mods_ref, n1_ref, wgu_ref, wd_ref, nm_ref, win_ref, qn_ref, kvn_ref, wuq_ref, wuk_ref, wuv_ref,
         x1_ref, qa_ref, kap_ref, vad_ref, qb_ref, kb_ref, vb_ref,
         knat_ref, vnat_ref, ckvn_ref, krope_ref) = refs

    x = x_ref[...]
    h1 = _mod_norm(x, n1_ref[...], mods_ref, 0).astype(BF16)
    x1 = x + 0.5 * _gate(mods_ref, 0) * _ffn(h1, wgu_ref, wd_ref)
    x1_ref[...] = x1

    h2 = _mod_norm(x1, nm_ref[...], mods_ref, 1).astype(BF16)
    z = _dot(h2, win_ref[...])
    low = _low_lanes(x.shape[0])

    for c in range(A_Q_W // LANES):
        q = z[:, c * LANES:(c + 1) * LANES]
        if latent:
            q = _rope(q, cosa_ref[...], sina_ref[...], A_HEAD_DIM // 4)
        qa_ref[:, c * LANES:(c + 1) * LANES] = (q * (A_SCALE * LOG2E)).astype(BF16)
    k = z[:, OFF_KA:OFF_KA + LANES]
    v = z[:, OFF_VA:OFF_VA + LANES]
    if latent:
        k = _rope(k, cosa_ref[...], sina_ref[...], A_HEAD_DIM // 4)
    else:
        _store_seq_minor(knat_ref, k)
        _store_seq_minor(vnat_ref, v)
    _store_gqa_kv(k, v, low, kap_ref, vad_ref)

    q_lat = _rms(z[:, OFF_QLAT:OFF_QLAT + Q_LORA], qn_ref[...]).astype(BF16)
    qb = _dot(q_lat, wuq_ref[...])
    ckv_n = _rms(z[:, OFF_CKV:OFF_CKV + KV_LORA], kvn_ref[...])
    krp = z[:, OFF_KROPE:OFF_KROPE + LANES]
    if latent:
        krp = _rope(krp, cosb_ref[...], sinb_ref[...], B_ROPE_DIM // 4)
    else:
        ckvn_ref[...] = ckv_n
        _store_seq_minor(krope_ref, pltpu.roll(krp, HALF, axis=1))
    ckv_b = ckv_n.astype(BF16)
    kn = _dot(ckv_b, wuk_ref[...])
    if latent:
        vb_ref[...] = _dot_nt(wuv_ref[...], ckv_b).astype(BF16)
    else:
        vb_ref[...] = _dot(ckv_b, wuv_ref[...]).astype(BF16)
    for h in range(B_HEADS):
        qh = qb[:, h * LANES:(h + 1) * LANES]
        if latent:
            qh = _rope(qh, cosb_ref[...], sinb_ref[...], B_ROPE_DIM // 4)
        qb_ref[:, h * LANES:(h + 1) * LANES] = (qh * (B_SCALE * LOG2E)).astype(BF16)
        kb_ref[:, h * LANES:(h + 1) * LANES] = (kn[:, h * LANES:(h + 1) * LANES] + krp).astype(BF16)


def _const_spec(shape):
    nd = len(shape)
    return pl.BlockSpec(shape, lambda i: (0,) * nd, pipeline_mode=pl.Buffered(1))


def _pre(x, mods3, w, latent, seq, rope):
    t = x.shape[0]
    tiles_per_batch = max(seq // TOKEN_TILE, 1)
    tm = TOKEN_TILE
    n_tiles = t // tm
    if latent:
        mod_row = lambda i: (1 + i // tiles_per_batch, 0, 0)
        pos_blk = lambda i: (i % tiles_per_batch, 0)
    else:
        mod_row = lambda i: (0, 0, 0)
    row_blk = lambda i: (i, 0)

    weights = [w["n1"], w["wgu1"], w["wd1"], w["nm"], w["win_a"], w["qn"], w["kvn"], w["wuq"], w["wuk"],
               w["wuv_t"] if latent else w["wuv"]]
    in_specs = [pl.BlockSpec((tm, D_MODEL), row_blk),
                pl.BlockSpec((None, 1, N_MOD * D_MODEL), mod_row)]
    in_specs += [_const_spec(a.shape) for a in weights]
    args = [x, mods3] + weights
    if latent:
        in_specs += [pl.BlockSpec((tm, LANES), pos_blk)] * 4
        args += list(rope)

    widths = [(D_MODEL, F32), (A_Q_W, BF16), (4 * LANES, BF16), (2 * LANES, BF16),
              (B_HEADS * LANES, BF16), (B_HEADS * LANES, BF16)]
    out_specs = [pl.BlockSpec((tm, wd), row_blk) for wd, _ in widths]
    out_shape = [jax.ShapeDtypeStruct((t, wd), dt) for wd, dt in widths]
    if latent:
        out_specs.append(pl.BlockSpec((B_HEADS * B_V_DIM, tm), lambda i: (0, i)))
        out_shape.append(jax.ShapeDtypeStruct((B_HEADS * B_V_DIM, t), BF16))
    else:
        out_specs.append(pl.BlockSpec((tm, B_HEADS * B_V_DIM), row_blk))
        out_shape.append(jax.ShapeDtypeStruct((t, B_HEADS * B_V_DIM), BF16))
    if not latent:
        nb = tm // seq
        for feat in (A_KV_W, A_KV_W):
            out_specs.append(pl.BlockSpec((nb, feat, seq), lambda i: (i, 0, 0)))
            out_shape.append(jax.ShapeDtypeStruct((t // seq, feat, seq), F32))
        out_specs.append(pl.BlockSpec((tm, KV_LORA), row_blk))
        out_shape.append(jax.ShapeDtypeStruct((t, KV_LORA), F32))
        out_specs.append(pl.BlockSpec((nb, B_ROPE_DIM, seq), lambda i: (i, 0, 0)))
        out_shape.append(jax.ShapeDtypeStruct((t // seq, B_ROPE_DIM, seq), F32))

    return pl.pallas_call(
        functools.partial(_pre_kernel, latent),
        grid=(n_tiles,),
        in_specs=in_specs,
        out_specs=out_specs,
        out_shape=out_shape,
        compiler_params=pltpu.CompilerParams(dimension_semantics=("parallel",), vmem_limit_bytes=VMEM_LIMIT),
        name="pre_latent" if latent else "pre_context",
    )(*args)


def _cache_kernel(ckv_ref, krp_ref, ck_ref, cv_ref, wuk_ref, wuv_ref, kb_ref, vb_ref, kap_ref, vad_ref):
    c = ckv_ref[...].astype(BF16)
    kn = _dot(c, wuk_ref[...])
    krp = krp_ref[...]
    for h in range(B_HEADS):
        kb_ref[:, h * LANES:(h + 1) * LANES] = (kn[:, h * LANES:(h + 1) * LANES] + krp).astype(BF16)
    vb_ref[...] = _dot_nt(wuv_ref[...], c).astype(BF16)
    _store_gqa_kv(ck_ref[...], cv_ref[...], _low_lanes(ck_ref.shape[0]), kap_ref, vad_ref)


def _cache_kv(ckv, krp, ck, cv, wuk, wuv_t):
    b, p, _ = ckv.shape
    blk = lambda wd: pl.BlockSpec((None, p, wd), lambda i: (i, 0, 0))
    vdim = B_HEADS * B_V_DIM
    return pl.pallas_call(
        _cache_kernel,
        grid=(b,),
        in_specs=[blk(KV_LORA), blk(LANES), blk(A_KV_W), blk(A_KV_W),
                  pl.BlockSpec(wuk.shape, lambda i: (0, 0)),
                  pl.BlockSpec(wuv_t.shape, lambda i: (0, 0))],
        out_specs=[blk(B_HEADS * LANES), pl.BlockSpec((vdim, p), lambda i: (0, i)), blk(4 * LANES), blk(2 * LANES)],
        out_shape=[jax.ShapeDtypeStruct((b, p, B_HEADS * LANES), BF16),
                   jax.ShapeDtypeStruct((vdim, b * p), BF16),
                   jax.ShapeDtypeStruct((b, p, 4 * LANES), BF16),
                   jax.ShapeDtypeStruct((b, p, 2 * LANES), BF16)],
        compiler_params=pltpu.CompilerParams(dimension_semantics=("parallel",), vmem_limit_bytes=VMEM_LIMIT),
        name="cache_kv",
    )(ckv, krp, ck, cv, wuk, wuv_t)


def _att_ctx_kernel(sink_ref, qa_ref, kap_ref, vad_ref, qb_ref, kb_ref, vb_ref, oa_ref, ob_ref):
    low = _low_lanes(qa_ref.shape[0])
    n_a = A_HEADS

    def scores(u):
        if u < n_a:
            c, half = divmod(u, 2)
            kc = 2 * (c // 2) + half
            return _dot_nt(qa_ref[:, c * LANES:(c + 1) * LANES], kap_ref[:, kc * LANES:(kc + 1) * LANES])
        h = u - n_a
        return _dot_nt(qb_ref[:, h * LANES:(h + 1) * LANES], kb_ref[:, h * LANES:(h + 1) * LANES])

    nxt = scores(0)
    res = []
    for u in range(n_a + B_HEADS):
        s = nxt
        if u + 1 < n_a + B_HEADS:
            nxt = scores(u + 1)
        if u < n_a:
            kv = u // 4
            res.append(_attend([s], [vad_ref[:, kv * LANES:(kv + 1) * LANES]], sink_ref[u] * LOG2E))
            out_ref, c = oa_ref, u // 2
        else:
            c = (u - n_a) // 2
            res.append(_attend([s], [vb_ref[:, c * LANES:(c + 1) * LANES]], None))
            out_ref = ob_ref
        if u % 2 == 1:
            out_ref[:, c * LANES:(c + 1) * LANES] = jnp.where(low, res[u - 1], res[u]).astype(BF16)


def _att_ctx(sink, qa, kap, vad, qb, kb, vb, seq):
    t = qa.shape[0]
    blk = lambda wd: pl.BlockSpec((seq, wd), lambda b: (b, 0))
    return pl.pallas_call(
        _att_ctx_kernel,
        grid=(t // seq,),
        in_specs=[pl.BlockSpec(memory_space=pltpu.SMEM),
                  blk(qa.shape[1]), blk(kap.shape[1]), blk(vad.shape[1]),
                  blk(qb.shape[1]), blk(kb.shape[1]), blk(vb.shape[1])],
        out_specs=[blk(A_Q_W), blk(B_HEADS * B_V_DIM)],
        out_shape=[jax.ShapeDtypeStruct((t, A_Q_W), BF16), jax.ShapeDtypeStruct((t, B_HEADS * B_V_DIM), BF16)],
        compiler_params=pltpu.CompilerParams(dimension_semantics=("parallel",), vmem_limit_bytes=VMEM_LIMIT),
        name="att_context",
    )(sink, qa, kap, vad, qb, kb, vb)


def _att_win_kernel(n_lat, sink_ref, qa_ref, kap_ref, vad_ref, kapc_ref, vadc_ref, oa_ref):
    i = pl.program_id(1)
    start = pl.multiple_of(jnp.clip(i * Q_BLOCK - Q_BLOCK, 0, n_lat - BAND), Q_BLOCK)
    low_q = _low_lanes(Q_BLOCK)

    r = lax.broadcasted_iota(jnp.int32, (2 * Q_BLOCK, BAND), 0) & (Q_BLOCK - 1)
    j = lax.broadcasted_iota(jnp.int32, (2 * Q_BLOCK, BAND), 1)
    rel = (j - r) + (start - i * Q_BLOCK)
    valid = jnp.abs(rel) <= WINDOW
    upper = lax.broadcasted_iota(jnp.int32, (2 * Q_BLOCK, 1), 0) >= Q_BLOCK

    def scores(u):
        kv, half = divmod(u, 2)
        q2 = jnp.concatenate([qa_ref[:, (2 * kv) * LANES:(2 * kv + 1) * LANES],
                              qa_ref[:, (2 * kv + 1) * LANES:(2 * kv + 2) * LANES]], axis=0)
        s_band = _dot_nt(q2, kap_ref[pl.ds(start, BAND), u * LANES:(u + 1) * LANES])
        s_ctx = _dot_nt(q2, kapc_ref[:, u * LANES:(u + 1) * LANES])
        return [jnp.where(valid, s_band, NEG), s_ctx]

    nxt = scores(0)
    res = []
    for u in range(2 * A_KV_HEADS):
        kv, half = divmod(u, 2)
        s = nxt
        if u + 1 < 2 * A_KV_HEADS:
            nxt = scores(u + 1)
        sink = jnp.where(upper, sink_ref[4 * kv + 2 + half], sink_ref[4 * kv + half]) * LOG2E
        values = [vad_ref[pl.ds(start, BAND), kv * LANES:(kv + 1) * LANES], vadc_ref[:, kv * LANES:(kv + 1) * LANES]]
        res.append(_attend(s, values, sink))
        if half == 1:
            for pr in range(2):
                c = 2 * kv + pr
                rows = slice(pr * Q_BLOCK, (pr + 1) * Q_BLOCK)
                oa_ref[:, c * LANES:(c + 1) * LANES] = jnp.where(low_q, res[u - 1][rows], res[u][rows]).astype(BF16)


def _att_win(sink, qa, kap, vad, kapc, vadc):
    b, n, _ = qa.shape
    past = kapc.shape[1]
    return pl.pallas_call(
        functools.partial(_att_win_kernel, n),
        grid=(b, n // Q_BLOCK),
        in_specs=[pl.BlockSpec(memory_space=pltpu.SMEM),
                  pl.BlockSpec((None, Q_BLOCK, A_Q_W), lambda bi, i: (bi, i, 0)),
                  pl.BlockSpec((None, n, kap.shape[2]), lambda bi, i: (bi, 0, 0)),
                  pl.BlockSpec((None, n, vad.shape[2]), lambda bi, i: (bi, 0, 0)),
                  pl.BlockSpec((None, past, kapc.shape[2]), lambda bi, i: (bi, 0, 0)),
                  pl.BlockSpec((None, past, vadc.shape[2]), lambda bi, i: (bi, 0, 0))],
        out_specs=pl.BlockSpec((None, Q_BLOCK, A_Q_W), lambda bi, i: (bi, i, 0)),
        out_shape=jax.ShapeDtypeStruct((b, n, A_Q_W), BF16),
        compiler_params=pltpu.CompilerParams(dimension_semantics=("parallel", "parallel"),
                                             vmem_limit_bytes=VMEM_LIMIT),
        name="att_window",
    )(sink, qa, kap, vad, kapc, vadc)


def _probs_keys_major(scores_t):
    m = None
    for s in scores_t:
        mx = jnp.max(s, axis=0, keepdims=True)
        m = mx if m is None else jnp.maximum(m, mx)
    return [jnp.exp2(s - m).astype(BF16) for s in scores_t]


def _att_mla_kernel(qb_ref, kbl_ref, vtl_ref, kbc_ref, vtc_ref, ob_ref):
    kb = MLA_KEY_BLOCK
    ones = jnp.ones((SUM_ROWS, kb), BF16)
    blocks = [(kbc_ref, vtc_ref, j) for j in range(kbc_ref.shape[0] // kb)]
    blocks += [(kbl_ref, vtl_ref, j) for j in range(kbl_ref.shape[0] // kb)]

    def score_block(h, blk):
        k_ref, _, j = blk
        return _dot_nt(k_ref[j * kb:(j + 1) * kb, h * LANES:(h + 1) * LANES], qb_ref[:, h * LANES:(h + 1) * LANES])

    def value_block(h, blk, p):
        _, v_ref, j = blk
        vt = jnp.concatenate([v_ref[h * B_V_DIM:(h + 1) * B_V_DIM, j * kb:(j + 1) * kb], ones], axis=0)
        return _dot(vt, p)

    scores, probs, outs = {}, {}, {}
    for t in range(B_HEADS + 2):
        acc = None
        new_scores = []
        for bi, blk in enumerate(blocks):
            if t < B_HEADS:
                new_scores.append(score_block(t, blk))
            if 0 <= t - 2 < B_HEADS:
                r = value_block(t - 2, blk, probs[t - 2][bi])
                acc = r if acc is None else acc + r
        if t < B_HEADS:
            scores[t] = new_scores
        if 0 <= t - 1 < B_HEADS:
            probs[t - 1] = _probs_keys_major(scores.pop(t - 1))
        if acc is not None:
            h = t - 2
            probs.pop(h)
            outs[h] = acc[:B_V_DIM] / acc[B_V_DIM:B_V_DIM + 1]
            if h % 2 == 1:
                c = h // 2
                pair = jnp.concatenate([outs.pop(h - 1), outs.pop(h)], axis=0)
                ob_ref[:, c * LANES:(c + 1) * LANES] = pair.T.astype(BF16)


def _att_mla(qb, kbl, vtl, kbc, vtc):
    b, n, _ = qb.shape
    past = kbc.shape[1]
    tq = MLA_Q_TILE
    vdim = B_HEADS * B_V_DIM
    return pl.pallas_call(
        _att_mla_kernel,
        grid=(b, n // tq),
        in_specs=[pl.BlockSpec((None, tq, qb.shape[2]), lambda bi, i: (bi, i, 0)),
                  pl.BlockSpec((None, n, kbl.shape[2]), lambda bi, i: (bi, 0, 0)),
                  pl.BlockSpec((vdim, n), lambda bi, i: (0, bi)),
                  pl.BlockSpec((None, past, kbc.shape[2]), lambda bi, i: (bi, 0, 0)),
                  pl.BlockSpec((vdim, past), lambda bi, i: (0, bi))],
        out_specs=pl.BlockSpec((None, tq, vdim), lambda bi, i: (bi, i, 0)),
        out_shape=jax.ShapeDtypeStruct((b, n, vdim), BF16),
        compiler_params=pltpu.CompilerParams(dimension_semantics=("parallel", "parallel"),
                                             vmem_limit_bytes=VMEM_LIMIT),
        name="att_mla",
    )(qb, kbl, vtl, kbc, vtc)


def _post_kernel(x1_ref, mods_ref, oa_ref, ob_ref, nm_ref, wing_ref, woa_ref, wob_ref, wout_ref,
                 n2_ref, wgu_ref, wd_ref, nf_ref, y_ref):
    x1 = x1_ref[...]
    h2 = _mod_norm(x1, nm_ref[...], mods_ref, 1).astype(BF16)
    g = _dot(h2, wing_ref[...])
    m = (jax.nn.sigmoid(g[:, :D_MODEL]) * _dot(oa_ref[...], woa_ref[...])
         + jax.nn.sigmoid(g[:, D_MODEL:]) * _dot(ob_ref[...], wob_ref[...]))
    x2 = x1 + _gate(mods_ref, 1) * _dot(m.astype(BF16), wout_ref[...])
    h3 = _mod_norm(x2, n2_ref[...], mods_ref, 2).astype(BF16)
    x3 = x2 + 0.5 * _gate(mods_ref, 2) * _ffn(h3, wgu_ref, wd_ref)
    y_ref[...] = _rms(x3, nf_ref[...])


def _post(x1, mods3, oa, ob, w, latent, tiles_per_batch):
    t = x1.shape[0]
    tm = TOKEN_TILE
    if latent:
        mod_row = lambda i: (1 + i // tiles_per_batch, 0, 0)
    else:
        mod_row = lambda i: (0, 0, 0)
    row_blk = lambda i: (i, 0)
    weights = [w["nm"], w["win_g"], w["woa"], w["wob"], w["wout"], w["n2"], w["wgu2"], w["wd2"], w["nf"]]
    in_specs = [pl.BlockSpec((tm, D_MODEL), row_blk),
                pl.BlockSpec((None, 1, N_MOD * D_MODEL), mod_row),
                pl.BlockSpec((tm, A_Q_W), row_blk),
                pl.BlockSpec((tm, B_HEADS * B_V_DIM), row_blk)]
    in_specs += [_const_spec(a.shape) for a in weights]
    return pl.pallas_call(
        _post_kernel,
        grid=(t // tm,),
        in_specs=in_specs,
        out_specs=pl.BlockSpec((tm, D_MODEL), row_blk),
        out_shape=jax.ShapeDtypeStruct((t, D_MODEL), F32),
        compiler_params=pltpu.CompilerParams(dimension_semantics=("parallel",), vmem_limit_bytes=VMEM_LIMIT),
        name="post_latent" if latent else "post_context",
    )(x1, mods3, oa, ob, *weights)


def _rope_tables(n):
    rows = n // GRID_W
    t_row = jnp.repeat(jnp.arange(rows, dtype=F32), GRID_W)
    t_col = jnp.tile(jnp.arange(GRID_W, dtype=F32), rows)

    def angles(d_rot):
        d_half = d_rot // 2
        inv = 1.0 / (ROPE_THETA ** (jnp.arange(0, d_half, 2, dtype=F32) / d_half))
        ar = t_row[:, None] * inv[None, :]
        ac = t_col[:, None] * inv[None, :]
        return jnp.concatenate([ar, ar, ac, ac], axis=-1)

    def signed(sin, d_rot):
        q = d_rot // 4
        sign = jnp.where((jnp.arange(d_rot) % (2 * q)) < q, -1.0, 1.0).astype(F32)
        return sin * sign[None, :]

    ang_a = angles(A_HEAD_DIM)
    cos_a = jnp.tile(jnp.cos(ang_a), (1, LANES // A_HEAD_DIM))
    sin_a = jnp.tile(signed(jnp.sin(ang_a), A_HEAD_DIM), (1, LANES // A_HEAD_DIM))
    ang_b = angles(B_ROPE_DIM)
    pad_l = B_NOPE_DIM
    pad_r = LANES - B_NOPE_DIM - B_ROPE_DIM
    cos_b = jnp.pad(jnp.cos(ang_b), ((0, 0), (pad_l, pad_r)), constant_values=1.0)
    sin_b = jnp.pad(signed(jnp.sin(ang_b), B_ROPE_DIM), ((0, 0), (pad_l, pad_r)))
    return cos_a, sin_a, cos_b, sin_b


def _prep_weights(ffn1_norm, ffn1_w_gu, ffn1_w_down, mix_norm, w_in, q_lat_norm, kv_lat_norm, w_uq, w_ukv,
                  w_o_a, w_o_b, w_out, ffn2_norm, ffn2_w_gu, ffn2_w_down, final_norm):
    win = w_in[0]
    n_attn = A_Q_W + 2 * A_KV_W + Q_LORA + KV_LORA
    krope_cols = jnp.pad(win[:, n_attn:n_attn + B_ROPE_DIM],
                         ((0, 0), (B_NOPE_DIM, LANES - B_NOPE_DIM - B_ROPE_DIM)))
    win_a = jnp.concatenate([win[:, :n_attn], krope_cols], axis=1).astype(BF16)
    win_g = win[:, n_attn + B_ROPE_DIM:].astype(BF16)

    wuq = jnp.pad(w_uq[0].reshape(Q_LORA, B_HEADS, B_QK_DIM),
                  ((0, 0), (0, 0), (0, LANES - B_QK_DIM))).reshape(Q_LORA, B_HEADS * LANES).astype(BF16)
    wukv = w_ukv[0].reshape(KV_LORA, B_HEADS, B_NOPE_DIM + B_V_DIM)
    wuk = jnp.pad(wukv[:, :, :B_NOPE_DIM],
                  ((0, 0), (0, 0), (0, LANES - B_NOPE_DIM))).reshape(KV_LORA, B_HEADS * LANES).astype(BF16)
    wuv = wukv[:, :, B_NOPE_DIM:].reshape(KV_LORA, B_HEADS * B_V_DIM).astype(BF16)

    return {
        "n1": ffn1_norm, "wgu1": ffn1_w_gu[0].astype(BF16), "wd1": ffn1_w_down[0].astype(BF16),
        "nm": mix_norm, "win_a": win_a, "win_g": win_g,
        "qn": q_lat_norm, "kvn": kv_lat_norm, "wuq": wuq, "wuk": wuk, "wuv": wuv, "wuv_t": wuv.T,
        "woa": w_o_a[0].astype(BF16), "wob": w_o_b[0].astype(BF16), "wout": w_out[0].astype(BF16),
        "n2": ffn2_norm, "wgu2": ffn2_w_gu[0].astype(BF16), "wd2": ffn2_w_down[0].astype(BF16),
        "nf": final_norm.reshape(1, D_MODEL),
    }


def kernel(x_prompt, x_sample, cache_attn_k, cache_attn_v, cache_mla_ckv, cache_mla_krope, c, c_ctx, ada_w, ada_b, ffn1_norm, ffn1_w_gu, ffn1_w_down, mix_norm, w_in, attn_sink, q_lat_norm, kv_lat_norm, w_uq, w_ukv, w_o_a, w_o_b, w_out, ffn2_norm, ffn2_w_gu, ffn2_w_down, final_norm):
    assert ada_w.shape[0] == 1, "single trunk layer"
    bp, sp, d = x_prompt.shape
    bs, ns, _ = x_sample.shape
    past = cache_attn_k.shape[2]
    assert d == D_MODEL and bs + 1 <= MOD_ROWS
    assert TOKEN_TILE % sp == 0 and (bp * sp) % TOKEN_TILE == 0
    assert ns % TOKEN_TILE == 0 and ns % MLA_Q_TILE == 0 and ns % Q_BLOCK == 0 and ns >= BAND

    w = _prep_weights(ffn1_norm, ffn1_w_gu, ffn1_w_down, mix_norm, w_in, q_lat_norm, kv_lat_norm, w_uq, w_ukv,
                      w_o_a, w_o_b, w_out, ffn2_norm, ffn2_w_gu, ffn2_w_down, final_norm)
    sink = attn_sink[0]

    cvec = jnp.concatenate([c_ctx[None, :], c, jnp.zeros((MOD_ROWS - 1 - bs, d), F32)], axis=0)
    mods = _ada_mods(cvec, ada_w[0], ada_b)
    mods3 = mods.reshape(MOD_ROWS, 1, N_MOD * D_MODEL)

    xp = x_prompt.reshape(bp * sp, d)
    (x1p, qa, kap, vad, qb, kb, vb, k_t, v_t, ckv_n, krope_t) = _pre(xp, mods3, w, False, sp, None)
    oa, ob = _att_ctx(sink, qa, kap, vad, qb, kb, vb, sp)
    y_prompt = _post(x1p, mods3, oa, ob, w, False, 1).reshape(bp, sp, d)

    tiles_per_batch = ns // TOKEN_TILE
    xs = x_sample.reshape(bs * ns, d)
    (x1s, qa, kap, vad, qb, kb, vb_t) = _pre(xs, mods3, w, True, ns, _rope_tables(ns))
    krp_c = jnp.pad(cache_mla_krope[:, 0], ((0, 0), (0, 0), (B_NOPE_DIM, LANES - B_NOPE_DIM - B_ROPE_DIM)))
    kbc, vbc, kapc, vadc = _cache_kv(
        cache_mla_ckv[:, 0], krp_c,
        cache_attn_k[:, 0].reshape(bs, past, A_KV_W), cache_attn_v[:, 0].reshape(bs, past, A_KV_W),
        w["wuk"], w["wuv_t"])
    r3 = lambda a: a.reshape(bs, ns, a.shape[1])
    oa = _att_win(sink, r3(qa), r3(kap), r3(vad), kapc, vadc)
    ob = _att_mla(r3(qb), r3(kb), vb_t, kbc, vbc)
    y_sample = _post(x1s, mods3, oa.reshape(bs * ns, A_Q_W), ob.reshape(bs * ns, B_HEADS * B_V_DIM),
                     w, True, tiles_per_batch).reshape(bs, ns, d)

    new_attn_k = k_t.reshape(bp, 1, A_KV_HEADS, A_HEAD_DIM, sp).transpose(0, 1, 4, 2, 3)
    new_attn_v = v_t.reshape(bp, 1, A_KV_HEADS, A_HEAD_DIM, sp).transpose(0, 1, 4, 2, 3)
    new_mla_ckv = ckv_n.reshape(bp, 1, sp, KV_LORA)
    new_mla_krope = krope_t.reshape(bp, 1, B_ROPE_DIM, sp).transpose(0, 1, 3, 2)
    return (y_prompt, y_sample, new_attn_k, new_attn_v, new_mla_ckv, new_mla_krope)
```

```python
import functools

import jax
import jax.numpy as jnp
from jax import lax
from jax.experimental import pallas as pl
from jax.experimental.pallas import tpu as pltpu

F32 = jnp.float32
BF16 = jnp.bfloat16

D_MODEL = 1024
N_MOD = 9
GRID_W = 64
WINDOW = 128
A_HEADS = 8
A_KV_HEADS = 2
A_HEAD_DIM = 64
A_Q_W = A_HEADS * A_HEAD_DIM
A_KV_W = A_KV_HEADS * A_HEAD_DIM
B_HEADS = 8
B_NOPE_DIM = 64
B_ROPE_DIM = 32
B_V_DIM = 64
B_QK_DIM = B_NOPE_DIM + B_ROPE_DIM
Q_LORA = 256
KV_LORA = 256
D_FF = 2816
ROPE_THETA = 10000.0
EPS = 1e-6
NEG = -1e30
A_SCALE = A_HEAD_DIM ** -0.5
B_SCALE = B_QK_DIM ** -0.5
LOG2E = 1.4426950408889634

LANES = 128
HALF = LANES // 2
FF_CHUNK = 256
N_FF_CHUNKS = D_FF // FF_CHUNK
TOKEN_TILE = 512
ADA_TILE = 1024
Q_BLOCK = 128
BAND = 3 * Q_BLOCK
WIN_BLOCKS = 4
SUM_ROWS = 16
MLA_Q_TILE = 512
MLA_KEY_BLOCK = 512
MOD_ROWS = 8
VMEM_LIMIT = 56 * 1024 * 1024

IN_A_W = A_Q_W + 2 * A_KV_W + Q_LORA + KV_LORA + LANES
OFF_KA = A_Q_W
OFF_VA = OFF_KA + A_KV_W
OFF_QLAT = OFF_VA + A_KV_W
OFF_CKV = OFF_QLAT + Q_LORA
OFF_KROPE = OFF_CKV + KV_LORA


def _dot(a, b):
    return jnp.dot(a, b, preferred_element_type=F32)


def _dot_nt(a, b):
    return lax.dot_general(a, b, (((1,), (1,)), ((), ())), preferred_element_type=F32)


def _rms(x, g):
    ms = jnp.mean(x * x, axis=-1, keepdims=True)
    return x * lax.rsqrt(ms + EPS) * g


def _mod_norm(x, g, mods_ref, k):
    shift = mods_ref[:, (3 * k) * D_MODEL:(3 * k + 1) * D_MODEL]
    scale = mods_ref[:, (3 * k + 1) * D_MODEL:(3 * k + 2) * D_MODEL]
    return _rms(x, g) * (1.0 + scale) + shift


def _gate(mods_ref, k):
    return mods_ref[:, (3 * k + 2) * D_MODEL:(3 * k + 3) * D_MODEL]


def _ffn(h, wgu_ref, wd_ref, hooks=None):
    def gate_up(c):
        a = _dot(h, wgu_ref[:, c * FF_CHUNK:(c + 1) * FF_CHUNK])
        u = _dot(h, wgu_ref[:, D_FF + c * FF_CHUNK:D_FF + (c + 1) * FF_CHUNK])
        return a, u

    acc = None
    nxt = gate_up(0)
    for c in range(N_FF_CHUNKS):
        a, u = nxt
        if c + 1 < N_FF_CHUNKS:
            nxt = gate_up(c + 1)
        act = (a * jax.nn.sigmoid(a) * u).astype(BF16)
        d = _dot(act, wd_ref[c * FF_CHUNK:(c + 1) * FF_CHUNK, :])
        acc = d if acc is None else acc + d
        if hooks and c in hooks:
            hooks[c]()
    return acc


def _rope(x, cos, sin_signed, dist):
    lane = lax.broadcasted_iota(jnp.int32, x.shape, 1)
    first = (lane & (2 * dist - 1)) < dist
    partner = jnp.where(first, pltpu.roll(x, LANES - dist, axis=1), pltpu.roll(x, dist, axis=1))
    return x * cos + partner * sin_signed


def _store_gqa_k(k, low, kap_ref):
    k_sw = pltpu.roll(k, HALF, axis=1)
    zero = jnp.zeros_like(k)
    kap_ref[:, 0 * LANES:1 * LANES] = jnp.where(low, k, zero).astype(BF16)
    kap_ref[:, 1 * LANES:2 * LANES] = jnp.where(low, zero, k_sw).astype(BF16)
    kap_ref[:, 2 * LANES:3 * LANES] = jnp.where(low, k_sw, zero).astype(BF16)
    kap_ref[:, 3 * LANES:4 * LANES] = jnp.where(low, zero, k).astype(BF16)


def _store_gqa_v_dup(v, low, vad_ref):
    v_sw = pltpu.roll(v, HALF, axis=1)
    vad_ref[:, 0 * LANES:1 * LANES] = jnp.where(low, v, v_sw).astype(BF16)
    vad_ref[:, 1 * LANES:2 * LANES] = jnp.where(low, v_sw, v).astype(BF16)


def _store_gqa_v_blocks(v, vt_ref):
    for r in range(vt_ref.shape[0]):
        vt_ref[r] = v[r * LANES:(r + 1) * LANES, :].T.astype(BF16)


def _store_seq_minor(out_ref, val):
    nb, feat, seq = out_ref.shape
    for bi in range(nb):
        out_ref[bi] = val[bi * seq:(bi + 1) * seq, :].T[:feat, :]


def _attend(scores, values, sink):
    m = None
    for s in scores:
        mx = jnp.max(s, axis=-1, keepdims=True)
        m = mx if m is None else jnp.maximum(m, mx)
    if sink is not None:
        m = jnp.maximum(m, sink)
    l = None
    r = None
    for s, v in zip(scores, values):
        p = jnp.exp2(s - m)
        ls = jnp.sum(p, axis=-1, keepdims=True)
        pv = _dot(p.astype(BF16), v)
        l = ls if l is None else l + ls
        r = pv if r is None else r + pv
    if sink is not None:
        l = l + jnp.exp2(sink - m)
    return r / l


def _low_lanes(rows):
    return lax.broadcasted_iota(jnp.int32, (rows, LANES), 1) < HALF


def _ada_kernel(c_ref, w_ref, b_ref, o_ref):
    c = c_ref[...]
    s = (c * jax.nn.sigmoid(c)).astype(BF16)
    o_ref[...] = _dot(s, w_ref[...].astype(BF16)) + b_ref[...]


def _ada_mods(cvec, ada_w, ada_b):
    n = ada_w.shape[1]
    return pl.pallas_call(
        _ada_kernel,
        grid=(n // ADA_TILE,),
        in_specs=[
            pl.BlockSpec((MOD_ROWS, D_MODEL), lambda j: (0, 0)),
            pl.BlockSpec((D_MODEL, ADA_TILE), lambda j: (0, j)),
            pl.BlockSpec((1, ADA_TILE), lambda j: (0, j)),
        ],
        out_specs=pl.BlockSpec((MOD_ROWS, ADA_TILE), lambda j: (0, j)),
        out_shape=jax.ShapeDtypeStruct((MOD_ROWS, n), F32),
        compiler_params=pltpu.CompilerParams(dimension_semantics=("parallel",), vmem_limit_bytes=VMEM_LIMIT),
        name="ada_mods",
    )(cvec, ada_w, ada_b)


PRE_HOOK_IN_PROJ = 1
PRE_HOOK_UP_PROJ = 6


def _pre_kernel(latent, n_tiles, *refs):
    x1s_ref = refs[-1]
    if latent:
        (x_ref, mods_ref, modsp_ref, n1_ref, wgu_ref, wd_ref, nm_ref, win_ref, qn_ref, kvn_ref, wuq_ref, wuk_ref,
         wuv_ref, cosa_ref, sina_ref, cosb_ref, sinb_ref,
         x1_ref, qa_ref, kap_ref, vad_ref, qb_ref, kb_ref, vb_ref) = refs[:-1]
    else:
        (x_ref, mods_ref, modsp_ref, n1_ref, wgu_ref, wd_ref, nm_ref, win_ref, qn_ref, kvn_ref, wuq_ref, wuk_ref,
         wuv_ref,
         x1_ref, qa_ref, kap_ref, vad_ref, qb_ref, kb_ref, vb_ref,
         knat_ref, vnat_ref, ckvn_ref, krope_ref) = refs[:-1]
    i = pl.program_id(0)
    carry = {}

    def in_proj():
        h2 = _mod_norm(x1s_ref[...], nm_ref[...], modsp_ref, 1).astype(BF16)
        z = _dot(h2, win_ref[...])
        low = _low_lanes(z.shape[0])

        for c in range(A_Q_W // LANES):
            q = z[:, c * LANES:(c + 1) * LANES]
            if latent:
                q = _rope(q, cosa_ref[...], sina_ref[...], A_HEAD_DIM // 4)
            qa_ref[:, c * LANES:(c + 1) * LANES] = (q * (A_SCALE * LOG2E)).astype(BF16)
        k = z[:, OFF_KA:OFF_KA + LANES]
        v = z[:, OFF_VA:OFF_VA + LANES]
        if latent:
            k = _rope(k, cosa_ref[...], sina_ref[...], A_HEAD_DIM // 4)
            _store_gqa_v_blocks(v, vad_ref)
        else:
            _store_seq_minor(knat_ref, k)
            _store_seq_minor(vnat_ref, v)
            _store_gqa_v_dup(v, low, vad_ref)
        _store_gqa_k(k, low, kap_ref)

        carry["q_lat"] = _rms(z[:, OFF_QLAT:OFF_QLAT + Q_LORA], qn_ref[...]).astype(BF16)
        ckv_n = _rms(z[:, OFF_CKV:OFF_CKV + KV_LORA], kvn_ref[...])
        krp = z[:, OFF_KROPE:OFF_KROPE + LANES]
        if latent:
            krp = _rope(krp, cosb_ref[...], sinb_ref[...], B_ROPE_DIM // 4)
        else:
            ckvn_ref[...] = ckv_n
            _store_seq_minor(krope_ref, pltpu.roll(krp, HALF, axis=1))
        carry["ckv_b"] = ckv_n.astype(BF16)
        carry["krp"] = krp

    def up_proj():
        q_lat, ckv_b, krp = carry["q_lat"], carry["ckv_b"], carry["krp"]
        qb = _dot(q_lat, wuq_ref[...])
        kn = _dot(ckv_b, wuk_ref[...])
        if latent:
            vb_ref[...] = _dot_nt(wuv_ref[...], ckv_b).astype(BF16)
        else:
            vb_ref[...] = _dot(ckv_b, wuv_ref[...]).astype(BF16)
        for h in range(B_HEADS):
            qh = qb[:, h * LANES:(h + 1) * LANES]
            if latent:
                qh = _rope(qh, cosb_ref[...], sinb_ref[...], B_ROPE_DIM // 4)
            qb_ref[:, h * LANES:(h + 1) * LANES] = (qh * (B_SCALE * LOG2E)).astype(BF16)
            kb_ref[:, h * LANES:(h + 1) * LANES] = (kn[:, h * LANES:(h + 1) * LANES] + krp).astype(BF16)

    def ffn(hooks):
        x = x_ref[...]
        h1 = _mod_norm(x, n1_ref[...], mods_ref, 0).astype(BF16)
        x1 = x + 0.5 * _gate(mods_ref, 0) * _ffn(h1, wgu_ref, wd_ref, hooks)
        x1_ref[...] = x1
        x1s_ref[...] = x1

    @pl.when(i == 0)
    def _():
        ffn(None)

    @pl.when(jnp.logical_and(i > 0, i < n_tiles))
    def _():
        ffn({PRE_HOOK_IN_PROJ: in_proj, PRE_HOOK_UP_PROJ: up_proj})

    @pl.when(i == n_tiles)
    def _():
        in_proj()
        up_proj()


def _const_spec(shape):
    nd = len(shape)
    return pl.BlockSpec(shape, lambda i: (0,) * nd, pipeline_mode=pl.Buffered(1))


def _pre(x, mods3, w, latent, seq, rope):
    t = x.shape[0]
    tiles_per_batch = max(seq // TOKEN_TILE, 1)
    tm = TOKEN_TILE
    n_tiles = t // tm
    cur = lambda i: jnp.minimum(i, n_tiles - 1)
    prev = lambda i: jnp.maximum(i - 1, 0)
    mod_of = (lambda j: 1 + j // tiles_per_batch) if latent else (lambda j: 0)
    cur_rows = lambda i: (cur(i), 0)
    prev_rows = lambda i: (prev(i), 0)

    weights = [w["n1"], w["wgu1"], w["wd1"], w["nm"], w["win_a"], w["qn"], w["kvn"], w["wuq"], w["wuk"],
               w["wuv_t"] if latent else w["wuv"]]
    in_specs = [pl.BlockSpec((tm, D_MODEL), cur_rows),
                pl.BlockSpec((None, 1, N_MOD * D_MODEL), lambda i: (mod_of(cur(i)), 0, 0)),
                pl.BlockSpec((None, 1, N_MOD * D_MODEL), lambda i: (mod_of(prev(i)), 0, 0))]
    in_specs += [_const_spec(a.shape) for a in weights]
    args = [x, mods3, mods3] + weights
    if latent:
        in_specs += [pl.BlockSpec((tm, LANES), lambda i: (prev(i) % tiles_per_batch, 0))] * 4
        args += list(rope)

    out_specs = [pl.BlockSpec((tm, D_MODEL), cur_rows)]
    out_shape = [jax.ShapeDtypeStruct((t, D_MODEL), F32)]
    for wd in (A_Q_W, 4 * LANES, 2 * LANES, B_HEADS * LANES, B_HEADS * LANES):
        out_specs.append(pl.BlockSpec((tm, wd), prev_rows))
        out_shape.append(jax.ShapeDtypeStruct((t, wd), BF16))
    if latent:
        out_specs[3] = pl.BlockSpec((tm // LANES, A_KV_W, LANES), lambda i: (prev(i), 0, 0))
        out_shape[3] = jax.ShapeDtypeStruct((t // LANES, A_KV_W, LANES), BF16)
        out_specs.append(pl.BlockSpec((B_HEADS * B_V_DIM, tm), lambda i: (0, prev(i))))
        out_shape.append(jax.ShapeDtypeStruct((B_HEADS * B_V_DIM, t), BF16))
    else:
        out_specs.append(pl.BlockSpec((tm, B_HEADS * B_V_DIM), prev_rows))
        out_shape.append(jax.ShapeDtypeStruct((t, B_HEADS * B_V_DIM), BF16))
        nb = tm // seq
        for feat in (A_KV_W, A_KV_W):
            out_specs.append(pl.BlockSpec((nb, feat, seq), lambda i: (prev(i), 0, 0)))
            out_shape.append(jax.ShapeDtypeStruct((t // seq, feat, seq), F32))
        out_specs.append(pl.BlockSpec((tm, KV_LORA), prev_rows))
        out_shape.append(jax.ShapeDtypeStruct((t, KV_LORA), F32))
        out_specs.append(pl.BlockSpec((nb, B_ROPE_DIM, seq), lambda i: (prev(i), 0, 0)))
        out_shape.append(jax.ShapeDtypeStruct((t // seq, B_ROPE_DIM, seq), F32))

    return pl.pallas_call(
        functools.partial(_pre_kernel, latent, n_tiles),
        grid=(n_tiles + 1,),
        in_specs=in_specs,
        out_specs=out_specs,
        out_shape=out_shape,
        scratch_shapes=[pltpu.VMEM((tm, D_MODEL), F32)],
        compiler_params=pltpu.CompilerParams(dimension_semantics=("arbitrary",), vmem_limit_bytes=VMEM_LIMIT),
        name="pre_latent" if latent else "pre_context",
    )(*args)


def _cache_kernel(ckv_ref, krp_ref, ck_ref, cv_ref, wuk_ref, wuv_ref, kb_ref, vb_ref, kap_ref, vat_ref):
    c = ckv_ref[...].astype(BF16)
    kn = _dot(c, wuk_ref[...])
    krp = krp_ref[...]
    for h in range(B_HEADS):
        kb_ref[:, h * LANES:(h + 1) * LANES] = (kn[:, h * LANES:(h + 1) * LANES] + krp).astype(BF16)
    vb_ref[...] = _dot_nt(wuv_ref[...], c).astype(BF16)
    _store_gqa_k(ck_ref[...], _low_lanes(ck_ref.shape[0]), kap_ref)
    vat_ref[...] = cv_ref[...].T.astype(BF16)


def _cache_kv(ckv, krp, ck, cv, wuk, wuv_t):
    b, p, _ = ckv.shape
    blk = lambda wd: pl.BlockSpec((None, p, wd), lambda i: (i, 0, 0))
    vdim = B_HEADS * B_V_DIM
    return pl.pallas_call(
        _cache_kernel,
        grid=(b,),
        in_specs=[blk(KV_LORA), blk(LANES), blk(A_KV_W), blk(A_KV_W),
                  pl.BlockSpec(wuk.shape, lambda i: (0, 0)),
                  pl.BlockSpec(wuv_t.shape, lambda i: (0, 0))],
        out_specs=[blk(B_HEADS * LANES), pl.BlockSpec((vdim, p), lambda i: (0, i)), blk(4 * LANES),
                   pl.BlockSpec((A_KV_W, p), lambda i: (0, i))],
        out_shape=[jax.ShapeDtypeStruct((b, p, B_HEADS * LANES), BF16),
                   jax.ShapeDtypeStruct((vdim, b * p), BF16),
                   jax.ShapeDtypeStruct((b, p, 4 * LANES), BF16),
                   jax.ShapeDtypeStruct((A_KV_W, b * p), BF16)],
        compiler_params=pltpu.CompilerParams(dimension_semantics=("parallel",), vmem_limit_bytes=VMEM_LIMIT),
        name="cache_kv",
    )(ckv, krp, ck, cv, wuk, wuv_t)


def _att_ctx_kernel(sink_ref, qa_ref, kap_ref, vad_ref, qb_ref, kb_ref, vb_ref, oa_ref, ob_ref):
    low = _low_lanes(qa_ref.shape[0])
    n_a = A_HEADS

    def scores(u):
        if u < n_a:
            c, half = divmod(u, 2)
            kc = 2 * (c // 2) + half
            return _dot_nt(qa_ref[:, c * LANES:(c + 1) * LANES], kap_ref[:, kc * LANES:(kc + 1) * LANES])
        h = u - n_a
        return _dot_nt(qb_ref[:, h * LANES:(h + 1) * LANES], kb_ref[:, h * LANES:(h + 1) * LANES])

    nxt = scores(0)
    res = []
    for u in range(n_a + B_HEADS):
        s = nxt
        if u + 1 < n_a + B_HEADS:
            nxt = scores(u + 1)
        if u < n_a:
            kv = u // 4
            res.append(_attend([s], [vad_ref[:, kv * LANES:(kv + 1) * LANES]], sink_ref[u] * LOG2E))
            out_ref, c = oa_ref, u // 2
        else:
            c = (u - n_a) // 2
            res.append(_attend([s], [vb_ref[:, c * LANES:(c + 1) * LANES]], None))
            out_ref = ob_ref
        if u % 2 == 1:
            out_ref[:, c * LANES:(c + 1) * LANES] = jnp.where(low, res[u - 1], res[u]).astype(BF16)


def _att_ctx(sink, qa, kap, vad, qb, kb, vb, seq):
    t = qa.shape[0]
    blk = lambda wd: pl.BlockSpec((seq, wd), lambda b: (b, 0))
    return pl.pallas_call(
        _att_ctx_kernel,
        grid=(t // seq,),
        in_specs=[pl.BlockSpec(memory_space=pltpu.SMEM),
                  blk(qa.shape[1]), blk(kap.shape[1]), blk(vad.shape[1]),
                  blk(qb.shape[1]), blk(kb.shape[1]), blk(vb.shape[1])],
        out_specs=[blk(A_Q_W), blk(B_HEADS * B_V_DIM)],
        out_shape=[jax.ShapeDtypeStruct((t, A_Q_W), BF16), jax.ShapeDtypeStruct((t, B_HEADS * B_V_DIM), BF16)],
        compiler_params=pltpu.CompilerParams(dimension_semantics=("parallel",), vmem_limit_bytes=VMEM_LIMIT),
        name="att_context",
    )(sink, qa, kap, vad, qb, kb, vb)


def _att_win_kernel(n_lat, sink_ref, qa_ref, kap_ref, vat_ref, kapc_ref, vatc_ref, oa_ref):
    nq = 2 * Q_BLOCK
    past = kapc_ref.shape[0]
    upper = lax.broadcasted_iota(jnp.int32, (1, nq), 1) >= Q_BLOCK
    ones_band = jnp.ones((SUM_ROWS, BAND), BF16)
    ones_ctx = jnp.ones((SUM_ROWS, past), BF16)
    key = lax.broadcasted_iota(jnp.int32, (BAND, nq), 0)
    qry = lax.broadcasted_iota(jnp.int32, (BAND, nq), 1) & (Q_BLOCK - 1)
    key_minus_qry = key - qry

    starts, valids, v_bands = [], [], []
    for qb in range(WIN_BLOCKS):
        i = pl.program_id(1) * WIN_BLOCKS + qb
        start = pl.multiple_of(jnp.clip(i * Q_BLOCK - Q_BLOCK, 0, n_lat - BAND), Q_BLOCK)
        starts.append(start)
        valids.append(jnp.abs(key_minus_qry + (start - i * Q_BLOCK)) <= WINDOW)
        v_bands.append(vat_ref[pl.ds(start // LANES, BAND // LANES)])

    units = [(qb, kv, half) for qb in range(WIN_BLOCKS) for kv in range(A_KV_HEADS) for half in range(2)]

    def score_blocks(unit):
        qb, kv, half = unit
        u = 2 * kv + half
        rows = slice(qb * Q_BLOCK, (qb + 1) * Q_BLOCK)
        q2 = jnp.concatenate([qa_ref[rows, (2 * kv) * LANES:(2 * kv + 1) * LANES],
                              qa_ref[rows, (2 * kv + 1) * LANES:(2 * kv + 2) * LANES]], axis=0)
        s_band = _dot_nt(kap_ref[pl.ds(starts[qb], BAND), u * LANES:(u + 1) * LANES], q2)
        s_ctx = _dot_nt(kapc_ref[:, u * LANES:(u + 1) * LANES], q2)
        return [jnp.where(valids[qb], s_band, NEG), s_ctx]

    def value_products(unit, p_blocks):
        qb, kv, _ = unit
        rows = slice(kv * A_HEAD_DIM, (kv + 1) * A_HEAD_DIM)
        vb = jnp.concatenate([v_bands[qb][r][rows] for r in range(BAND // LANES)], axis=1)
        r_band = _dot(jnp.concatenate([vb, ones_band], axis=0), p_blocks[0])
        r_ctx = _dot(jnp.concatenate([vatc_ref[rows, :], ones_ctx], axis=0), p_blocks[1])
        return r_band + r_ctx

    scores, probs, outs = {}, {}, {}
    for t in range(len(units) + 2):
        if t < len(units):
            scores[t] = score_blocks(units[t])
        if 0 <= t - 2 < len(units):
            n = t - 2
            qb, kv, half = units[n]
            p_blocks, p_sink = probs.pop(n)
            acc = value_products(units[n], p_blocks)
            outs[n] = acc[:A_HEAD_DIM] / (acc[A_HEAD_DIM:A_HEAD_DIM + 1] + p_sink)
            if half == 1:
                for pr in range(2):
                    cols = slice(pr * Q_BLOCK, (pr + 1) * Q_BLOCK)
                    pair = jnp.concatenate([outs[n - 1][:, cols], outs[n][:, cols]], axis=0)
                    c = 2 * kv + pr
                    oa_ref[qb * Q_BLOCK:(qb + 1) * Q_BLOCK, c * LANES:(c + 1) * LANES] = pair.T.astype(BF16)
                del outs[n - 1], outs[n]
        if 0 <= t - 1 < len(units):
            n = t - 1
            _, kv, half = units[n]
            sink = jnp.where(upper, sink_ref[4 * kv + 2 + half], sink_ref[4 * kv + half]) * LOG2E
            probs[n] = _probs_keys_major(scores.pop(n), sink)


def _att_win(sink, qa, kap, vat, kapc, vatc):
    b, n, _ = qa.shape
    past = kapc.shape[1]
    tq = WIN_BLOCKS * Q_BLOCK
    return pl.pallas_call(
        functools.partial(_att_win_kernel, n),
        grid=(b, n // tq),
        in_specs=[pl.BlockSpec(memory_space=pltpu.SMEM),
                  pl.BlockSpec((None, tq, A_Q_W), lambda bi, i: (bi, i, 0)),
                  pl.BlockSpec((None, n, kap.shape[2]), lambda bi, i: (bi, 0, 0)),
                  pl.BlockSpec((n // LANES, A_KV_W, LANES), lambda bi, i: (bi, 0, 0)),
                  pl.BlockSpec((None, past, kapc.shape[2]), lambda bi, i: (bi, 0, 0)),
                  pl.BlockSpec((A_KV_W, past), lambda bi, i: (0, bi))],
        out_specs=pl.BlockSpec((None, tq, A_Q_W), lambda bi, i: (bi, i, 0)),
        out_shape=jax.ShapeDtypeStruct((b, n, A_Q_W), BF16),
        compiler_params=pltpu.CompilerParams(dimension_semantics=("parallel", "parallel"),
                                             vmem_limit_bytes=VMEM_LIMIT),
        name="att_window",
    )(sink, qa, kap, vat, kapc, vatc)


def _probs_keys_major(scores_t, sink=None):
    m = None
    for s in scores_t:
        mx = jnp.max(s, axis=0, keepdims=True)
        m = mx if m is None else jnp.maximum(m, mx)
    if sink is not None:
        m = jnp.maximum(m, sink)
    probs = [jnp.exp2(s - m).astype(BF16) for s in scores_t]
    return probs, (None if sink is None else jnp.exp2(sink - m))


def _att_mla_kernel(qb_ref, kbl_ref, vtl_ref, kbc_ref, vtc_ref, ob_ref):
    kb = MLA_KEY_BLOCK
    ones = jnp.ones((SUM_ROWS, kb), BF16)
    blocks = [(kbc_ref, vtc_ref, j) for j in range(kbc_ref.shape[0] // kb)]
    blocks += [(kbl_ref, vtl_ref, j) for j in range(kbl_ref.shape[0] // kb)]

    def score_block(h, blk):
        k_ref, _, j = blk
        return _dot_nt(k_ref[j * kb:(j + 1) * kb, h * LANES:(h + 1) * LANES], qb_ref[:, h * LANES:(h + 1) * LANES])

    def value_block(h, blk, p):
        _, v_ref, j = blk
        vt = jnp.concatenate([v_ref[h * B_V_DIM:(h + 1) * B_V_DIM, j * kb:(j + 1) * kb], ones], axis=0)
        return _dot(vt, p)

    scores, probs, outs = {}, {}, {}
    for t in range(B_HEADS + 2):
        acc = None
        new_scores = []
        for bi, blk in enumerate(blocks):
            if t < B_HEADS:
                new_scores.append(score_block(t, blk))
            if 0 <= t - 2 < B_HEADS:
                r = value_block(t - 2, blk, probs[t - 2][bi])
                acc = r if acc is None else acc + r
        if t < B_HEADS:
            scores[t] = new_scores
        if 0 <= t - 1 < B_HEADS:
            probs[t - 1], _ = _probs_keys_major(scores.pop(t - 1))
        if acc is not None:
            h = t - 2
            probs.pop(h)
            outs[h] = acc[:B_V_DIM] / acc[B_V_DIM:B_V_DIM + 1]
            if h % 2 == 1:
                c = h // 2
                pair = jnp.concatenate([outs.pop(h - 1), outs.pop(h)], axis=0)
                ob_ref[:, c * LANES:(c + 1) * LANES] = pair.T.astype(BF16)


def _att_mla(qb, kbl, vtl, kbc, vtc):
    b, n, _ = qb.shape
    past = kbc.shape[1]
    tq = MLA_Q_TILE
    vdim = B_HEADS * B_V_DIM
    return pl.pallas_call(
        _att_mla_kernel,
        grid=(b, n // tq),
        in_specs=[pl.BlockSpec((None, tq, qb.shape[2]), lambda bi, i: (bi, i, 0)),
                  pl.BlockSpec((None, n, kbl.shape[2]), lambda bi, i: (bi, 0, 0)),
                  pl.BlockSpec((vdim, n), lambda bi, i: (0, bi)),
                  pl.BlockSpec((None, past, kbc.shape[2]), lambda bi, i: (bi, 0, 0)),
                  pl.BlockSpec((vdim, past), lambda bi, i: (0, bi))],
        out_specs=pl.BlockSpec((None, tq, vdim), lambda bi, i: (bi, i, 0)),
        out_shape=jax.ShapeDtypeStruct((b, n, vdim), BF16),
        compiler_params=pltpu.CompilerParams(dimension_semantics=("parallel", "parallel"),
                                             vmem_limit_bytes=VMEM_LIMIT),
        name="att_mla",
    )(qb, kbl, vtl, kbc, vtc)


def _post_kernel(x1_ref, mods_ref, oa_ref, ob_ref, nm_ref, wing_ref, woa_ref, wob_ref, wout_ref,
                 n2_ref, wgu_ref, wd_ref, nf_ref, y_ref):
    x1 = x1_ref[...]
    h2 = _mod_norm(x1, nm_ref[...], mods_ref, 1).astype(BF16)
    g = _dot(h2, wing_ref[...])
    m = (jax.nn.sigmoid(g[:, :D_MODEL]) * _dot(oa_ref[...], woa_ref[...])
         + jax.nn.sigmoid(g[:, D_MODEL:]) * _dot(ob_ref[...], wob_ref[...]))
    x2 = x1 + _gate(mods_ref, 1) * _dot(m.astype(BF16), wout_ref[...])
    h3 = _mod_norm(x2, n2_ref[...], mods_ref, 2).astype(BF16)
    x3 = x2 + 0.5 * _gate(mods_ref, 2) * _ffn(h3, wgu_ref, wd_ref)
    y_ref[...] = _rms(x3, nf_ref[...])


def _post(x1, mods3, oa, ob, w, latent, tiles_per_batch):
    t = x1.shape[0]
    tm = TOKEN_TILE
    if latent:
        mod_row = lambda i: (1 + i // tiles_per_batch, 0, 0)
    else:
        mod_row = lambda i: (0, 0, 0)
    row_blk = lambda i: (i, 0)
    weights = [w["nm"], w["win_g"], w["woa"], w["wob"], w["wout"], w["n2"], w["wgu2"], w["wd2"], w["nf"]]
    in_specs = [pl.BlockSpec((tm, D_MODEL), row_blk),
                pl.BlockSpec((None, 1, N_MOD * D_MODEL), mod_row),
                pl.BlockSpec((tm, A_Q_W), row_blk),
                pl.BlockSpec((tm, B_HEADS * B_V_DIM), row_blk)]
    in_specs += [_const_spec(a.shape) for a in weights]
    return pl.pallas_call(
        _post_kernel,
        grid=(t // tm,),
        in_specs=in_specs,
        out_specs=pl.BlockSpec((tm, D_MODEL), row_blk),
        out_shape=jax.ShapeDtypeStruct((t, D_MODEL), F32),
        compiler_params=pltpu.CompilerParams(dimension_semantics=("parallel",), vmem_limit_bytes=VMEM_LIMIT),
        name="post_latent" if latent else "post_context",
    )(x1, mods3, oa, ob, *weights)


def _rope_tables(n):
    rows = n // GRID_W
    t_row = jnp.repeat(jnp.arange(rows, dtype=F32), GRID_W)
    t_col = jnp.tile(jnp.arange(GRID_W, dtype=F32), rows)

    def angles(d_rot):
        d_half = d_rot // 2
        inv = 1.0 / (ROPE_THETA ** (jnp.arange(0, d_half, 2, dtype=F32) / d_half))
        ar = t_row[:, None] * inv[None, :]
        ac = t_col[:, None] * inv[None, :]
        return jnp.concatenate([ar, ar, ac, ac], axis=-1)

    def signed(sin, d_rot):
        q = d_rot // 4
        sign = jnp.where((jnp.arange(d_rot) % (2 * q)) < q, -1.0, 1.0).astype(F32)
        return sin * sign[None, :]

    ang_a = angles(A_HEAD_DIM)
    cos_a = jnp.tile(jnp.cos(ang_a), (1, LANES // A_HEAD_DIM))
    sin_a = jnp.tile(signed(jnp.sin(ang_a), A_HEAD_DIM), (1, LANES // A_HEAD_DIM))
    ang_b = angles(B_ROPE_DIM)
    pad_l = B_NOPE_DIM
    pad_r = LANES - B_NOPE_DIM - B_ROPE_DIM
    cos_b = jnp.pad(jnp.cos(ang_b), ((0, 0), (pad_l, pad_r)), constant_values=1.0)
    sin_b = jnp.pad(signed(jnp.sin(ang_b), B_ROPE_DIM), ((0, 0), (pad_l, pad_r)))
    return cos_a, sin_a, cos_b, sin_b


def _prep_weights(ffn1_norm, ffn1_w_gu, ffn1_w_down, mix_norm, w_in, q_lat_norm, kv_lat_norm, w_uq, w_ukv,
                  w_o_a, w_o_b, w_out, ffn2_norm, ffn2_w_gu, ffn2_w_down, final_norm):
    win = w_in[0]
    n_attn = A_Q_W + 2 * A_KV_W + Q_LORA + KV_LORA
    krope_cols = jnp.pad(win[:, n_attn:n_attn + B_ROPE_DIM],
                         ((0, 0), (B_NOPE_DIM, LANES - B_NOPE_DIM - B_ROPE_DIM)))
    win_a = jnp.concatenate([win[:, :n_attn], krope_cols], axis=1).astype(BF16)
    win_g = win[:, n_attn + B_ROPE_DIM:].astype(BF16)

    wuq = jnp.pad(w_uq[0].reshape(Q_LORA, B_HEADS, B_QK_DIM),
                  ((0, 0), (0, 0), (0, LANES - B_QK_DIM))).reshape(Q_LORA, B_HEADS * LANES).astype(BF16)
    wukv = w_ukv[0].reshape(KV_LORA, B_HEADS, B_NOPE_DIM + B_V_DIM)
    wuk = jnp.pad(wukv[:, :, :B_NOPE_DIM],
                  ((0, 0), (0, 0), (0, LANES - B_NOPE_DIM))).reshape(KV_LORA, B_HEADS * LANES).astype(BF16)
    wuv = wukv[:, :, B_NOPE_DIM:].reshape(KV_LORA, B_HEADS * B_V_DIM).astype(BF16)

    return {
        "n1": ffn1_norm, "wgu1": ffn1_w_gu[0].astype(BF16), "wd1": ffn1_w_down[0].astype(BF16),
        "nm": mix_norm, "win_a": win_a, "win_g": win_g,
        "qn": q_lat_norm, "kvn": kv_lat_norm, "wuq": wuq, "wuk": wuk, "wuv": wuv, "wuv_t": wuv.T,
        "woa": w_o_a[0].astype(BF16), "wob": w_o_b[0].astype(BF16), "wout": w_out[0].astype(BF16),
        "n2": ffn2_norm, "wgu2": ffn2_w_gu[0].astype(BF16), "wd2": ffn2_w_down[0].astype(BF16),
        "nf": final_norm.reshape(1, D_MODEL),
    }


def kernel(x_prompt, x_sample, cache_attn_k, cache_attn_v, cache_mla_ckv, cache_mla_krope, c, c_ctx, ada_w, ada_b, ffn1_norm, ffn1_w_gu, ffn1_w_down, mix_norm, w_in, attn_sink, q_lat_norm, kv_lat_norm, w_uq, w_ukv, w_o_a, w_o_b, w_out, ffn2_norm, ffn2_w_gu, ffn2_w_down, final_norm):
    assert ada_w.shape[0] == 1, "single trunk layer"
    bp, sp, d = x_prompt.shape
    bs, ns, _ = x_sample.shape
    past = cache_attn_k.shape[2]
    assert d == D_MODEL and bs + 1 <= MOD_ROWS
    assert TOKEN_TILE % sp == 0 and (bp * sp) % TOKEN_TILE == 0
    assert ns % TOKEN_TILE == 0 and ns % MLA_Q_TILE == 0 and ns % (WIN_BLOCKS * Q_BLOCK) == 0 and ns >= BAND

    w = _prep_weights(ffn1_norm, ffn1_w_gu, ffn1_w_down, mix_norm, w_in, q_lat_norm, kv_lat_norm, w_uq, w_ukv,
                      w_o_a, w_o_b, w_out, ffn2_norm, ffn2_w_gu, ffn2_w_down, final_norm)
    sink = attn_sink[0]

    cvec = jnp.concatenate([c_ctx[None, :], c, jnp.zeros((MOD_ROWS - 1 - bs, d), F32)], axis=0)
    mods = _ada_mods(cvec, ada_w[0], ada_b)
    mods3 = mods.reshape(MOD_ROWS, 1, N_MOD * D_MODEL)

    xp = x_prompt.reshape(bp * sp, d)
    (x1p, qa, kap, vad, qb, kb, vb, k_t, v_t, ckv_n, krope_t) = _pre(xp, mods3, w, False, sp, None)
    oa, ob = _att_ctx(sink, qa, kap, vad, qb, kb, vb, sp)
    y_prompt = _post(x1p, mods3, oa, ob, w, False, 1).reshape(bp, sp, d)

    tiles_per_batch = ns // TOKEN_TILE
    xs = x_sample.reshape(bs * ns, d)
    (x1s, qa, kap, vad, qb, kb, vb_t) = _pre(xs, mods3, w, True, ns, _rope_tables(ns))
    krp_c = jnp.pad(cache_mla_krope[:, 0], ((0, 0), (0, 0), (B_NOPE_DIM, LANES - B_NOPE_DIM - B_ROPE_DIM)))
    kbc, vbc, kapc, vadc = _cache_kv(
        cache_mla_ckv[:, 0], krp_c,
        cache_attn_k[:, 0].reshape(bs, past, A_KV_W), cache_attn_v[:, 0].reshape(bs, past, A_KV_W),
        w["wuk"], w["wuv_t"])
    r3 = lambda a: a.reshape(bs, ns, a.shape[1])
    oa = _att_win(sink, r3(qa), r3(kap), vad, kapc, vadc)
    ob = _att_mla(r3(qb), r3(kb), vb_t, kbc, vbc)
    y_sample = _post(x1s, mods3, oa.reshape(bs * ns, A_Q_W), ob.reshape(bs * ns, B_HEADS * B_V_DIM),
                     w, True, tiles_per_batch).reshape(bs, ns, d)

    new_attn_k = k_t.reshape(bp, 1, A_KV_HEADS, A_HEAD_DIM, sp).transpose(0, 1, 4, 2, 3)
    new_attn_v = v_t.reshape(bp, 1, A_KV_HEADS, A_HEAD_DIM, sp).transpose(0, 1, 4, 2, 3)
    new_mla_ckv = ckv_n.reshape(bp, 1, sp, KV_LORA)
    new_mla_krope = krope_t.reshape(bp, 1, B_ROPE_DIM, sp).transpose(0, 1, 3, 2)
    return (y_prompt, y_sample, new_attn_k, new_attn_v, new_mla_ckv, new_mla_krope)
```

```python
import functools

import jax
import jax.numpy as jnp
from jax import lax
from jax.experimental import pallas as pl
from jax.experimental.pallas import tpu as pltpu

F32 = jnp.float32
BF16 = jnp.bfloat16

D_MODEL = 1024
N_MOD = 9
GRID_W = 64
WINDOW = 128
A_HEADS = 8
A_KV_HEADS = 2
A_HEAD_DIM = 64
A_Q_W = A_HEADS * A_HEAD_DIM
A_KV_W = A_KV_HEADS * A_HEAD_DIM
B_HEADS = 8
B_NOPE_DIM = 64
B_ROPE_DIM = 32
B_V_DIM = 64
B_QK_DIM = B_NOPE_DIM + B_ROPE_DIM
Q_LORA = 256
KV_LORA = 256
D_FF = 2816
ROPE_THETA = 10000.0
EPS = 1e-6
NEG = -1e30
A_SCALE = A_HEAD_DIM ** -0.5
B_SCALE = B_QK_DIM ** -0.5
LOG2E = 1.4426950408889634

LANES = 128
HALF = LANES // 2
FF_CHUNK = 256
N_FF_CHUNKS = D_FF // FF_CHUNK
TOKEN_TILE = 512
ADA_TILE = 1024
Q_BLOCK = 128
BAND = 3 * Q_BLOCK
WIN_BLOCKS = 4
SUM_ROWS = 16
MLA_Q_TILE = 512
MLA_KEY_BLOCK = 512
MOD_ROWS = 8
VMEM_LIMIT = 56 * 1024 * 1024

IN_A_W = A_Q_W + 2 * A_KV_W + Q_LORA + KV_LORA + LANES
OFF_KA = A_Q_W
OFF_VA = OFF_KA + A_KV_W
OFF_QLAT = OFF_VA + A_KV_W
OFF_CKV = OFF_QLAT + Q_LORA
OFF_KROPE = OFF_CKV + KV_LORA


def _dot(a, b):
    return jnp.dot(a, b, preferred_element_type=F32)


def _dot_nt(a, b):
    return lax.dot_general(a, b, (((1,), (1,)), ((), ())), preferred_element_type=F32)


def _rms(x, g):
    ms = jnp.mean(x * x, axis=-1, keepdims=True)
    return x * lax.rsqrt(ms + EPS) * g


def _mod_norm(x, g, mods_ref, k):
    shift = mods_ref[:, (3 * k) * D_MODEL:(3 * k + 1) * D_MODEL]
    scale = mods_ref[:, (3 * k + 1) * D_MODEL:(3 * k + 2) * D_MODEL]
    return _rms(x, g) * (1.0 + scale) + shift


def _gate(mods_ref, k):
    return mods_ref[:, (3 * k + 2) * D_MODEL:(3 * k + 3) * D_MODEL]


def _ffn(h, wgu_ref, wd_ref, hooks=None):
    def gate_up(c):
        a = _dot(h, wgu_ref[:, c * FF_CHUNK:(c + 1) * FF_CHUNK])
        u = _dot(h, wgu_ref[:, D_FF + c * FF_CHUNK:D_FF + (c + 1) * FF_CHUNK])
        return a, u

    acc = None
    nxt = gate_up(0)
    for c in range(N_FF_CHUNKS):
        a, u = nxt
        if c + 1 < N_FF_CHUNKS:
            nxt = gate_up(c + 1)
        act = (a * jax.nn.sigmoid(a) * u).astype(BF16)
        d = _dot(act, wd_ref[c * FF_CHUNK:(c + 1) * FF_CHUNK, :])
        acc = d if acc is None else acc + d
        if hooks and c in hooks:
            hooks[c]()
    return acc


def _rope(x, cos, sin_signed, dist):
    lane = lax.broadcasted_iota(jnp.int32, x.shape, 1)
    first = (lane & (2 * dist - 1)) < dist
    partner = jnp.where(first, pltpu.roll(x, LANES - dist, axis=1), pltpu.roll(x, dist, axis=1))
    return x * cos + partner * sin_signed


def _store_gqa_k(k, low, kap_ref):
    k_sw = pltpu.roll(k, HALF, axis=1)
    zero = jnp.zeros_like(k)
    kap_ref[:, 0 * LANES:1 * LANES] = jnp.where(low, k, zero).astype(BF16)
    kap_ref[:, 1 * LANES:2 * LANES] = jnp.where(low, zero, k_sw).astype(BF16)
    kap_ref[:, 2 * LANES:3 * LANES] = jnp.where(low, k_sw, zero).astype(BF16)
    kap_ref[:, 3 * LANES:4 * LANES] = jnp.where(low, zero, k).astype(BF16)


def _store_gqa_v_dup(v, low, vad_ref):
    v_sw = pltpu.roll(v, HALF, axis=1)
    vad_ref[:, 0 * LANES:1 * LANES] = jnp.where(low, v, v_sw).astype(BF16)
    vad_ref[:, 1 * LANES:2 * LANES] = jnp.where(low, v_sw, v).astype(BF16)


def _store_gqa_v_blocks(v, vt_ref):
    for r in range(vt_ref.shape[0]):
        vt_ref[r] = v[r * LANES:(r + 1) * LANES, :].T.astype(BF16)


def _store_seq_minor(out_ref, val):
    nb, feat, seq = out_ref.shape
    for bi in range(nb):
        out_ref[bi] = val[bi * seq:(bi + 1) * seq, :].T[:feat, :]


def _attend(scores, values, sink):
    m = None
    for s in scores:
        mx = jnp.max(s, axis=-1, keepdims=True)
        m = mx if m is None else jnp.maximum(m, mx)
    if sink is not None:
        m = jnp.maximum(m, sink)
    l = None
    r = None
    for s, v in zip(scores, values):
        p = jnp.exp2(s - m)
        ls = jnp.sum(p, axis=-1, keepdims=True)
        pv = _dot(p.astype(BF16), v)
        l = ls if l is None else l + ls
        r = pv if r is None else r + pv
    if sink is not None:
        l = l + jnp.exp2(sink - m)
    return r / l


def _low_lanes(rows):
    return lax.broadcasted_iota(jnp.int32, (rows, LANES), 1) < HALF


def _ada_kernel(c_ref, w_ref, b_ref, o_ref):
    c = c_ref[...]
    s = (c * jax.nn.sigmoid(c)).astype(BF16)
    o_ref[...] = _dot(s, w_ref[...].astype(BF16)) + b_ref[...]


def _ada_mods(cvec, ada_w, ada_b):
    n = ada_w.shape[1]
    return pl.pallas_call(
        _ada_kernel,
        grid=(n // ADA_TILE,),
        in_specs=[
            pl.BlockSpec((MOD_ROWS, D_MODEL), lambda j: (0, 0)),
            pl.BlockSpec((D_MODEL, ADA_TILE), lambda j: (0, j)),
            pl.BlockSpec((1, ADA_TILE), lambda j: (0, j)),
        ],
        out_specs=pl.BlockSpec((MOD_ROWS, ADA_TILE), lambda j: (0, j)),
        out_shape=jax.ShapeDtypeStruct((MOD_ROWS, n), F32),
        compiler_params=pltpu.CompilerParams(dimension_semantics=("parallel",), vmem_limit_bytes=VMEM_LIMIT),
        name="ada_mods",
    )(cvec, ada_w, ada_b)


PRE_HOOK_IN_PROJ = 1
PRE_HOOK_UP_PROJ = 6


def _pre_kernel(latent, n_tiles, *refs):
    x1s_ref = refs[-1]
    if latent:
        (x_ref, mods_ref, modsp_ref, n1_ref, wgu_ref, wd_ref, nm_ref, win_ref, qn_ref, kvn_ref, wuq_ref, wuk_ref,
         wuv_ref, cosa_ref, sina_ref, cosb_ref, sinb_ref,
         x1_ref, qa_ref, kap_ref, vad_ref, qb_ref, kb_ref, vb_ref) = refs[:-1]
    else:
        (x_ref, mods_ref, modsp_ref, n1_ref, wgu_ref, wd_ref, nm_ref, win_ref, qn_ref, kvn_ref, wuq_ref, wuk_ref,
         wuv_ref,
         x1_ref, qa_ref, kap_ref, vad_ref, qb_ref, kb_ref, vb_ref,
         knat_ref, vnat_ref, ckvn_ref, krope_ref) = refs[:-1]
    i = pl.program_id(0)
    carry = {}

    def in_proj():
        h2 = _mod_norm(x1s_ref[...], nm_ref[...], modsp_ref, 1).astype(BF16)
        z = _dot(h2, win_ref[...])
        low = _low_lanes(z.shape[0])

        for c in range(A_Q_W // LANES):
            q = z[:, c * LANES:(c + 1) * LANES]
            if latent:
                q = _rope(q, cosa_ref[...], sina_ref[...], A_HEAD_DIM // 4)
            qa_ref[:, c * LANES:(c + 1) * LANES] = (q * (A_SCALE * LOG2E)).astype(BF16)
        k = z[:, OFF_KA:OFF_KA + LANES]
        v = z[:, OFF_VA:OFF_VA + LANES]
        if latent:
            k = _rope(k, cosa_ref[...], sina_ref[...], A_HEAD_DIM // 4)
            _store_gqa_v_blocks(v, vad_ref)
        else:
            _store_seq_minor(knat_ref, k)
            _store_seq_minor(vnat_ref, v)
            _store_gqa_v_dup(v, low, vad_ref)
        _store_gqa_k(k, low, kap_ref)

        carry["q_lat"] = _rms(z[:, OFF_QLAT:OFF_QLAT + Q_LORA], qn_ref[...]).astype(BF16)
        ckv_n = _rms(z[:, OFF_CKV:OFF_CKV + KV_LORA], kvn_ref[...])
        krp = z[:, OFF_KROPE:OFF_KROPE + LANES]
        if latent:
            krp = _rope(krp, cosb_ref[...], sinb_ref[...], B_ROPE_DIM // 4)
        else:
            ckvn_ref[...] = ckv_n
            _store_seq_minor(krope_ref, pltpu.roll(krp, HALF, axis=1))
        carry["ckv_b"] = ckv_n.astype(BF16)
        carry["krp"] = krp

    def up_proj():
        q_lat, ckv_b, krp = carry["q_lat"], carry["ckv_b"], carry["krp"]
        qb = _dot(q_lat, wuq_ref[...])
        kn = _dot(ckv_b, wuk_ref[...])
        if latent:
            vb_ref[...] = _dot_nt(wuv_ref[...], ckv_b).astype(BF16)
        else:
            vb_ref[...] = _dot(ckv_b, wuv_ref[...]).astype(BF16)
        for h in range(B_HEADS):
            qh = qb[:, h * LANES:(h + 1) * LANES]
            if latent:
                qh = _rope(qh, cosb_ref[...], sinb_ref[...], B_ROPE_DIM // 4)
            qb_ref[:, h * LANES:(h + 1) * LANES] = (qh * (B_SCALE * LOG2E)).astype(BF16)
            kb_ref[:, h * LANES:(h + 1) * LANES] = (kn[:, h * LANES:(h + 1) * LANES] + krp).astype(BF16)

    def ffn(hooks):
        x = x_ref[...]
        h1 = _mod_norm(x, n1_ref[...], mods_ref, 0).astype(BF16)
        x1 = x + 0.5 * _gate(mods_ref, 0) * _ffn(h1, wgu_ref, wd_ref, hooks)
        x1_ref[...] = x1
        x1s_ref[...] = x1

    @pl.when(i == 0)
    def _():
        ffn(None)

    @pl.when(jnp.logical_and(i > 0, i < n_tiles))
    def _():
        ffn({PRE_HOOK_IN_PROJ: in_proj, PRE_HOOK_UP_PROJ: up_proj})

    @pl.when(i == n_tiles)
    def _():
        in_proj()
        up_proj()


def _const_spec(shape):
    nd = len(shape)
    return pl.BlockSpec(shape, lambda i: (0,) * nd, pipeline_mode=pl.Buffered(1))


def _pre(x, mods3, w, latent, seq, rope):
    t = x.shape[0]
    tiles_per_batch = max(seq // TOKEN_TILE, 1)
    tm = TOKEN_TILE
    n_tiles = t // tm
    cur = lambda i: jnp.minimum(i, n_tiles - 1)
    prev = lambda i: jnp.maximum(i - 1, 0)
    mod_of = (lambda j: 1 + j // tiles_per_batch) if latent else (lambda j: 0)
    cur_rows = lambda i: (cur(i), 0)
    prev_rows = lambda i: (prev(i), 0)

    weights = [w["n1"], w["wgu1"], w["wd1"], w["nm"], w["win_a"], w["qn"], w["kvn"], w["wuq"], w["wuk"],
               w["wuv_t"] if latent else w["wuv"]]
    in_specs = [pl.BlockSpec((tm, D_MODEL), cur_rows),
                pl.BlockSpec((None, 1, N_MOD * D_MODEL), lambda i: (mod_of(cur(i)), 0, 0)),
                pl.BlockSpec((None, 1, N_MOD * D_MODEL), lambda i: (mod_of(prev(i)), 0, 0))]
    in_specs += [_const_spec(a.shape) for a in weights]
    args = [x, mods3, mods3] + weights
    if latent:
        in_specs += [pl.BlockSpec((tm, LANES), lambda i: (prev(i) % tiles_per_batch, 0))] * 4
        args += list(rope)

    out_specs = [pl.BlockSpec((tm, D_MODEL), cur_rows)]
    out_shape = [jax.ShapeDtypeStruct((t, D_MODEL), F32)]
    for wd in (A_Q_W, 4 * LANES, 2 * LANES, B_HEADS * LANES, B_HEADS * LANES):
        out_specs.append(pl.BlockSpec((tm, wd), prev_rows))
        out_shape.append(jax.ShapeDtypeStruct((t, wd), BF16))
    if latent:
        out_specs[3] = pl.BlockSpec((tm // LANES, A_KV_W, LANES), lambda i: (prev(i), 0, 0))
        out_shape[3] = jax.ShapeDtypeStruct((t // LANES, A_KV_W, LANES), BF16)
        out_specs.append(pl.BlockSpec((B_HEADS * B_V_DIM, tm), lambda i: (0, prev(i))))
        out_shape.append(jax.ShapeDtypeStruct((B_HEADS * B_V_DIM, t), BF16))
    else:
        out_specs.append(pl.BlockSpec((tm, B_HEADS * B_V_DIM), prev_rows))
        out_shape.append(jax.ShapeDtypeStruct((t, B_HEADS * B_V_DIM), BF16))
        nb = tm // seq
        for feat in (A_KV_W, A_KV_W):
            out_specs.append(pl.BlockSpec((nb, feat, seq), lambda i: (prev(i), 0, 0)))
            out_shape.append(jax.ShapeDtypeStruct((t // seq, feat, seq), F32))
        out_specs.append(pl.BlockSpec((tm, KV_LORA), prev_rows))
        out_shape.append(jax.ShapeDtypeStruct((t, KV_LORA), F32))
        out_specs.append(pl.BlockSpec((nb, B_ROPE_DIM, seq), lambda i: (prev(i), 0, 0)))
        out_shape.append(jax.ShapeDtypeStruct((t // seq, B_ROPE_DIM, seq), F32))

    return pl.pallas_call(
        functools.partial(_pre_kernel, latent, n_tiles),
        grid=(n_tiles + 1,),
        in_specs=in_specs,
        out_specs=out_specs,
        out_shape=out_shape,
        scratch_shapes=[pltpu.VMEM((tm, D_MODEL), F32)],
        compiler_params=pltpu.CompilerParams(dimension_semantics=("arbitrary",), vmem_limit_bytes=VMEM_LIMIT),
        name="pre_latent" if latent else "pre_context",
    )(*args)


def _cache_kernel(ckv_ref, krp_ref, ck_ref, cv_ref, wuk_ref, wuv_ref, kb_ref, vb_ref, kap_ref, vat_ref):
    c = ckv_ref[...].astype(BF16)
    kn = _dot(c, wuk_ref[...])
    krp = krp_ref[...]
    for h in range(B_HEADS):
        kb_ref[:, h * LANES:(h + 1) * LANES] = (kn[:, h * LANES:(h + 1) * LANES] + krp).astype(BF16)
    vb_ref[...] = _dot_nt(wuv_ref[...], c).astype(BF16)
    _store_gqa_k(ck_ref[...], _low_lanes(ck_ref.shape[0]), kap_ref)
    vat_ref[...] = cv_ref[...].T.astype(BF16)


def _cache_kv(ckv, krp, ck, cv, wuk, wuv_t):
    b, p, _ = ckv.shape
    blk = lambda wd: pl.BlockSpec((None, p, wd), lambda i: (i, 0, 0))
    vdim = B_HEADS * B_V_DIM
    return pl.pallas_call(
        _cache_kernel,
        grid=(b,),
        in_specs=[blk(KV_LORA), blk(LANES), blk(A_KV_W), blk(A_KV_W),
                  pl.BlockSpec(wuk.shape, lambda i: (0, 0)),
                  pl.BlockSpec(wuv_t.shape, lambda i: (0, 0))],
        out_specs=[blk(B_HEADS * LANES), pl.BlockSpec((vdim, p), lambda i: (0, i)), blk(4 * LANES),
                   pl.BlockSpec((A_KV_W, p), lambda i: (0, i))],
        out_shape=[jax.ShapeDtypeStruct((b, p, B_HEADS * LANES), BF16),
                   jax.ShapeDtypeStruct((vdim, b * p), BF16),
                   jax.ShapeDtypeStruct((b, p, 4 * LANES), BF16),
                   jax.ShapeDtypeStruct((A_KV_W, b * p), BF16)],
        compiler_params=pltpu.CompilerParams(dimension_semantics=("parallel",), vmem_limit_bytes=VMEM_LIMIT),
        name="cache_kv",
    )(ckv, krp, ck, cv, wuk, wuv_t)


def _att_ctx_kernel(sink_ref, qa_ref, kap_ref, vad_ref, qb_ref, kb_ref, vb_ref, oa_ref, ob_ref):
    low = _low_lanes(qa_ref.shape[0])
    n_a = A_HEADS

    def scores(u):
        if u < n_a:
            c, half = divmod(u, 2)
            kc = 2 * (c // 2) + half
            return _dot_nt(qa_ref[:, c * LANES:(c + 1) * LANES], kap_ref[:, kc * LANES:(kc + 1) * LANES])
        h = u - n_a
        return _dot_nt(qb_ref[:, h * LANES:(h + 1) * LANES], kb_ref[:, h * LANES:(h + 1) * LANES])

    nxt = scores(0)
    res = []
    for u in range(n_a + B_HEADS):
        s = nxt
        if u + 1 < n_a + B_HEADS:
            nxt = scores(u + 1)
        if u < n_a:
            kv = u // 4
            res.append(_attend([s], [vad_ref[:, kv * LANES:(kv + 1) * LANES]], sink_ref[u] * LOG2E))
            out_ref, c = oa_ref, u // 2
        else:
            c = (u - n_a) // 2
            res.append(_attend([s], [vb_ref[:, c * LANES:(c + 1) * LANES]], None))
            out_ref = ob_ref
        if u % 2 == 1:
            out_ref[:, c * LANES:(c + 1) * LANES] = jnp.where(low, res[u - 1], res[u]).astype(BF16)


def _att_ctx(sink, qa, kap, vad, qb, kb, vb, seq):
    t = qa.shape[0]
    blk = lambda wd: pl.BlockSpec((seq, wd), lambda b: (b, 0))
    return pl.pallas_call(
        _att_ctx_kernel,
        grid=(t // seq,),
        in_specs=[pl.BlockSpec(memory_space=pltpu.SMEM),
                  blk(qa.shape[1]), blk(kap.shape[1]), blk(vad.shape[1]),
                  blk(qb.shape[1]), blk(kb.shape[1]), blk(vb.shape[1])],
        out_specs=[blk(A_Q_W), blk(B_HEADS * B_V_DIM)],
        out_shape=[jax.ShapeDtypeStruct((t, A_Q_W), BF16), jax.ShapeDtypeStruct((t, B_HEADS * B_V_DIM), BF16)],
        compiler_params=pltpu.CompilerParams(dimension_semantics=("parallel",), vmem_limit_bytes=VMEM_LIMIT),
        name="att_context",
    )(sink, qa, kap, vad, qb, kb, vb)


def _att_win_kernel(n_lat, sink_ref, qa_ref, kap_ref, vat_ref, kapc_ref, vatc_ref, oa_ref):
    nq = 2 * Q_BLOCK
    past = kapc_ref.shape[0]
    upper = lax.broadcasted_iota(jnp.int32, (1, nq), 1) >= Q_BLOCK
    ones_band = jnp.ones((SUM_ROWS, BAND), BF16)
    ones_ctx = jnp.ones((SUM_ROWS, past), BF16)
    key = lax.broadcasted_iota(jnp.int32, (BAND, nq), 0)
    qry = lax.broadcasted_iota(jnp.int32, (BAND, nq), 1) & (Q_BLOCK - 1)
    key_minus_qry = key - qry

    starts, valids, v_bands = [], [], []
    for qb in range(WIN_BLOCKS):
        i = pl.program_id(1) * WIN_BLOCKS + qb
        start = pl.multiple_of(jnp.clip(i * Q_BLOCK - Q_BLOCK, 0, n_lat - BAND), Q_BLOCK)
        starts.append(start)
        valids.append(jnp.abs(key_minus_qry + (start - i * Q_BLOCK)) <= WINDOW)
        v_bands.append(vat_ref[pl.ds(start // LANES, BAND // LANES)])

    units = [(qb, kv, half) for qb in range(WIN_BLOCKS) for kv in range(A_KV_HEADS) for half in range(2)]

    def score_blocks(unit):
        qb, kv, half = unit
        u = 2 * kv + half
        rows = slice(qb * Q_BLOCK, (qb + 1) * Q_BLOCK)
        q2 = jnp.concatenate([qa_ref[rows, (2 * kv) * LANES:(2 * kv + 1) * LANES],
                              qa_ref[rows, (2 * kv + 1) * LANES:(2 * kv + 2) * LANES]], axis=0)
        s_band = _dot_nt(kap_ref[pl.ds(starts[qb], BAND), u * LANES:(u + 1) * LANES], q2)
        s_ctx = _dot_nt(kapc_ref[:, u * LANES:(u + 1) * LANES], q2)
        return [jnp.where(valids[qb], s_band, NEG), s_ctx]

    def value_products(unit, p_blocks):
        qb, kv, _ = unit
        rows = slice(kv * A_HEAD_DIM, (kv + 1) * A_HEAD_DIM)
        vb = jnp.concatenate([v_bands[qb][r][rows] for r in range(BAND // LANES)], axis=1)
        r_band = _dot(jnp.concatenate([vb, ones_band], axis=0), p_blocks[0])
        r_ctx = _dot(jnp.concatenate([vatc_ref[rows, :], ones_ctx], axis=0), p_blocks[1])
        return r_band + r_ctx

    scores, probs, outs = {}, {}, {}
    for t in range(len(units) + 2):
        if t < len(units):
            scores[t] = score_blocks(units[t])
        if 0 <= t - 2 < len(units):
            n = t - 2
            qb, kv, half = units[n]
            p_blocks, p_sink = probs.pop(n)
            acc = value_products(units[n], p_blocks)
            outs[n] = acc[:A_HEAD_DIM] / (acc[A_HEAD_DIM:A_HEAD_DIM + 1] + p_sink)
            if half == 1:
                for pr in range(2):
                    cols = slice(pr * Q_BLOCK, (pr + 1) * Q_BLOCK)
                    pair = jnp.concatenate([outs[n - 1][:, cols], outs[n][:, cols]], axis=0)
                    c = 2 * kv + pr
                    oa_ref[qb * Q_BLOCK:(qb + 1) * Q_BLOCK, c * LANES:(c + 1) * LANES] = pair.T.astype(BF16)
                del outs[n - 1], outs[n]
        if 0 <= t - 1 < len(units):
            n = t - 1
            _, kv, half = units[n]
            sink = jnp.where(upper, sink_ref[4 * kv + 2 + half], sink_ref[4 * kv + half]) * LOG2E
            probs[n] = _probs_keys_major(scores.pop(n), sink)


def _att_win(sink, qa, kap, vat, kapc, vatc):
    b, n, _ = qa.shape
    past = kapc.shape[1]
    tq = WIN_BLOCKS * Q_BLOCK
    return pl.pallas_call(
        functools.partial(_att_win_kernel, n),
        grid=(b, n // tq),
        in_specs=[pl.BlockSpec(memory_space=pltpu.SMEM),
                  pl.BlockSpec((None, tq, A_Q_W), lambda bi, i: (bi, i, 0)),
                  pl.BlockSpec((None, n, kap.shape[2]), lambda bi, i: (bi, 0, 0)),
                  pl.BlockSpec((n // LANES, A_KV_W, LANES), lambda bi, i: (bi, 0, 0)),
                  pl.BlockSpec((None, past, kapc.shape[2]), lambda bi, i: (bi, 0, 0)),
                  pl.BlockSpec((A_KV_W, past), lambda bi, i: (0, bi))],
        out_specs=pl.BlockSpec((None, tq, A_Q_W), lambda bi, i: (bi, i, 0)),
        out_shape=jax.ShapeDtypeStruct((b, n, A_Q_W), BF16),
        compiler_params=pltpu.CompilerParams(dimension_semantics=("parallel", "parallel"),
                                             vmem_limit_bytes=VMEM_LIMIT),
        name="att_window",
    )(sink, qa, kap, vat, kapc, vatc)


def _probs_keys_major(scores_t, sink=None):
    m = None
    for s in scores_t:
        mx = jnp.max(s, axis=0, keepdims=True)
        m = mx if m is None else jnp.maximum(m, mx)
    if sink is not None:
        m = jnp.maximum(m, sink)
    probs = [jnp.exp2(s - m).astype(BF16) for s in scores_t]
    return probs, (None if sink is None else jnp.exp2(sink - m))


def _att_mla_kernel(n_cast, qb_ref, kbl_ref, vtl_ref, kbc_ref, vtc_ref, *rest):
    w32_refs, ob_ref, w16_refs = rest[:n_cast], rest[n_cast], rest[n_cast + 1:]
    for src, dst in zip(w32_refs, w16_refs):
        dst[...] = src[...].astype(BF16)

    kb = MLA_KEY_BLOCK
    ones = jnp.ones((SUM_ROWS, kb), BF16)
    blocks = [(kbc_ref, vtc_ref, j) for j in range(kbc_ref.shape[0] // kb)]
    blocks += [(kbl_ref, vtl_ref, j) for j in range(kbl_ref.shape[0] // kb)]

    def score_block(h, blk):
        k_ref, _, j = blk
        return _dot_nt(k_ref[j * kb:(j + 1) * kb, h * LANES:(h + 1) * LANES], qb_ref[:, h * LANES:(h + 1) * LANES])

    def value_block(h, blk, p):
        _, v_ref, j = blk
        vt = jnp.concatenate([v_ref[h * B_V_DIM:(h + 1) * B_V_DIM, j * kb:(j + 1) * kb], ones], axis=0)
        return _dot(vt, p)

    scores, probs, outs = {}, {}, {}
    for t in range(B_HEADS + 2):
        acc = None
        new_scores = []
        for bi, blk in enumerate(blocks):
            if t < B_HEADS:
                new_scores.append(score_block(t, blk))
            if 0 <= t - 2 < B_HEADS:
                r = value_block(t - 2, blk, probs[t - 2][bi])
                acc = r if acc is None else acc + r
        if t < B_HEADS:
            scores[t] = new_scores
        if 0 <= t - 1 < B_HEADS:
            probs[t - 1], _ = _probs_keys_major(scores.pop(t - 1))
        if acc is not None:
            h = t - 2
            probs.pop(h)
            outs[h] = acc[:B_V_DIM] / acc[B_V_DIM:B_V_DIM + 1]
            if h % 2 == 1:
                c = h // 2
                pair = jnp.concatenate([outs.pop(h - 1), outs.pop(h)], axis=0)
                ob_ref[:, c * LANES:(c + 1) * LANES] = pair.T.astype(BF16)


def _att_mla(qb, kbl, vtl, kbc, vtc, cast_weights):
    b, n, _ = qb.shape
    past = kbc.shape[1]
    tq = MLA_Q_TILE
    vdim = B_HEADS * B_V_DIM
    q_tiles = n // tq
    steps = b * q_tiles
    w_specs = []
    for wm in cast_weights:
        rows, cols = wm.shape
        assert rows % (steps * 16) == 0, "row blocks must be whole packed-bf16 sublane tiles"
        w_specs.append(pl.BlockSpec((rows // steps, cols), lambda bi, i: (bi * q_tiles + i, 0)))
    out = pl.pallas_call(
        functools.partial(_att_mla_kernel, len(cast_weights)),
        grid=(b, q_tiles),
        in_specs=[pl.BlockSpec((None, tq, qb.shape[2]), lambda bi, i: (bi, i, 0)),
                  pl.BlockSpec((None, n, kbl.shape[2]), lambda bi, i: (bi, 0, 0)),
                  pl.BlockSpec((vdim, n), lambda bi, i: (0, bi)),
                  pl.BlockSpec((None, past, kbc.shape[2]), lambda bi, i: (bi, 0, 0)),
                  pl.BlockSpec((vdim, past), lambda bi, i: (0, bi))] + w_specs,
        out_specs=[pl.BlockSpec((None, tq, vdim), lambda bi, i: (bi, i, 0))] + w_specs,
        out_shape=[jax.ShapeDtypeStruct((b, n, vdim), BF16)]
        + [jax.ShapeDtypeStruct(wm.shape, BF16) for wm in cast_weights],
        compiler_params=pltpu.CompilerParams(dimension_semantics=("parallel", "parallel"),
                                             vmem_limit_bytes=VMEM_LIMIT),
        name="att_mla",
    )(qb, kbl, vtl, kbc, vtc, *cast_weights)
    return out[0], out[1:]


def _post_kernel(x1_ref, mods_ref, oa_ref, ob_ref, nm_ref, wing_ref, woa_ref, wob_ref, wout_ref,
                 n2_ref, wgu_ref, wd_ref, nf_ref, y_ref):
    x1 = x1_ref[...]
    h2 = _mod_norm(x1, nm_ref[...], mods_ref, 1).astype(BF16)
    g = _dot(h2, wing_ref[...])
    m = (jax.nn.sigmoid(g[:, :D_MODEL]) * _dot(oa_ref[...], woa_ref[...])
         + jax.nn.sigmoid(g[:, D_MODEL:]) * _dot(ob_ref[...], wob_ref[...]))
    x2 = x1 + _gate(mods_ref, 1) * _dot(m.astype(BF16), wout_ref[...])
    h3 = _mod_norm(x2, n2_ref[...], mods_ref, 2).astype(BF16)
    x3 = x2 + 0.5 * _gate(mods_ref, 2) * _ffn(h3, wgu_ref, wd_ref)
    y_ref[...] = _rms(x3, nf_ref[...])


def _post(x1, mods3, oa, ob, w, latent, tiles_per_batch):
    t = x1.shape[0]
    tm = TOKEN_TILE
    if latent:
        mod_row = lambda i: (1 + i // tiles_per_batch, 0, 0)
    else:
        mod_row = lambda i: (0, 0, 0)
    row_blk = lambda i: (i, 0)
    weights = [w["nm"], w["win_g"], w["woa"], w["wob"], w["wout"], w["n2"], w["wgu2"], w["wd2"], w["nf"]]
    in_specs = [pl.BlockSpec((tm, D_MODEL), row_blk),
                pl.BlockSpec((None, 1, N_MOD * D_MODEL), mod_row),
                pl.BlockSpec((tm, A_Q_W), row_blk),
                pl.BlockSpec((tm, B_HEADS * B_V_DIM), row_blk)]
    in_specs += [_const_spec(a.shape) for a in weights]
    return pl.pallas_call(
        _post_kernel,
        grid=(t // tm,),
        in_specs=in_specs,
        out_specs=pl.BlockSpec((tm, D_MODEL), row_blk),
        out_shape=jax.ShapeDtypeStruct((t, D_MODEL), F32),
        compiler_params=pltpu.CompilerParams(dimension_semantics=("parallel",), vmem_limit_bytes=VMEM_LIMIT),
        name="post_latent" if latent else "post_context",
    )(x1, mods3, oa, ob, *weights)


def _rope_tables(n):
    rows = n // GRID_W
    t_row = jnp.repeat(jnp.arange(rows, dtype=F32), GRID_W)
    t_col = jnp.tile(jnp.arange(GRID_W, dtype=F32), rows)

    def angles(d_rot):
        d_half = d_rot // 2
        inv = 1.0 / (ROPE_THETA ** (jnp.arange(0, d_half, 2, dtype=F32) / d_half))
        ar = t_row[:, None] * inv[None, :]
        ac = t_col[:, None] * inv[None, :]
        return jnp.concatenate([ar, ar, ac, ac], axis=-1)

    def signed(sin, d_rot):
        q = d_rot // 4
        sign = jnp.where((jnp.arange(d_rot) % (2 * q)) < q, -1.0, 1.0).astype(F32)
        return sin * sign[None, :]

    ang_a = angles(A_HEAD_DIM)
    cos_a = jnp.tile(jnp.cos(ang_a), (1, LANES // A_HEAD_DIM))
    sin_a = jnp.tile(signed(jnp.sin(ang_a), A_HEAD_DIM), (1, LANES // A_HEAD_DIM))
    ang_b = angles(B_ROPE_DIM)
    pad_l = B_NOPE_DIM
    pad_r = LANES - B_NOPE_DIM - B_ROPE_DIM
    cos_b = jnp.pad(jnp.cos(ang_b), ((0, 0), (pad_l, pad_r)), constant_values=1.0)
    sin_b = jnp.pad(signed(jnp.sin(ang_b), B_ROPE_DIM), ((0, 0), (pad_l, pad_r)))
    return cos_a, sin_a, cos_b, sin_b


def _prep_weights(ffn1_norm, ffn1_w_gu, ffn1_w_down, mix_norm, w_in, q_lat_norm, kv_lat_norm, w_uq, w_ukv,
                  w_o_a, w_o_b, w_out, ffn2_norm, ffn2_w_gu, ffn2_w_down, final_norm):
    win = w_in[0]
    n_attn = A_Q_W + 2 * A_KV_W + Q_LORA + KV_LORA
    krope_cols = jnp.pad(win[:, n_attn:n_attn + B_ROPE_DIM],
                         ((0, 0), (B_NOPE_DIM, LANES - B_NOPE_DIM - B_ROPE_DIM)))
    win_a = jnp.concatenate([win[:, :n_attn], krope_cols], axis=1).astype(BF16)
    win_g = win[:, n_attn + B_ROPE_DIM:].astype(BF16)

    wuq = jnp.pad(w_uq[0].reshape(Q_LORA, B_HEADS, B_QK_DIM),
                  ((0, 0), (0, 0), (0, LANES - B_QK_DIM))).reshape(Q_LORA, B_HEADS * LANES).astype(BF16)
    wukv = w_ukv[0].reshape(KV_LORA, B_HEADS, B_NOPE_DIM + B_V_DIM)
    wuk = jnp.pad(wukv[:, :, :B_NOPE_DIM],
                  ((0, 0), (0, 0), (0, LANES - B_NOPE_DIM))).reshape(KV_LORA, B_HEADS * LANES).astype(BF16)
    wuv = wukv[:, :, B_NOPE_DIM:].reshape(KV_LORA, B_HEADS * B_V_DIM).astype(BF16)

    return {
        "n1": ffn1_norm, "wgu1": ffn1_w_gu[0].astype(BF16), "wd1": ffn1_w_down[0].astype(BF16),
        "nm": mix_norm, "win_a": win_a, "win_g": win_g,
        "qn": q_lat_norm, "kvn": kv_lat_norm, "wuq": wuq, "wuk": wuk, "wuv": wuv, "wuv_t": wuv.T,
        "n2": ffn2_norm, "nf": final_norm.reshape(1, D_MODEL),
        "post_f32": [w_o_a[0], w_o_b[0], w_out[0], ffn2_w_gu[0], ffn2_w_down[0]],
    }


def kernel(x_prompt, x_sample, cache_attn_k, cache_attn_v, cache_mla_ckv, cache_mla_krope, c, c_ctx, ada_w, ada_b, ffn1_norm, ffn1_w_gu, ffn1_w_down, mix_norm, w_in, attn_sink, q_lat_norm, kv_lat_norm, w_uq, w_ukv, w_o_a, w_o_b, w_out, ffn2_norm, ffn2_w_gu, ffn2_w_down, final_norm):
    assert ada_w.shape[0] == 1, "single trunk layer"
    bp, sp, d = x_prompt.shape
    bs, ns, _ = x_sample.shape
    past = cache_attn_k.shape[2]
    assert d == D_MODEL and bs + 1 <= MOD_ROWS
    assert TOKEN_TILE % sp == 0 and (bp * sp) % TOKEN_TILE == 0
    assert ns % TOKEN_TILE == 0 and ns % MLA_Q_TILE == 0 and ns % (WIN_BLOCKS * Q_BLOCK) == 0 and ns >= BAND

    w = _prep_weights(ffn1_norm, ffn1_w_gu, ffn1_w_down, mix_norm, w_in, q_lat_norm, kv_lat_norm, w_uq, w_ukv,
                      w_o_a, w_o_b, w_out, ffn2_norm, ffn2_w_gu, ffn2_w_down, final_norm)
    sink = attn_sink[0]

    cvec = jnp.concatenate([c_ctx[None, :], c, jnp.zeros((MOD_ROWS - 1 - bs, d), F32)], axis=0)
    mods = _ada_mods(cvec, ada_w[0], ada_b)
    mods3 = mods.reshape(MOD_ROWS, 1, N_MOD * D_MODEL)

    xp = x_prompt.reshape(bp * sp, d)
    (x1p, qa, kap, vad, qb, kb, vb, k_t, v_t, ckv_n, krope_t) = _pre(xp, mods3, w, False, sp, None)
    oa_p, ob_p = _att_ctx(sink, qa, kap, vad, qb, kb, vb, sp)

    tiles_per_batch = ns // TOKEN_TILE
    xs = x_sample.reshape(bs * ns, d)
    (x1s, qa, kap, vad, qb, kb, vb_t) = _pre(xs, mods3, w, True, ns, _rope_tables(ns))
    krp_c = jnp.pad(cache_mla_krope[:, 0], ((0, 0), (0, 0), (B_NOPE_DIM, LANES - B_NOPE_DIM - B_ROPE_DIM)))
    kbc, vbc, kapc, vadc = _cache_kv(
        cache_mla_ckv[:, 0], krp_c,
        cache_attn_k[:, 0].reshape(bs, past, A_KV_W), cache_attn_v[:, 0].reshape(bs, past, A_KV_W),
        w["wuk"], w["wuv_t"])
    r3 = lambda a: a.reshape(bs, ns, a.shape[1])
    oa = _att_win(sink, r3(qa), r3(kap), vad, kapc, vadc)
    ob, (w["woa"], w["wob"], w["wout"], w["wgu2"], w["wd2"]) = _att_mla(
        r3(qb), r3(kb), vb_t, kbc, vbc, w["post_f32"])

    y_prompt = _post(x1p, mods3, oa_p, ob_p, w, False, 1).reshape(bp, sp, d)
    y_sample = _post(x1s, mods3, oa.reshape(bs * ns, A_Q_W), ob.reshape(bs * ns, B_HEADS * B_V_DIM),
                     w, True, tiles_per_batch).reshape(bs, ns, d)

    new_attn_k = k_t.reshape(bp, 1, A_KV_HEADS, A_HEAD_DIM, sp).transpose(0, 1, 4, 2, 3)
    new_attn_v = v_t.reshape(bp, 1, A_KV_HEADS, A_HEAD_DIM, sp).transpose(0, 1, 4, 2, 3)
    new_mla_ckv = ckv_n.reshape(bp, 1, sp, KV_LORA)
    new_mla_krope = krope_t.reshape(bp, 1, B_ROPE_DIM, sp).transpose(0, 1, 3, 2)
    return (y_prompt, y_sample, new_attn_k, new_attn_v, new_mla_ckv, new_mla_krope)
```

```python
import functools

import jax
import jax.numpy as jnp
import numpy as np
from jax import lax
from jax.experimental import pallas as pl
from jax.experimental.pallas import tpu as pltpu

F32 = jnp.float32
BF16 = jnp.bfloat16

D_MODEL = 1024
N_MOD = 9
GRID_W = 64
WINDOW = 128
A_HEADS = 8
A_KV_HEADS = 2
A_HEAD_DIM = 64
A_Q_W = A_HEADS * A_HEAD_DIM
A_KV_W = A_KV_HEADS * A_HEAD_DIM
B_HEADS = 8
B_NOPE_DIM = 64
B_ROPE_DIM = 32
B_V_DIM = 64
B_QK_DIM = B_NOPE_DIM + B_ROPE_DIM
Q_LORA = 256
KV_LORA = 256
D_FF = 2816
ROPE_THETA = 10000.0
EPS = 1e-6
NEG = -1e30
A_SCALE = A_HEAD_DIM ** -0.5
B_SCALE = B_QK_DIM ** -0.5
LOG2E = 1.4426950408889634

LANES = 128
HALF = LANES // 2
FF_CHUNK = 256
N_FF_CHUNKS = D_FF // FF_CHUNK
TOKEN_TILE = 512
ADA_TILE = 1024
Q_BLOCK = 128
BAND = 3 * Q_BLOCK
CTX_BATCHES = 4
WIN_BLOCKS = 4
SUM_ROWS = 16
MLA_Q_TILE = 512
MLA_KEY_BLOCK = 512
MOD_ROWS = 8
VMEM_LIMIT = 56 * 1024 * 1024

IN_A_W = A_Q_W + 2 * A_KV_W + Q_LORA + KV_LORA + LANES
OFF_KA = A_Q_W
OFF_VA = OFF_KA + A_KV_W
OFF_QLAT = OFF_VA + A_KV_W
OFF_CKV = OFF_QLAT + Q_LORA
OFF_KROPE = OFF_CKV + KV_LORA


def _dot(a, b):
    return jnp.dot(a, b, preferred_element_type=F32)


def _dot_nt(a, b):
    return lax.dot_general(a, b, (((1,), (1,)), ((), ())), preferred_element_type=F32)


def _rms(x, g):
    ms = jnp.mean(x * x, axis=-1, keepdims=True)
    return x * lax.rsqrt(ms + EPS) * g


def _mod_norm(x, g, mods_ref, k):
    shift = mods_ref[:, (3 * k) * D_MODEL:(3 * k + 1) * D_MODEL]
    scale = mods_ref[:, (3 * k + 1) * D_MODEL:(3 * k + 2) * D_MODEL]
    return _rms(x, g) * (1.0 + scale) + shift


def _gate(mods_ref, k):
    return mods_ref[:, (3 * k + 2) * D_MODEL:(3 * k + 3) * D_MODEL]


def _ffn(h, wgu_ref, wd_ref, hooks=None):
    def gate_up(c):
        a = _dot(h, wgu_ref[:, c * FF_CHUNK:(c + 1) * FF_CHUNK])
        u = _dot(h, wgu_ref[:, D_FF + c * FF_CHUNK:D_FF + (c + 1) * FF_CHUNK])
        return a, u

    acc = None
    nxt = gate_up(0)
    for c in range(N_FF_CHUNKS):
        a, u = nxt
        if c + 1 < N_FF_CHUNKS:
            nxt = gate_up(c + 1)
        act = (a * jax.nn.sigmoid(a) * u).astype(BF16)
        d = _dot(act, wd_ref[c * FF_CHUNK:(c + 1) * FF_CHUNK, :])
        acc = d if acc is None else acc + d
        if hooks and c in hooks:
            hooks[c]()
    return acc


def _rope(x, cos, sin_signed, dist):
    lane = lax.broadcasted_iota(jnp.int32, x.shape, 1)
    first = (lane & (2 * dist - 1)) < dist
    partner = jnp.where(first, pltpu.roll(x, LANES - dist, axis=1), pltpu.roll(x, dist, axis=1))
    return x * cos + partner * sin_signed


def _store_gqa_k(k, low, kap_ref):
    k_sw = pltpu.roll(k, HALF, axis=1)
    zero = jnp.zeros_like(k)
    kap_ref[:, 0 * LANES:1 * LANES] = jnp.where(low, k, zero).astype(BF16)
    kap_ref[:, 1 * LANES:2 * LANES] = jnp.where(low, zero, k_sw).astype(BF16)
    kap_ref[:, 2 * LANES:3 * LANES] = jnp.where(low, k_sw, zero).astype(BF16)
    kap_ref[:, 3 * LANES:4 * LANES] = jnp.where(low, zero, k).astype(BF16)


def _store_gqa_v_blocks(v, vt_ref):
    for r in range(vt_ref.shape[0]):
        vt_ref[r] = v[r * LANES:(r + 1) * LANES, :].T.astype(BF16)


def _store_seq_minor(out_ref, val):
    nb, feat, seq = out_ref.shape
    for bi in range(nb):
        out_ref[bi] = val[bi * seq:(bi + 1) * seq, :].T[:feat, :]


def _low_lanes(rows):
    return lax.broadcasted_iota(jnp.int32, (rows, LANES), 1) < HALF


def _ada_kernel(c_ref, w_ref, b_ref, o_ref):
    c = c_ref[...]
    s = (c * jax.nn.sigmoid(c)).astype(BF16)
    o_ref[...] = _dot(s, w_ref[...].astype(BF16)) + b_ref[...]


def _ada_mods(cvec, ada_w, ada_b):
    n = ada_w.shape[1]
    return pl.pallas_call(
        _ada_kernel,
        grid=(n // ADA_TILE,),
        in_specs=[
            pl.BlockSpec((MOD_ROWS, D_MODEL), lambda j: (0, 0)),
            pl.BlockSpec((D_MODEL, ADA_TILE), lambda j: (0, j)),
            pl.BlockSpec((1, ADA_TILE), lambda j: (0, j)),
        ],
        out_specs=pl.BlockSpec((MOD_ROWS, ADA_TILE), lambda j: (0, j)),
        out_shape=jax.ShapeDtypeStruct((MOD_ROWS, n), F32),
        compiler_params=pltpu.CompilerParams(dimension_semantics=("parallel",), vmem_limit_bytes=VMEM_LIMIT),
        name="ada_mods",
    )(cvec, ada_w, ada_b)


PRE_HOOK_IN_PROJ = 1
PRE_HOOK_UP_PROJ = 6


def _pre_kernel(latent, n_tiles, *refs):
    x1s_ref = refs[-1]
    if latent:
        (x_ref, mods_ref, modsp_ref, n1_ref, wgu_ref, wd_ref, nm_ref, win_ref, qn_ref, kvn_ref, wuq_ref, wuk_ref,
         wuv_ref, cosa_ref, sina_ref, cosb_ref, sinb_ref,
         x1_ref, qa_ref, kap_ref, vad_ref, qb_ref, kb_ref, vb_ref) = refs[:-1]
    else:
        (x_ref, mods_ref, modsp_ref, n1_ref, wgu_ref, wd_ref, nm_ref, win_ref, qn_ref, kvn_ref, wuq_ref, wuk_ref,
         wuv_ref,
         x1_ref, qa_ref, kap_ref, qb_ref, kb_ref, vb_ref,
         knat_ref, vnat_ref, ckvn_ref, krope_ref) = refs[:-1]
    i = pl.program_id(0)
    carry = {}

    def in_proj():
        h2 = _mod_norm(x1s_ref[...], nm_ref[...], modsp_ref, 1).astype(BF16)
        z = _dot_nt(h2, win_ref[...])
        low = _low_lanes(z.shape[0])

        for c in range(A_Q_W // LANES):
            q = z[:, c * LANES:(c + 1) * LANES]
            if latent:
                q = _rope(q, cosa_ref[...], sina_ref[...], A_HEAD_DIM // 4)
            qa_ref[:, c * LANES:(c + 1) * LANES] = (q * (A_SCALE * LOG2E)).astype(BF16)
        k = z[:, OFF_KA:OFF_KA + LANES]
        v = z[:, OFF_VA:OFF_VA + LANES]
        if latent:
            k = _rope(k, cosa_ref[...], sina_ref[...], A_HEAD_DIM // 4)
            _store_gqa_v_blocks(v, vad_ref)
        else:
            _store_seq_minor(knat_ref, k)
            _store_seq_minor(vnat_ref, v)
        _store_gqa_k(k, low, kap_ref)

        carry["q_lat"] = _rms(z[:, OFF_QLAT:OFF_QLAT + Q_LORA], qn_ref[...]).astype(BF16)
        ckv_n = _rms(z[:, OFF_CKV:OFF_CKV + KV_LORA], kvn_ref[...])
        krp = z[:, OFF_KROPE:OFF_KROPE + LANES]
        if latent:
            krp = _rope(krp, cosb_ref[...], sinb_ref[...], B_ROPE_DIM // 4)
        else:
            ckvn_ref[...] = ckv_n
            _store_seq_minor(krope_ref, pltpu.roll(krp, HALF, axis=1))
        carry["ckv_b"] = ckv_n.astype(BF16)
        carry["krp"] = krp

    def up_proj():
        q_lat, ckv_b, krp = carry["q_lat"], carry["ckv_b"], carry["krp"]
        qb = _dot(q_lat, wuq_ref[...])
        kn = _dot(ckv_b, wuk_ref[...])
        vb_ref[...] = _dot_nt(wuv_ref[...], ckv_b).astype(BF16)
        for h in range(B_HEADS):
            qh = qb[:, h * LANES:(h + 1) * LANES]
            if latent:
                qh = _rope(qh, cosb_ref[...], sinb_ref[...], B_ROPE_DIM // 4)
            qb_ref[:, h * LANES:(h + 1) * LANES] = (qh * (B_SCALE * LOG2E)).astype(BF16)
            kb_ref[:, h * LANES:(h + 1) * LANES] = (kn[:, h * LANES:(h + 1) * LANES] + krp).astype(BF16)

    def ffn(hooks):
        x = x_ref[...]
        h1 = _mod_norm(x, n1_ref[...], mods_ref, 0).astype(BF16)
        x1 = x + 0.5 * _gate(mods_ref, 0) * _ffn(h1, wgu_ref, wd_ref, hooks)
        x1_ref[...] = x1
        x1s_ref[...] = x1

    @pl.when(i == 0)
    def _():
        ffn(None)

    @pl.when(jnp.logical_and(i > 0, i < n_tiles))
    def _():
        ffn({PRE_HOOK_IN_PROJ: in_proj, PRE_HOOK_UP_PROJ: up_proj})

    @pl.when(i == n_tiles)
    def _():
        in_proj()
        up_proj()


def _const_spec(shape):
    nd = len(shape)
    return pl.BlockSpec(shape, lambda i: (0,) * nd, pipeline_mode=pl.Buffered(1))


def _pre(x, mods3, w, latent, seq, rope):
    t = x.shape[0]
    tiles_per_batch = max(seq // TOKEN_TILE, 1)
    tm = TOKEN_TILE
    n_tiles = t // tm
    cur = lambda i: jnp.minimum(i, n_tiles - 1)
    prev = lambda i: jnp.maximum(i - 1, 0)
    mod_of = (lambda j: 1 + j // tiles_per_batch) if latent else (lambda j: 0)
    cur_rows = lambda i: (cur(i), 0)
    prev_rows = lambda i: (prev(i), 0)

    weights = [w["n1"], w["wgu1"], w["wd1"], w["nm"], w["win_a"], w["qn"], w["kvn"], w["wuq"], w["wuk"], w["wuv_t"]]
    in_specs = [pl.BlockSpec((tm, D_MODEL), cur_rows),
                pl.BlockSpec((None, 1, N_MOD * D_MODEL), lambda i: (mod_of(cur(i)), 0, 0)),
                pl.BlockSpec((None, 1, N_MOD * D_MODEL), lambda i: (mod_of(prev(i)), 0, 0))]
    in_specs += [_const_spec(a.shape) for a in weights]
    args = [x, mods3, mods3] + weights
    if latent:
        in_specs += [pl.BlockSpec((tm, LANES), lambda i: (prev(i) % tiles_per_batch, 0))] * 4
        args += list(rope)

    out_specs = [pl.BlockSpec((tm, D_MODEL), cur_rows)]
    out_shape = [jax.ShapeDtypeStruct((t, D_MODEL), F32)]
    for wd in (A_Q_W, 4 * LANES):
        out_specs.append(pl.BlockSpec((tm, wd), prev_rows))
        out_shape.append(jax.ShapeDtypeStruct((t, wd), BF16))
    if latent:
        out_specs.append(pl.BlockSpec((tm // LANES, A_KV_W, LANES), lambda i: (prev(i), 0, 0)))
        out_shape.append(jax.ShapeDtypeStruct((t // LANES, A_KV_W, LANES), BF16))
    for wd in (B_HEADS * LANES, B_HEADS * LANES):
        out_specs.append(pl.BlockSpec((tm, wd), prev_rows))
        out_shape.append(jax.ShapeDtypeStruct((t, wd), BF16))
    out_specs.append(pl.BlockSpec((B_HEADS * B_V_DIM, tm), lambda i: (0, prev(i))))
    out_shape.append(jax.ShapeDtypeStruct((B_HEADS * B_V_DIM, t), BF16))
    if not latent:
        nb = tm // seq
        for feat in (A_KV_W, A_KV_W):
            out_specs.append(pl.BlockSpec((nb, feat, seq), lambda i: (prev(i), 0, 0)))
            out_shape.append(jax.ShapeDtypeStruct((t // seq, feat, seq), F32))
        out_specs.append(pl.BlockSpec((tm, KV_LORA), prev_rows))
        out_shape.append(jax.ShapeDtypeStruct((t, KV_LORA), F32))
        out_specs.append(pl.BlockSpec((nb, B_ROPE_DIM, seq), lambda i: (prev(i), 0, 0)))
        out_shape.append(jax.ShapeDtypeStruct((t // seq, B_ROPE_DIM, seq), F32))

    return pl.pallas_call(
        functools.partial(_pre_kernel, latent, n_tiles),
        grid=(n_tiles + 1,),
        in_specs=in_specs,
        out_specs=out_specs,
        out_shape=out_shape,
        scratch_shapes=[pltpu.VMEM((tm, D_MODEL), F32)],
        compiler_params=pltpu.CompilerParams(dimension_semantics=("arbitrary",), vmem_limit_bytes=VMEM_LIMIT),
        name="pre_latent" if latent else "pre_context",
    )(*args)


def _cache_kernel(ckv_ref, kropet_ref, ckt_ref, cvt_ref, wuk_ref, wuv_ref, kb_ref, vb_ref, kap_ref, vat_ref):
    p = ckv_ref.shape[0]
    c = ckv_ref[...].astype(BF16)
    kn = _dot(c, wuk_ref[...])
    krp = jnp.concatenate([jnp.zeros((B_NOPE_DIM, p), F32), kropet_ref[...],
                           jnp.zeros((LANES - B_QK_DIM, p), F32)], axis=0).T
    for h in range(B_HEADS):
        kb_ref[:, h * LANES:(h + 1) * LANES] = (kn[:, h * LANES:(h + 1) * LANES] + krp).astype(BF16)
    vb_ref[...] = _dot_nt(wuv_ref[...], c).astype(BF16)
    _store_gqa_k(ckt_ref[...].T, _low_lanes(p), kap_ref)
    vat_ref[...] = cvt_ref[...].astype(BF16)


def _cache_kv(ckv, krope_t, ck_t, cv_t, wuk, wuv_t):
    b, p, _ = ckv.shape
    blk = lambda wd: pl.BlockSpec((None, p, wd), lambda i: (i, 0, 0))
    blk_t = lambda feat: pl.BlockSpec((None, feat, p), lambda i: (i, 0, 0))
    vdim = B_HEADS * B_V_DIM
    return pl.pallas_call(
        _cache_kernel,
        grid=(b,),
        in_specs=[blk(KV_LORA), blk_t(B_ROPE_DIM), blk_t(A_KV_W), blk_t(A_KV_W),
                  pl.BlockSpec(wuk.shape, lambda i: (0, 0)),
                  pl.BlockSpec(wuv_t.shape, lambda i: (0, 0))],
        out_specs=[blk(B_HEADS * LANES), pl.BlockSpec((vdim, p), lambda i: (0, i)), blk(4 * LANES),
                   pl.BlockSpec((A_KV_W, p), lambda i: (0, i))],
        out_shape=[jax.ShapeDtypeStruct((b, p, B_HEADS * LANES), BF16),
                   jax.ShapeDtypeStruct((vdim, b * p), BF16),
                   jax.ShapeDtypeStruct((b, p, 4 * LANES), BF16),
                   jax.ShapeDtypeStruct((A_KV_W, b * p), BF16)],
        compiler_params=pltpu.CompilerParams(dimension_semantics=("parallel",), vmem_limit_bytes=VMEM_LIMIT),
        name="cache_kv",
    )(ckv, krope_t, ck_t, cv_t, wuk, wuv_t)


def _att_ctx_kernel(seq, sink_ref, qa_ref, kap_ref, vt_ref, qb_ref, kb_ref, vbt_ref, oa_ref, ob_ref):
    nb = qa_ref.shape[0] // seq
    ones = jnp.ones((SUM_ROWS, seq), BF16)
    upper = lax.broadcasted_iota(jnp.int32, (1, 2 * seq), 1) >= seq
    units = []
    for bi in range(nb):
        units += [("a", bi, kv, half) for kv in range(A_KV_HEADS) for half in range(2)]
        units += [("b", bi, h, h % 2) for h in range(B_HEADS)]

    def score(unit):
        kind, bi, idx, half = unit
        rows = slice(bi * seq, (bi + 1) * seq)
        if kind == "a":
            kv, u = idx, 2 * idx + half
            q2 = jnp.concatenate([qa_ref[rows, (2 * kv) * LANES:(2 * kv + 1) * LANES],
                                  qa_ref[rows, (2 * kv + 1) * LANES:(2 * kv + 2) * LANES]], axis=0)
            return _dot_nt(kap_ref[rows, u * LANES:(u + 1) * LANES], q2)
        return _dot_nt(kb_ref[rows, idx * LANES:(idx + 1) * LANES], qb_ref[rows, idx * LANES:(idx + 1) * LANES])

    def values(unit):
        kind, bi, idx, _ = unit
        if kind == "a":
            v = vt_ref[bi, idx * A_HEAD_DIM:(idx + 1) * A_HEAD_DIM, :].astype(BF16)
        else:
            v = vbt_ref[idx * B_V_DIM:(idx + 1) * B_V_DIM, bi * seq:(bi + 1) * seq]
        return jnp.concatenate([v, ones], axis=0)

    def sink_row(unit):
        kind, _, kv, half = unit
        if kind == "b":
            return None
        return jnp.where(upper, sink_ref[4 * kv + 2 + half], sink_ref[4 * kv + half]) * LOG2E

    def finish_pair(first, second, o_first, o_second):
        kind, bi, idx, _ = second
        rows = slice(bi * seq, (bi + 1) * seq)
        if kind == "a":
            for pr in range(2):
                cols = slice(pr * seq, (pr + 1) * seq)
                pair = jnp.concatenate([o_first[:, cols], o_second[:, cols]], axis=0)
                c = 2 * idx + pr
                oa_ref[rows, c * LANES:(c + 1) * LANES] = pair.T.astype(BF16)
        else:
            c = idx // 2
            ob_ref[rows, c * LANES:(c + 1) * LANES] = jnp.concatenate([o_first, o_second], axis=0).T.astype(BF16)

    scores, probs, outs = {}, {}, {}
    for t in range(len(units) + 2):
        if t < len(units):
            scores[t] = score(units[t])
        if 0 <= t - 2 < len(units):
            n = t - 2
            p_blocks, p_sink = probs.pop(n)
            acc = _dot(values(units[n]), p_blocks[0])
            den = acc[A_HEAD_DIM:A_HEAD_DIM + 1]
            outs[n] = acc[:A_HEAD_DIM] / (den if p_sink is None else den + p_sink)
            if units[n][3] == 1:
                finish_pair(units[n - 1], units[n], outs.pop(n - 1), outs.pop(n))
        if 0 <= t - 1 < len(units):
            n = t - 1
            probs[n] = _probs_keys_major([scores.pop(n)], sink_row(units[n]))


def _att_ctx(sink, qa, kap, v_t, qb, kb, vb_t, seq):
    t = qa.shape[0]
    rows = CTX_BATCHES * seq
    blk = lambda wd: pl.BlockSpec((rows, wd), lambda b: (b, 0))
    return pl.pallas_call(
        functools.partial(_att_ctx_kernel, seq),
        grid=(t // rows,),
        in_specs=[pl.BlockSpec(memory_space=pltpu.SMEM),
                  blk(qa.shape[1]), blk(kap.shape[1]),
                  pl.BlockSpec((CTX_BATCHES, A_KV_W, seq), lambda b: (b, 0, 0)),
                  blk(qb.shape[1]), blk(kb.shape[1]),
                  pl.BlockSpec((vb_t.shape[0], rows), lambda b: (0, b))],
        out_specs=[blk(A_Q_W), blk(B_HEADS * B_V_DIM)],
        out_shape=[jax.ShapeDtypeStruct((t, A_Q_W), BF16), jax.ShapeDtypeStruct((t, B_HEADS * B_V_DIM), BF16)],
        compiler_params=pltpu.CompilerParams(dimension_semantics=("parallel",), vmem_limit_bytes=VMEM_LIMIT),
        name="att_context",
    )(sink, qa, kap, v_t, qb, kb, vb_t)


def _att_win_kernel(n_lat, sink_ref, qa_ref, kap_ref, vat_ref, kapc_ref, vatc_ref, oa_ref):
    nq = 2 * Q_BLOCK
    past = kapc_ref.shape[0]
    upper = lax.broadcasted_iota(jnp.int32, (1, nq), 1) >= Q_BLOCK
    ones_band = jnp.ones((SUM_ROWS, BAND), BF16)
    ones_ctx = jnp.ones((SUM_ROWS, past), BF16)
    key = lax.broadcasted_iota(jnp.int32, (BAND, nq), 0)
    qry = lax.broadcasted_iota(jnp.int32, (BAND, nq), 1) & (Q_BLOCK - 1)
    key_minus_qry = key - qry

    starts, valids, v_bands = [], [], []
    for qb in range(WIN_BLOCKS):
        i = pl.program_id(1) * WIN_BLOCKS + qb
        start = pl.multiple_of(jnp.clip(i * Q_BLOCK - Q_BLOCK, 0, n_lat - BAND), Q_BLOCK)
        starts.append(start)
        valids.append(jnp.abs(key_minus_qry + (start - i * Q_BLOCK)) <= WINDOW)
        v_bands.append(vat_ref[pl.ds(start // LANES, BAND // LANES)])

    units = [(qb, kv, half) for qb in range(WIN_BLOCKS) for kv in range(A_KV_HEADS) for half in range(2)]

    def score_blocks(unit):
        qb, kv, half = unit
        u = 2 * kv + half
        rows = slice(qb * Q_BLOCK, (qb + 1) * Q_BLOCK)
        q2 = jnp.concatenate([qa_ref[rows, (2 * kv) * LANES:(2 * kv + 1) * LANES],
                              qa_ref[rows, (2 * kv + 1) * LANES:(2 * kv + 2) * LANES]], axis=0)
        s_band = _dot_nt(kap_ref[pl.ds(starts[qb], BAND), u * LANES:(u + 1) * LANES], q2)
        s_ctx = _dot_nt(kapc_ref[:, u * LANES:(u + 1) * LANES], q2)
        return [jnp.where(valids[qb], s_band, NEG), s_ctx]

    def value_products(unit, p_blocks):
        qb, kv, _ = unit
        rows = slice(kv * A_HEAD_DIM, (kv + 1) * A_HEAD_DIM)
        vb = jnp.concatenate([v_bands[qb][r][rows] for r in range(BAND // LANES)], axis=1)
        r_band = _dot(jnp.concatenate([vb, ones_band], axis=0), p_blocks[0])
        r_ctx = _dot(jnp.concatenate([vatc_ref[rows, :], ones_ctx], axis=0), p_blocks[1])
        return r_band + r_ctx

    scores, probs, outs = {}, {}, {}
    for t in range(len(units) + 2):
        if t < len(units):
            scores[t] = score_blocks(units[t])
        if 0 <= t - 2 < len(units):
            n = t - 2
            qb, kv, half = units[n]
            p_blocks, p_sink = probs.pop(n)
            acc = value_products(units[n], p_blocks)
            outs[n] = acc[:A_HEAD_DIM] / (acc[A_HEAD_DIM:A_HEAD_DIM + 1] + p_sink)
            if half == 1:
                for pr in range(2):
                    cols = slice(pr * Q_BLOCK, (pr + 1) * Q_BLOCK)
                    pair = jnp.concatenate([outs[n - 1][:, cols], outs[n][:, cols]], axis=0)
                    c = 2 * kv + pr
                    oa_ref[qb * Q_BLOCK:(qb + 1) * Q_BLOCK, c * LANES:(c + 1) * LANES] = pair.T.astype(BF16)
                del outs[n - 1], outs[n]
        if 0 <= t - 1 < len(units):
            n = t - 1
            _, kv, half = units[n]
            sink = jnp.where(upper, sink_ref[4 * kv + 2 + half], sink_ref[4 * kv + half]) * LOG2E
            probs[n] = _probs_keys_major(scores.pop(n), sink)


def _att_win(sink, qa, kap, vat, kapc, vatc):
    b, n, _ = qa.shape
    past = kapc.shape[1]
    tq = WIN_BLOCKS * Q_BLOCK
    return pl.pallas_call(
        functools.partial(_att_win_kernel, n),
        grid=(b, n // tq),
        in_specs=[pl.BlockSpec(memory_space=pltpu.SMEM),
                  pl.BlockSpec((None, tq, A_Q_W), lambda bi, i: (bi, i, 0)),
                  pl.BlockSpec((None, n, kap.shape[2]), lambda bi, i: (bi, 0, 0)),
                  pl.BlockSpec((n // LANES, A_KV_W, LANES), lambda bi, i: (bi, 0, 0)),
                  pl.BlockSpec((None, past, kapc.shape[2]), lambda bi, i: (bi, 0, 0)),
                  pl.BlockSpec((A_KV_W, past), lambda bi, i: (0, bi))],
        out_specs=pl.BlockSpec((None, tq, A_Q_W), lambda bi, i: (bi, i, 0)),
        out_shape=jax.ShapeDtypeStruct((b, n, A_Q_W), BF16),
        compiler_params=pltpu.CompilerParams(dimension_semantics=("parallel", "parallel"),
                                             vmem_limit_bytes=VMEM_LIMIT),
        name="att_window",
    )(sink, qa, kap, vat, kapc, vatc)


def _probs_keys_major(scores_t, sink=None):
    m = None
    for s in scores_t:
        mx = jnp.max(s, axis=0, keepdims=True)
        m = mx if m is None else jnp.maximum(m, mx)
    if sink is not None:
        m = jnp.maximum(m, sink)
    probs = [jnp.exp2(s - m).astype(BF16) for s in scores_t]
    return probs, (None if sink is None else jnp.exp2(sink - m))


def _att_mla_kernel(n_cast, qb_ref, kbl_ref, vtl_ref, kbc_ref, vtc_ref, *rest):
    w32_refs, ob_ref, w16_refs = rest[:n_cast], rest[n_cast], rest[n_cast + 1:]
    for src, dst in zip(w32_refs, w16_refs):
        dst[...] = src[...].astype(BF16)

    kb = MLA_KEY_BLOCK
    ones = jnp.ones((SUM_ROWS, kb), BF16)
    blocks = [(kbc_ref, vtc_ref, j) for j in range(kbc_ref.shape[0] // kb)]
    blocks += [(kbl_ref, vtl_ref, j) for j in range(kbl_ref.shape[0] // kb)]

    def score_block(h, blk):
        k_ref, _, j = blk
        return _dot_nt(k_ref[j * kb:(j + 1) * kb, h * LANES:(h + 1) * LANES], qb_ref[:, h * LANES:(h + 1) * LANES])

    def value_block(h, blk, p):
        _, v_ref, j = blk
        vt = jnp.concatenate([v_ref[h * B_V_DIM:(h + 1) * B_V_DIM, j * kb:(j + 1) * kb], ones], axis=0)
        return _dot(vt, p)

    scores, probs, outs = {}, {}, {}
    for t in range(B_HEADS + 2):
        acc = None
        new_scores = []
        for bi, blk in enumerate(blocks):
            if t < B_HEADS:
                new_scores.append(score_block(t, blk))
            if 0 <= t - 2 < B_HEADS:
                r = value_block(t - 2, blk, probs[t - 2][bi])
                acc = r if acc is None else acc + r
        if t < B_HEADS:
            scores[t] = new_scores
        if 0 <= t - 1 < B_HEADS:
            probs[t - 1], _ = _probs_keys_major(scores.pop(t - 1))
        if acc is not None:
            h = t - 2
            probs.pop(h)
            outs[h] = acc[:B_V_DIM] / acc[B_V_DIM:B_V_DIM + 1]
            if h % 2 == 1:
                c = h // 2
                pair = jnp.concatenate([outs.pop(h - 1), outs.pop(h)], axis=0)
                ob_ref[:, c * LANES:(c + 1) * LANES] = pair.T.astype(BF16)


def _att_mla(qb, kbl, vtl, kbc, vtc, cast_weights):
    b, n, _ = qb.shape
    past = kbc.shape[1]
    tq = MLA_Q_TILE
    vdim = B_HEADS * B_V_DIM
    q_tiles = n // tq
    steps = b * q_tiles
    w_specs = []
    for wm in cast_weights:
        rows, cols = wm.shape
        assert rows % (steps * 16) == 0, "row blocks must be whole packed-bf16 sublane tiles"
        w_specs.append(pl.BlockSpec((rows // steps, cols), lambda bi, i: (bi * q_tiles + i, 0)))
    out = pl.pallas_call(
        functools.partial(_att_mla_kernel, len(cast_weights)),
        grid=(b, q_tiles),
        in_specs=[pl.BlockSpec((None, tq, qb.shape[2]), lambda bi, i: (bi, i, 0)),
                  pl.BlockSpec((None, n, kbl.shape[2]), lambda bi, i: (bi, 0, 0)),
                  pl.BlockSpec((vdim, n), lambda bi, i: (0, bi)),
                  pl.BlockSpec((None, past, kbc.shape[2]), lambda bi, i: (bi, 0, 0)),
                  pl.BlockSpec((vdim, past), lambda bi, i: (0, bi))] + w_specs,
        out_specs=[pl.BlockSpec((None, tq, vdim), lambda bi, i: (bi, i, 0))] + w_specs,
        out_shape=[jax.ShapeDtypeStruct((b, n, vdim), BF16)]
        + [jax.ShapeDtypeStruct(wm.shape, BF16) for wm in cast_weights],
        compiler_params=pltpu.CompilerParams(dimension_semantics=("parallel", "parallel"),
                                             vmem_limit_bytes=VMEM_LIMIT),
        name="att_mla",
    )(qb, kbl, vtl, kbc, vtc, *cast_weights)
    return out[0], out[1:]


def _post_kernel(x1_ref, mods_ref, oa_ref, ob_ref, nm_ref, wing_ref, woa_ref, wob_ref, wout_ref,
                 n2_ref, wgu_ref, wd_ref, nf_ref, y_ref):
    x1 = x1_ref[...]
    h2 = _mod_norm(x1, nm_ref[...], mods_ref, 1).astype(BF16)
    g = _dot_nt(h2, wing_ref[...])
    m = (jax.nn.sigmoid(g[:, :D_MODEL]) * _dot(oa_ref[...], woa_ref[...])
         + jax.nn.sigmoid(g[:, D_MODEL:]) * _dot(ob_ref[...], wob_ref[...]))
    x2 = x1 + _gate(mods_ref, 1) * _dot(m.astype(BF16), wout_ref[...])
    h3 = _mod_norm(x2, n2_ref[...], mods_ref, 2).astype(BF16)
    x3 = x2 + 0.5 * _gate(mods_ref, 2) * _ffn(h3, wgu_ref, wd_ref)
    y_ref[...] = _rms(x3, nf_ref[...])


def _post(x1, mods3, oa, ob, w, latent, tiles_per_batch):
    t = x1.shape[0]
    tm = TOKEN_TILE
    if latent:
        mod_row = lambda i: (1 + i // tiles_per_batch, 0, 0)
    else:
        mod_row = lambda i: (0, 0, 0)
    row_blk = lambda i: (i, 0)
    weights = [w["nm"], w["win_g"], w["woa"], w["wob"], w["wout"], w["n2"], w["wgu2"], w["wd2"], w["nf"]]
    in_specs = [pl.BlockSpec((tm, D_MODEL), row_blk),
                pl.BlockSpec((None, 1, N_MOD * D_MODEL), mod_row),
                pl.BlockSpec((tm, A_Q_W), row_blk),
                pl.BlockSpec((tm, B_HEADS * B_V_DIM), row_blk)]
    in_specs += [_const_spec(a.shape) for a in weights]
    return pl.pallas_call(
        _post_kernel,
        grid=(t // tm,),
        in_specs=in_specs,
        out_specs=pl.BlockSpec((tm, D_MODEL), row_blk),
        out_shape=jax.ShapeDtypeStruct((t, D_MODEL), F32),
        compiler_params=pltpu.CompilerParams(dimension_semantics=("parallel",), vmem_limit_bytes=VMEM_LIMIT),
        name="post_latent" if latent else "post_context",
    )(x1, mods3, oa, ob, *weights)


def _rope_tables(n):
    f32 = np.float32
    rows = n // GRID_W
    t_row = np.repeat(np.arange(rows, dtype=f32), GRID_W)
    t_col = np.tile(np.arange(GRID_W, dtype=f32), rows)

    def angles(d_rot):
        d_half = d_rot // 2
        inv = (f32(1.0) / np.power(f32(ROPE_THETA), np.arange(0, d_half, 2, dtype=f32) / f32(d_half))).astype(f32)
        ar = t_row[:, None] * inv[None, :]
        ac = t_col[:, None] * inv[None, :]
        return np.concatenate([ar, ar, ac, ac], axis=-1).astype(f32)

    def signed(sin, d_rot):
        q = d_rot // 4
        sign = np.where((np.arange(d_rot) % (2 * q)) < q, f32(-1.0), f32(1.0)).astype(f32)
        return sin * sign[None, :]

    ang_a = angles(A_HEAD_DIM)
    cos_a = np.tile(np.cos(ang_a), (1, LANES // A_HEAD_DIM))
    sin_a = np.tile(signed(np.sin(ang_a), A_HEAD_DIM), (1, LANES // A_HEAD_DIM))
    ang_b = angles(B_ROPE_DIM)
    pad = ((0, 0), (B_NOPE_DIM, LANES - B_NOPE_DIM - B_ROPE_DIM))
    cos_b = np.pad(np.cos(ang_b), pad, constant_values=1.0)
    sin_b = np.pad(signed(np.sin(ang_b), B_ROPE_DIM), pad)
    return tuple(jnp.asarray(t, dtype=F32) for t in (cos_a, sin_a, cos_b, sin_b))


def _prep_weights(ffn1_norm, ffn1_w_gu, ffn1_w_down, mix_norm, w_in, q_lat_norm, kv_lat_norm, w_uq, w_ukv,
                  w_o_a, w_o_b, w_out, ffn2_norm, ffn2_w_gu, ffn2_w_down, final_norm):
    win_t = jnp.swapaxes(w_in[0], 0, 1)
    n_attn = A_Q_W + 2 * A_KV_W + Q_LORA + KV_LORA
    krope_rows = jnp.pad(win_t[n_attn:n_attn + B_ROPE_DIM],
                         ((B_NOPE_DIM, LANES - B_NOPE_DIM - B_ROPE_DIM), (0, 0)))
    win_a = jnp.concatenate([win_t[:n_attn], krope_rows], axis=0).astype(BF16)
    win_g = win_t[n_attn + B_ROPE_DIM:].astype(BF16)

    wuq = jnp.pad(w_uq[0].reshape(Q_LORA, B_HEADS, B_QK_DIM),
                  ((0, 0), (0, 0), (0, LANES - B_QK_DIM))).reshape(Q_LORA, B_HEADS * LANES).astype(BF16)
    wukv = w_ukv[0].reshape(KV_LORA, B_HEADS, B_NOPE_DIM + B_V_DIM)
    wuk = jnp.pad(wukv[:, :, :B_NOPE_DIM],
                  ((0, 0), (0, 0), (0, LANES - B_NOPE_DIM))).reshape(KV_LORA, B_HEADS * LANES).astype(BF16)
    wuv = wukv[:, :, B_NOPE_DIM:].reshape(KV_LORA, B_HEADS * B_V_DIM).astype(BF16)

    return {
        "n1": ffn1_norm, "wgu1": ffn1_w_gu[0].astype(BF16), "wd1": ffn1_w_down[0].astype(BF16),
        "nm": mix_norm, "win_a": win_a, "win_g": win_g,
        "qn": q_lat_norm, "kvn": kv_lat_norm, "wuq": wuq, "wuk": wuk, "wuv_t": wuv.T,
        "n2": ffn2_norm, "nf": final_norm.reshape(1, D_MODEL),
        "post_f32": [w_o_a[0], w_o_b[0], w_out[0], ffn2_w_gu[0], ffn2_w_down[0]],
    }


def kernel(x_prompt, x_sample, cache_attn_k, cache_attn_v, cache_mla_ckv, cache_mla_krope, c, c_ctx, ada_w, ada_b, ffn1_norm, ffn1_w_gu, ffn1_w_down, mix_norm, w_in, attn_sink, q_lat_norm, kv_lat_norm, w_uq, w_ukv, w_o_a, w_o_b, w_out, ffn2_norm, ffn2_w_gu, ffn2_w_down, final_norm):
    assert ada_w.shape[0] == 1, "single trunk layer"
    bp, sp, d = x_prompt.shape
    bs, ns, _ = x_sample.shape
    past = cache_attn_k.shape[2]
    assert d == D_MODEL and bs + 1 <= MOD_ROWS
    assert TOKEN_TILE % sp == 0 and (bp * sp) % TOKEN_TILE == 0
    assert ns % TOKEN_TILE == 0 and ns % MLA_Q_TILE == 0 and ns % (WIN_BLOCKS * Q_BLOCK) == 0 and ns >= BAND

    w = _prep_weights(ffn1_norm, ffn1_w_gu, ffn1_w_down, mix_norm, w_in, q_lat_norm, kv_lat_norm, w_uq, w_ukv,
                      w_o_a, w_o_b, w_out, ffn2_norm, ffn2_w_gu, ffn2_w_down, final_norm)
    sink = attn_sink[0]

    cvec = jnp.concatenate([c_ctx[None, :], c, jnp.zeros((MOD_ROWS - 1 - bs, d), F32)], axis=0)
    mods = _ada_mods(cvec, ada_w[0], ada_b)
    mods3 = mods.reshape(MOD_ROWS, 1, N_MOD * D_MODEL)

    xp = x_prompt.reshape(bp * sp, d)
    (x1p, qa, kap, qb, kb, vb_t, k_t, v_t, ckv_n, krope_t) = _pre(xp, mods3, w, False, sp, None)
    oa_p, ob_p = _att_ctx(sink, qa, kap, v_t, qb, kb, vb_t, sp)

    tiles_per_batch = ns // TOKEN_TILE
    xs = x_sample.reshape(bs * ns, d)
    (x1s, qa, kap, vad, qb, kb, vb_t) = _pre(xs, mods3, w, True, ns, _rope_tables(ns))
    feat_major = lambda a: a[:, 0].transpose(0, 2, 3, 1).reshape(bs, A_KV_W, past)
    kbc, vbc, kapc, vadc = _cache_kv(
        cache_mla_ckv[:, 0], jnp.swapaxes(cache_mla_krope[:, 0], 1, 2),
        feat_major(cache_attn_k), feat_major(cache_attn_v), w["wuk"], w["wuv_t"])
    r3 = lambda a: a.reshape(bs, ns, a.shape[1])
    oa = _att_win(sink, r3(qa), r3(kap), vad, kapc, vadc)
    ob, (w["woa"], w["wob"], w["wout"], w["wgu2"], w["wd2"]) = _att_mla(
        r3(qb), r3(kb), vb_t, kbc, vbc, w["post_f32"])

    y_prompt = _post(x1p, mods3, oa_p, ob_p, w, False, 1).reshape(bp, sp, d)
    y_sample = _post(x1s, mods3, oa.reshape(bs * ns, A_Q_W), ob.reshape(bs * ns, B_HEADS * B_V_DIM),
                     w, True, tiles_per_batch).reshape(bs, ns, d)

    new_attn_k = k_t.reshape(bp, 1, A_KV_HEADS, A_HEAD_DIM, sp).transpose(0, 1, 4, 2, 3)
    new_attn_v = v_t.reshape(bp, 1, A_KV_HEADS, A_HEAD_DIM, sp).transpose(0, 1, 4, 2, 3)
    new_mla_ckv = ckv_n.reshape(bp, 1, sp, KV_LORA)
    new_mla_krope = krope_t.reshape(bp, 1, B_ROPE_DIM, sp).transpose(0, 1, 3, 2)
    return (y_prompt, y_sample, new_attn_k, new_attn_v, new_mla_ckv, new_mla_krope)
```

```python
import functools

import jax
import jax.numpy as jnp
import numpy as np
from jax import lax
from jax.experimental import pallas as pl
from jax.experimental.pallas import tpu as pltpu

F32 = jnp.float32
BF16 = jnp.bfloat16

D_MODEL = 1024
N_MOD = 9
GRID_W = 64
WINDOW = 128
A_HEADS = 8
A_KV_HEADS = 2
A_HEAD_DIM = 64
A_Q_W = A_HEADS * A_HEAD_DIM
A_KV_W = A_KV_HEADS * A_HEAD_DIM
B_HEADS = 8
B_NOPE_DIM = 64
B_ROPE_DIM = 32
B_V_DIM = 64
B_QK_DIM = B_NOPE_DIM + B_ROPE_DIM
Q_LORA = 256
KV_LORA = 256
D_FF = 2816
ROPE_THETA = 10000.0
EPS = 1e-6
NEG = -1e30
A_SCALE = A_HEAD_DIM ** -0.5
B_SCALE = B_QK_DIM ** -0.5
LOG2E = 1.4426950408889634

LANES = 128
HALF = LANES // 2
FF_CHUNK = 256
N_FF_CHUNKS = D_FF // FF_CHUNK
TOKEN_TILE = 512
ADA_TILE = 1024
Q_BLOCK = 128
BAND = 3 * Q_BLOCK
CTX_BATCHES = 8
WIN_BLOCKS = 8
SUM_ROWS = 16
MLA_Q_TILE = 512
MLA_KEY_BLOCK = 256
MOD_ROWS = 8
VMEM_LIMIT = 56 * 1024 * 1024

IN_A_W = A_Q_W + 2 * A_KV_W + Q_LORA + KV_LORA + LANES
OFF_KA = A_Q_W
OFF_VA = OFF_KA + A_KV_W
OFF_QLAT = OFF_VA + A_KV_W
OFF_CKV = OFF_QLAT + Q_LORA
OFF_KROPE = OFF_CKV + KV_LORA


def _dot(a, b):
    return jnp.dot(a, b, preferred_element_type=F32)


def _dot_nt(a, b):
    return lax.dot_general(a, b, (((1,), (1,)), ((), ())), preferred_element_type=F32)


def _rms(x, g):
    ms = jnp.mean(x * x, axis=-1, keepdims=True)
    return x * lax.rsqrt(ms + EPS) * g


def _mod_norm(x, g, mods_ref, k):
    shift = mods_ref[:, (3 * k) * D_MODEL:(3 * k + 1) * D_MODEL]
    scale = mods_ref[:, (3 * k + 1) * D_MODEL:(3 * k + 2) * D_MODEL]
    return _rms(x, g) * (1.0 + scale) + shift


def _gate(mods_ref, k):
    return mods_ref[:, (3 * k + 2) * D_MODEL:(3 * k + 3) * D_MODEL]


def _ffn(h, wgu_ref, wd_ref, hooks=None):
    def gate_up(c):
        a = _dot(h, wgu_ref[:, c * FF_CHUNK:(c + 1) * FF_CHUNK])
        u = _dot(h, wgu_ref[:, D_FF + c * FF_CHUNK:D_FF + (c + 1) * FF_CHUNK])
        return a, u

    acc = None
    nxt = gate_up(0)
    for c in range(N_FF_CHUNKS):
        a, u = nxt
        if c + 1 < N_FF_CHUNKS:
            nxt = gate_up(c + 1)
        act = (a * jax.nn.sigmoid(a) * u).astype(BF16)
        d = _dot(act, wd_ref[c * FF_CHUNK:(c + 1) * FF_CHUNK, :])
        acc = d if acc is None else acc + d
        if hooks and c in hooks:
            hooks[c]()
    return acc


def _rope(x, cos, sin_signed, dist):
    lane = lax.broadcasted_iota(jnp.int32, x.shape, 1)
    first = (lane & (2 * dist - 1)) < dist
    partner = jnp.where(first, pltpu.roll(x, LANES - dist, axis=1), pltpu.roll(x, dist, axis=1))
    return x * cos + partner * sin_signed


def _store_gqa_k(k, low, kap_ref):
    k_sw = pltpu.roll(k, HALF, axis=1)
    zero = jnp.zeros_like(k)
    kap_ref[:, 0 * LANES:1 * LANES] = jnp.where(low, k, zero).astype(BF16)
    kap_ref[:, 1 * LANES:2 * LANES] = jnp.where(low, zero, k_sw).astype(BF16)
    kap_ref[:, 2 * LANES:3 * LANES] = jnp.where(low, k_sw, zero).astype(BF16)
    kap_ref[:, 3 * LANES:4 * LANES] = jnp.where(low, zero, k).astype(BF16)


def _store_gqa_v_blocks(v, vt_ref):
    for r in range(vt_ref.shape[0]):
        vt_ref[r] = v[r * LANES:(r + 1) * LANES, :].T.astype(BF16)


def _store_seq_minor(out_ref, val):
    nb, feat, seq = out_ref.shape
    for bi in range(nb):
        out_ref[bi] = val[bi * seq:(bi + 1) * seq, :].T[:feat, :]


def _low_lanes(rows):
    return lax.broadcasted_iota(jnp.int32, (rows, LANES), 1) < HALF


def _ada_kernel(c_ref, w_ref, b_ref, o_ref):
    c = c_ref[...]
    s = (c * jax.nn.sigmoid(c)).astype(BF16)
    o_ref[...] = _dot(s, w_ref[...].astype(BF16)) + b_ref[...]


def _ada_mods(cvec, ada_w, ada_b):
    n = ada_w.shape[1]
    return pl.pallas_call(
        _ada_kernel,
        grid=(n // ADA_TILE,),
        in_specs=[
            pl.BlockSpec((MOD_ROWS, D_MODEL), lambda j: (0, 0)),
            pl.BlockSpec((D_MODEL, ADA_TILE), lambda j: (0, j)),
            pl.BlockSpec((1, ADA_TILE), lambda j: (0, j)),
        ],
        out_specs=pl.BlockSpec((MOD_ROWS, ADA_TILE), lambda j: (0, j)),
        out_shape=jax.ShapeDtypeStruct((MOD_ROWS, n), F32),
        compiler_params=pltpu.CompilerParams(dimension_semantics=("parallel",), vmem_limit_bytes=VMEM_LIMIT),
        name="ada_mods",
    )(cvec, ada_w, ada_b)


PRE_HOOK_IN_PROJ = 1
PRE_HOOK_UP_PROJ = 6


def _pre_kernel(latent, n_tiles, *refs):
    x1s_ref = refs[-1]
    if latent:
        (x_ref, mods_ref, modsp_ref, n1_ref, wgu_ref, wd_ref, nm_ref, win_ref, qn_ref, kvn_ref, wuq_ref, wuk_ref,
         wuv_ref, cosa_ref, sina_ref, cosb_ref, sinb_ref,
         x1_ref, qa_ref, kap_ref, vad_ref, qb_ref, kb_ref, vb_ref) = refs[:-1]
    else:
        (x_ref, mods_ref, modsp_ref, n1_ref, wgu_ref, wd_ref, nm_ref, win_ref, qn_ref, kvn_ref, wuq_ref, wuk_ref,
         wuv_ref,
         x1_ref, qa_ref, kap_ref, qb_ref, kb_ref, vb_ref,
         knat_ref, vnat_ref, ckvn_ref, krope_ref) = refs[:-1]
    i = pl.program_id(0)
    carry = {}

    def in_proj():
        h2 = _mod_norm(x1s_ref[...], nm_ref[...], modsp_ref, 1).astype(BF16)
        z = _dot_nt(h2, win_ref[...])
        low = _low_lanes(z.shape[0])

        for c in range(A_Q_W // LANES):
            q = z[:, c * LANES:(c + 1) * LANES]
            if latent:
                q = _rope(q, cosa_ref[...], sina_ref[...], A_HEAD_DIM // 4)
            qa_ref[:, c * LANES:(c + 1) * LANES] = (q * (A_SCALE * LOG2E)).astype(BF16)
        k = z[:, OFF_KA:OFF_KA + LANES]
        v = z[:, OFF_VA:OFF_VA + LANES]
        if latent:
            k = _rope(k, cosa_ref[...], sina_ref[...], A_HEAD_DIM // 4)
            _store_gqa_v_blocks(v, vad_ref)
        else:
            _store_seq_minor(knat_ref, k)
            _store_seq_minor(vnat_ref, v)
        _store_gqa_k(k, low, kap_ref)

        carry["q_lat"] = _rms(z[:, OFF_QLAT:OFF_QLAT + Q_LORA], qn_ref[...]).astype(BF16)
        ckv_n = _rms(z[:, OFF_CKV:OFF_CKV + KV_LORA], kvn_ref[...])
        krp = z[:, OFF_KROPE:OFF_KROPE + LANES]
        if latent:
            krp = _rope(krp, cosb_ref[...], sinb_ref[...], B_ROPE_DIM // 4)
        else:
            ckvn_ref[...] = ckv_n
            _store_seq_minor(krope_ref, pltpu.roll(krp, HALF, axis=1))
        carry["ckv_b"] = ckv_n.astype(BF16)
        carry["krp"] = krp

    def up_proj():
        q_lat, ckv_b, krp = carry["q_lat"], carry["ckv_b"], carry["krp"]
        qb = _dot(q_lat, wuq_ref[...])
        kn = _dot(ckv_b, wuk_ref[...])
        vb_ref[...] = _dot_nt(wuv_ref[...], ckv_b).astype(BF16)
        for h in range(B_HEADS):
            qh = qb[:, h * LANES:(h + 1) * LANES]
            if latent:
                qh = _rope(qh, cosb_ref[...], sinb_ref[...], B_ROPE_DIM // 4)
            qb_ref[:, h * LANES:(h + 1) * LANES] = (qh * (B_SCALE * LOG2E)).astype(BF16)
            kb_ref[:, h * LANES:(h + 1) * LANES] = (kn[:, h * LANES:(h + 1) * LANES] + krp).astype(BF16)

    def ffn(hooks):
        x = x_ref[...]
        h1 = _mod_norm(x, n1_ref[...], mods_ref, 0).astype(BF16)
        x1 = x + 0.5 * _gate(mods_ref, 0) * _ffn(h1, wgu_ref, wd_ref, hooks)
        x1_ref[...] = x1
        x1s_ref[...] = x1

    @pl.when(i == 0)
    def _():
        ffn(None)

    @pl.when(jnp.logical_and(i > 0, i < n_tiles))
    def _():
        ffn({PRE_HOOK_IN_PROJ: in_proj, PRE_HOOK_UP_PROJ: up_proj})

    @pl.when(i == n_tiles)
    def _():
        in_proj()
        up_proj()


def _const_spec(shape):
    nd = len(shape)
    return pl.BlockSpec(shape, lambda i: (0,) * nd, pipeline_mode=pl.Buffered(1))


def _pre(x, mods3, w, latent, seq, rope):
    t = x.shape[0]
    tiles_per_batch = max(seq // TOKEN_TILE, 1)
    tm = TOKEN_TILE
    n_tiles = t // tm
    cur = lambda i: jnp.minimum(i, n_tiles - 1)
    prev = lambda i: jnp.maximum(i - 1, 0)
    mod_of = (lambda j: 1 + j // tiles_per_batch) if latent else (lambda j: 0)
    cur_rows = lambda i: (cur(i), 0)
    prev_rows = lambda i: (prev(i), 0)

    weights = [w["n1"], w["wgu1"], w["wd1"], w["nm"], w["win_a"], w["qn"], w["kvn"], w["wuq"], w["wuk"], w["wuv_t"]]
    in_specs = [pl.BlockSpec((tm, D_MODEL), cur_rows),
                pl.BlockSpec((None, 1, N_MOD * D_MODEL), lambda i: (mod_of(cur(i)), 0, 0)),
                pl.BlockSpec((None, 1, N_MOD * D_MODEL), lambda i: (mod_of(prev(i)), 0, 0))]
    in_specs += [_const_spec(a.shape) for a in weights]
    args = [x, mods3, mods3] + weights
    if latent:
        in_specs += [pl.BlockSpec((tm, LANES), lambda i: (prev(i) % tiles_per_batch, 0))] * 4
        args += list(rope)

    out_specs = [pl.BlockSpec((tm, D_MODEL), cur_rows)]
    out_shape = [jax.ShapeDtypeStruct((t, D_MODEL), F32)]
    for wd in (A_Q_W, 4 * LANES):
        out_specs.append(pl.BlockSpec((tm, wd), prev_rows))
        out_shape.append(jax.ShapeDtypeStruct((t, wd), BF16))
    if latent:
        out_specs.append(pl.BlockSpec((tm // LANES, A_KV_W, LANES), lambda i: (prev(i), 0, 0)))
        out_shape.append(jax.ShapeDtypeStruct((t // LANES, A_KV_W, LANES), BF16))
    for wd in (B_HEADS * LANES, B_HEADS * LANES):
        out_specs.append(pl.BlockSpec((tm, wd), prev_rows))
        out_shape.append(jax.ShapeDtypeStruct((t, wd), BF16))
    out_specs.append(pl.BlockSpec((B_HEADS * B_V_DIM, tm), lambda i: (0, prev(i))))
    out_shape.append(jax.ShapeDtypeStruct((B_HEADS * B_V_DIM, t), BF16))
    if not latent:
        nb = tm // seq
        for feat in (A_KV_W, A_KV_W):
            out_specs.append(pl.BlockSpec((nb, feat, seq), lambda i: (prev(i), 0, 0)))
            out_shape.append(jax.ShapeDtypeStruct((t // seq, feat, seq), F32))
        out_specs.append(pl.BlockSpec((tm, KV_LORA), prev_rows))
        out_shape.append(jax.ShapeDtypeStruct((t, KV_LORA), F32))
        out_specs.append(pl.BlockSpec((nb, B_ROPE_DIM, seq), lambda i: (prev(i), 0, 0)))
        out_shape.append(jax.ShapeDtypeStruct((t // seq, B_ROPE_DIM, seq), F32))

    return pl.pallas_call(
        functools.partial(_pre_kernel, latent, n_tiles),
        grid=(n_tiles + 1,),
        in_specs=in_specs,
        out_specs=out_specs,
        out_shape=out_shape,
        scratch_shapes=[pltpu.VMEM((tm, D_MODEL), F32)],
        compiler_params=pltpu.CompilerParams(dimension_semantics=("arbitrary",), vmem_limit_bytes=VMEM_LIMIT),
        name="pre_latent" if latent else "pre_context",
    )(*args)


def _cache_kernel(ckv_ref, kropet_ref, ckt_ref, cvt_ref, wuk_ref, wuv_ref, kb_ref, vb_ref, kap_ref, vat_ref):
    p = ckv_ref.shape[0]
    c = ckv_ref[...].astype(BF16)
    kn = _dot(c, wuk_ref[...])
    krp = jnp.concatenate([jnp.zeros((B_NOPE_DIM, p), F32), kropet_ref[...],
                           jnp.zeros((LANES - B_QK_DIM, p), F32)], axis=0).T
    for h in range(B_HEADS):
        kb_ref[:, h * LANES:(h + 1) * LANES] = (kn[:, h * LANES:(h + 1) * LANES] + krp).astype(BF16)
    vb_ref[...] = _dot_nt(wuv_ref[...], c).astype(BF16)
    _store_gqa_k(ckt_ref[...].T, _low_lanes(p), kap_ref)
    vat_ref[...] = cvt_ref[...].astype(BF16)


def _cache_kv(ckv, krope_t, ck_t, cv_t, wuk, wuv_t):
    b, p, _ = ckv.shape
    blk = lambda wd: pl.BlockSpec((None, p, wd), lambda i: (i, 0, 0))
    blk_t = lambda feat: pl.BlockSpec((None, feat, p), lambda i: (i, 0, 0))
    vdim = B_HEADS * B_V_DIM
    return pl.pallas_call(
        _cache_kernel,
        grid=(b,),
        in_specs=[blk(KV_LORA), blk_t(B_ROPE_DIM), blk_t(A_KV_W), blk_t(A_KV_W),
                  pl.BlockSpec(wuk.shape, lambda i: (0, 0)),
                  pl.BlockSpec(wuv_t.shape, lambda i: (0, 0))],
        out_specs=[blk(B_HEADS * LANES), pl.BlockSpec((vdim, p), lambda i: (0, i)), blk(4 * LANES),
                   pl.BlockSpec((A_KV_W, p), lambda i: (0, i))],
        out_shape=[jax.ShapeDtypeStruct((b, p, B_HEADS * LANES), BF16),
                   jax.ShapeDtypeStruct((vdim, b * p), BF16),
                   jax.ShapeDtypeStruct((b, p, 4 * LANES), BF16),
                   jax.ShapeDtypeStruct((A_KV_W, b * p), BF16)],
        compiler_params=pltpu.CompilerParams(dimension_semantics=("parallel",), vmem_limit_bytes=VMEM_LIMIT),
        name="cache_kv",
    )(ckv, krope_t, ck_t, cv_t, wuk, wuv_t)


def _att_ctx_kernel(seq, sink_ref, qa_ref, kap_ref, vt_ref, qb_ref, kb_ref, vbt_ref, oa_ref, ob_ref):
    nb = qa_ref.shape[0] // seq
    ones = jnp.ones((SUM_ROWS, seq), BF16)
    upper = lax.broadcasted_iota(jnp.int32, (1, 2 * seq), 1) >= seq
    units = []
    for bi in range(nb):
        units += [("a", bi, kv, half) for kv in range(A_KV_HEADS) for half in range(2)]
        units += [("b", bi, h, h % 2) for h in range(B_HEADS)]

    def score(unit):
        kind, bi, idx, half = unit
        rows = slice(bi * seq, (bi + 1) * seq)
        if kind == "a":
            kv, u = idx, 2 * idx + half
            q2 = jnp.concatenate([qa_ref[rows, (2 * kv) * LANES:(2 * kv + 1) * LANES],
                                  qa_ref[rows, (2 * kv + 1) * LANES:(2 * kv + 2) * LANES]], axis=0)
            return _dot_nt(kap_ref[rows, u * LANES:(u + 1) * LANES], q2)
        return _dot_nt(kb_ref[rows, idx * LANES:(idx + 1) * LANES], qb_ref[rows, idx * LANES:(idx + 1) * LANES])

    def values(unit):
        kind, bi, idx, _ = unit
        if kind == "a":
            v = vt_ref[bi, idx * A_HEAD_DIM:(idx + 1) * A_HEAD_DIM, :].astype(BF16)
        else:
            v = vbt_ref[idx * B_V_DIM:(idx + 1) * B_V_DIM, bi * seq:(bi + 1) * seq]
        return jnp.concatenate([v, ones], axis=0)

    def sink_row(unit):
        kind, _, kv, half = unit
        if kind == "b":
            return None
        return jnp.where(upper, sink_ref[4 * kv + 2 + half], sink_ref[4 * kv + half]) * LOG2E

    def finish_pair(first, second, o_first, o_second):
        kind, bi, idx, _ = second
        rows = slice(bi * seq, (bi + 1) * seq)
        if kind == "a":
            for pr in range(2):
                cols = slice(pr * seq, (pr + 1) * seq)
                pair = jnp.concatenate([o_first[:, cols], o_second[:, cols]], axis=0)
                c = 2 * idx + pr
                oa_ref[rows, c * LANES:(c + 1) * LANES] = pair.T.astype(BF16)
        else:
            c = idx // 2
            ob_ref[rows, c * LANES:(c + 1) * LANES] = jnp.concatenate([o_first, o_second], axis=0).T.astype(BF16)

    scores, probs, outs = {}, {}, {}
    for t in range(len(units) + 2):
        if t < len(units):
            scores[t] = score(units[t])
        if 0 <= t - 2 < len(units):
            n = t - 2
            p_blocks, p_sink = probs.pop(n)
            acc = _dot(values(units[n]), p_blocks[0])
            den = acc[A_HEAD_DIM:A_HEAD_DIM + 1]
            outs[n] = acc[:A_HEAD_DIM] / (den if p_sink is None else den + p_sink)
            if units[n][3] == 1:
                finish_pair(units[n - 1], units[n], outs.pop(n - 1), outs.pop(n))
        if 0 <= t - 1 < len(units):
            n = t - 1
            probs[n] = _probs_keys_major([scores.pop(n)], sink_row(units[n]))


def _att_ctx(sink, qa, kap, v_t, qb, kb, vb_t, seq):
    t = qa.shape[0]
    rows = CTX_BATCHES * seq
    blk = lambda wd: pl.BlockSpec((rows, wd), lambda b: (b, 0))
    return pl.pallas_call(
        functools.partial(_att_ctx_kernel, seq),
        grid=(t // rows,),
        in_specs=[pl.BlockSpec(memory_space=pltpu.SMEM),
                  blk(qa.shape[1]), blk(kap.shape[1]),
                  pl.BlockSpec((CTX_BATCHES, A_KV_W, seq), lambda b: (b, 0, 0)),
                  blk(qb.shape[1]), blk(kb.shape[1]),
                  pl.BlockSpec((vb_t.shape[0], rows), lambda b: (0, b))],
        out_specs=[blk(A_Q_W), blk(B_HEADS * B_V_DIM)],
        out_shape=[jax.ShapeDtypeStruct((t, A_Q_W), BF16), jax.ShapeDtypeStruct((t, B_HEADS * B_V_DIM), BF16)],
        compiler_params=pltpu.CompilerParams(dimension_semantics=("parallel",), vmem_limit_bytes=VMEM_LIMIT),
        name="att_context",
    )(sink, qa, kap, v_t, qb, kb, vb_t)


def _att_win_kernel(n_lat, sink_ref, qa_ref, kap_ref, vat_ref, kapc_ref, vatc_ref, oa_ref):
    nq = 2 * Q_BLOCK
    past = kapc_ref.shape[0]
    upper = lax.broadcasted_iota(jnp.int32, (1, nq), 1) >= Q_BLOCK
    ones_band = jnp.ones((SUM_ROWS, BAND), BF16)
    ones_ctx = jnp.ones((SUM_ROWS, past), BF16)
    key = lax.broadcasted_iota(jnp.int32, (BAND, nq), 0)
    qry = lax.broadcasted_iota(jnp.int32, (BAND, nq), 1) & (Q_BLOCK - 1)
    key_minus_qry = key - qry

    starts, valids, v_bands = [], [], []
    for qb in range(WIN_BLOCKS):
        i = pl.program_id(1) * WIN_BLOCKS + qb
        start = pl.multiple_of(jnp.clip(i * Q_BLOCK - Q_BLOCK, 0, n_lat - BAND), Q_BLOCK)
        starts.append(start)
        valids.append(jnp.abs(key_minus_qry + (start - i * Q_BLOCK)) <= WINDOW)
        v_bands.append(vat_ref[pl.ds(start // LANES, BAND // LANES)])

    units = [(qb, kv, half) for qb in range(WIN_BLOCKS) for kv in range(A_KV_HEADS) for half in range(2)]

    def score_blocks(unit):
        qb, kv, half = unit
        u = 2 * kv + half
        rows = slice(qb * Q_BLOCK, (qb + 1) * Q_BLOCK)
        q2 = jnp.concatenate([qa_ref[rows, (2 * kv) * LANES:(2 * kv + 1) * LANES],
                              qa_ref[rows, (2 * kv + 1) * LANES:(2 * kv + 2) * LANES]], axis=0)
        s_band = _dot_nt(kap_ref[pl.ds(starts[qb], BAND), u * LANES:(u + 1) * LANES], q2)
        s_ctx = _dot_nt(kapc_ref[:, u * LANES:(u + 1) * LANES], q2)
        return [jnp.where(valids[qb], s_band, NEG), s_ctx]

    def value_products(unit, p_blocks):
        qb, kv, _ = unit
        rows = slice(kv * A_HEAD_DIM, (kv + 1) * A_HEAD_DIM)
        vb = jnp.concatenate([v_bands[qb][r][rows] for r in range(BAND // LANES)], axis=1)
        r_band = _dot(jnp.concatenate([vb, ones_band], axis=0), p_blocks[0])
        r_ctx = _dot(jnp.concatenate([vatc_ref[rows, :], ones_ctx], axis=0), p_blocks[1])
        return r_band + r_ctx

    scores, probs, outs = {}, {}, {}
    for t in range(len(units) + 2):
        if t < len(units):
            scores[t] = score_blocks(units[t])
        if 0 <= t - 2 < len(units):
            n = t - 2
            qb, kv, half = units[n]
            p_blocks, p_sink = probs.pop(n)
            acc = value_products(units[n], p_blocks)
            outs[n] = acc[:A_HEAD_DIM] / (acc[A_HEAD_DIM:A_HEAD_DIM + 1] + p_sink)
            if half == 1:
                for pr in range(2):
                    cols = slice(pr * Q_BLOCK, (pr + 1) * Q_BLOCK)
                    pair = jnp.concatenate([outs[n - 1][:, cols], outs[n][:, cols]], axis=0)
                    c = 2 * kv + pr
                    oa_ref[qb * Q_BLOCK:(qb + 1) * Q_BLOCK, c * LANES:(c + 1) * LANES] = pair.T.astype(BF16)
                del outs[n - 1], outs[n]
        if 0 <= t - 1 < len(units):
            n = t - 1
            _, kv, half = units[n]
            sink = jnp.where(upper, sink_ref[4 * kv + 2 + half], sink_ref[4 * kv + half]) * LOG2E
            probs[n] = _probs_keys_major(scores.pop(n), sink)


def _att_win(sink, qa, kap, vat, kapc, vatc):
    b, n, _ = qa.shape
    past = kapc.shape[1]
    tq = WIN_BLOCKS * Q_BLOCK
    return pl.pallas_call(
        functools.partial(_att_win_kernel, n),
        grid=(b, n // tq),
        in_specs=[pl.BlockSpec(memory_space=pltpu.SMEM),
                  pl.BlockSpec((None, tq, A_Q_W), lambda bi, i: (bi, i, 0)),
                  pl.BlockSpec((None, n, kap.shape[2]), lambda bi, i: (bi, 0, 0)),
                  pl.BlockSpec((n // LANES, A_KV_W, LANES), lambda bi, i: (bi, 0, 0)),
                  pl.BlockSpec((None, past, kapc.shape[2]), lambda bi, i: (bi, 0, 0)),
                  pl.BlockSpec((A_KV_W, past), lambda bi, i: (0, bi))],
        out_specs=pl.BlockSpec((None, tq, A_Q_W), lambda bi, i: (bi, i, 0)),
        out_shape=jax.ShapeDtypeStruct((b, n, A_Q_W), BF16),
        compiler_params=pltpu.CompilerParams(dimension_semantics=("parallel", "parallel"),
                                             vmem_limit_bytes=VMEM_LIMIT),
        name="att_window",
    )(sink, qa, kap, vat, kapc, vatc)


def _probs_keys_major(scores_t, sink=None):
    m = None
    for s in scores_t:
        mx = jnp.max(s, axis=0, keepdims=True)
        m = mx if m is None else jnp.maximum(m, mx)
    if sink is not None:
        m = jnp.maximum(m, sink)
    probs = [jnp.exp2(s - m).astype(BF16) for s in scores_t]
    return probs, (None if sink is None else jnp.exp2(sink - m))


def _att_mla_kernel(n_cast, qb_ref, kbl_ref, vtl_ref, kbc_ref, vtc_ref, *rest):
    w32_refs, ob_ref, w16_refs = rest[:n_cast], rest[n_cast], rest[n_cast + 1:]
    for src, dst in zip(w32_refs, w16_refs):
        dst[...] = src[...].astype(BF16)

    kb = MLA_KEY_BLOCK
    ones = jnp.ones((SUM_ROWS, kb), BF16)
    blocks = [(kbc_ref, vtc_ref, j) for j in range(kbc_ref.shape[0] // kb)]
    blocks += [(kbl_ref, vtl_ref, j) for j in range(kbl_ref.shape[0] // kb)]

    def score_block(h, blk):
        k_ref, _, j = blk
        return _dot_nt(k_ref[j * kb:(j + 1) * kb, h * LANES:(h + 1) * LANES], qb_ref[:, h * LANES:(h + 1) * LANES])

    def value_block(h, blk, p):
        _, v_ref, j = blk
        vt = jnp.concatenate([v_ref[h * B_V_DIM:(h + 1) * B_V_DIM, j * kb:(j + 1) * kb], ones], axis=0)
        return _dot(vt, p)

    scores, probs, outs = {}, {}, {}
    for t in range(B_HEADS + 2):
        acc = None
        new_scores = []
        for bi, blk in enumerate(blocks):
            if t < B_HEADS:
                new_scores.append(score_block(t, blk))
            if 0 <= t - 2 < B_HEADS:
                r = value_block(t - 2, blk, probs[t - 2][bi])
                acc = r if acc is None else acc + r
        if t < B_HEADS:
            scores[t] = new_scores
        if 0 <= t - 1 < B_HEADS:
            probs[t - 1], _ = _probs_keys_major(scores.pop(t - 1))
        if acc is not None:
            h = t - 2
            probs.pop(h)
            outs[h] = acc[:B_V_DIM] / acc[B_V_DIM:B_V_DIM + 1]
            if h % 2 == 1:
                c = h // 2
                pair = jnp.concatenate([outs.pop(h - 1), outs.pop(h)], axis=0)
                ob_ref[:, c * LANES:(c + 1) * LANES] = pair.T.astype(BF16)


def _att_mla(qb, kbl, vtl, kbc, vtc, cast_weights):
    b, n, _ = qb.shape
    past = kbc.shape[1]
    tq = MLA_Q_TILE
    vdim = B_HEADS * B_V_DIM
    q_tiles = n // tq
    steps = b * q_tiles
    w_specs = []
    for wm in cast_weights:
        rows, cols = wm.shape
        assert rows % (steps * 16) == 0, "row blocks must be whole packed-bf16 sublane tiles"
        w_specs.append(pl.BlockSpec((rows // steps, cols), lambda bi, i: (bi * q_tiles + i, 0)))
    out = pl.pallas_call(
        functools.partial(_att_mla_kernel, len(cast_weights)),
        grid=(b, q_tiles),
        in_specs=[pl.BlockSpec((None, tq, qb.shape[2]), lambda bi, i: (bi, i, 0)),
                  pl.BlockSpec((None, n, kbl.shape[2]), lambda bi, i: (bi, 0, 0)),
                  pl.BlockSpec((vdim, n), lambda bi, i: (0, bi)),
                  pl.BlockSpec((None, past, kbc.shape[2]), lambda bi, i: (bi, 0, 0)),
                  pl.BlockSpec((vdim, past), lambda bi, i: (0, bi))] + w_specs,
        out_specs=[pl.BlockSpec((None, tq, vdim), lambda bi, i: (bi, i, 0))] + w_specs,
        out_shape=[jax.ShapeDtypeStruct((b, n, vdim), BF16)]
        + [jax.ShapeDtypeStruct(wm.shape, BF16) for wm in cast_weights],
        compiler_params=pltpu.CompilerParams(dimension_semantics=("parallel", "parallel"),
                                             vmem_limit_bytes=VMEM_LIMIT),
        name="att_mla",
    )(qb, kbl, vtl, kbc, vtc, *cast_weights)
    return out[0], out[1:]


def _post_kernel(x1_ref, mods_ref, oa_ref, ob_ref, nm_ref, wing_ref, woa_ref, wob_ref, wout_ref,
                 n2_ref, wgu_ref, wd_ref, nf_ref, y_ref):
    x1 = x1_ref[...]
    h2 = _mod_norm(x1, nm_ref[...], mods_ref, 1).astype(BF16)
    g = _dot_nt(h2, wing_ref[...])
    m = (jax.nn.sigmoid(g[:, :D_MODEL]) * _dot(oa_ref[...], woa_ref[...])
         + jax.nn.sigmoid(g[:, D_MODEL:]) * _dot(ob_ref[...], wob_ref[...]))
    x2 = x1 + _gate(mods_ref, 1) * _dot(m.astype(BF16), wout_ref[...])
    h3 = _mod_norm(x2, n2_ref[...], mods_ref, 2).astype(BF16)
    x3 = x2 + 0.5 * _gate(mods_ref, 2) * _ffn(h3, wgu_ref, wd_ref)
    y_ref[...] = _rms(x3, nf_ref[...])


def _post(x1, mods3, oa, ob, w, latent, tiles_per_batch):
    t = x1.shape[0]
    tm = TOKEN_TILE
    if latent:
        mod_row = lambda i: (1 + i // tiles_per_batch, 0, 0)
    else:
        mod_row = lambda i: (0, 0, 0)
    row_blk = lambda i: (i, 0)
    weights = [w["nm"], w["win_g"], w["woa"], w["wob"], w["wout"], w["n2"], w["wgu2"], w["wd2"], w["nf"]]
    in_specs = [pl.BlockSpec((tm, D_MODEL), row_blk),
                pl.BlockSpec((None, 1, N_MOD * D_MODEL), mod_row),
                pl.BlockSpec((tm, A_Q_W), row_blk),
                pl.BlockSpec((tm, B_HEADS * B_V_DIM), row_blk)]
    in_specs += [_const_spec(a.shape) for a in weights]
    return pl.pallas_call(
        _post_kernel,
        grid=(t // tm,),
        in_specs=in_specs,
        out_specs=pl.BlockSpec((tm, D_MODEL), row_blk),
        out_shape=jax.ShapeDtypeStruct((t, D_MODEL), F32),
        compiler_params=pltpu.CompilerParams(dimension_semantics=("parallel",), vmem_limit_bytes=VMEM_LIMIT),
        name="post_latent" if latent else "post_context",
    )(x1, mods3, oa, ob, *weights)


def _rope_tables(n):
    f32 = np.float32
    rows = n // GRID_W
    t_row = np.repeat(np.arange(rows, dtype=f32), GRID_W)
    t_col = np.tile(np.arange(GRID_W, dtype=f32), rows)

    def angles(d_rot):
        d_half = d_rot // 2
        inv = (f32(1.0) / np.power(f32(ROPE_THETA), np.arange(0, d_half, 2, dtype=f32) / f32(d_half))).astype(f32)
        ar = t_row[:, None] * inv[None, :]
        ac = t_col[:, None] * inv[None, :]
        return np.concatenate([ar, ar, ac, ac], axis=-1).astype(f32)

    def signed(sin, d_rot):
        q = d_rot // 4
        sign = np.where((np.arange(d_rot) % (2 * q)) < q, f32(-1.0), f32(1.0)).astype(f32)
        return sin * sign[None, :]

    ang_a = angles(A_HEAD_DIM)
    cos_a = np.tile(np.cos(ang_a), (1, LANES // A_HEAD_DIM))
    sin_a = np.tile(signed(np.sin(ang_a), A_HEAD_DIM), (1, LANES // A_HEAD_DIM))
    ang_b = angles(B_ROPE_DIM)
    pad = ((0, 0), (B_NOPE_DIM, LANES - B_NOPE_DIM - B_ROPE_DIM))
    cos_b = np.pad(np.cos(ang_b), pad, constant_values=1.0)
    sin_b = np.pad(signed(np.sin(ang_b), B_ROPE_DIM), pad)
    return tuple(jnp.asarray(t, dtype=F32) for t in (cos_a, sin_a, cos_b, sin_b))


def _prep_weights(ffn1_norm, ffn1_w_gu, ffn1_w_down, mix_norm, w_in, q_lat_norm, kv_lat_norm, w_uq, w_ukv,
                  w_o_a, w_o_b, w_out, ffn2_norm, ffn2_w_gu, ffn2_w_down, final_norm):
    win_t = jnp.swapaxes(w_in[0], 0, 1)
    n_attn = A_Q_W + 2 * A_KV_W + Q_LORA + KV_LORA
    krope_rows = jnp.pad(win_t[n_attn:n_attn + B_ROPE_DIM],
                         ((B_NOPE_DIM, LANES - B_NOPE_DIM - B_ROPE_DIM), (0, 0)))
    win_a = jnp.concatenate([win_t[:n_attn], krope_rows], axis=0).astype(BF16)
    win_g = win_t[n_attn + B_ROPE_DIM:].astype(BF16)

    wuq = jnp.pad(w_uq[0].reshape(Q_LORA, B_HEADS, B_QK_DIM),
                  ((0, 0), (0, 0), (0, LANES - B_QK_DIM))).reshape(Q_LORA, B_HEADS * LANES).astype(BF16)
    wukv = w_ukv[0].reshape(KV_LORA, B_HEADS, B_NOPE_DIM + B_V_DIM)
    wuk = jnp.pad(wukv[:, :, :B_NOPE_DIM],
                  ((0, 0), (0, 0), (0, LANES - B_NOPE_DIM))).reshape(KV_LORA, B_HEADS * LANES).astype(BF16)
    wuv = wukv[:, :, B_NOPE_DIM:].reshape(KV_LORA, B_HEADS * B_V_DIM).astype(BF16)

    return {
        "n1": ffn1_norm, "wgu1": ffn1_w_gu[0].astype(BF16), "wd1": ffn1_w_down[0].astype(BF16),
        "nm": mix_norm, "win_a": win_a, "win_g": win_g,
        "qn": q_lat_norm, "kvn": kv_lat_norm, "wuq": wuq, "wuk": wuk, "wuv_t": wuv.T,
        "n2": ffn2_norm, "nf": final_norm.reshape(1, D_MODEL),
        "post_f32": [w_o_a[0], w_o_b[0], w_out[0], ffn2_w_gu[0], ffn2_w_down[0]],
    }


def kernel(x_prompt, x_sample, cache_attn_k, cache_attn_v, cache_mla_ckv, cache_mla_krope, c, c_ctx, ada_w, ada_b, ffn1_norm, ffn1_w_gu, ffn1_w_down, mix_norm, w_in, attn_sink, q_lat_norm, kv_lat_norm, w_uq, w_ukv, w_o_a, w_o_b, w_out, ffn2_norm, ffn2_w_gu, ffn2_w_down, final_norm):
    assert ada_w.shape[0] == 1, "single trunk layer"
    bp, sp, d = x_prompt.shape
    bs, ns, _ = x_sample.shape
    past = cache_attn_k.shape[2]
    assert d == D_MODEL and bs + 1 <= MOD_ROWS
    assert TOKEN_TILE % sp == 0 and (bp * sp) % TOKEN_TILE == 0
    assert ns % TOKEN_TILE == 0 and ns % MLA_Q_TILE == 0 and ns % (WIN_BLOCKS * Q_BLOCK) == 0 and ns >= BAND

    w = _prep_weights(ffn1_norm, ffn1_w_gu, ffn1_w_down, mix_norm, w_in, q_lat_norm, kv_lat_norm, w_uq, w_ukv,
                      w_o_a, w_o_b, w_out, ffn2_norm, ffn2_w_gu, ffn2_w_down, final_norm)
    sink = attn_sink[0]

    cvec = jnp.concatenate([c_ctx[None, :], c, jnp.zeros((MOD_ROWS - 1 - bs, d), F32)], axis=0)
    mods = _ada_mods(cvec, ada_w[0], ada_b)
    mods3 = mods.reshape(MOD_ROWS, 1, N_MOD * D_MODEL)

    xp = x_prompt.reshape(bp * sp, d)
    (x1p, qa, kap, qb, kb, vb_t, k_t, v_t, ckv_n, krope_t) = _pre(xp, mods3, w, False, sp, None)
    oa_p, ob_p = _att_ctx(sink, qa, kap, v_t, qb, kb, vb_t, sp)

    tiles_per_batch = ns // TOKEN_TILE
    xs = x_sample.reshape(bs * ns, d)
    (x1s, qa, kap, vad, qb, kb, vb_t) = _pre(xs, mods3, w, True, ns, _rope_tables(ns))
    feat_major = lambda a: a[:, 0].transpose(0, 2, 3, 1).reshape(bs, A_KV_W, past)
    kbc, vbc, kapc, vadc = _cache_kv(
        cache_mla_ckv[:, 0], jnp.swapaxes(cache_mla_krope[:, 0], 1, 2),
        feat_major(cache_attn_k), feat_major(cache_attn_v), w["wuk"], w["wuv_t"])
    r3 = lambda a: a.reshape(bs, ns, a.shape[1])
    oa = _att_win(sink, r3(qa), r3(kap), vad, kapc, vadc)
    ob, (w["woa"], w["wob"], w["wout"], w["wgu2"], w["wd2"]) = _att_mla(
        r3(qb), r3(kb), vb_t, kbc, vbc, w["post_f32"])

    y_prompt = _post(x1p, mods3, oa_p, ob_p, w, False, 1).reshape(bp, sp, d)
    y_sample = _post(x1s, mods3, oa.reshape(bs * ns, A_Q_W), ob.reshape(bs * ns, B_HEADS * B_V_DIM),
                     w, True, tiles_per_batch).reshape(bs, ns, d)

    new_attn_k = k_t.reshape(bp, 1, A_KV_HEADS, A_HEAD_DIM, sp).transpose(0, 1, 4, 2, 3)
    new_attn_v = v_t.reshape(bp, 1, A_KV_HEADS, A_HEAD_DIM, sp).transpose(0, 1, 4, 2, 3)
    new_mla_ckv = ckv_n.reshape(bp, 1, sp, KV_LORA)
    new_mla_krope = krope_t.reshape(bp, 1, B_ROPE_DIM, sp).transpose(0, 1, 3, 2)
    return (y_prompt, y_sample, new_attn_k, new_attn_v, new_mla_ckv, new_mla_krope)
```

```python
import functools

import jax
import jax.numpy as jnp
import numpy as np
from jax import lax
from jax.experimental import pallas as pl
from jax.experimental.pallas import tpu as pltpu

F32 = jnp.float32
BF16 = jnp.bfloat16

D_MODEL = 1024
N_MOD = 9
GRID_W = 64
WINDOW = 128
A_HEADS = 8
A_KV_HEADS = 2
A_HEAD_DIM = 64
A_Q_W = A_HEADS * A_HEAD_DIM
A_KV_W = A_KV_HEADS * A_HEAD_DIM
B_HEADS = 8
B_NOPE_DIM = 64
B_ROPE_DIM = 32
B_V_DIM = 64
B_QK_DIM = B_NOPE_DIM + B_ROPE_DIM
Q_LORA = 256
KV_LORA = 256
D_FF = 2816
ROPE_THETA = 10000.0
EPS = 1e-6
NEG = -1e30
A_SCALE = A_HEAD_DIM ** -0.5
B_SCALE = B_QK_DIM ** -0.5
LOG2E = 1.4426950408889634

LANES = 128
HALF = LANES // 2
FF_CHUNK = 256
N_FF_CHUNKS = D_FF // FF_CHUNK
PRE_TILE_CTX = 512
PRE_TILE_LAT = 256
POST_TILE = 512
ADA_TILE = 1024
Q_BLOCK = 128
BAND = 3 * Q_BLOCK
CTX_BATCHES = 8
SUM_ROWS = 16
LAT_Q_TILE = 512
MLA_KEY_BLOCK = 256
MOD_ROWS = 8
VMEM_LIMIT = 56 * 1024 * 1024

IN_A_W = A_Q_W + 2 * A_KV_W + Q_LORA + KV_LORA + LANES
OFF_KA = A_Q_W
OFF_VA = OFF_KA + A_KV_W
OFF_QLAT = OFF_VA + A_KV_W
OFF_CKV = OFF_QLAT + Q_LORA
OFF_KROPE = OFF_CKV + KV_LORA


def _dot(a, b):
    return jnp.dot(a, b, preferred_element_type=F32)


def _dot_nt(a, b):
    return lax.dot_general(a, b, (((1,), (1,)), ((), ())), preferred_element_type=F32)


def _rms(x, g):
    ms = jnp.mean(x * x, axis=-1, keepdims=True)
    return x * lax.rsqrt(ms + EPS) * g


def _mod_norm(x, g, mods_ref, k):
    shift = mods_ref[:, (3 * k) * D_MODEL:(3 * k + 1) * D_MODEL]
    scale = mods_ref[:, (3 * k + 1) * D_MODEL:(3 * k + 2) * D_MODEL]
    return _rms(x, g) * (1.0 + scale) + shift


def _gate(mods_ref, k):
    return mods_ref[:, (3 * k + 2) * D_MODEL:(3 * k + 3) * D_MODEL]


def _ffn(h, wgu_ref, wd_ref, hooks=None):
    def gate_up(c):
        a = _dot(h, wgu_ref[:, c * FF_CHUNK:(c + 1) * FF_CHUNK])
        u = _dot(h, wgu_ref[:, D_FF + c * FF_CHUNK:D_FF + (c + 1) * FF_CHUNK])
        return a, u

    acc = None
    nxt = gate_up(0)
    for c in range(N_FF_CHUNKS):
        a, u = nxt
        if c + 1 < N_FF_CHUNKS:
            nxt = gate_up(c + 1)
        act = (a * jax.nn.sigmoid(a) * u).astype(BF16)
        d = _dot(act, wd_ref[c * FF_CHUNK:(c + 1) * FF_CHUNK, :])
        acc = d if acc is None else acc + d
        if hooks and c in hooks:
            hooks[c]()
    return acc


def _rope(x, cos, sin_signed, dist):
    lane = lax.broadcasted_iota(jnp.int32, x.shape, 1)
    first = (lane & (2 * dist - 1)) < dist
    partner = jnp.where(first, pltpu.roll(x, LANES - dist, axis=1), pltpu.roll(x, dist, axis=1))
    return x * cos + partner * sin_signed


def _store_gqa_k(k, low, kap_ref):
    k_sw = pltpu.roll(k, HALF, axis=1)
    zero = jnp.zeros_like(k)
    kap_ref[:, 0 * LANES:1 * LANES] = jnp.where(low, k, zero).astype(BF16)
    kap_ref[:, 1 * LANES:2 * LANES] = jnp.where(low, zero, k_sw).astype(BF16)
    kap_ref[:, 2 * LANES:3 * LANES] = jnp.where(low, k_sw, zero).astype(BF16)
    kap_ref[:, 3 * LANES:4 * LANES] = jnp.where(low, zero, k).astype(BF16)


def _store_gqa_v_blocks(v, vt_ref):
    for r in range(vt_ref.shape[0]):
        vt_ref[r] = v[r * LANES:(r + 1) * LANES, :].T.astype(BF16)


def _store_seq_minor(out_ref, val):
    nb, feat, seq = out_ref.shape
    for bi in range(nb):
        out_ref[bi] = val[bi * seq:(bi + 1) * seq, :].T[:feat, :]


def _low_lanes(rows):
    return lax.broadcasted_iota(jnp.int32, (rows, LANES), 1) < HALF


def _ada_kernel(c_ref, w_ref, b_ref, o_ref):
    c = c_ref[...]
    s = (c * jax.nn.sigmoid(c)).astype(BF16)
    o_ref[...] = _dot(s, w_ref[...].astype(BF16)) + b_ref[...]


def _ada_mods(cvec, ada_w, ada_b):
    n = ada_w.shape[1]
    return pl.pallas_call(
        _ada_kernel,
        grid=(n // ADA_TILE,),
        in_specs=[
            pl.BlockSpec((MOD_ROWS, D_MODEL), lambda j: (0, 0)),
            pl.BlockSpec((D_MODEL, ADA_TILE), lambda j: (0, j)),
            pl.BlockSpec((1, ADA_TILE), lambda j: (0, j)),
        ],
        out_specs=pl.BlockSpec((MOD_ROWS, ADA_TILE), lambda j: (0, j)),
        out_shape=jax.ShapeDtypeStruct((MOD_ROWS, n), F32),
        compiler_params=pltpu.CompilerParams(dimension_semantics=("parallel",), vmem_limit_bytes=VMEM_LIMIT),
        name="ada_mods",
    )(cvec, ada_w, ada_b)


PRE_HOOK_IN_PROJ = 1
PRE_HOOK_UP_PROJ = 6


def _pre_kernel(latent, n_tiles, *refs):
    x1s_ref = refs[-1]
    if latent:
        (x_ref, mods_ref, modsp_ref, n1_ref, wgu_ref, wd_ref, nm_ref, win_ref, qn_ref, kvn_ref, wuq_ref, wuk_ref,
         wuv_ref, cosa_ref, sina_ref, cosb_ref, sinb_ref,
         x1_ref, qa_ref, kap_ref, vad_ref, qb_ref, kb_ref, vb_ref) = refs[:-1]
    else:
        (x_ref, mods_ref, modsp_ref, n1_ref, wgu_ref, wd_ref, nm_ref, win_ref, qn_ref, kvn_ref, wuq_ref, wuk_ref,
         wuv_ref,
         x1_ref, qa_ref, kap_ref, qb_ref, kb_ref, vb_ref,
         knat_ref, vnat_ref, ckvn_ref, krope_ref) = refs[:-1]
    i = pl.program_id(0)
    carry = {}

    def in_proj():
        h2 = _mod_norm(x1s_ref[...], nm_ref[...], modsp_ref, 1).astype(BF16)
        z = _dot_nt(h2, win_ref[...])
        low = _low_lanes(z.shape[0])

        for c in range(A_Q_W // LANES):
            q = z[:, c * LANES:(c + 1) * LANES]
            if latent:
                q = _rope(q, cosa_ref[...], sina_ref[...], A_HEAD_DIM // 4)
            qa_ref[:, c * LANES:(c + 1) * LANES] = (q * (A_SCALE * LOG2E)).astype(BF16)
        k = z[:, OFF_KA:OFF_KA + LANES]
        v = z[:, OFF_VA:OFF_VA + LANES]
        if latent:
            k = _rope(k, cosa_ref[...], sina_ref[...], A_HEAD_DIM // 4)
            _store_gqa_v_blocks(v, vad_ref)
        else:
            _store_seq_minor(knat_ref, k)
            _store_seq_minor(vnat_ref, v)
        _store_gqa_k(k, low, kap_ref)

        carry["q_lat"] = _rms(z[:, OFF_QLAT:OFF_QLAT + Q_LORA], qn_ref[...]).astype(BF16)
        ckv_n = _rms(z[:, OFF_CKV:OFF_CKV + KV_LORA], kvn_ref[...])
        krp = z[:, OFF_KROPE:OFF_KROPE + LANES]
        if latent:
            krp = _rope(krp, cosb_ref[...], sinb_ref[...], B_ROPE_DIM // 4)
        else:
            ckvn_ref[...] = ckv_n
            _store_seq_minor(krope_ref, pltpu.roll(krp, HALF, axis=1))
        carry["ckv_b"] = ckv_n.astype(BF16)
        carry["krp"] = krp

    def up_proj():
        q_lat, ckv_b, krp = carry["q_lat"], carry["ckv_b"], carry["krp"]
        qb = _dot(q_lat, wuq_ref[...])
        kn = _dot(ckv_b, wuk_ref[...])
        vb_ref[...] = _dot_nt(wuv_ref[...], ckv_b).astype(BF16)
        for h in range(B_HEADS):
            qh = qb[:, h * LANES:(h + 1) * LANES]
            if latent:
                qh = _rope(qh, cosb_ref[...], sinb_ref[...], B_ROPE_DIM // 4)
            qb_ref[:, h * LANES:(h + 1) * LANES] = (qh * (B_SCALE * LOG2E)).astype(BF16)
            kb_ref[:, h * LANES:(h + 1) * LANES] = (kn[:, h * LANES:(h + 1) * LANES] + krp).astype(BF16)

    def ffn(hooks):
        x = x_ref[...]
        h1 = _mod_norm(x, n1_ref[...], mods_ref, 0).astype(BF16)
        x1 = x + 0.5 * _gate(mods_ref, 0) * _ffn(h1, wgu_ref, wd_ref, hooks)
        x1_ref[...] = x1
        x1s_ref[...] = x1

    @pl.when(i == 0)
    def _():
        ffn(None)

    @pl.when(jnp.logical_and(i > 0, i < n_tiles))
    def _():
        ffn({PRE_HOOK_IN_PROJ: in_proj, PRE_HOOK_UP_PROJ: up_proj})

    @pl.when(i == n_tiles)
    def _():
        in_proj()
        up_proj()


def _const_spec(shape):
    nd = len(shape)
    return pl.BlockSpec(shape, lambda i: (0,) * nd, pipeline_mode=pl.Buffered(1))


def _pre(x, mods3, w, latent, seq, rope, tm):
    t = x.shape[0]
    tiles_per_batch = max(seq // tm, 1)
    n_tiles = t // tm
    cur = lambda i: jnp.minimum(i, n_tiles - 1)
    prev = lambda i: jnp.maximum(i - 1, 0)
    mod_of = (lambda j: 1 + j // tiles_per_batch) if latent else (lambda j: 0)
    cur_rows = lambda i: (cur(i), 0)
    prev_rows = lambda i: (prev(i), 0)

    weights = [w["n1"], w["wgu1"], w["wd1"], w["nm"], w["win_a"], w["qn"], w["kvn"], w["wuq"], w["wuk"], w["wuv_t"]]
    in_specs = [pl.BlockSpec((tm, D_MODEL), cur_rows),
                pl.BlockSpec((None, 1, N_MOD * D_MODEL), lambda i: (mod_of(cur(i)), 0, 0)),
                pl.BlockSpec((None, 1, N_MOD * D_MODEL), lambda i: (mod_of(prev(i)), 0, 0))]
    in_specs += [_const_spec(a.shape) for a in weights]
    args = [x, mods3, mods3] + weights
    if latent:
        in_specs += [pl.BlockSpec((tm, LANES), lambda i: (prev(i) % tiles_per_batch, 0))] * 4
        args += list(rope)

    out_specs = [pl.BlockSpec((tm, D_MODEL), cur_rows)]
    out_shape = [jax.ShapeDtypeStruct((t, D_MODEL), F32)]
    for wd in (A_Q_W, 4 * LANES):
        out_specs.append(pl.BlockSpec((tm, wd), prev_rows))
        out_shape.append(jax.ShapeDtypeStruct((t, wd), BF16))
    if latent:
        out_specs.append(pl.BlockSpec((tm // LANES, A_KV_W, LANES), lambda i: (prev(i), 0, 0)))
        out_shape.append(jax.ShapeDtypeStruct((t // LANES, A_KV_W, LANES), BF16))
    for wd in (B_HEADS * LANES, B_HEADS * LANES):
        out_specs.append(pl.BlockSpec((tm, wd), prev_rows))
        out_shape.append(jax.ShapeDtypeStruct((t, wd), BF16))
    out_specs.append(pl.BlockSpec((B_HEADS * B_V_DIM, tm), lambda i: (0, prev(i))))
    out_shape.append(jax.ShapeDtypeStruct((B_HEADS * B_V_DIM, t), BF16))
    if not latent:
        nb = tm // seq
        for feat in (A_KV_W, A_KV_W):
            out_specs.append(pl.BlockSpec((nb, feat, seq), lambda i: (prev(i), 0, 0)))
            out_shape.append(jax.ShapeDtypeStruct((t // seq, feat, seq), F32))
        out_specs.append(pl.BlockSpec((tm, KV_LORA), prev_rows))
        out_shape.append(jax.ShapeDtypeStruct((t, KV_LORA), F32))
        out_specs.append(pl.BlockSpec((nb, B_ROPE_DIM, seq), lambda i: (prev(i), 0, 0)))
        out_shape.append(jax.ShapeDtypeStruct((t // seq, B_ROPE_DIM, seq), F32))

    return pl.pallas_call(
        functools.partial(_pre_kernel, latent, n_tiles),
        grid=(n_tiles + 1,),
        in_specs=in_specs,
        out_specs=out_specs,
        out_shape=out_shape,
        scratch_shapes=[pltpu.VMEM((tm, D_MODEL), F32)],
        compiler_params=pltpu.CompilerParams(dimension_semantics=("arbitrary",), vmem_limit_bytes=VMEM_LIMIT),
        name="pre_latent" if latent else "pre_context",
    )(*args)


def _cache_kernel(ckv_ref, kropet_ref, ckt_ref, cvt_ref, wuk_ref, wuv_ref, kb_ref, vb_ref, kap_ref, vat_ref):
    p = ckv_ref.shape[0]
    c = ckv_ref[...].astype(BF16)
    kn = _dot(c, wuk_ref[...])
    krp = jnp.concatenate([jnp.zeros((B_NOPE_DIM, p), F32), kropet_ref[...],
                           jnp.zeros((LANES - B_QK_DIM, p), F32)], axis=0).T
    for h in range(B_HEADS):
        kb_ref[:, h * LANES:(h + 1) * LANES] = (kn[:, h * LANES:(h + 1) * LANES] + krp).astype(BF16)
    vb_ref[...] = _dot_nt(wuv_ref[...], c).astype(BF16)
    _store_gqa_k(ckt_ref[...].T, _low_lanes(p), kap_ref)
    vat_ref[...] = cvt_ref[...].astype(BF16)


def _cache_kv(ckv, krope_t, ck_t, cv_t, wuk, wuv_t):
    b, p, _ = ckv.shape
    blk = lambda wd: pl.BlockSpec((None, p, wd), lambda i: (i, 0, 0))
    blk_t = lambda feat: pl.BlockSpec((None, feat, p), lambda i: (i, 0, 0))
    vdim = B_HEADS * B_V_DIM
    return pl.pallas_call(
        _cache_kernel,
        grid=(b,),
        in_specs=[blk(KV_LORA), blk_t(B_ROPE_DIM), blk_t(A_KV_W), blk_t(A_KV_W),
                  pl.BlockSpec(wuk.shape, lambda i: (0, 0)),
                  pl.BlockSpec(wuv_t.shape, lambda i: (0, 0))],
        out_specs=[blk(B_HEADS * LANES), pl.BlockSpec((vdim, p), lambda i: (0, i)), blk(4 * LANES),
                   pl.BlockSpec((A_KV_W, p), lambda i: (0, i))],
        out_shape=[jax.ShapeDtypeStruct((b, p, B_HEADS * LANES), BF16),
                   jax.ShapeDtypeStruct((vdim, b * p), BF16),
                   jax.ShapeDtypeStruct((b, p, 4 * LANES), BF16),
                   jax.ShapeDtypeStruct((A_KV_W, b * p), BF16)],
        compiler_params=pltpu.CompilerParams(dimension_semantics=("parallel",), vmem_limit_bytes=VMEM_LIMIT),
        name="cache_kv",
    )(ckv, krope_t, ck_t, cv_t, wuk, wuv_t)


def _att_ctx_kernel(seq, sink_ref, qa_ref, kap_ref, vt_ref, qb_ref, kb_ref, vbt_ref, oa_ref, ob_ref):
    nb = qa_ref.shape[0] // seq
    ones = jnp.ones((SUM_ROWS, seq), BF16)
    upper = lax.broadcasted_iota(jnp.int32, (1, 2 * seq), 1) >= seq
    units = []
    for bi in range(nb):
        units += [("a", bi, kv, half) for kv in range(A_KV_HEADS) for half in range(2)]
        units += [("b", bi, h, h % 2) for h in range(B_HEADS)]

    def score(unit):
        kind, bi, idx, half = unit
        rows = slice(bi * seq, (bi + 1) * seq)
        if kind == "a":
            kv, u = idx, 2 * idx + half
            q2 = jnp.concatenate([qa_ref[rows, (2 * kv) * LANES:(2 * kv + 1) * LANES],
                                  qa_ref[rows, (2 * kv + 1) * LANES:(2 * kv + 2) * LANES]], axis=0)
            return _dot_nt(kap_ref[rows, u * LANES:(u + 1) * LANES], q2)
        return _dot_nt(kb_ref[rows, idx * LANES:(idx + 1) * LANES], qb_ref[rows, idx * LANES:(idx + 1) * LANES])

    def values(unit):
        kind, bi, idx, _ = unit
        if kind == "a":
            v = vt_ref[bi, idx * A_HEAD_DIM:(idx + 1) * A_HEAD_DIM, :].astype(BF16)
        else:
            v = vbt_ref[idx * B_V_DIM:(idx + 1) * B_V_DIM, bi * seq:(bi + 1) * seq]
        return jnp.concatenate([v, ones], axis=0)

    def sink_row(unit):
        kind, _, kv, half = unit
        if kind == "b":
            return None
        return jnp.where(upper, sink_ref[4 * kv + 2 + half], sink_ref[4 * kv + half]) * LOG2E

    def finish_pair(first, second, o_first, o_second):
        kind, bi, idx, _ = second
        rows = slice(bi * seq, (bi + 1) * seq)
        if kind == "a":
            for pr in range(2):
                cols = slice(pr * seq, (pr + 1) * seq)
                pair = jnp.concatenate([o_first[:, cols], o_second[:, cols]], axis=0)
                c = 2 * idx + pr
                oa_ref[rows, c * LANES:(c + 1) * LANES] = pair.T.astype(BF16)
        else:
            c = idx // 2
            ob_ref[rows, c * LANES:(c + 1) * LANES] = jnp.concatenate([o_first, o_second], axis=0).T.astype(BF16)

    scores, probs, outs = {}, {}, {}
    for t in range(len(units) + 2):
        if t < len(units):
            scores[t] = score(units[t])
        if 0 <= t - 2 < len(units):
            n = t - 2
            p_blocks, p_sink = probs.pop(n)
            acc = _dot(values(units[n]), p_blocks[0])
            den = acc[A_HEAD_DIM:A_HEAD_DIM + 1]
            outs[n] = acc[:A_HEAD_DIM] / (den if p_sink is None else den + p_sink)
            if units[n][3] == 1:
                finish_pair(units[n - 1], units[n], outs.pop(n - 1), outs.pop(n))
        if 0 <= t - 1 < len(units):
            n = t - 1
            probs[n] = _probs_keys_major([scores.pop(n)], sink_row(units[n]))


def _att_ctx(sink, qa, kap, v_t, qb, kb, vb_t, seq):
    t = qa.shape[0]
    rows = CTX_BATCHES * seq
    blk = lambda wd: pl.BlockSpec((rows, wd), lambda b: (b, 0))
    return pl.pallas_call(
        functools.partial(_att_ctx_kernel, seq),
        grid=(t // rows,),
        in_specs=[pl.BlockSpec(memory_space=pltpu.SMEM),
                  blk(qa.shape[1]), blk(kap.shape[1]),
                  pl.BlockSpec((CTX_BATCHES, A_KV_W, seq), lambda b: (b, 0, 0)),
                  blk(qb.shape[1]), blk(kb.shape[1]),
                  pl.BlockSpec((vb_t.shape[0], rows), lambda b: (0, b))],
        out_specs=[blk(A_Q_W), blk(B_HEADS * B_V_DIM)],
        out_shape=[jax.ShapeDtypeStruct((t, A_Q_W), BF16), jax.ShapeDtypeStruct((t, B_HEADS * B_V_DIM), BF16)],
        compiler_params=pltpu.CompilerParams(dimension_semantics=("parallel",), vmem_limit_bytes=VMEM_LIMIT),
        name="att_context",
    )(sink, qa, kap, v_t, qb, kb, vb_t)


class _Unit:
    def __init__(self, scores, sink, values, finish):
        self.scores, self.sink, self.values, self.finish = scores, sink, values, finish


def _attention_pipeline(groups, dv):
    n = len(groups)
    scores, probs = {}, {}
    for t in range(n + 2):
        if t < n:
            scores[t] = [unit.scores() for unit in groups[t]]
        if 0 <= t - 2 < n:
            for unit, (p_blocks, p_sink) in zip(groups[t - 2], probs.pop(t - 2)):
                acc = unit.values(p_blocks)
                den = acc[dv:dv + 1]
                unit.finish(acc[:dv] / (den if p_sink is None else den + p_sink))
        if 0 <= t - 1 < n:
            probs[t - 1] = [_probs_keys_major(s, unit.sink()) for unit, s in zip(groups[t - 1], scores.pop(t - 1))]


def _window_units(n_lat, tile, sink_ref, qa_ref, kap_ref, vat_ref, kapc_ref, vatc_ref, oa_ref):
    n_blocks = qa_ref.shape[0] // Q_BLOCK
    nq = 2 * Q_BLOCK
    past = kapc_ref.shape[0]
    upper = lax.broadcasted_iota(jnp.int32, (1, nq), 1) >= Q_BLOCK
    ones_band = jnp.ones((SUM_ROWS, BAND), BF16)
    ones_ctx = jnp.ones((SUM_ROWS, past), BF16)
    key = lax.broadcasted_iota(jnp.int32, (BAND, nq), 0)
    qry = lax.broadcasted_iota(jnp.int32, (BAND, nq), 1) & (Q_BLOCK - 1)
    key_minus_qry = key - qry
    pending = {}

    def make(qb, kv, half, start, valid, v_band):
        u = 2 * kv + half
        rows = slice(qb * Q_BLOCK, (qb + 1) * Q_BLOCK)
        feat = slice(kv * A_HEAD_DIM, (kv + 1) * A_HEAD_DIM)

        def scores():
            q2 = jnp.concatenate([qa_ref[rows, (2 * kv) * LANES:(2 * kv + 1) * LANES],
                                  qa_ref[rows, (2 * kv + 1) * LANES:(2 * kv + 2) * LANES]], axis=0)
            s_band = _dot_nt(kap_ref[pl.ds(start, BAND), u * LANES:(u + 1) * LANES], q2)
            s_ctx = _dot_nt(kapc_ref[:, u * LANES:(u + 1) * LANES], q2)
            return [jnp.where(valid, s_band, NEG), s_ctx]

        def sink():
            return jnp.where(upper, sink_ref[4 * kv + 2 + half], sink_ref[4 * kv + half]) * LOG2E

        def values(p_blocks):
            vb = jnp.concatenate([v_band[r][feat] for r in range(BAND // LANES)], axis=1)
            r_band = _dot(jnp.concatenate([vb, ones_band], axis=0), p_blocks[0])
            r_ctx = _dot(jnp.concatenate([vatc_ref[feat, :], ones_ctx], axis=0), p_blocks[1])
            return r_band + r_ctx

        def finish(o):
            if half == 0:
                pending[(qb, kv)] = o
                return
            first = pending.pop((qb, kv))
            for pr in range(2):
                cols = slice(pr * Q_BLOCK, (pr + 1) * Q_BLOCK)
                pair = jnp.concatenate([first[:, cols], o[:, cols]], axis=0)
                c = 2 * kv + pr
                oa_ref[rows, c * LANES:(c + 1) * LANES] = pair.T.astype(BF16)

        return _Unit(scores, sink, values, finish)

    units = []
    for qb in range(n_blocks):
        i = tile * n_blocks + qb
        start = pl.multiple_of(jnp.clip(i * Q_BLOCK - Q_BLOCK, 0, n_lat - BAND), Q_BLOCK)
        valid = jnp.abs(key_minus_qry + (start - i * Q_BLOCK)) <= WINDOW
        v_band = vat_ref[pl.ds(start // LANES, BAND // LANES)]
        units += [make(qb, kv, half, start, valid, v_band) for kv in range(A_KV_HEADS) for half in range(2)]
    return units


def _probs_keys_major(scores_t, sink=None):
    m = None
    for s in scores_t:
        mx = jnp.max(s, axis=0, keepdims=True)
        m = mx if m is None else jnp.maximum(m, mx)
    if sink is not None:
        m = jnp.maximum(m, sink)
    probs = [jnp.exp2(s - m).astype(BF16) for s in scores_t]
    return probs, (None if sink is None else jnp.exp2(sink - m))


def _mla_units(qb_ref, kbl_ref, vtl_ref, kbc_ref, vtc_ref, ob_ref):
    kb = MLA_KEY_BLOCK
    ones = jnp.ones((SUM_ROWS, kb), BF16)
    blocks = [(kbc_ref, vtc_ref, j) for j in range(kbc_ref.shape[0] // kb)]
    blocks += [(kbl_ref, vtl_ref, j) for j in range(kbl_ref.shape[0] // kb)]
    pending = {}

    def make(h):
        lanes = slice(h * LANES, (h + 1) * LANES)
        feat = slice(h * B_V_DIM, (h + 1) * B_V_DIM)

        def scores():
            return [_dot_nt(k_ref[j * kb:(j + 1) * kb, lanes], qb_ref[:, lanes]) for k_ref, _, j in blocks]

        def values(p_blocks):
            acc = None
            for (_, v_ref, j), p in zip(blocks, p_blocks):
                r = _dot(jnp.concatenate([v_ref[feat, j * kb:(j + 1) * kb], ones], axis=0), p)
                acc = r if acc is None else acc + r
            return acc

        def finish(o):
            if h % 2 == 0:
                pending[h] = o
                return
            c = h // 2
            pair = jnp.concatenate([pending.pop(h - 1), o], axis=0)
            ob_ref[:, c * LANES:(c + 1) * LANES] = pair.T.astype(BF16)

        return _Unit(scores, lambda: None, values, finish)

    return [make(h) for h in range(B_HEADS)]


def _att_lat_kernel(n_cast, n_lat, sink_ref, qa_ref, kap_ref, vat_ref, kapc_ref, vatc_ref,
                    qb_ref, kbl_ref, vtl_ref, kbc_ref, vtc_ref, *rest):
    w32_refs, (oa_ref, ob_ref), w16_refs = rest[:n_cast], rest[n_cast:n_cast + 2], rest[n_cast + 2:]
    for src, dst in zip(w32_refs, w16_refs):
        dst[...] = src[...].astype(BF16)

    mla = _mla_units(qb_ref, kbl_ref, vtl_ref, kbc_ref, vtc_ref, ob_ref)
    win = _window_units(n_lat, pl.program_id(1), sink_ref, qa_ref, kap_ref, vat_ref, kapc_ref, vatc_ref, oa_ref)
    assert len(win) % len(mla) == 0
    per_head = len(win) // len(mla)
    groups = [[unit] + win[h * per_head:(h + 1) * per_head] for h, unit in enumerate(mla)]
    _attention_pipeline(groups, B_V_DIM)


def _att_lat(sink, qa, kap, vat, kapc, vatc, qb, kbl, vtl, kbc, vtc, cast_weights):
    b, n, _ = qb.shape
    past = kbc.shape[1]
    tq = LAT_Q_TILE
    vdim = B_HEADS * B_V_DIM
    q_tiles = n // tq
    steps = b * q_tiles
    w_specs = []
    for wm in cast_weights:
        rows, cols = wm.shape
        assert rows % (steps * 16) == 0, "row blocks must be whole packed-bf16 sublane tiles"
        w_specs.append(pl.BlockSpec((rows // steps, cols), lambda bi, i: (bi * q_tiles + i, 0)))
    q_blk = lambda wd: pl.BlockSpec((None, tq, wd), lambda bi, i: (bi, i, 0))
    rows_blk = lambda r, wd: pl.BlockSpec((None, r, wd), lambda bi, i: (bi, 0, 0))
    feat_blk = lambda feat, r: pl.BlockSpec((feat, r), lambda bi, i: (0, bi))
    out = pl.pallas_call(
        functools.partial(_att_lat_kernel, len(cast_weights), n),
        grid=(b, q_tiles),
        in_specs=[pl.BlockSpec(memory_space=pltpu.SMEM),
                  q_blk(A_Q_W), rows_blk(n, kap.shape[2]),
                  pl.BlockSpec((n // LANES, A_KV_W, LANES), lambda bi, i: (bi, 0, 0)),
                  rows_blk(past, kapc.shape[2]), feat_blk(A_KV_W, past),
                  q_blk(qb.shape[2]), rows_blk(n, kbl.shape[2]), feat_blk(vdim, n),
                  rows_blk(past, kbc.shape[2]), feat_blk(vdim, past)] + w_specs,
        out_specs=[q_blk(A_Q_W), q_blk(vdim)] + w_specs,
        out_shape=[jax.ShapeDtypeStruct((b, n, A_Q_W), BF16), jax.ShapeDtypeStruct((b, n, vdim), BF16)]
        + [jax.ShapeDtypeStruct(wm.shape, BF16) for wm in cast_weights],
        compiler_params=pltpu.CompilerParams(dimension_semantics=("parallel", "parallel"),
                                             vmem_limit_bytes=VMEM_LIMIT),
        name="att_latent",
    )(sink, qa, kap, vat, kapc, vatc, qb, kbl, vtl, kbc, vtc, *cast_weights)
    return out[0], out[1], out[2:]


def _post_kernel(x1_ref, mods_ref, oa_ref, ob_ref, nm_ref, wing_ref, woa_ref, wob_ref, wout_ref,
                 n2_ref, wgu_ref, wd_ref, nf_ref, y_ref):
    x1 = x1_ref[...]
    h2 = _mod_norm(x1, nm_ref[...], mods_ref, 1).astype(BF16)
    g = _dot_nt(h2, wing_ref[...])
    m = (jax.nn.sigmoid(g[:, :D_MODEL]) * _dot(oa_ref[...], woa_ref[...])
         + jax.nn.sigmoid(g[:, D_MODEL:]) * _dot(ob_ref[...], wob_ref[...]))
    x2 = x1 + _gate(mods_ref, 1) * _dot(m.astype(BF16), wout_ref[...])
    h3 = _mod_norm(x2, n2_ref[...], mods_ref, 2).astype(BF16)
    x3 = x2 + 0.5 * _gate(mods_ref, 2) * _ffn(h3, wgu_ref, wd_ref)
    y_ref[...] = _rms(x3, nf_ref[...])


def _post(x1, mods3, oa, ob, w, latent, tiles_per_batch):
    t = x1.shape[0]
    tm = POST_TILE
    if latent:
        mod_row = lambda i: (1 + i // tiles_per_batch, 0, 0)
    else:
        mod_row = lambda i: (0, 0, 0)
    row_blk = lambda i: (i, 0)
    weights = [w["nm"], w["win_g"], w["woa"], w["wob"], w["wout"], w["n2"], w["wgu2"], w["wd2"], w["nf"]]
    in_specs = [pl.BlockSpec((tm, D_MODEL), row_blk),
                pl.BlockSpec((None, 1, N_MOD * D_MODEL), mod_row),
                pl.BlockSpec((tm, A_Q_W), row_blk),
                pl.BlockSpec((tm, B_HEADS * B_V_DIM), row_blk)]
    in_specs += [_const_spec(a.shape) for a in weights]
    return pl.pallas_call(
        _post_kernel,
        grid=(t // tm,),
        in_specs=in_specs,
        out_specs=pl.BlockSpec((tm, D_MODEL), row_blk),
        out_shape=jax.ShapeDtypeStruct((t, D_MODEL), F32),
        compiler_params=pltpu.CompilerParams(dimension_semantics=("parallel",), vmem_limit_bytes=VMEM_LIMIT),
        name="post_latent" if latent else "post_context",
    )(x1, mods3, oa, ob, *weights)


def _rope_tables(n):
    f32 = np.float32
    rows = n // GRID_W
    t_row = np.repeat(np.arange(rows, dtype=f32), GRID_W)
    t_col = np.tile(np.arange(GRID_W, dtype=f32), rows)

    def angles(d_rot):
        d_half = d_rot // 2
        inv = (f32(1.0) / np.power(f32(ROPE_THETA), np.arange(0, d_half, 2, dtype=f32) / f32(d_half))).astype(f32)
        ar = t_row[:, None] * inv[None, :]
        ac = t_col[:, None] * inv[None, :]
        return np.concatenate([ar, ar, ac, ac], axis=-1).astype(f32)

    def signed(sin, d_rot):
        q = d_rot // 4
        sign = np.where((np.arange(d_rot) % (2 * q)) < q, f32(-1.0), f32(1.0)).astype(f32)
        return sin * sign[None, :]

    ang_a = angles(A_HEAD_DIM)
    cos_a = np.tile(np.cos(ang_a), (1, LANES // A_HEAD_DIM))
    sin_a = np.tile(signed(np.sin(ang_a), A_HEAD_DIM), (1, LANES // A_HEAD_DIM))
    ang_b = angles(B_ROPE_DIM)
    pad = ((0, 0), (B_NOPE_DIM, LANES - B_NOPE_DIM - B_ROPE_DIM))
    cos_b = np.pad(np.cos(ang_b), pad, constant_values=1.0)
    sin_b = np.pad(signed(np.sin(ang_b), B_ROPE_DIM), pad)
    return tuple(jnp.asarray(t, dtype=F32) for t in (cos_a, sin_a, cos_b, sin_b))


def _prep_weights(ffn1_norm, ffn1_w_gu, ffn1_w_down, mix_norm, w_in, q_lat_norm, kv_lat_norm, w_uq, w_ukv,
                  w_o_a, w_o_b, w_out, ffn2_norm, ffn2_w_gu, ffn2_w_down, final_norm):
    win_t = jnp.swapaxes(w_in[0], 0, 1)
    n_attn = A_Q_W + 2 * A_KV_W + Q_LORA + KV_LORA
    krope_rows = jnp.pad(win_t[n_attn:n_attn + B_ROPE_DIM],
                         ((B_NOPE_DIM, LANES - B_NOPE_DIM - B_ROPE_DIM), (0, 0)))
    win_a = jnp.concatenate([win_t[:n_attn], krope_rows], axis=0).astype(BF16)
    win_g = win_t[n_attn + B_ROPE_DIM:].astype(BF16)

    wuq = jnp.pad(w_uq[0].reshape(Q_LORA, B_HEADS, B_QK_DIM),
                  ((0, 0), (0, 0), (0, LANES - B_QK_DIM))).reshape(Q_LORA, B_HEADS * LANES).astype(BF16)
    wukv = w_ukv[0].reshape(KV_LORA, B_HEADS, B_NOPE_DIM + B_V_DIM)
    wuk = jnp.pad(wukv[:, :, :B_NOPE_DIM],
                  ((0, 0), (0, 0), (0, LANES - B_NOPE_DIM))).reshape(KV_LORA, B_HEADS * LANES).astype(BF16)
    wuv = wukv[:, :, B_NOPE_DIM:].reshape(KV_LORA, B_HEADS * B_V_DIM).astype(BF16)

    return {
        "n1": ffn1_norm, "wgu1": ffn1_w_gu[0].astype(BF16), "wd1": ffn1_w_down[0].astype(BF16),
        "nm": mix_norm, "win_a": win_a, "win_g": win_g,
        "qn": q_lat_norm, "kvn": kv_lat_norm, "wuq": wuq, "wuk": wuk, "wuv_t": wuv.T,
        "n2": ffn2_norm, "nf": final_norm.reshape(1, D_MODEL),
        "post_f32": [w_o_a[0], w_o_b[0], w_out[0], ffn2_w_gu[0], ffn2_w_down[0]],
    }


def kernel(x_prompt, x_sample, cache_attn_k, cache_attn_v, cache_mla_ckv, cache_mla_krope, c, c_ctx, ada_w, ada_b, ffn1_norm, ffn1_w_gu, ffn1_w_down, mix_norm, w_in, attn_sink, q_lat_norm, kv_lat_norm, w_uq, w_ukv, w_o_a, w_o_b, w_out, ffn2_norm, ffn2_w_gu, ffn2_w_down, final_norm):
    assert ada_w.shape[0] == 1, "single trunk layer"
    bp, sp, d = x_prompt.shape
    bs, ns, _ = x_sample.shape
    past = cache_attn_k.shape[2]
    assert d == D_MODEL and bs + 1 <= MOD_ROWS
    assert PRE_TILE_CTX % sp == 0 and (bp * sp) % PRE_TILE_CTX == 0 and (bp * sp) % POST_TILE == 0
    assert ns % PRE_TILE_LAT == 0 and ns % POST_TILE == 0
    assert ns % LAT_Q_TILE == 0 and LAT_Q_TILE % Q_BLOCK == 0 and ns >= BAND

    w = _prep_weights(ffn1_norm, ffn1_w_gu, ffn1_w_down, mix_norm, w_in, q_lat_norm, kv_lat_norm, w_uq, w_ukv,
                      w_o_a, w_o_b, w_out, ffn2_norm, ffn2_w_gu, ffn2_w_down, final_norm)
    sink = attn_sink[0]

    cvec = jnp.concatenate([c_ctx[None, :], c, jnp.zeros((MOD_ROWS - 1 - bs, d), F32)], axis=0)
    mods = _ada_mods(cvec, ada_w[0], ada_b)
    mods3 = mods.reshape(MOD_ROWS, 1, N_MOD * D_MODEL)

    xp = x_prompt.reshape(bp * sp, d)
    (x1p, qa, kap, qb, kb, vb_t, k_t, v_t, ckv_n, krope_t) = _pre(xp, mods3, w, False, sp, None, PRE_TILE_CTX)
    oa_p, ob_p = _att_ctx(sink, qa, kap, v_t, qb, kb, vb_t, sp)

    xs = x_sample.reshape(bs * ns, d)
    (x1s, qa, kap, vad, qb, kb, vb_t) = _pre(xs, mods3, w, True, ns, _rope_tables(ns), PRE_TILE_LAT)
    feat_major = lambda a: a[:, 0].transpose(0, 2, 3, 1).reshape(bs, A_KV_W, past)
    kbc, vbc, kapc, vadc = _cache_kv(
        cache_mla_ckv[:, 0], jnp.swapaxes(cache_mla_krope[:, 0], 1, 2),
        feat_major(cache_attn_k), feat_major(cache_attn_v), w["wuk"], w["wuv_t"])
    r3 = lambda a: a.reshape(bs, ns, a.shape[1])
    oa, ob, (w["woa"], w["wob"], w["wout"], w["wgu2"], w["wd2"]) = _att_lat(
        sink, r3(qa), r3(kap), vad, kapc, vadc, r3(qb), r3(kb), vb_t, kbc, vbc, w["post_f32"])

    y_prompt = _post(x1p, mods3, oa_p, ob_p, w, False, 1).reshape(bp, sp, d)
    y_sample = _post(x1s, mods3, oa.reshape(bs * ns, A_Q_W), ob.reshape(bs * ns, B_HEADS * B_V_DIM),
                     w, True, ns // POST_TILE).reshape(bs, ns, d)

    new_attn_k = k_t.reshape(bp, 1, A_KV_HEADS, A_HEAD_DIM, sp).transpose(0, 1, 4, 2, 3)
    new_attn_v = v_t.reshape(bp, 1, A_KV_HEADS, A_HEAD_DIM, sp).transpose(0, 1, 4, 2, 3)
    new_mla_ckv = ckv_n.reshape(bp, 1, sp, KV_LORA)
    new_mla_krope = krope_t.reshape(bp, 1, B_ROPE_DIM, sp).transpose(0, 1, 3, 2)
    return (y_prompt, y_sample, new_attn_k, new_attn_v, new_mla_ckv, new_mla_krope)
```

```python
import functools
import math

import jax
import jax.numpy as jnp
import numpy as np
from jax import lax
from jax.experimental import pallas as pl
from jax.experimental.pallas import tpu as pltpu

F32 = jnp.float32
BF16 = jnp.bfloat16

D_MODEL = 1024
N_MOD = 9
GRID_W = 64
WINDOW = 128
A_HEADS = 8
A_KV_HEADS = 2
A_HEAD_DIM = 64
A_Q_W = A_HEADS * A_HEAD_DIM
A_KV_W = A_KV_HEADS * A_HEAD_DIM
B_HEADS = 8
B_NOPE_DIM = 64
B_ROPE_DIM = 32
B_V_DIM = 64
B_QK_DIM = B_NOPE_DIM + B_ROPE_DIM
Q_LORA = 256
KV_LORA = 256
D_FF = 2816
ROPE_THETA = 10000.0
EPS = 1e-6
NEG = -1e30
A_SCALE = A_HEAD_DIM ** -0.5
B_SCALE = B_QK_DIM ** -0.5
LOG2E = 1.4426950408889634

LANES = 128
HALF = LANES // 2
FF_CHUNK = 256
N_FF_CHUNKS = D_FF // FF_CHUNK
PRE_TILE_CTX = 512
PRE_TILE_LAT = 256
POST_TILE = 512
ADA_TILE = 1024
Q_BLOCK = 128
BAND = 3 * Q_BLOCK
CTX_BATCHES = 4
BF16_TILE_ROWS = 16
SUM_ROWS = BF16_TILE_ROWS
LAT_Q_TILE = 512
MLA_KEY_BLOCK = 256
MOD_ROWS = 8
VMEM_LIMIT = 56 * 1024 * 1024

IN_A_W = A_Q_W + 2 * A_KV_W + Q_LORA + KV_LORA + LANES
OFF_KA = A_Q_W
OFF_VA = OFF_KA + A_KV_W
OFF_QLAT = OFF_VA + A_KV_W
OFF_CKV = OFF_QLAT + Q_LORA
OFF_KROPE = OFF_CKV + KV_LORA


def _dot(a, b):
    return jnp.dot(a, b, preferred_element_type=F32)


def _dot_nt(a, b):
    return lax.dot_general(a, b, (((1,), (1,)), ((), ())), preferred_element_type=F32)


def _rms(x, g):
    ms = jnp.mean(x * x, axis=-1, keepdims=True)
    return x * lax.rsqrt(ms + EPS) * g


def _mod_norm(x, g, mods_ref, k):
    shift = mods_ref[:, (3 * k) * D_MODEL:(3 * k + 1) * D_MODEL]
    scale = mods_ref[:, (3 * k + 1) * D_MODEL:(3 * k + 2) * D_MODEL]
    return _rms(x, g) * (1.0 + scale) + shift


def _gate(mods_ref, k):
    return mods_ref[:, (3 * k + 2) * D_MODEL:(3 * k + 3) * D_MODEL]


def _ffn(h, wgu_ref, wd_ref, hooks=None):
    def gate_up(c):
        a = _dot(h, wgu_ref[:, c * FF_CHUNK:(c + 1) * FF_CHUNK])
        u = _dot(h, wgu_ref[:, D_FF + c * FF_CHUNK:D_FF + (c + 1) * FF_CHUNK])
        return a, u

    acc = None
    nxt = gate_up(0)
    for c in range(N_FF_CHUNKS):
        a, u = nxt
        if c + 1 < N_FF_CHUNKS:
            nxt = gate_up(c + 1)
        act = (a * jax.nn.sigmoid(a) * u).astype(BF16)
        d = _dot(act, wd_ref[c * FF_CHUNK:(c + 1) * FF_CHUNK, :])
        acc = d if acc is None else acc + d
        if hooks and c in hooks:
            hooks[c]()
    return acc


def _rope(x, cos, sin_signed, dist):
    lane = lax.broadcasted_iota(jnp.int32, x.shape, 1)
    first = (lane & (2 * dist - 1)) < dist
    partner = jnp.where(first, pltpu.roll(x, LANES - dist, axis=1), pltpu.roll(x, dist, axis=1))
    return x * cos + partner * sin_signed


def _store_gqa_k(k, low, kap_ref):
    k_sw = pltpu.roll(k, HALF, axis=1)
    zero = jnp.zeros_like(k)
    kap_ref[:, 0 * LANES:1 * LANES] = jnp.where(low, k, zero).astype(BF16)
    kap_ref[:, 1 * LANES:2 * LANES] = jnp.where(low, zero, k_sw).astype(BF16)
    kap_ref[:, 2 * LANES:3 * LANES] = jnp.where(low, k_sw, zero).astype(BF16)
    kap_ref[:, 3 * LANES:4 * LANES] = jnp.where(low, zero, k).astype(BF16)


def _store_gqa_v_blocks(v, vt_ref):
    for r in range(vt_ref.shape[0]):
        vt_ref[r] = v[r * LANES:(r + 1) * LANES, :].T.astype(BF16)


def _store_seq_minor(out_ref, val):
    nb, feat, seq = out_ref.shape
    for bi in range(nb):
        out_ref[bi] = val[bi * seq:(bi + 1) * seq, :].T[:feat, :]


def _low_lanes(rows):
    return lax.broadcasted_iota(jnp.int32, (rows, LANES), 1) < HALF


def _ada_kernel(c_ref, w_ref, b_ref, o_ref):
    c = c_ref[...]
    s = (c * jax.nn.sigmoid(c)).astype(BF16)
    o_ref[...] = _dot(s, w_ref[...].astype(BF16)) + b_ref[...]


def _ada_mods(cvec, ada_w, ada_b):
    n = ada_w.shape[1]
    return pl.pallas_call(
        _ada_kernel,
        grid=(n // ADA_TILE,),
        in_specs=[
            pl.BlockSpec((MOD_ROWS, D_MODEL), lambda j: (0, 0)),
            pl.BlockSpec((D_MODEL, ADA_TILE), lambda j: (0, j)),
            pl.BlockSpec((1, ADA_TILE), lambda j: (0, j)),
        ],
        out_specs=pl.BlockSpec((MOD_ROWS, ADA_TILE), lambda j: (0, j)),
        out_shape=jax.ShapeDtypeStruct((MOD_ROWS, n), F32),
        compiler_params=pltpu.CompilerParams(dimension_semantics=("parallel",), vmem_limit_bytes=VMEM_LIMIT),
        name="ada_mods",
    )(cvec, ada_w, ada_b)


PRE_HOOK_IN_PROJ = 1
PRE_HOOK_UP_PROJ = 6


def _pre_kernel(latent, n_tiles, *refs):
    x1s_ref = refs[-1]
    if latent:
        (x_ref, mods_ref, modsp_ref, n1_ref, wgu_ref, wd_ref, nm_ref, win_ref, qn_ref, kvn_ref, wuq_ref, wuk_ref,
         wuv_ref, cosa_ref, sina_ref, cosb_ref, sinb_ref,
         x1_ref, qa_ref, kap_ref, vad_ref, qb_ref, kb_ref, vb_ref) = refs[:-1]
    else:
        (x_ref, mods_ref, modsp_ref, n1_ref, wgu_ref, wd_ref, nm_ref, win_ref, qn_ref, kvn_ref, wuq_ref, wuk_ref,
         wuv_ref,
         x1_ref, qa_ref, kap_ref, qb_ref, kb_ref, vb_ref,
         knat_ref, vnat_ref, ckvn_ref, krope_ref) = refs[:-1]
    i = pl.program_id(0)
    carry = {}

    def in_proj():
        h2 = _mod_norm(x1s_ref[...], nm_ref[...], modsp_ref, 1).astype(BF16)
        z = _dot_nt(h2, win_ref[...])
        low = _low_lanes(z.shape[0])

        for c in range(A_Q_W // LANES):
            q = z[:, c * LANES:(c + 1) * LANES]
            if latent:
                q = _rope(q, cosa_ref[...], sina_ref[...], A_HEAD_DIM // 4)
            qa_ref[:, c * LANES:(c + 1) * LANES] = (q * (A_SCALE * LOG2E)).astype(BF16)
        k = z[:, OFF_KA:OFF_KA + LANES]
        v = z[:, OFF_VA:OFF_VA + LANES]
        if latent:
            k = _rope(k, cosa_ref[...], sina_ref[...], A_HEAD_DIM // 4)
            _store_gqa_v_blocks(v, vad_ref)
        else:
            _store_seq_minor(knat_ref, k)
            _store_seq_minor(vnat_ref, v)
        _store_gqa_k(k, low, kap_ref)

        carry["q_lat"] = _rms(z[:, OFF_QLAT:OFF_QLAT + Q_LORA], qn_ref[...]).astype(BF16)
        ckv_n = _rms(z[:, OFF_CKV:OFF_CKV + KV_LORA], kvn_ref[...])
        krp = z[:, OFF_KROPE:OFF_KROPE + LANES]
        if latent:
            krp = _rope(krp, cosb_ref[...], sinb_ref[...], B_ROPE_DIM // 4)
        else:
            ckvn_ref[...] = ckv_n
            _store_seq_minor(krope_ref, pltpu.roll(krp, HALF, axis=1))
        carry["ckv_b"] = ckv_n.astype(BF16)
        carry["krp"] = krp

    def up_proj():
        q_lat, ckv_b, krp = carry["q_lat"], carry["ckv_b"], carry["krp"]
        qb = _dot(q_lat, wuq_ref[...])
        kn = _dot(ckv_b, wuk_ref[...])
        vb_ref[...] = _dot_nt(wuv_ref[...], ckv_b).astype(BF16)
        for h in range(B_HEADS):
            qh = qb[:, h * LANES:(h + 1) * LANES]
            if latent:
                qh = _rope(qh, cosb_ref[...], sinb_ref[...], B_ROPE_DIM // 4)
            qb_ref[:, h * LANES:(h + 1) * LANES] = (qh * (B_SCALE * LOG2E)).astype(BF16)
            kb_ref[:, h * LANES:(h + 1) * LANES] = (kn[:, h * LANES:(h + 1) * LANES] + krp).astype(BF16)

    def ffn(hooks):
        x = x_ref[...]
        h1 = _mod_norm(x, n1_ref[...], mods_ref, 0).astype(BF16)
        x1 = x + 0.5 * _gate(mods_ref, 0) * _ffn(h1, wgu_ref, wd_ref, hooks)
        x1_ref[...] = x1
        x1s_ref[...] = x1

    @pl.when(i == 0)
    def _():
        ffn(None)

    @pl.when(jnp.logical_and(i > 0, i < n_tiles))
    def _():
        ffn({PRE_HOOK_IN_PROJ: in_proj, PRE_HOOK_UP_PROJ: up_proj})

    @pl.when(i == n_tiles)
    def _():
        in_proj()
        up_proj()


def _const_spec(shape):
    nd = len(shape)
    return pl.BlockSpec(shape, lambda i: (0,) * nd, pipeline_mode=pl.Buffered(1))


def _pre(x, mods3, w, latent, seq, rope, tm):
    t = x.shape[0]
    tiles_per_batch = max(seq // tm, 1)
    n_tiles = t // tm
    cur = lambda i: jnp.minimum(i, n_tiles - 1)
    prev = lambda i: jnp.maximum(i - 1, 0)
    mod_of = (lambda j: 1 + j // tiles_per_batch) if latent else (lambda j: 0)
    cur_rows = lambda i: (cur(i), 0)
    prev_rows = lambda i: (prev(i), 0)

    weights = [w["n1"], w["wgu1"], w["wd1"], w["nm"], w["win_a"], w["qn"], w["kvn"], w["wuq"], w["wuk"], w["wuv_t"]]
    in_specs = [pl.BlockSpec((tm, D_MODEL), cur_rows),
                pl.BlockSpec((None, 1, N_MOD * D_MODEL), lambda i: (mod_of(cur(i)), 0, 0)),
                pl.BlockSpec((None, 1, N_MOD * D_MODEL), lambda i: (mod_of(prev(i)), 0, 0))]
    in_specs += [_const_spec(a.shape) for a in weights]
    args = [x, mods3, mods3] + weights
    if latent:
        in_specs += [pl.BlockSpec((tm, LANES), lambda i: (prev(i) % tiles_per_batch, 0))] * 4
        args += list(rope)

    out_specs = [pl.BlockSpec((tm, D_MODEL), cur_rows)]
    out_shape = [jax.ShapeDtypeStruct((t, D_MODEL), F32)]
    for wd in (A_Q_W, 4 * LANES):
        out_specs.append(pl.BlockSpec((tm, wd), prev_rows))
        out_shape.append(jax.ShapeDtypeStruct((t, wd), BF16))
    if latent:
        out_specs.append(pl.BlockSpec((tm // LANES, A_KV_W, LANES), lambda i: (prev(i), 0, 0)))
        out_shape.append(jax.ShapeDtypeStruct((t // LANES, A_KV_W, LANES), BF16))
    for wd in (B_HEADS * LANES, B_HEADS * LANES):
        out_specs.append(pl.BlockSpec((tm, wd), prev_rows))
        out_shape.append(jax.ShapeDtypeStruct((t, wd), BF16))
    out_specs.append(pl.BlockSpec((B_HEADS * B_V_DIM, tm), lambda i: (0, prev(i))))
    out_shape.append(jax.ShapeDtypeStruct((B_HEADS * B_V_DIM, t), BF16))
    if not latent:
        nb = tm // seq
        for feat in (A_KV_W, A_KV_W):
            out_specs.append(pl.BlockSpec((nb, feat, seq), lambda i: (prev(i), 0, 0)))
            out_shape.append(jax.ShapeDtypeStruct((t // seq, feat, seq), F32))
        out_specs.append(pl.BlockSpec((tm, KV_LORA), prev_rows))
        out_shape.append(jax.ShapeDtypeStruct((t, KV_LORA), F32))
        out_specs.append(pl.BlockSpec((nb, B_ROPE_DIM, seq), lambda i: (prev(i), 0, 0)))
        out_shape.append(jax.ShapeDtypeStruct((t // seq, B_ROPE_DIM, seq), F32))

    return pl.pallas_call(
        functools.partial(_pre_kernel, latent, n_tiles),
        grid=(n_tiles + 1,),
        in_specs=in_specs,
        out_specs=out_specs,
        out_shape=out_shape,
        scratch_shapes=[pltpu.VMEM((tm, D_MODEL), F32)],
        compiler_params=pltpu.CompilerParams(dimension_semantics=("arbitrary",), vmem_limit_bytes=VMEM_LIMIT),
        name="pre_latent" if latent else "pre_context",
    )(*args)


def _cache_kernel(ckv_ref, kropet_ref, ckt_ref, cvt_ref, wuk_ref, wuv_ref, kb_ref, vb_ref, kap_ref, vat_ref):
    p = ckv_ref.shape[0]
    c = ckv_ref[...].astype(BF16)
    kn = _dot(c, wuk_ref[...])
    krp = jnp.concatenate([jnp.zeros((B_NOPE_DIM, p), F32), kropet_ref[...],
                           jnp.zeros((LANES - B_QK_DIM, p), F32)], axis=0).T
    for h in range(B_HEADS):
        kb_ref[:, h * LANES:(h + 1) * LANES] = (kn[:, h * LANES:(h + 1) * LANES] + krp).astype(BF16)
    vb_ref[...] = _dot_nt(wuv_ref[...], c).astype(BF16)
    _store_gqa_k(ckt_ref[...].T, _low_lanes(p), kap_ref)
    vat_ref[...] = cvt_ref[...].astype(BF16)


def _cache_kv(ckv, krope_t, ck_t, cv_t, wuk, wuv_t):
    b, p, _ = ckv.shape
    blk = lambda wd: pl.BlockSpec((None, p, wd), lambda i: (i, 0, 0))
    blk_t = lambda feat: pl.BlockSpec((None, feat, p), lambda i: (i, 0, 0))
    vdim = B_HEADS * B_V_DIM
    return pl.pallas_call(
        _cache_kernel,
        grid=(b,),
        in_specs=[blk(KV_LORA), blk_t(B_ROPE_DIM), blk_t(A_KV_W), blk_t(A_KV_W),
                  pl.BlockSpec(wuk.shape, lambda i: (0, 0)),
                  pl.BlockSpec(wuv_t.shape, lambda i: (0, 0))],
        out_specs=[blk(B_HEADS * LANES), pl.BlockSpec((vdim, p), lambda i: (0, i)), blk(4 * LANES),
                   pl.BlockSpec((A_KV_W, p), lambda i: (0, i))],
        out_shape=[jax.ShapeDtypeStruct((b, p, B_HEADS * LANES), BF16),
                   jax.ShapeDtypeStruct((vdim, b * p), BF16),
                   jax.ShapeDtypeStruct((b, p, 4 * LANES), BF16),
                   jax.ShapeDtypeStruct((A_KV_W, b * p), BF16)],
        compiler_params=pltpu.CompilerParams(dimension_semantics=("parallel",), vmem_limit_bytes=VMEM_LIMIT),
        name="cache_kv",
    )(ckv, krope_t, ck_t, cv_t, wuk, wuv_t)


def _att_ctx_kernel(seq, sink_ref, qa_ref, kap_ref, vt_ref, qb_ref, kb_ref, vbt_ref, oa_ref, ob_ref):
    nb = qa_ref.shape[0] // seq
    ones = jnp.ones((SUM_ROWS, seq), BF16)
    upper = lax.broadcasted_iota(jnp.int32, (1, 2 * seq), 1) >= seq
    units = []
    for bi in range(nb):
        units += [("a", bi, kv, half) for kv in range(A_KV_HEADS) for half in range(2)]
        units += [("b", bi, h, h % 2) for h in range(B_HEADS)]

    def score(unit):
        kind, bi, idx, half = unit
        rows = slice(bi * seq, (bi + 1) * seq)
        if kind == "a":
            kv, u = idx, 2 * idx + half
            q2 = jnp.concatenate([qa_ref[rows, (2 * kv) * LANES:(2 * kv + 1) * LANES],
                                  qa_ref[rows, (2 * kv + 1) * LANES:(2 * kv + 2) * LANES]], axis=0)
            return _dot_nt(kap_ref[rows, u * LANES:(u + 1) * LANES], q2)
        return _dot_nt(kb_ref[rows, idx * LANES:(idx + 1) * LANES], qb_ref[rows, idx * LANES:(idx + 1) * LANES])

    def values(unit):
        kind, bi, idx, _ = unit
        if kind == "a":
            v = vt_ref[bi, idx * A_HEAD_DIM:(idx + 1) * A_HEAD_DIM, :].astype(BF16)
        else:
            v = vbt_ref[idx * B_V_DIM:(idx + 1) * B_V_DIM, bi * seq:(bi + 1) * seq]
        return jnp.concatenate([v, ones], axis=0)

    def sink_row(unit):
        kind, _, kv, half = unit
        if kind == "b":
            return None
        return jnp.where(upper, sink_ref[4 * kv + 2 + half], sink_ref[4 * kv + half]) * LOG2E

    def finish_pair(first, second, o_first, o_second):
        kind, bi, idx, _ = second
        rows = slice(bi * seq, (bi + 1) * seq)
        if kind == "a":
            for pr in range(2):
                cols = slice(pr * seq, (pr + 1) * seq)
                pair = jnp.concatenate([o_first[:, cols], o_second[:, cols]], axis=0)
                c = 2 * idx + pr
                oa_ref[rows, c * LANES:(c + 1) * LANES] = pair.T.astype(BF16)
        else:
            c = idx // 2
            ob_ref[rows, c * LANES:(c + 1) * LANES] = jnp.concatenate([o_first, o_second], axis=0).T.astype(BF16)

    scores, probs, outs = {}, {}, {}
    for t in range(len(units) + 2):
        if t < len(units):
            scores[t] = score(units[t])
        if 0 <= t - 2 < len(units):
            n = t - 2
            p_blocks, p_sink = probs.pop(n)
            acc = _dot(values(units[n]), p_blocks[0])
            den = acc[A_HEAD_DIM:A_HEAD_DIM + 1]
            outs[n] = acc[:A_HEAD_DIM] / (den if p_sink is None else den + p_sink)
            if units[n][3] == 1:
                finish_pair(units[n - 1], units[n], outs.pop(n - 1), outs.pop(n))
        if 0 <= t - 1 < len(units):
            n = t - 1
            probs[n] = _probs_keys_major([scores.pop(n)], sink_row(units[n]))


def _att_ctx(sink, qa, kap, v_t, qb, kb, vb_t, seq):
    t = qa.shape[0]
    rows = CTX_BATCHES * seq
    blk = lambda wd: pl.BlockSpec((rows, wd), lambda b: (b, 0))
    return pl.pallas_call(
        functools.partial(_att_ctx_kernel, seq),
        grid=(t // rows,),
        in_specs=[pl.BlockSpec(memory_space=pltpu.SMEM),
                  blk(qa.shape[1]), blk(kap.shape[1]),
                  pl.BlockSpec((CTX_BATCHES, A_KV_W, seq), lambda b: (b, 0, 0)),
                  blk(qb.shape[1]), blk(kb.shape[1]),
                  pl.BlockSpec((vb_t.shape[0], rows), lambda b: (0, b))],
        out_specs=[blk(A_Q_W), blk(B_HEADS * B_V_DIM)],
        out_shape=[jax.ShapeDtypeStruct((t, A_Q_W), BF16), jax.ShapeDtypeStruct((t, B_HEADS * B_V_DIM), BF16)],
        compiler_params=pltpu.CompilerParams(dimension_semantics=("parallel",), vmem_limit_bytes=VMEM_LIMIT),
        name="att_context",
    )(sink, qa, kap, v_t, qb, kb, vb_t)


class _Unit:
    def __init__(self, scores, sink, values, finish):
        self.scores, self.sink, self.values, self.finish = scores, sink, values, finish


def _attention_pipeline(groups, dv):
    n = len(groups)
    scores, probs = {}, {}
    for t in range(n + 2):
        if t < n:
            scores[t] = [unit.scores() for unit in groups[t]]
        if 0 <= t - 2 < n:
            for unit, (p_blocks, p_sink) in zip(groups[t - 2], probs.pop(t - 2)):
                acc = unit.values(p_blocks)
                den = acc[dv:dv + 1]
                unit.finish(acc[:dv] / (den if p_sink is None else den + p_sink))
        if 0 <= t - 1 < n:
            probs[t - 1] = [_probs_keys_major(s, unit.sink()) for unit, s in zip(groups[t - 1], scores.pop(t - 1))]


def _window_units(n_lat, tile, sink_ref, qa_ref, kap_ref, vat_ref, kapc_ref, vatc_ref, oa_ref):
    n_blocks = qa_ref.shape[0] // Q_BLOCK
    nq = 2 * Q_BLOCK
    past = kapc_ref.shape[0]
    upper = lax.broadcasted_iota(jnp.int32, (1, nq), 1) >= Q_BLOCK
    ones_band = jnp.ones((SUM_ROWS, BAND), BF16)
    ones_ctx = jnp.ones((SUM_ROWS, past), BF16)
    key = lax.broadcasted_iota(jnp.int32, (BAND, nq), 0)
    qry = lax.broadcasted_iota(jnp.int32, (BAND, nq), 1) & (Q_BLOCK - 1)
    key_minus_qry = key - qry
    pending = {}

    def make(qb, kv, half, start, valid, v_band):
        u = 2 * kv + half
        rows = slice(qb * Q_BLOCK, (qb + 1) * Q_BLOCK)
        feat = slice(kv * A_HEAD_DIM, (kv + 1) * A_HEAD_DIM)

        def scores():
            q2 = jnp.concatenate([qa_ref[rows, (2 * kv) * LANES:(2 * kv + 1) * LANES],
                                  qa_ref[rows, (2 * kv + 1) * LANES:(2 * kv + 2) * LANES]], axis=0)
            s_band = _dot_nt(kap_ref[pl.ds(start, BAND), u * LANES:(u + 1) * LANES], q2)
            s_ctx = _dot_nt(kapc_ref[:, u * LANES:(u + 1) * LANES], q2)
            return [jnp.where(valid, s_band, NEG), s_ctx]

        def sink():
            return jnp.where(upper, sink_ref[4 * kv + 2 + half], sink_ref[4 * kv + half]) * LOG2E

        def values(p_blocks):
            vb = jnp.concatenate([v_band[r][feat] for r in range(BAND // LANES)], axis=1)
            r_band = _dot(jnp.concatenate([vb, ones_band], axis=0), p_blocks[0])
            r_ctx = _dot(jnp.concatenate([vatc_ref[feat, :], ones_ctx], axis=0), p_blocks[1])
            return r_band + r_ctx

        def finish(o):
            if half == 0:
                pending[(qb, kv)] = o
                return
            first = pending.pop((qb, kv))
            for pr in range(2):
                cols = slice(pr * Q_BLOCK, (pr + 1) * Q_BLOCK)
                pair = jnp.concatenate([first[:, cols], o[:, cols]], axis=0)
                c = 2 * kv + pr
                oa_ref[rows, c * LANES:(c + 1) * LANES] = pair.T.astype(BF16)

        return _Unit(scores, sink, values, finish)

    units = []
    for qb in range(n_blocks):
        i = tile * n_blocks + qb
        start = pl.multiple_of(jnp.clip(i * Q_BLOCK - Q_BLOCK, 0, n_lat - BAND), Q_BLOCK)
        valid = jnp.abs(key_minus_qry + (start - i * Q_BLOCK)) <= WINDOW
        v_band = vat_ref[pl.ds(start // LANES, BAND // LANES)]
        units += [make(qb, kv, half, start, valid, v_band) for kv in range(A_KV_HEADS) for half in range(2)]
    return units


def _probs_keys_major(scores_t, sink=None):
    m = None
    for s in scores_t:
        mx = jnp.max(s, axis=0, keepdims=True)
        m = mx if m is None else jnp.maximum(m, mx)
    if sink is not None:
        m = jnp.maximum(m, sink)
    probs = [jnp.exp2(s - m).astype(BF16) for s in scores_t]
    return probs, (None if sink is None else jnp.exp2(sink - m))


def _mla_units(qb_ref, kbl_ref, vtl_ref, kbc_ref, vtc_ref, ob_ref):
    kb = MLA_KEY_BLOCK
    ones = jnp.ones((SUM_ROWS, kb), BF16)
    blocks = [(kbc_ref, vtc_ref, j) for j in range(kbc_ref.shape[0] // kb)]
    blocks += [(kbl_ref, vtl_ref, j) for j in range(kbl_ref.shape[0] // kb)]
    pending = {}

    def make(h):
        lanes = slice(h * LANES, (h + 1) * LANES)
        feat = slice(h * B_V_DIM, (h + 1) * B_V_DIM)

        def scores():
            return [_dot_nt(k_ref[j * kb:(j + 1) * kb, lanes], qb_ref[:, lanes]) for k_ref, _, j in blocks]

        def values(p_blocks):
            acc = None
            for (_, v_ref, j), p in zip(blocks, p_blocks):
                r = _dot(jnp.concatenate([v_ref[feat, j * kb:(j + 1) * kb], ones], axis=0), p)
                acc = r if acc is None else acc + r
            return acc

        def finish(o):
            if h % 2 == 0:
                pending[h] = o
                return
            c = h // 2
            pair = jnp.concatenate([pending.pop(h - 1), o], axis=0)
            ob_ref[:, c * LANES:(c + 1) * LANES] = pair.T.astype(BF16)

        return _Unit(scores, lambda: None, values, finish)

    return [make(h) for h in range(B_HEADS)]


def _att_lat_kernel(cast_groups, n_lat, sink_ref, qa_ref, kap_ref, vat_ref, kapc_ref, vatc_ref,
                    qb_ref, kbl_ref, vtl_ref, kbc_ref, vtc_ref, *rest):
    n_src = sum(cast_groups)
    w32_refs, (oa_ref, ob_ref), w16_refs = rest[:n_src], rest[n_src:n_src + 2], rest[n_src + 2:]
    first = 0
    for dst, k in zip(w16_refs, cast_groups):
        sub = dst.shape[0] // k
        for j in range(k):
            dst[j * sub:(j + 1) * sub, :] = w32_refs[first + j][...].astype(BF16)
        first += k

    mla = _mla_units(qb_ref, kbl_ref, vtl_ref, kbc_ref, vtc_ref, ob_ref)
    win = _window_units(n_lat, pl.program_id(1), sink_ref, qa_ref, kap_ref, vat_ref, kapc_ref, vatc_ref, oa_ref)
    assert len(win) % len(mla) == 0
    per_head = len(win) // len(mla)
    groups = [[unit] + win[h * per_head:(h + 1) * per_head] for h, unit in enumerate(mla)]
    _attention_pipeline(groups, B_V_DIM)


def _att_lat(sink, qa, kap, vat, kapc, vatc, qb, kbl, vtl, kbc, vtc, cast_weights):
    b, n, _ = qb.shape
    past = kbc.shape[1]
    tq = LAT_Q_TILE
    vdim = B_HEADS * B_V_DIM
    q_tiles = n // tq
    steps = b * q_tiles
    w_in_specs, w_args, w_out_specs, w_out_shapes, groups = [], [], [], [], []
    for wm, row0, rows in cast_weights:
        cols = wm.shape[1]
        assert rows % (steps * BF16_TILE_ROWS) == 0, "row blocks must be whole packed-bf16 sublane tiles"
        per_step = rows // steps
        sub = math.gcd(per_step, row0) if row0 else per_step
        assert sub % BF16_TILE_ROWS == 0
        k = per_step // sub
        for j in range(k):
            w_in_specs.append(pl.BlockSpec(
                (sub, cols), lambda bi, i, j=j, k=k, base=row0 // sub: (base + (bi * q_tiles + i) * k + j, 0)))
            w_args.append(wm)
        w_out_specs.append(pl.BlockSpec((per_step, cols), lambda bi, i: (bi * q_tiles + i, 0)))
        w_out_shapes.append(jax.ShapeDtypeStruct((rows, cols), BF16))
        groups.append(k)
    q_blk = lambda wd: pl.BlockSpec((None, tq, wd), lambda bi, i: (bi, i, 0))
    rows_blk = lambda r, wd: pl.BlockSpec((None, r, wd), lambda bi, i: (bi, 0, 0))
    feat_blk = lambda feat, r: pl.BlockSpec((feat, r), lambda bi, i: (0, bi))
    out = pl.pallas_call(
        functools.partial(_att_lat_kernel, tuple(groups), n),
        grid=(b, q_tiles),
        in_specs=[pl.BlockSpec(memory_space=pltpu.SMEM),
                  q_blk(A_Q_W), rows_blk(n, kap.shape[2]),
                  pl.BlockSpec((n // LANES, A_KV_W, LANES), lambda bi, i: (bi, 0, 0)),
                  rows_blk(past, kapc.shape[2]), feat_blk(A_KV_W, past),
                  q_blk(qb.shape[2]), rows_blk(n, kbl.shape[2]), feat_blk(vdim, n),
                  rows_blk(past, kbc.shape[2]), feat_blk(vdim, past)] + w_in_specs,
        out_specs=[q_blk(A_Q_W), q_blk(vdim)] + w_out_specs,
        out_shape=[jax.ShapeDtypeStruct((b, n, A_Q_W), BF16), jax.ShapeDtypeStruct((b, n, vdim), BF16)]
        + w_out_shapes,
        compiler_params=pltpu.CompilerParams(dimension_semantics=("parallel", "parallel"),
                                             vmem_limit_bytes=VMEM_LIMIT),
        name="att_latent",
    )(sink, qa, kap, vat, kapc, vatc, qb, kbl, vtl, kbc, vtc, *w_args)
    return out[0], out[1], out[2:]


def _post_kernel(x1_ref, mods_ref, oa_ref, ob_ref, nm_ref, wing_ref, woa_ref, wob_ref, wout_ref,
                 n2_ref, wgu_ref, wd_ref, nf_ref, y_ref):
    x1 = x1_ref[...]
    h2 = _mod_norm(x1, nm_ref[...], mods_ref, 1).astype(BF16)
    g = _dot_nt(h2, wing_ref[...])
    m = (jax.nn.sigmoid(g[:, :D_MODEL]) * _dot(oa_ref[...], woa_ref[...])
         + jax.nn.sigmoid(g[:, D_MODEL:]) * _dot(ob_ref[...], wob_ref[...]))
    x2 = x1 + _gate(mods_ref, 1) * _dot(m.astype(BF16), wout_ref[...])
    h3 = _mod_norm(x2, n2_ref[...], mods_ref, 2).astype(BF16)
    x3 = x2 + 0.5 * _gate(mods_ref, 2) * _ffn(h3, wgu_ref, wd_ref)
    y_ref[...] = _rms(x3, nf_ref[...])


def _post(x1, mods3, oa, ob, w, latent, tiles_per_batch):
    t = x1.shape[0]
    tm = POST_TILE
    if latent:
        mod_row = lambda i: (1 + i // tiles_per_batch, 0, 0)
    else:
        mod_row = lambda i: (0, 0, 0)
    row_blk = lambda i: (i, 0)
    weights = [w["nm"], w["win_g"], w["woa"], w["wob"], w["wout"], w["n2"], w["wgu2"], w["wd2"], w["nf"]]
    in_specs = [pl.BlockSpec((tm, D_MODEL), row_blk),
                pl.BlockSpec((None, 1, N_MOD * D_MODEL), mod_row),
                pl.BlockSpec((tm, A_Q_W), row_blk),
                pl.BlockSpec((tm, B_HEADS * B_V_DIM), row_blk)]
    in_specs += [_const_spec(a.shape) for a in weights]
    return pl.pallas_call(
        _post_kernel,
        grid=(t // tm,),
        in_specs=in_specs,
        out_specs=pl.BlockSpec((tm, D_MODEL), row_blk),
        out_shape=jax.ShapeDtypeStruct((t, D_MODEL), F32),
        compiler_params=pltpu.CompilerParams(dimension_semantics=("parallel",), vmem_limit_bytes=VMEM_LIMIT),
        name="post_latent" if latent else "post_context",
    )(x1, mods3, oa, ob, *weights)


def _rope_tables(n):
    f32 = np.float32
    rows = n // GRID_W
    t_row = np.repeat(np.arange(rows, dtype=f32), GRID_W)
    t_col = np.tile(np.arange(GRID_W, dtype=f32), rows)

    def angles(d_rot):
        d_half = d_rot // 2
        inv = (f32(1.0) / np.power(f32(ROPE_THETA), np.arange(0, d_half, 2, dtype=f32) / f32(d_half))).astype(f32)
        ar = t_row[:, None] * inv[None, :]
        ac = t_col[:, None] * inv[None, :]
        return np.concatenate([ar, ar, ac, ac], axis=-1).astype(f32)

    def signed(sin, d_rot):
        q = d_rot // 4
        sign = np.where((np.arange(d_rot) % (2 * q)) < q, f32(-1.0), f32(1.0)).astype(f32)
        return sin * sign[None, :]

    ang_a = angles(A_HEAD_DIM)
    cos_a = np.tile(np.cos(ang_a), (1, LANES // A_HEAD_DIM))
    sin_a = np.tile(signed(np.sin(ang_a), A_HEAD_DIM), (1, LANES // A_HEAD_DIM))
    ang_b = angles(B_ROPE_DIM)
    pad = ((0, 0), (B_NOPE_DIM, LANES - B_NOPE_DIM - B_ROPE_DIM))
    cos_b = np.pad(np.cos(ang_b), pad, constant_values=1.0)
    sin_b = np.pad(signed(np.sin(ang_b), B_ROPE_DIM), pad)
    return tuple(jnp.asarray(t, dtype=F32) for t in (cos_a, sin_a, cos_b, sin_b))


def _prep_weights(ffn1_norm, ffn1_w_gu, ffn1_w_down, mix_norm, w_in, q_lat_norm, kv_lat_norm, w_uq, w_ukv,
                  w_o_a, w_o_b, w_out, ffn2_norm, ffn2_w_gu, ffn2_w_down, final_norm):
    win_t = jnp.swapaxes(w_in[0], 0, 1)
    n_attn = A_Q_W + 2 * A_KV_W + Q_LORA + KV_LORA
    krope_rows = jnp.pad(win_t[n_attn:n_attn + B_ROPE_DIM],
                         ((B_NOPE_DIM, LANES - B_NOPE_DIM - B_ROPE_DIM), (0, 0)))
    win_a = jnp.concatenate([win_t[:n_attn], krope_rows], axis=0).astype(BF16)
    n_gate = n_attn + B_ROPE_DIM
    whole = lambda wm: (wm, 0, wm.shape[0])

    wuq = jnp.pad(w_uq[0].reshape(Q_LORA, B_HEADS, B_QK_DIM),
                  ((0, 0), (0, 0), (0, LANES - B_QK_DIM))).reshape(Q_LORA, B_HEADS * LANES).astype(BF16)
    wukv = w_ukv[0].reshape(KV_LORA, B_HEADS, B_NOPE_DIM + B_V_DIM)
    wuk = jnp.pad(wukv[:, :, :B_NOPE_DIM],
                  ((0, 0), (0, 0), (0, LANES - B_NOPE_DIM))).reshape(KV_LORA, B_HEADS * LANES).astype(BF16)
    wuv = wukv[:, :, B_NOPE_DIM:].reshape(KV_LORA, B_HEADS * B_V_DIM).astype(BF16)

    return {
        "n1": ffn1_norm, "wgu1": ffn1_w_gu[0].astype(BF16), "wd1": ffn1_w_down[0].astype(BF16),
        "nm": mix_norm, "win_a": win_a,
        "qn": q_lat_norm, "kvn": kv_lat_norm, "wuq": wuq, "wuk": wuk, "wuv_t": wuv.T,
        "n2": ffn2_norm, "nf": final_norm.reshape(1, D_MODEL),
        "post_f32": [(win_t, n_gate, 2 * D_MODEL), whole(w_o_a[0]), whole(w_o_b[0]), whole(w_out[0]),
                     whole(ffn2_w_gu[0]), whole(ffn2_w_down[0])],
    }


def kernel(x_prompt, x_sample, cache_attn_k, cache_attn_v, cache_mla_ckv, cache_mla_krope, c, c_ctx, ada_w, ada_b, ffn1_norm, ffn1_w_gu, ffn1_w_down, mix_norm, w_in, attn_sink, q_lat_norm, kv_lat_norm, w_uq, w_ukv, w_o_a, w_o_b, w_out, ffn2_norm, ffn2_w_gu, ffn2_w_down, final_norm):
    assert ada_w.shape[0] == 1, "single trunk layer"
    bp, sp, d = x_prompt.shape
    bs, ns, _ = x_sample.shape
    past = cache_attn_k.shape[2]
    assert d == D_MODEL and bs + 1 <= MOD_ROWS
    assert PRE_TILE_CTX % sp == 0 and (bp * sp) % PRE_TILE_CTX == 0 and (bp * sp) % POST_TILE == 0
    assert ns % PRE_TILE_LAT == 0 and ns % POST_TILE == 0
    assert ns % LAT_Q_TILE == 0 and LAT_Q_TILE % Q_BLOCK == 0 and ns >= BAND

    w = _prep_weights(ffn1_norm, ffn1_w_gu, ffn1_w_down, mix_norm, w_in, q_lat_norm, kv_lat_norm, w_uq, w_ukv,
                      w_o_a, w_o_b, w_out, ffn2_norm, ffn2_w_gu, ffn2_w_down, final_norm)
    sink = attn_sink[0]

    cvec = jnp.concatenate([c_ctx[None, :], c, jnp.zeros((MOD_ROWS - 1 - bs, d), F32)], axis=0)
    mods = _ada_mods(cvec, ada_w[0], ada_b)
    mods3 = mods.reshape(MOD_ROWS, 1, N_MOD * D_MODEL)

    xp = x_prompt.reshape(bp * sp, d)
    (x1p, qa, kap, qb, kb, vb_t, k_t, v_t, ckv_n, krope_t) = _pre(xp, mods3, w, False, sp, None, PRE_TILE_CTX)
    oa_p, ob_p = _att_ctx(sink, qa, kap, v_t, qb, kb, vb_t, sp)

    xs = x_sample.reshape(bs * ns, d)
    (x1s, qa, kap, vad, qb, kb, vb_t) = _pre(xs, mods3, w, True, ns, _rope_tables(ns), PRE_TILE_LAT)
    feat_major = lambda a: a[:, 0].transpose(0, 2, 3, 1).reshape(bs, A_KV_W, past)
    kbc, vbc, kapc, vadc = _cache_kv(
        cache_mla_ckv[:, 0], jnp.swapaxes(cache_mla_krope[:, 0], 1, 2),
        feat_major(cache_attn_k), feat_major(cache_attn_v), w["wuk"], w["wuv_t"])
    r3 = lambda a: a.reshape(bs, ns, a.shape[1])
    oa, ob, (w["win_g"], w["woa"], w["wob"], w["wout"], w["wgu2"], w["wd2"]) = _att_lat(
        sink, r3(qa), r3(kap), vad, kapc, vadc, r3(qb), r3(kb), vb_t, kbc, vbc, w["post_f32"])

    y_prompt = _post(x1p, mods3, oa_p, ob_p, w, False, 1).reshape(bp, sp, d)
    y_sample = _post(x1s, mods3, oa.reshape(bs * ns, A_Q_W), ob.reshape(bs * ns, B_HEADS * B_V_DIM),
                     w, True, ns // POST_TILE).reshape(bs, ns, d)

    new_attn_k = k_t.reshape(bp, 1, A_KV_HEADS, A_HEAD_DIM, sp).transpose(0, 1, 4, 2, 3)
    new_attn_v = v_t.reshape(bp, 1, A_KV_HEADS, A_HEAD_DIM, sp).transpose(0, 1, 4, 2, 3)
    new_mla_ckv = ckv_n.reshape(bp, 1, sp, KV_LORA)
    new_mla_krope = krope_t.reshape(bp, 1, B_ROPE_DIM, sp).transpose(0, 1, 3, 2)
    return (y_prompt, y_sample, new_attn_k, new_attn_v, new_mla_ckv, new_mla_krope)
```

```python
import functools
import math

import jax
import jax.numpy as jnp
import numpy as np
from jax import lax
from jax.experimental import pallas as pl
from jax.experimental.pallas import tpu as pltpu

F32 = jnp.float32
BF16 = jnp.bfloat16

D_MODEL = 1024
N_MOD = 9
GRID_W = 64
WINDOW = 128
A_HEADS = 8
A_KV_HEADS = 2
A_HEAD_DIM = 64
A_Q_W = A_HEADS * A_HEAD_DIM
A_KV_W = A_KV_HEADS * A_HEAD_DIM
B_HEADS = 8
B_NOPE_DIM = 64
B_ROPE_DIM = 32
B_V_DIM = 64
B_QK_DIM = B_NOPE_DIM + B_ROPE_DIM
Q_LORA = 256
KV_LORA = 256
D_FF = 2816
ROPE_THETA = 10000.0
EPS = 1e-6
NEG = -1e30
A_SCALE = A_HEAD_DIM ** -0.5
B_SCALE = B_QK_DIM ** -0.5
LOG2E = 1.4426950408889634

LANES = 128
HALF = LANES // 2
FF_CHUNK = 256
N_FF_CHUNKS = D_FF // FF_CHUNK
PRE_TILE_CTX = 512
PRE_TILE_LAT = 256
POST_TILE = 512
ADA_TILE = 1152
Q_BLOCK = 128
BAND = 3 * Q_BLOCK
CTX_BATCHES = 4
BF16_TILE_ROWS = 16
SUM_ROWS = BF16_TILE_ROWS
LAT_Q_TILE = 512
MLA_KEY_BLOCK = 256
MOD_ROWS = 8
VMEM_LIMIT = 56 * 1024 * 1024

IN_A_W = A_Q_W + 2 * A_KV_W + Q_LORA + KV_LORA
OFF_KA = A_Q_W
OFF_VA = OFF_KA + A_KV_W
OFF_QLAT = OFF_VA + A_KV_W
OFF_CKV = OFF_QLAT + Q_LORA


def _dot(a, b):
    return jnp.dot(a, b, preferred_element_type=F32)


def _dot_nt(a, b):
    return lax.dot_general(a, b, (((1,), (1,)), ((), ())), preferred_element_type=F32)


def _rms(x, g):
    ms = jnp.mean(x * x, axis=-1, keepdims=True)
    return x * lax.rsqrt(ms + EPS) * g


def _mod_norm(x, g, mods_ref, k):
    shift = mods_ref[:, (3 * k) * D_MODEL:(3 * k + 1) * D_MODEL]
    scale = mods_ref[:, (3 * k + 1) * D_MODEL:(3 * k + 2) * D_MODEL]
    return _rms(x, g) * (1.0 + scale) + shift


def _gate(mods_ref, k):
    return mods_ref[:, (3 * k + 2) * D_MODEL:(3 * k + 3) * D_MODEL]


def _ffn(h, wgu_ref, wd_ref, hooks=None):
    def gate_up(c):
        a = _dot(h, wgu_ref[:, c * FF_CHUNK:(c + 1) * FF_CHUNK])
        u = _dot(h, wgu_ref[:, D_FF + c * FF_CHUNK:D_FF + (c + 1) * FF_CHUNK])
        return a, u

    acc = None
    nxt = gate_up(0)
    for c in range(N_FF_CHUNKS):
        a, u = nxt
        if c + 1 < N_FF_CHUNKS:
            nxt = gate_up(c + 1)
        act = (a * jax.nn.sigmoid(a) * u).astype(BF16)
        d = _dot(act, wd_ref[c * FF_CHUNK:(c + 1) * FF_CHUNK, :])
        acc = d if acc is None else acc + d
        if hooks and c in hooks:
            hooks[c]()
    return acc


def _rope(x, cos, sin_signed, dist):
    lane = lax.broadcasted_iota(jnp.int32, x.shape, 1)
    first = (lane & (2 * dist - 1)) < dist
    partner = jnp.where(first, pltpu.roll(x, LANES - dist, axis=1), pltpu.roll(x, dist, axis=1))
    return x * cos + partner * sin_signed


def _store_gqa_k(k, low, kap_ref):
    k_sw = pltpu.roll(k, HALF, axis=1)
    zero = jnp.zeros_like(k)
    kap_ref[:, 0 * LANES:1 * LANES] = jnp.where(low, k, zero).astype(BF16)
    kap_ref[:, 1 * LANES:2 * LANES] = jnp.where(low, zero, k_sw).astype(BF16)
    kap_ref[:, 2 * LANES:3 * LANES] = jnp.where(low, k_sw, zero).astype(BF16)
    kap_ref[:, 3 * LANES:4 * LANES] = jnp.where(low, zero, k).astype(BF16)


def _store_gqa_v_blocks(v, vt_ref):
    for r in range(vt_ref.shape[0]):
        vt_ref[r] = v[r * LANES:(r + 1) * LANES, :].T.astype(BF16)


def _store_seq_minor(out_ref, val):
    nb, feat, seq = out_ref.shape
    for bi in range(nb):
        out_ref[bi] = val[bi * seq:(bi + 1) * seq, :].T[:feat, :]


def _low_lanes(rows):
    return lax.broadcasted_iota(jnp.int32, (rows, LANES), 1) < HALF


def _cast_windows(windows, steps, step_of):
    in_specs, args, out_specs, out_shapes, groups = [], [], [], [], []
    for wm, row0, rows in windows:
        cols = wm.shape[1]
        assert rows % (steps * BF16_TILE_ROWS) == 0, "row blocks must be whole packed-bf16 sublane tiles"
        per_step = rows // steps
        sub = math.gcd(per_step, row0) if row0 else per_step
        assert sub % BF16_TILE_ROWS == 0
        k = per_step // sub
        for j in range(k):
            in_specs.append(pl.BlockSpec(
                (sub, cols), lambda *g, j=j, k=k, base=row0 // sub: (base + step_of(*g) * k + j, 0)))
            args.append(wm)
        out_specs.append(pl.BlockSpec((per_step, cols), lambda *g: (step_of(*g), 0)))
        out_shapes.append(jax.ShapeDtypeStruct((rows, cols), BF16))
        groups.append(k)
    return in_specs, args, out_specs, out_shapes, tuple(groups)


def _cast_blocks(w32_refs, w16_refs, groups):
    first = 0
    for dst, k in zip(w16_refs, groups):
        sub = dst.shape[0] // k
        for j in range(k):
            dst[j * sub:(j + 1) * sub, :] = w32_refs[first + j][...].astype(BF16)
        first += k


def _ada_kernel(cast_groups, c_ref, w_ref, b_ref, *rest):
    n_src = sum(cast_groups)
    _cast_blocks(rest[:n_src], rest[n_src + 1:], cast_groups)
    o_ref = rest[n_src]
    c = c_ref[...]
    s = (c * jax.nn.sigmoid(c)).astype(BF16)
    o_ref[...] = _dot(s, w_ref[...].astype(BF16)) + b_ref[...]


def _ada_mods(cvec, ada_w, ada_b, cast_weights):
    n = ada_w.shape[1]
    steps = n // ADA_TILE
    w_in_specs, w_args, w_out_specs, w_out_shapes, groups = _cast_windows(cast_weights, steps, lambda j: j)
    out = pl.pallas_call(
        functools.partial(_ada_kernel, groups),
        grid=(steps,),
        in_specs=[
            pl.BlockSpec((MOD_ROWS, D_MODEL), lambda j: (0, 0)),
            pl.BlockSpec((D_MODEL, ADA_TILE), lambda j: (0, j)),
            pl.BlockSpec((1, ADA_TILE), lambda j: (0, j)),
        ] + w_in_specs,
        out_specs=[pl.BlockSpec((MOD_ROWS, ADA_TILE), lambda j: (0, j))] + w_out_specs,
        out_shape=[jax.ShapeDtypeStruct((MOD_ROWS, n), F32)] + w_out_shapes,
        compiler_params=pltpu.CompilerParams(dimension_semantics=("parallel",), vmem_limit_bytes=VMEM_LIMIT),
        name="ada_mods",
    )(cvec, ada_w, ada_b, *w_args)
    return out[0], out[1:]


PRE_HOOK_IN_PROJ = 1
PRE_HOOK_UP_PROJ = 6


def _pre_kernel(latent, n_tiles, *refs):
    x1s_ref = refs[-1]
    if latent:
        (x_ref, mods_ref, modsp_ref, n1_ref, wgu_ref, wd_ref, nm_ref, win_ref, wkr_ref, qn_ref, kvn_ref, wuq_ref, wuk_ref,
         wuv_ref, cosa_ref, sina_ref, cosb_ref, sinb_ref,
         x1_ref, qa_ref, kap_ref, vad_ref, qb_ref, kb_ref, vb_ref) = refs[:-1]
    else:
        (x_ref, mods_ref, modsp_ref, n1_ref, wgu_ref, wd_ref, nm_ref, win_ref, wkr_ref, qn_ref, kvn_ref, wuq_ref, wuk_ref,
         wuv_ref,
         x1_ref, qa_ref, kap_ref, qb_ref, kb_ref, vb_ref,
         knat_ref, vnat_ref, ckvn_ref, krope_ref) = refs[:-1]
    i = pl.program_id(0)
    carry = {}

    def in_proj():
        h2 = _mod_norm(x1s_ref[...], nm_ref[...], modsp_ref, 1).astype(BF16)
        z = _dot_nt(h2, win_ref[...])
        low = _low_lanes(z.shape[0])

        for c in range(A_Q_W // LANES):
            q = z[:, c * LANES:(c + 1) * LANES]
            if latent:
                q = _rope(q, cosa_ref[...], sina_ref[...], A_HEAD_DIM // 4)
            qa_ref[:, c * LANES:(c + 1) * LANES] = (q * (A_SCALE * LOG2E)).astype(BF16)
        k = z[:, OFF_KA:OFF_KA + LANES]
        v = z[:, OFF_VA:OFF_VA + LANES]
        if latent:
            k = _rope(k, cosa_ref[...], sina_ref[...], A_HEAD_DIM // 4)
            _store_gqa_v_blocks(v, vad_ref)
        else:
            _store_seq_minor(knat_ref, k)
            _store_seq_minor(vnat_ref, v)
        _store_gqa_k(k, low, kap_ref)

        carry["q_lat"] = _rms(z[:, OFF_QLAT:OFF_QLAT + Q_LORA], qn_ref[...]).astype(BF16)
        ckv_n = _rms(z[:, OFF_CKV:OFF_CKV + KV_LORA], kvn_ref[...])
        krp = _dot_nt(h2, wkr_ref[...])
        if latent:
            krp = _rope(krp, cosb_ref[...], sinb_ref[...], B_ROPE_DIM // 4)
        else:
            ckvn_ref[...] = ckv_n
            _store_seq_minor(krope_ref, pltpu.roll(krp, HALF, axis=1))
        carry["ckv_b"] = ckv_n.astype(BF16)
        carry["krp"] = krp

    def up_proj():
        q_lat, ckv_b, krp = carry["q_lat"], carry["ckv_b"], carry["krp"]
        qb = _dot(q_lat, wuq_ref[...])
        kn = _dot(ckv_b, wuk_ref[...])
        vb_ref[...] = _dot_nt(wuv_ref[...], ckv_b).astype(BF16)
        for h in range(B_HEADS):
            qh = qb[:, h * LANES:(h + 1) * LANES]
            if latent:
                qh = _rope(qh, cosb_ref[...], sinb_ref[...], B_ROPE_DIM // 4)
            qb_ref[:, h * LANES:(h + 1) * LANES] = (qh * (B_SCALE * LOG2E)).astype(BF16)
            kb_ref[:, h * LANES:(h + 1) * LANES] = (kn[:, h * LANES:(h + 1) * LANES] + krp).astype(BF16)

    def ffn(hooks):
        x = x_ref[...]
        h1 = _mod_norm(x, n1_ref[...], mods_ref, 0).astype(BF16)
        x1 = x + 0.5 * _gate(mods_ref, 0) * _ffn(h1, wgu_ref, wd_ref, hooks)
        x1_ref[...] = x1
        x1s_ref[...] = x1

    @pl.when(i == 0)
    def _():
        ffn(None)

    @pl.when(jnp.logical_and(i > 0, i < n_tiles))
    def _():
        ffn({PRE_HOOK_IN_PROJ: in_proj, PRE_HOOK_UP_PROJ: up_proj})

    @pl.when(i == n_tiles)
    def _():
        in_proj()
        up_proj()


def _const_spec(shape):
    nd = len(shape)
    return pl.BlockSpec(shape, lambda i: (0,) * nd, pipeline_mode=pl.Buffered(1))


def _pre(x, mods3, w, latent, seq, rope, tm):
    t = x.shape[0]
    tiles_per_batch = max(seq // tm, 1)
    n_tiles = t // tm
    cur = lambda i: jnp.minimum(i, n_tiles - 1)
    prev = lambda i: jnp.maximum(i - 1, 0)
    mod_of = (lambda j: 1 + j // tiles_per_batch) if latent else (lambda j: 0)
    cur_rows = lambda i: (cur(i), 0)
    prev_rows = lambda i: (prev(i), 0)

    weights = [w["n1"], w["wgu1"], w["wd1"], w["nm"], w["win_a"], w["win_kr"], w["qn"], w["kvn"], w["wuq"], w["wuk"],
               w["wuv_t"]]
    in_specs = [pl.BlockSpec((tm, D_MODEL), cur_rows),
                pl.BlockSpec((None, 1, N_MOD * D_MODEL), lambda i: (mod_of(cur(i)), 0, 0)),
                pl.BlockSpec((None, 1, N_MOD * D_MODEL), lambda i: (mod_of(prev(i)), 0, 0))]
    in_specs += [_const_spec(a.shape) for a in weights]
    args = [x, mods3, mods3] + weights
    if latent:
        in_specs += [pl.BlockSpec((tm, LANES), lambda i: (prev(i) % tiles_per_batch, 0))] * 4
        args += list(rope)

    out_specs = [pl.BlockSpec((tm, D_MODEL), cur_rows)]
    out_shape = [jax.ShapeDtypeStruct((t, D_MODEL), F32)]
    for wd in (A_Q_W, 4 * LANES):
        out_specs.append(pl.BlockSpec((tm, wd), prev_rows))
        out_shape.append(jax.ShapeDtypeStruct((t, wd), BF16))
    if latent:
        out_specs.append(pl.BlockSpec((tm // LANES, A_KV_W, LANES), lambda i: (prev(i), 0, 0)))
        out_shape.append(jax.ShapeDtypeStruct((t // LANES, A_KV_W, LANES), BF16))
    for wd in (B_HEADS * LANES, B_HEADS * LANES):
        out_specs.append(pl.BlockSpec((tm, wd), prev_rows))
        out_shape.append(jax.ShapeDtypeStruct((t, wd), BF16))
    out_specs.append(pl.BlockSpec((B_HEADS * B_V_DIM, tm), lambda i: (0, prev(i))))
    out_shape.append(jax.ShapeDtypeStruct((B_HEADS * B_V_DIM, t), BF16))
    if not latent:
        nb = tm // seq
        for feat in (A_KV_W, A_KV_W):
            out_specs.append(pl.BlockSpec((nb, feat, seq), lambda i: (prev(i), 0, 0)))
            out_shape.append(jax.ShapeDtypeStruct((t // seq, feat, seq), F32))
        out_specs.append(pl.BlockSpec((tm, KV_LORA), prev_rows))
        out_shape.append(jax.ShapeDtypeStruct((t, KV_LORA), F32))
        out_specs.append(pl.BlockSpec((nb, B_ROPE_DIM, seq), lambda i: (prev(i), 0, 0)))
        out_shape.append(jax.ShapeDtypeStruct((t // seq, B_ROPE_DIM, seq), F32))

    return pl.pallas_call(
        functools.partial(_pre_kernel, latent, n_tiles),
        grid=(n_tiles + 1,),
        in_specs=in_specs,
        out_specs=out_specs,
        out_shape=out_shape,
        scratch_shapes=[pltpu.VMEM((tm, D_MODEL), F32)],
        compiler_params=pltpu.CompilerParams(dimension_semantics=("arbitrary",), vmem_limit_bytes=VMEM_LIMIT),
        name="pre_latent" if latent else "pre_context",
    )(*args)


def _cache_kernel(ckv_ref, kropet_ref, ckt_ref, cvt_ref, wuk_ref, wuv_ref, kb_ref, vb_ref, kap_ref, vat_ref):
    p = ckv_ref.shape[0]
    c = ckv_ref[...].astype(BF16)
    kn = _dot(c, wuk_ref[...])
    krp = jnp.concatenate([jnp.zeros((B_NOPE_DIM, p), F32), kropet_ref[...],
                           jnp.zeros((LANES - B_QK_DIM, p), F32)], axis=0).T
    for h in range(B_HEADS):
        kb_ref[:, h * LANES:(h + 1) * LANES] = (kn[:, h * LANES:(h + 1) * LANES] + krp).astype(BF16)
    vb_ref[...] = _dot_nt(wuv_ref[...], c).astype(BF16)
    _store_gqa_k(ckt_ref[...].T, _low_lanes(p), kap_ref)
    vat_ref[...] = cvt_ref[...].astype(BF16)


def _cache_kv(ckv, krope_t, ck_t, cv_t, wuk, wuv_t):
    b, p, _ = ckv.shape
    blk = lambda wd: pl.BlockSpec((None, p, wd), lambda i: (i, 0, 0))
    blk_t = lambda feat: pl.BlockSpec((None, feat, p), lambda i: (i, 0, 0))
    vdim = B_HEADS * B_V_DIM
    return pl.pallas_call(
        _cache_kernel,
        grid=(b,),
        in_specs=[blk(KV_LORA), blk_t(B_ROPE_DIM), blk_t(A_KV_W), blk_t(A_KV_W),
                  pl.BlockSpec(wuk.shape, lambda i: (0, 0)),
                  pl.BlockSpec(wuv_t.shape, lambda i: (0, 0))],
        out_specs=[blk(B_HEADS * LANES), pl.BlockSpec((vdim, p), lambda i: (0, i)), blk(4 * LANES),
                   pl.BlockSpec((A_KV_W, p), lambda i: (0, i))],
        out_shape=[jax.ShapeDtypeStruct((b, p, B_HEADS * LANES), BF16),
                   jax.ShapeDtypeStruct((vdim, b * p), BF16),
                   jax.ShapeDtypeStruct((b, p, 4 * LANES), BF16),
                   jax.ShapeDtypeStruct((A_KV_W, b * p), BF16)],
        compiler_params=pltpu.CompilerParams(dimension_semantics=("parallel",), vmem_limit_bytes=VMEM_LIMIT),
        name="cache_kv",
    )(ckv, krope_t, ck_t, cv_t, wuk, wuv_t)


def _att_ctx_kernel(seq, sink_ref, qa_ref, kap_ref, vt_ref, qb_ref, kb_ref, vbt_ref, oa_ref, ob_ref):
    nb = qa_ref.shape[0] // seq
    ones = jnp.ones((SUM_ROWS, seq), BF16)
    upper = lax.broadcasted_iota(jnp.int32, (1, 2 * seq), 1) >= seq
    units = []
    for bi in range(nb):
        units += [("a", bi, kv, half) for kv in range(A_KV_HEADS) for half in range(2)]
        units += [("b", bi, h, h % 2) for h in range(B_HEADS)]

    def score(unit):
        kind, bi, idx, half = unit
        rows = slice(bi * seq, (bi + 1) * seq)
        if kind == "a":
            kv, u = idx, 2 * idx + half
            q2 = jnp.concatenate([qa_ref[rows, (2 * kv) * LANES:(2 * kv + 1) * LANES],
                                  qa_ref[rows, (2 * kv + 1) * LANES:(2 * kv + 2) * LANES]], axis=0)
            return _dot_nt(kap_ref[rows, u * LANES:(u + 1) * LANES], q2)
        return _dot_nt(kb_ref[rows, idx * LANES:(idx + 1) * LANES], qb_ref[rows, idx * LANES:(idx + 1) * LANES])

    def values(unit):
        kind, bi, idx, _ = unit
        if kind == "a":
            v = vt_ref[bi, idx * A_HEAD_DIM:(idx + 1) * A_HEAD_DIM, :].astype(BF16)
        else:
            v = vbt_ref[idx * B_V_DIM:(idx + 1) * B_V_DIM, bi * seq:(bi + 1) * seq]
        return jnp.concatenate([v, ones], axis=0)

    def sink_row(unit):
        kind, _, kv, half = unit
        if kind == "b":
            return None
        return jnp.where(upper, sink_ref[4 * kv + 2 + half], sink_ref[4 * kv + half]) * LOG2E

    def finish_pair(first, second, o_first, o_second):
        kind, bi, idx, _ = second
        rows = slice(bi * seq, (bi + 1) * seq)
        if kind == "a":
            for pr in range(2):
                cols = slice(pr * seq, (pr + 1) * seq)
                pair = jnp.concatenate([o_first[:, cols], o_second[:, cols]], axis=0)
                c = 2 * idx + pr
                oa_ref[rows, c * LANES:(c + 1) * LANES] = pair.T.astype(BF16)
        else:
            c = idx // 2
            ob_ref[rows, c * LANES:(c + 1) * LANES] = jnp.concatenate([o_first, o_second], axis=0).T.astype(BF16)

    scores, probs, outs = {}, {}, {}
    for t in range(len(units) + 2):
        if t < len(units):
            scores[t] = score(units[t])
        if 0 <= t - 2 < len(units):
            n = t - 2
            p_blocks, p_sink = probs.pop(n)
            acc = _dot(values(units[n]), p_blocks[0])
            den = acc[A_HEAD_DIM:A_HEAD_DIM + 1]
            outs[n] = acc[:A_HEAD_DIM] / (den if p_sink is None else den + p_sink)
            if units[n][3] == 1:
                finish_pair(units[n - 1], units[n], outs.pop(n - 1), outs.pop(n))
        if 0 <= t - 1 < len(units):
            n = t - 1
            probs[n] = _probs_keys_major([scores.pop(n)], sink_row(units[n]))


def _att_ctx(sink, qa, kap, v_t, qb, kb, vb_t, seq):
    t = qa.shape[0]
    rows = CTX_BATCHES * seq
    blk = lambda wd: pl.BlockSpec((rows, wd), lambda b: (b, 0))
    return pl.pallas_call(
        functools.partial(_att_ctx_kernel, seq),
        grid=(t // rows,),
        in_specs=[pl.BlockSpec(memory_space=pltpu.SMEM),
                  blk(qa.shape[1]), blk(kap.shape[1]),
                  pl.BlockSpec((CTX_BATCHES, A_KV_W, seq), lambda b: (b, 0, 0)),
                  blk(qb.shape[1]), blk(kb.shape[1]),
                  pl.BlockSpec((vb_t.shape[0], rows), lambda b: (0, b))],
        out_specs=[blk(A_Q_W), blk(B_HEADS * B_V_DIM)],
        out_shape=[jax.ShapeDtypeStruct((t, A_Q_W), BF16), jax.ShapeDtypeStruct((t, B_HEADS * B_V_DIM), BF16)],
        compiler_params=pltpu.CompilerParams(dimension_semantics=("parallel",), vmem_limit_bytes=VMEM_LIMIT),
        name="att_context",
    )(sink, qa, kap, v_t, qb, kb, vb_t)


class _Unit:
    def __init__(self, scores, sink, values, finish):
        self.scores, self.sink, self.values, self.finish = scores, sink, values, finish


def _attention_pipeline(groups, dv):
    n = len(groups)
    scores, probs = {}, {}
    for t in range(n + 2):
        if t < n:
            scores[t] = [unit.scores() for unit in groups[t]]
        if 0 <= t - 2 < n:
            for unit, (p_blocks, p_sink) in zip(groups[t - 2], probs.pop(t - 2)):
                acc = unit.values(p_blocks)
                den = acc[dv:dv + 1]
                unit.finish(acc[:dv] / (den if p_sink is None else den + p_sink))
        if 0 <= t - 1 < n:
            probs[t - 1] = [_probs_keys_major(s, unit.sink()) for unit, s in zip(groups[t - 1], scores.pop(t - 1))]


def _window_units(n_lat, tile, sink_ref, qa_ref, kap_ref, vat_ref, kapc_ref, vatc_ref, oa_ref):
    n_blocks = qa_ref.shape[0] // Q_BLOCK
    nq = 2 * Q_BLOCK
    past = kapc_ref.shape[0]
    upper = lax.broadcasted_iota(jnp.int32, (1, nq), 1) >= Q_BLOCK
    ones_band = jnp.ones((SUM_ROWS, BAND), BF16)
    ones_ctx = jnp.ones((SUM_ROWS, past), BF16)
    key = lax.broadcasted_iota(jnp.int32, (BAND, nq), 0)
    qry = lax.broadcasted_iota(jnp.int32, (BAND, nq), 1) & (Q_BLOCK - 1)
    key_minus_qry = key - qry
    pending = {}

    def make(qb, kv, half, start, valid, v_band):
        u = 2 * kv + half
        rows = slice(qb * Q_BLOCK, (qb + 1) * Q_BLOCK)
        feat = slice(kv * A_HEAD_DIM, (kv + 1) * A_HEAD_DIM)

        def scores():
            q2 = jnp.concatenate([qa_ref[rows, (2 * kv) * LANES:(2 * kv + 1) * LANES],
                                  qa_ref[rows, (2 * kv + 1) * LANES:(2 * kv + 2) * LANES]], axis=0)
            s_band = _dot_nt(kap_ref[pl.ds(start, BAND), u * LANES:(u + 1) * LANES], q2)
            s_ctx = _dot_nt(kapc_ref[:, u * LANES:(u + 1) * LANES], q2)
            return [jnp.where(valid, s_band, NEG), s_ctx]

        def sink():
            return jnp.where(upper, sink_ref[4 * kv + 2 + half], sink_ref[4 * kv + half]) * LOG2E

        def values(p_blocks):
            vb = jnp.concatenate([v_band[r][feat] for r in range(BAND // LANES)], axis=1)
            r_band = _dot(jnp.concatenate([vb, ones_band], axis=0), p_blocks[0])
            r_ctx = _dot(jnp.concatenate([vatc_ref[feat, :], ones_ctx], axis=0), p_blocks[1])
            return r_band + r_ctx

        def finish(o):
            if half == 0:
                pending[(qb, kv)] = o
                return
            first = pending.pop((qb, kv))
            for pr in range(2):
                cols = slice(pr * Q_BLOCK, (pr + 1) * Q_BLOCK)
                pair = jnp.concatenate([first[:, cols], o[:, cols]], axis=0)
                c = 2 * kv + pr
                oa_ref[rows, c * LANES:(c + 1) * LANES] = pair.T.astype(BF16)

        return _Unit(scores, sink, values, finish)

    units = []
    for qb in range(n_blocks):
        i = tile * n_blocks + qb
        start = pl.multiple_of(jnp.clip(i * Q_BLOCK - Q_BLOCK, 0, n_lat - BAND), Q_BLOCK)
        valid = jnp.abs(key_minus_qry + (start - i * Q_BLOCK)) <= WINDOW
        v_band = vat_ref[pl.ds(start // LANES, BAND // LANES)]
        units += [make(qb, kv, half, start, valid, v_band) for kv in range(A_KV_HEADS) for half in range(2)]
    return units


def _probs_keys_major(scores_t, sink=None):
    m = None
    for s in scores_t:
        mx = jnp.max(s, axis=0, keepdims=True)
        m = mx if m is None else jnp.maximum(m, mx)
    if sink is not None:
        m = jnp.maximum(m, sink)
    probs = [jnp.exp2(s - m).astype(BF16) for s in scores_t]
    return probs, (None if sink is None else jnp.exp2(sink - m))


def _mla_units(qb_ref, kbl_ref, vtl_ref, kbc_ref, vtc_ref, ob_ref):
    kb = MLA_KEY_BLOCK
    ones = jnp.ones((SUM_ROWS, kb), BF16)
    blocks = [(kbc_ref, vtc_ref, j) for j in range(kbc_ref.shape[0] // kb)]
    blocks += [(kbl_ref, vtl_ref, j) for j in range(kbl_ref.shape[0] // kb)]
    pending = {}

    def make(h):
        lanes = slice(h * LANES, (h + 1) * LANES)
        feat = slice(h * B_V_DIM, (h + 1) * B_V_DIM)

        def scores():
            return [_dot_nt(k_ref[j * kb:(j + 1) * kb, lanes], qb_ref[:, lanes]) for k_ref, _, j in blocks]

        def values(p_blocks):
            acc = None
            for (_, v_ref, j), p in zip(blocks, p_blocks):
                r = _dot(jnp.concatenate([v_ref[feat, j * kb:(j + 1) * kb], ones], axis=0), p)
                acc = r if acc is None else acc + r
            return acc

        def finish(o):
            if h % 2 == 0:
                pending[h] = o
                return
            c = h // 2
            pair = jnp.concatenate([pending.pop(h - 1), o], axis=0)
            ob_ref[:, c * LANES:(c + 1) * LANES] = pair.T.astype(BF16)

        return _Unit(scores, lambda: None, values, finish)

    return [make(h) for h in range(B_HEADS)]


def _att_lat_kernel(cast_groups, n_lat, sink_ref, qa_ref, kap_ref, vat_ref, kapc_ref, vatc_ref,
                    qb_ref, kbl_ref, vtl_ref, kbc_ref, vtc_ref, *rest):
    n_src = sum(cast_groups)
    oa_ref, ob_ref = rest[n_src:n_src + 2]
    _cast_blocks(rest[:n_src], rest[n_src + 2:], cast_groups)

    mla = _mla_units(qb_ref, kbl_ref, vtl_ref, kbc_ref, vtc_ref, ob_ref)
    win = _window_units(n_lat, pl.program_id(1), sink_ref, qa_ref, kap_ref, vat_ref, kapc_ref, vatc_ref, oa_ref)
    assert len(win) % len(mla) == 0
    per_head = len(win) // len(mla)
    groups = [[unit] + win[h * per_head:(h + 1) * per_head] for h, unit in enumerate(mla)]
    _attention_pipeline(groups, B_V_DIM)


def _att_lat(sink, qa, kap, vat, kapc, vatc, qb, kbl, vtl, kbc, vtc, cast_weights):
    b, n, _ = qb.shape
    past = kbc.shape[1]
    tq = LAT_Q_TILE
    vdim = B_HEADS * B_V_DIM
    q_tiles = n // tq
    steps = b * q_tiles
    w_in_specs, w_args, w_out_specs, w_out_shapes, groups = _cast_windows(
        cast_weights, steps, lambda bi, i: bi * q_tiles + i)
    q_blk = lambda wd: pl.BlockSpec((None, tq, wd), lambda bi, i: (bi, i, 0))
    rows_blk = lambda r, wd: pl.BlockSpec((None, r, wd), lambda bi, i: (bi, 0, 0))
    feat_blk = lambda feat, r: pl.BlockSpec((feat, r), lambda bi, i: (0, bi))
    out = pl.pallas_call(
        functools.partial(_att_lat_kernel, groups, n),
        grid=(b, q_tiles),
        in_specs=[pl.BlockSpec(memory_space=pltpu.SMEM),
                  q_blk(A_Q_W), rows_blk(n, kap.shape[2]),
                  pl.BlockSpec((n // LANES, A_KV_W, LANES), lambda bi, i: (bi, 0, 0)),
                  rows_blk(past, kapc.shape[2]), feat_blk(A_KV_W, past),
                  q_blk(qb.shape[2]), rows_blk(n, kbl.shape[2]), feat_blk(vdim, n),
                  rows_blk(past, kbc.shape[2]), feat_blk(vdim, past)] + w_in_specs,
        out_specs=[q_blk(A_Q_W), q_blk(vdim)] + w_out_specs,
        out_shape=[jax.ShapeDtypeStruct((b, n, A_Q_W), BF16), jax.ShapeDtypeStruct((b, n, vdim), BF16)]
        + w_out_shapes,
        compiler_params=pltpu.CompilerParams(dimension_semantics=("parallel", "parallel"),
                                             vmem_limit_bytes=VMEM_LIMIT),
        name="att_latent",
    )(sink, qa, kap, vat, kapc, vatc, qb, kbl, vtl, kbc, vtc, *w_args)
    return out[0], out[1], out[2:]


def _post_kernel(x1_ref, mods_ref, oa_ref, ob_ref, nm_ref, wing_ref, woa_ref, wob_ref, wout_ref,
                 n2_ref, wgu_ref, wd_ref, nf_ref, y_ref):
    x1 = x1_ref[...]
    h2 = _mod_norm(x1, nm_ref[...], mods_ref, 1).astype(BF16)
    g = _dot_nt(h2, wing_ref[...])
    m = (jax.nn.sigmoid(g[:, :D_MODEL]) * _dot(oa_ref[...], woa_ref[...])
         + jax.nn.sigmoid(g[:, D_MODEL:]) * _dot(ob_ref[...], wob_ref[...]))
    x2 = x1 + _gate(mods_ref, 1) * _dot(m.astype(BF16), wout_ref[...])
    h3 = _mod_norm(x2, n2_ref[...], mods_ref, 2).astype(BF16)
    x3 = x2 + 0.5 * _gate(mods_ref, 2) * _ffn(h3, wgu_ref, wd_ref)
    y_ref[...] = _rms(x3, nf_ref[...])


def _post(x1, mods3, oa, ob, w, latent, tiles_per_batch):
    t = x1.shape[0]
    tm = POST_TILE
    if latent:
        mod_row = lambda i: (1 + i // tiles_per_batch, 0, 0)
    else:
        mod_row = lambda i: (0, 0, 0)
    row_blk = lambda i: (i, 0)
    weights = [w["nm"], w["win_g"], w["woa"], w["wob"], w["wout"], w["n2"], w["wgu2"], w["wd2"], w["nf"]]
    in_specs = [pl.BlockSpec((tm, D_MODEL), row_blk),
                pl.BlockSpec((None, 1, N_MOD * D_MODEL), mod_row),
                pl.BlockSpec((tm, A_Q_W), row_blk),
                pl.BlockSpec((tm, B_HEADS * B_V_DIM), row_blk)]
    in_specs += [_const_spec(a.shape) for a in weights]
    return pl.pallas_call(
        _post_kernel,
        grid=(t // tm,),
        in_specs=in_specs,
        out_specs=pl.BlockSpec((tm, D_MODEL), row_blk),
        out_shape=jax.ShapeDtypeStruct((t, D_MODEL), F32),
        compiler_params=pltpu.CompilerParams(dimension_semantics=("parallel",), vmem_limit_bytes=VMEM_LIMIT),
        name="post_latent" if latent else "post_context",
    )(x1, mods3, oa, ob, *weights)


def _rope_tables(n):
    f32 = np.float32
    rows = n // GRID_W
    t_row = np.repeat(np.arange(rows, dtype=f32), GRID_W)
    t_col = np.tile(np.arange(GRID_W, dtype=f32), rows)

    def angles(d_rot):
        d_half = d_rot // 2
        inv = (f32(1.0) / np.power(f32(ROPE_THETA), np.arange(0, d_half, 2, dtype=f32) / f32(d_half))).astype(f32)
        ar = t_row[:, None] * inv[None, :]
        ac = t_col[:, None] * inv[None, :]
        return np.concatenate([ar, ar, ac, ac], axis=-1).astype(f32)

    def signed(sin, d_rot):
        q = d_rot // 4
        sign = np.where((np.arange(d_rot) % (2 * q)) < q, f32(-1.0), f32(1.0)).astype(f32)
        return sin * sign[None, :]

    ang_a = angles(A_HEAD_DIM)
    cos_a = np.tile(np.cos(ang_a), (1, LANES // A_HEAD_DIM))
    sin_a = np.tile(signed(np.sin(ang_a), A_HEAD_DIM), (1, LANES // A_HEAD_DIM))
    ang_b = angles(B_ROPE_DIM)
    pad = ((0, 0), (B_NOPE_DIM, LANES - B_NOPE_DIM - B_ROPE_DIM))
    cos_b = np.pad(np.cos(ang_b), pad, constant_values=1.0)
    sin_b = np.pad(signed(np.sin(ang_b), B_ROPE_DIM), pad)
    return tuple(jnp.asarray(t, dtype=F32) for t in (cos_a, sin_a, cos_b, sin_b))


def _prep_weights(ffn1_norm, ffn1_w_gu, ffn1_w_down, mix_norm, w_in, q_lat_norm, kv_lat_norm, w_uq, w_ukv,
                  w_o_a, w_o_b, w_out, ffn2_norm, ffn2_w_gu, ffn2_w_down, final_norm):
    win_t = jnp.swapaxes(w_in[0], 0, 1)
    n_attn = A_Q_W + 2 * A_KV_W + Q_LORA + KV_LORA
    win_kr = jnp.pad(win_t[n_attn:n_attn + B_ROPE_DIM],
                     ((B_NOPE_DIM, LANES - B_NOPE_DIM - B_ROPE_DIM), (0, 0))).astype(BF16)
    n_gate = n_attn + B_ROPE_DIM
    whole = lambda wm: (wm, 0, wm.shape[0])

    wuq = jnp.pad(w_uq[0].reshape(Q_LORA, B_HEADS, B_QK_DIM),
                  ((0, 0), (0, 0), (0, LANES - B_QK_DIM))).reshape(Q_LORA, B_HEADS * LANES).astype(BF16)
    wukv = w_ukv[0].reshape(KV_LORA, B_HEADS, B_NOPE_DIM + B_V_DIM)
    wuk = jnp.pad(wukv[:, :, :B_NOPE_DIM],
                  ((0, 0), (0, 0), (0, LANES - B_NOPE_DIM))).reshape(KV_LORA, B_HEADS * LANES).astype(BF16)
    wuv = wukv[:, :, B_NOPE_DIM:].reshape(KV_LORA, B_HEADS * B_V_DIM).astype(BF16)

    return {
        "n1": ffn1_norm, "nm": mix_norm, "win_kr": win_kr,
        "qn": q_lat_norm, "kvn": kv_lat_norm, "wuq": wuq, "wuk": wuk, "wuv_t": wuv.T,
        "n2": ffn2_norm, "nf": final_norm.reshape(1, D_MODEL),
        "pre_f32": [whole(ffn1_w_gu[0]), whole(ffn1_w_down[0]), (win_t, 0, IN_A_W)],
        "post_f32": [(win_t, n_gate, 2 * D_MODEL), whole(w_o_a[0]), whole(w_o_b[0]), whole(w_out[0]),
                     whole(ffn2_w_gu[0]), whole(ffn2_w_down[0])],
    }


def kernel(x_prompt, x_sample, cache_attn_k, cache_attn_v, cache_mla_ckv, cache_mla_krope, c, c_ctx, ada_w, ada_b, ffn1_norm, ffn1_w_gu, ffn1_w_down, mix_norm, w_in, attn_sink, q_lat_norm, kv_lat_norm, w_uq, w_ukv, w_o_a, w_o_b, w_out, ffn2_norm, ffn2_w_gu, ffn2_w_down, final_norm):
    assert ada_w.shape[0] == 1, "single trunk layer"
    bp, sp, d = x_prompt.shape
    bs, ns, _ = x_sample.shape
    past = cache_attn_k.shape[2]
    assert d == D_MODEL and bs + 1 <= MOD_ROWS
    assert PRE_TILE_CTX % sp == 0 and (bp * sp) % PRE_TILE_CTX == 0 and (bp * sp) % POST_TILE == 0
    assert ns % PRE_TILE_LAT == 0 and ns % POST_TILE == 0
    assert ns % LAT_Q_TILE == 0 and LAT_Q_TILE % Q_BLOCK == 0 and ns >= BAND

    w = _prep_weights(ffn1_norm, ffn1_w_gu, ffn1_w_down, mix_norm, w_in, q_lat_norm, kv_lat_norm, w_uq, w_ukv,
                      w_o_a, w_o_b, w_out, ffn2_norm, ffn2_w_gu, ffn2_w_down, final_norm)
    sink = attn_sink[0]

    cvec = jnp.concatenate([c_ctx[None, :], c, jnp.zeros((MOD_ROWS - 1 - bs, d), F32)], axis=0)
    mods, (w["wgu1"], w["wd1"], w["win_a"]) = _ada_mods(cvec, ada_w[0], ada_b, w["pre_f32"])
    mods3 = mods.reshape(MOD_ROWS, 1, N_MOD * D_MODEL)

    xp = x_prompt.reshape(bp * sp, d)
    (x1p, qa, kap, qb, kb, vb_t, k_t, v_t, ckv_n, krope_t) = _pre(xp, mods3, w, False, sp, None, PRE_TILE_CTX)
    oa_p, ob_p = _att_ctx(sink, qa, kap, v_t, qb, kb, vb_t, sp)

    xs = x_sample.reshape(bs * ns, d)
    (x1s, qa, kap, vad, qb, kb, vb_t) = _pre(xs, mods3, w, True, ns, _rope_tables(ns), PRE_TILE_LAT)
    feat_major = lambda a: a[:, 0].transpose(0, 2, 3, 1).reshape(bs, A_KV_W, past)
    kbc, vbc, kapc, vadc = _cache_kv(
        cache_mla_ckv[:, 0], jnp.swapaxes(cache_mla_krope[:, 0], 1, 2),
        feat_major(cache_attn_k), feat_major(cache_attn_v), w["wuk"], w["wuv_t"])
    r3 = lambda a: a.reshape(bs, ns, a.shape[1])
    oa, ob, (w["win_g"], w["woa"], w["wob"], w["wout"], w["wgu2"], w["wd2"]) = _att_lat(
        sink, r3(qa), r3(kap), vad, kapc, vadc, r3(qb), r3(kb), vb_t, kbc, vbc, w["post_f32"])

    y_prompt = _post(x1p, mods3, oa_p, ob_p, w, False, 1).reshape(bp, sp, d)
    y_sample = _post(x1s, mods3, oa.reshape(bs * ns, A_Q_W), ob.reshape(bs * ns, B_HEADS * B_V_DIM),
                     w, True, ns // POST_TILE).reshape(bs, ns, d)

    new_attn_k = k_t.reshape(bp, 1, A_KV_HEADS, A_HEAD_DIM, sp).transpose(0, 1, 4, 2, 3)
    new_attn_v = v_t.reshape(bp, 1, A_KV_HEADS, A_HEAD_DIM, sp).transpose(0, 1, 4, 2, 3)
    new_mla_ckv = ckv_n.reshape(bp, 1, sp, KV_LORA)
    new_mla_krope = krope_t.reshape(bp, 1, B_ROPE_DIM, sp).transpose(0, 1, 3, 2)
    return (y_prompt, y_sample, new_attn_k, new_attn_v, new_mla_ckv, new_mla_krope)
```

```python
import functools
import math

import jax
import jax.numpy as jnp
import numpy as np
from jax import lax
from jax.experimental import pallas as pl
from jax.experimental.pallas import tpu as pltpu

F32 = jnp.float32
BF16 = jnp.bfloat16

D_MODEL = 1024
N_MOD = 9
GRID_W = 64
WINDOW = 128
A_HEADS = 8
A_KV_HEADS = 2
A_HEAD_DIM = 64
A_Q_W = A_HEADS * A_HEAD_DIM
A_KV_W = A_KV_HEADS * A_HEAD_DIM
B_HEADS = 8
B_NOPE_DIM = 64
B_ROPE_DIM = 32
B_V_DIM = 64
B_QK_DIM = B_NOPE_DIM + B_ROPE_DIM
Q_LORA = 256
KV_LORA = 256
D_FF = 2816
ROPE_THETA = 10000.0
EPS = 1e-6
NEG = -1e30
A_SCALE = A_HEAD_DIM ** -0.5
B_SCALE = B_QK_DIM ** -0.5
LOG2E = 1.4426950408889634

LANES = 128
HALF = LANES // 2
FF_CHUNK = 256
N_FF_CHUNKS = D_FF // FF_CHUNK
PRE_TILE_CTX = 512
PRE_TILE_LAT = 256
POST_TILE = 512
ADA_TILE = 2304
Q_BLOCK = 128
BAND = 3 * Q_BLOCK
CTX_BATCHES = 4
CTX_GROUP = 4
BF16_TILE_ROWS = 16
SUM_ROWS = BF16_TILE_ROWS
LAT_Q_TILE = 512
MLA_KEY_BLOCK = 256
MOD_ROWS = 8
VMEM_LIMIT = 56 * 1024 * 1024

IN_A_W = A_Q_W + 2 * A_KV_W + Q_LORA + KV_LORA
OFF_KA = A_Q_W
OFF_VA = OFF_KA + A_KV_W
OFF_QLAT = OFF_VA + A_KV_W
OFF_CKV = OFF_QLAT + Q_LORA


def _dot(a, b):
    return jnp.dot(a, b, preferred_element_type=F32)


def _dot_nt(a, b):
    return lax.dot_general(a, b, (((1,), (1,)), ((), ())), preferred_element_type=F32)


def _rms(x, g):
    ms = jnp.mean(x * x, axis=-1, keepdims=True)
    return x * lax.rsqrt(ms + EPS) * g


def _mod_norm(x, g, mods_ref, k):
    shift = mods_ref[:, (3 * k) * D_MODEL:(3 * k + 1) * D_MODEL]
    scale = mods_ref[:, (3 * k + 1) * D_MODEL:(3 * k + 2) * D_MODEL]
    return _rms(x, g) * (1.0 + scale) + shift


def _gate(mods_ref, k):
    return mods_ref[:, (3 * k + 2) * D_MODEL:(3 * k + 3) * D_MODEL]


def _ffn_stages(h, wgu_ref, wd_ref, hooks=None):
    def gate_up(c):
        a = _dot(h, wgu_ref[:, c * FF_CHUNK:(c + 1) * FF_CHUNK])
        u = _dot(h, wgu_ref[:, D_FF + c * FF_CHUNK:D_FF + (c + 1) * FF_CHUNK])
        return a, u

    acc = None
    nxt = gate_up(0)
    yield
    for c in range(N_FF_CHUNKS):
        a, u = nxt
        if c + 1 < N_FF_CHUNKS:
            nxt = gate_up(c + 1)
        yield
        act = (a * jax.nn.sigmoid(a) * u).astype(BF16)
        d = _dot(act, wd_ref[c * FF_CHUNK:(c + 1) * FF_CHUNK, :])
        acc = d if acc is None else acc + d
        if hooks and c in hooks:
            hooks[c]()
    return acc


def _ffn(h, wgu_ref, wd_ref, hooks=None):
    stages = _ffn_stages(h, wgu_ref, wd_ref, hooks)
    while True:
        try:
            next(stages)
        except StopIteration as done:
            return done.value


def _interleave(streams):
    live = list(streams)
    next(live[0])
    while live:
        for s in list(live):
            try:
                next(s)
            except StopIteration:
                live.remove(s)


def _rope(x, cos, sin_signed, dist):
    lane = lax.broadcasted_iota(jnp.int32, x.shape, 1)
    first = (lane & (2 * dist - 1)) < dist
    partner = jnp.where(first, pltpu.roll(x, LANES - dist, axis=1), pltpu.roll(x, dist, axis=1))
    return x * cos + partner * sin_signed


def _store_gqa_k(k, low, kap_ref):
    k_sw = pltpu.roll(k, HALF, axis=1)
    zero = jnp.zeros_like(k)
    kap_ref[:, 0 * LANES:1 * LANES] = jnp.where(low, k, zero).astype(BF16)
    kap_ref[:, 1 * LANES:2 * LANES] = jnp.where(low, zero, k_sw).astype(BF16)
    kap_ref[:, 2 * LANES:3 * LANES] = jnp.where(low, k_sw, zero).astype(BF16)
    kap_ref[:, 3 * LANES:4 * LANES] = jnp.where(low, zero, k).astype(BF16)


def _store_gqa_v_blocks(v, vt_ref):
    for r in range(vt_ref.shape[0]):
        vt_ref[r] = v[r * LANES:(r + 1) * LANES, :].T.astype(BF16)


def _store_seq_minor(out_ref, val):
    nb, feat, seq = out_ref.shape
    for bi in range(nb):
        out_ref[bi] = val[bi * seq:(bi + 1) * seq, :].T[:feat, :]


def _low_lanes(rows):
    return lax.broadcasted_iota(jnp.int32, (rows, LANES), 1) < HALF


def _cast_windows(windows, steps, step_of):
    in_specs, args, out_specs, out_shapes, groups = [], [], [], [], []
    for wm, row0, rows in windows:
        cols = wm.shape[1]
        assert rows % (steps * BF16_TILE_ROWS) == 0, "row blocks must be whole packed-bf16 sublane tiles"
        per_step = rows // steps
        sub = math.gcd(per_step, row0) if row0 else per_step
        assert sub % BF16_TILE_ROWS == 0
        k = per_step // sub
        for j in range(k):
            in_specs.append(pl.BlockSpec(
                (sub, cols), lambda *g, j=j, k=k, base=row0 // sub: (base + step_of(*g) * k + j, 0)))
            args.append(wm)
        out_specs.append(pl.BlockSpec((per_step, cols), lambda *g: (step_of(*g), 0)))
        out_shapes.append(jax.ShapeDtypeStruct((rows, cols), BF16))
        groups.append(k)
    return in_specs, args, out_specs, out_shapes, tuple(groups)


def _cast_blocks(w32_refs, w16_refs, groups):
    first = 0
    for dst, k in zip(w16_refs, groups):
        sub = dst.shape[0] // k
        for j in range(k):
            dst[j * sub:(j + 1) * sub, :] = w32_refs[first + j][...].astype(BF16)
        first += k


def _ada_kernel(cast_groups, c_ref, w_ref, b_ref, *rest):
    n_src = sum(cast_groups)
    _cast_blocks(rest[:n_src], rest[n_src + 1:], cast_groups)
    o_ref = rest[n_src]
    c = c_ref[...]
    s = (c * jax.nn.sigmoid(c)).astype(BF16)
    o_ref[...] = _dot(s, w_ref[...].astype(BF16)) + b_ref[...]


def _ada_mods(cvec, ada_w, ada_b, cast_weights):
    n = ada_w.shape[1]
    steps = n // ADA_TILE
    w_in_specs, w_args, w_out_specs, w_out_shapes, groups = _cast_windows(cast_weights, steps, lambda j: j)
    out = pl.pallas_call(
        functools.partial(_ada_kernel, groups),
        grid=(steps,),
        in_specs=[
            pl.BlockSpec((MOD_ROWS, D_MODEL), lambda j: (0, 0)),
            pl.BlockSpec((D_MODEL, ADA_TILE), lambda j: (0, j)),
            pl.BlockSpec((1, ADA_TILE), lambda j: (0, j)),
        ] + w_in_specs,
        out_specs=[pl.BlockSpec((MOD_ROWS, ADA_TILE), lambda j: (0, j))] + w_out_specs,
        out_shape=[jax.ShapeDtypeStruct((MOD_ROWS, n), F32)] + w_out_shapes,
        compiler_params=pltpu.CompilerParams(dimension_semantics=("parallel",), vmem_limit_bytes=VMEM_LIMIT),
        name="ada_mods",
    )(cvec, ada_w, ada_b, *w_args)
    return out[0], out[1:]


PRE_HOOK_IN_PROJ = 1
PRE_HOOK_UP_PROJ = 6


def _pre_kernel(latent, n_tiles, *refs):
    x1s_ref = refs[-1]
    if latent:
        (x_ref, mods_ref, modsp_ref, n1_ref, wgu_ref, wd_ref, nm_ref, win_ref, wkr_ref, qn_ref, kvn_ref, wuq_ref, wuk_ref,
         wuv_ref, cosa_ref, sina_ref, cosb_ref, sinb_ref,
         x1_ref, qa_ref, kap_ref, vad_ref, qb_ref, kb_ref, vb_ref) = refs[:-1]
    else:
        (x_ref, mods_ref, modsp_ref, n1_ref, wgu_ref, wd_ref, nm_ref, win_ref, wkr_ref, qn_ref, kvn_ref, wuq_ref, wuk_ref,
         wuv_ref,
         x1_ref, qa_ref, kap_ref, qb_ref, kb_ref, vb_ref,
         knat_ref, vnat_ref, ckvn_ref, krope_ref) = refs[:-1]
    i = pl.program_id(0)
    carry = {}

    def in_proj():
        h2 = _mod_norm(x1s_ref[...], nm_ref[...], modsp_ref, 1).astype(BF16)
        z = _dot_nt(h2, win_ref[...])
        low = _low_lanes(z.shape[0])

        for c in range(A_Q_W // LANES):
            q = z[:, c * LANES:(c + 1) * LANES]
            if latent:
                q = _rope(q, cosa_ref[...], sina_ref[...], A_HEAD_DIM // 4)
            qa_ref[:, c * LANES:(c + 1) * LANES] = (q * (A_SCALE * LOG2E)).astype(BF16)
        k = z[:, OFF_KA:OFF_KA + LANES]
        v = z[:, OFF_VA:OFF_VA + LANES]
        if latent:
            k = _rope(k, cosa_ref[...], sina_ref[...], A_HEAD_DIM // 4)
            _store_gqa_v_blocks(v, vad_ref)
        else:
            _store_seq_minor(knat_ref, k)
            _store_seq_minor(vnat_ref, v)
        _store_gqa_k(k, low, kap_ref)

        carry["q_lat"] = _rms(z[:, OFF_QLAT:OFF_QLAT + Q_LORA], qn_ref[...]).astype(BF16)
        ckv_n = _rms(z[:, OFF_CKV:OFF_CKV + KV_LORA], kvn_ref[...])
        krp = _dot_nt(h2, wkr_ref[...])
        if latent:
            krp = _rope(krp, cosb_ref[...], sinb_ref[...], B_ROPE_DIM // 4)
        else:
            ckvn_ref[...] = ckv_n
            _store_seq_minor(krope_ref, pltpu.roll(krp, HALF, axis=1))
        carry["ckv_b"] = ckv_n.astype(BF16)
        carry["krp"] = krp

    def up_proj():
        q_lat, ckv_b, krp = carry["q_lat"], carry["ckv_b"], carry["krp"]
        qb = _dot(q_lat, wuq_ref[...])
        kn = _dot(ckv_b, wuk_ref[...])
        vb_ref[...] = _dot_nt(wuv_ref[...], ckv_b).astype(BF16)
        for h in range(B_HEADS):
            qh = qb[:, h * LANES:(h + 1) * LANES]
            if latent:
                qh = _rope(qh, cosb_ref[...], sinb_ref[...], B_ROPE_DIM // 4)
            qb_ref[:, h * LANES:(h + 1) * LANES] = (qh * (B_SCALE * LOG2E)).astype(BF16)
            kb_ref[:, h * LANES:(h + 1) * LANES] = (kn[:, h * LANES:(h + 1) * LANES] + krp).astype(BF16)

    def ffn(hooks):
        x = x_ref[...]
        h1 = _mod_norm(x, n1_ref[...], mods_ref, 0).astype(BF16)
        x1 = x + 0.5 * _gate(mods_ref, 0) * _ffn(h1, wgu_ref, wd_ref, hooks)
        x1_ref[...] = x1
        x1s_ref[...] = x1

    @pl.when(i == 0)
    def _():
        ffn(None)

    @pl.when(jnp.logical_and(i > 0, i < n_tiles))
    def _():
        ffn({PRE_HOOK_IN_PROJ: in_proj, PRE_HOOK_UP_PROJ: up_proj})

    @pl.when(i == n_tiles)
    def _():
        in_proj()
        up_proj()


def _const_spec(shape):
    nd = len(shape)
    return pl.BlockSpec(shape, lambda i: (0,) * nd, pipeline_mode=pl.Buffered(1))


def _pre(x, mods3, w, latent, seq, rope, tm):
    t = x.shape[0]
    tiles_per_batch = max(seq // tm, 1)
    n_tiles = t // tm
    cur = lambda i: jnp.minimum(i, n_tiles - 1)
    prev = lambda i: jnp.maximum(i - 1, 0)
    mod_of = (lambda j: 1 + j // tiles_per_batch) if latent else (lambda j: 0)
    cur_rows = lambda i: (cur(i), 0)
    prev_rows = lambda i: (prev(i), 0)

    weights = [w["n1"], w["wgu1"], w["wd1"], w["nm"], w["win_a"], w["win_kr"], w["qn"], w["kvn"], w["wuq"], w["wuk"],
               w["wuv_t"]]
    in_specs = [pl.BlockSpec((tm, D_MODEL), cur_rows),
                pl.BlockSpec((None, 1, N_MOD * D_MODEL), lambda i: (mod_of(cur(i)), 0, 0)),
                pl.BlockSpec((None, 1, N_MOD * D_MODEL), lambda i: (mod_of(prev(i)), 0, 0))]
    in_specs += [_const_spec(a.shape) for a in weights]
    args = [x, mods3, mods3] + weights
    if latent:
        in_specs += [pl.BlockSpec((tm, LANES), lambda i: (prev(i) % tiles_per_batch, 0))] * 4
        args += list(rope)

    out_specs = [pl.BlockSpec((tm, D_MODEL), cur_rows)]
    out_shape = [jax.ShapeDtypeStruct((t, D_MODEL), F32)]
    for wd in (A_Q_W, 4 * LANES):
        out_specs.append(pl.BlockSpec((tm, wd), prev_rows))
        out_shape.append(jax.ShapeDtypeStruct((t, wd), BF16))
    if latent:
        out_specs.append(pl.BlockSpec((tm // LANES, A_KV_W, LANES), lambda i: (prev(i), 0, 0)))
        out_shape.append(jax.ShapeDtypeStruct((t // LANES, A_KV_W, LANES), BF16))
    for wd in (B_HEADS * LANES, B_HEADS * LANES):
        out_specs.append(pl.BlockSpec((tm, wd), prev_rows))
        out_shape.append(jax.ShapeDtypeStruct((t, wd), BF16))
    out_specs.append(pl.BlockSpec((B_HEADS * B_V_DIM, tm), lambda i: (0, prev(i))))
    out_shape.append(jax.ShapeDtypeStruct((B_HEADS * B_V_DIM, t), BF16))
    if not latent:
        nb = tm // seq
        for feat in (A_KV_W, A_KV_W):
            out_specs.append(pl.BlockSpec((nb, feat, seq), lambda i: (prev(i), 0, 0)))
            out_shape.append(jax.ShapeDtypeStruct((t // seq, feat, seq), F32))
        out_specs.append(pl.BlockSpec((tm, KV_LORA), prev_rows))
        out_shape.append(jax.ShapeDtypeStruct((t, KV_LORA), F32))
        out_specs.append(pl.BlockSpec((nb, B_ROPE_DIM, seq), lambda i: (prev(i), 0, 0)))
        out_shape.append(jax.ShapeDtypeStruct((t // seq, B_ROPE_DIM, seq), F32))

    return pl.pallas_call(
        functools.partial(_pre_kernel, latent, n_tiles),
        grid=(n_tiles + 1,),
        in_specs=in_specs,
        out_specs=out_specs,
        out_shape=out_shape,
        scratch_shapes=[pltpu.VMEM((tm, D_MODEL), F32)],
        compiler_params=pltpu.CompilerParams(dimension_semantics=("arbitrary",), vmem_limit_bytes=VMEM_LIMIT),
        name="pre_latent" if latent else "pre_context",
    )(*args)


def _cache_kernel(ckv_ref, kropet_ref, ckt_ref, cvt_ref, wuk_ref, wuv_ref, kb_ref, vb_ref, kap_ref, vat_ref):
    p = ckv_ref.shape[0]
    c = ckv_ref[...].astype(BF16)
    kn = _dot(c, wuk_ref[...])
    krp = jnp.concatenate([jnp.zeros((B_NOPE_DIM, p), F32), kropet_ref[...],
                           jnp.zeros((LANES - B_QK_DIM, p), F32)], axis=0).T
    for h in range(B_HEADS):
        kb_ref[:, h * LANES:(h + 1) * LANES] = (kn[:, h * LANES:(h + 1) * LANES] + krp).astype(BF16)
    vb_ref[...] = _dot_nt(wuv_ref[...], c).astype(BF16)
    _store_gqa_k(ckt_ref[...].T, _low_lanes(p), kap_ref)
    vat_ref[...] = cvt_ref[...].astype(BF16)


def _cache_kv(ckv, krope_t, ck_t, cv_t, wuk, wuv_t):
    b, p, _ = ckv.shape
    blk = lambda wd: pl.BlockSpec((None, p, wd), lambda i: (i, 0, 0))
    blk_t = lambda feat: pl.BlockSpec((None, feat, p), lambda i: (i, 0, 0))
    vdim = B_HEADS * B_V_DIM
    return pl.pallas_call(
        _cache_kernel,
        grid=(b,),
        in_specs=[blk(KV_LORA), blk_t(B_ROPE_DIM), blk_t(A_KV_W), blk_t(A_KV_W),
                  pl.BlockSpec(wuk.shape, lambda i: (0, 0)),
                  pl.BlockSpec(wuv_t.shape, lambda i: (0, 0))],
        out_specs=[blk(B_HEADS * LANES), pl.BlockSpec((vdim, p), lambda i: (0, i)), blk(4 * LANES),
                   pl.BlockSpec((A_KV_W, p), lambda i: (0, i))],
        out_shape=[jax.ShapeDtypeStruct((b, p, B_HEADS * LANES), BF16),
                   jax.ShapeDtypeStruct((vdim, b * p), BF16),
                   jax.ShapeDtypeStruct((b, p, 4 * LANES), BF16),
                   jax.ShapeDtypeStruct((A_KV_W, b * p), BF16)],
        compiler_params=pltpu.CompilerParams(dimension_semantics=("parallel",), vmem_limit_bytes=VMEM_LIMIT),
        name="cache_kv",
    )(ckv, krope_t, ck_t, cv_t, wuk, wuv_t)


def _att_ctx_kernel(seq, sink_ref, qa_ref, kap_ref, vt_ref, qb_ref, kb_ref, vbt_ref, oa_ref, ob_ref):
    nb = qa_ref.shape[0] // seq
    ones = jnp.ones((SUM_ROWS, seq), BF16)
    upper = lax.broadcasted_iota(jnp.int32, (1, 2 * seq), 1) >= seq
    pending = {}

    def gqa_unit(bi, kv, half):
        rows = slice(bi * seq, (bi + 1) * seq)
        u = 2 * kv + half

        def scores():
            q2 = jnp.concatenate([qa_ref[rows, (2 * kv) * LANES:(2 * kv + 1) * LANES],
                                  qa_ref[rows, (2 * kv + 1) * LANES:(2 * kv + 2) * LANES]], axis=0)
            return [_dot_nt(kap_ref[rows, u * LANES:(u + 1) * LANES], q2)]

        def sink():
            return jnp.where(upper, sink_ref[4 * kv + 2 + half], sink_ref[4 * kv + half]) * LOG2E

        def values(p_blocks):
            v = vt_ref[bi, kv * A_HEAD_DIM:(kv + 1) * A_HEAD_DIM, :].astype(BF16)
            return _dot(jnp.concatenate([v, ones], axis=0), p_blocks[0])

        def finish(o):
            if half == 0:
                pending[("a", bi, kv)] = o
                return
            first = pending.pop(("a", bi, kv))
            for pr in range(2):
                cols = slice(pr * seq, (pr + 1) * seq)
                pair = jnp.concatenate([first[:, cols], o[:, cols]], axis=0)
                c = 2 * kv + pr
                oa_ref[rows, c * LANES:(c + 1) * LANES] = pair.T.astype(BF16)

        return _Unit(scores, sink, values, finish)

    def mla_unit(bi, h):
        rows = slice(bi * seq, (bi + 1) * seq)
        lanes = slice(h * LANES, (h + 1) * LANES)

        def scores():
            return [_dot_nt(kb_ref[rows, lanes], qb_ref[rows, lanes])]

        def values(p_blocks):
            v = vbt_ref[h * B_V_DIM:(h + 1) * B_V_DIM, rows]
            return _dot(jnp.concatenate([v, ones], axis=0), p_blocks[0])

        def finish(o):
            if h % 2 == 0:
                pending[("b", bi, h)] = o
                return
            c = h // 2
            pair = jnp.concatenate([pending.pop(("b", bi, h - 1)), o], axis=0)
            ob_ref[rows, c * LANES:(c + 1) * LANES] = pair.T.astype(BF16)

        return _Unit(scores, lambda: None, values, finish)

    units = []
    for bi in range(nb):
        units += [gqa_unit(bi, kv, half) for kv in range(A_KV_HEADS) for half in range(2)]
        units += [mla_unit(bi, h) for h in range(B_HEADS)]
    assert len(units) % CTX_GROUP == 0
    _attention_pipeline([units[g:g + CTX_GROUP] for g in range(0, len(units), CTX_GROUP)], A_HEAD_DIM)


def _att_ctx(sink, qa, kap, v_t, qb, kb, vb_t, seq):
    t = qa.shape[0]
    rows = CTX_BATCHES * seq
    blk = lambda wd: pl.BlockSpec((rows, wd), lambda b: (b, 0))
    return pl.pallas_call(
        functools.partial(_att_ctx_kernel, seq),
        grid=(t // rows,),
        in_specs=[pl.BlockSpec(memory_space=pltpu.SMEM),
                  blk(qa.shape[1]), blk(kap.shape[1]),
                  pl.BlockSpec((CTX_BATCHES, A_KV_W, seq), lambda b: (b, 0, 0)),
                  blk(qb.shape[1]), blk(kb.shape[1]),
                  pl.BlockSpec((vb_t.shape[0], rows), lambda b: (0, b))],
        out_specs=[blk(A_Q_W), blk(B_HEADS * B_V_DIM)],
        out_shape=[jax.ShapeDtypeStruct((t, A_Q_W), BF16), jax.ShapeDtypeStruct((t, B_HEADS * B_V_DIM), BF16)],
        compiler_params=pltpu.CompilerParams(dimension_semantics=("parallel",), vmem_limit_bytes=VMEM_LIMIT),
        name="att_context",
    )(sink, qa, kap, v_t, qb, kb, vb_t)


class _Unit:
    def __init__(self, scores, sink, values, finish):
        self.scores, self.sink, self.values, self.finish = scores, sink, values, finish


def _attention_pipeline(groups, dv):
    n = len(groups)
    scores, probs = {}, {}
    for t in range(n + 2):
        if t < n:
            scores[t] = [unit.scores() for unit in groups[t]]
        if 0 <= t - 2 < n:
            for unit, (p_blocks, p_sink) in zip(groups[t - 2], probs.pop(t - 2)):
                acc = unit.values(p_blocks)
                den = acc[dv:dv + 1]
                unit.finish(acc[:dv] / (den if p_sink is None else den + p_sink))
        if 0 <= t - 1 < n:
            probs[t - 1] = [_probs_keys_major(s, unit.sink()) for unit, s in zip(groups[t - 1], scores.pop(t - 1))]


def _window_units(n_lat, tile, sink_ref, qa_ref, kap_ref, vat_ref, kapc_ref, vatc_ref, oa_ref):
    n_blocks = qa_ref.shape[0] // Q_BLOCK
    nq = 2 * Q_BLOCK
    past = kapc_ref.shape[0]
    upper = lax.broadcasted_iota(jnp.int32, (1, nq), 1) >= Q_BLOCK
    ones_band = jnp.ones((SUM_ROWS, BAND), BF16)
    ones_ctx = jnp.ones((SUM_ROWS, past), BF16)
    key = lax.broadcasted_iota(jnp.int32, (BAND, nq), 0)
    qry = lax.broadcasted_iota(jnp.int32, (BAND, nq), 1) & (Q_BLOCK - 1)
    key_minus_qry = key - qry
    pending = {}

    def make(qb, kv, half, start, valid, v_band):
        u = 2 * kv + half
        rows = slice(qb * Q_BLOCK, (qb + 1) * Q_BLOCK)
        feat = slice(kv * A_HEAD_DIM, (kv + 1) * A_HEAD_DIM)

        def scores():
            q2 = jnp.concatenate([qa_ref[rows, (2 * kv) * LANES:(2 * kv + 1) * LANES],
                                  qa_ref[rows, (2 * kv + 1) * LANES:(2 * kv + 2) * LANES]], axis=0)
            s_band = _dot_nt(kap_ref[pl.ds(start, BAND), u * LANES:(u + 1) * LANES], q2)
            s_ctx = _dot_nt(kapc_ref[:, u * LANES:(u + 1) * LANES], q2)
            return [jnp.where(valid, s_band, NEG), s_ctx]

        def sink():
            return jnp.where(upper, sink_ref[4 * kv + 2 + half], sink_ref[4 * kv + half]) * LOG2E

        def values(p_blocks):
            vb = jnp.concatenate([v_band[r][feat] for r in range(BAND // LANES)], axis=1)
            r_band = _dot(jnp.concatenate([vb, ones_band], axis=0), p_blocks[0])
            r_ctx = _dot(jnp.concatenate([vatc_ref[feat, :], ones_ctx], axis=0), p_blocks[1])
            return r_band + r_ctx

        def finish(o):
            if half == 0:
                pending[(qb, kv)] = o
                return
            first = pending.pop((qb, kv))
            for pr in range(2):
                cols = slice(pr * Q_BLOCK, (pr + 1) * Q_BLOCK)
                pair = jnp.concatenate([first[:, cols], o[:, cols]], axis=0)
                c = 2 * kv + pr
                oa_ref[rows, c * LANES:(c + 1) * LANES] = pair.T.astype(BF16)

        return _Unit(scores, sink, values, finish)

    units = []
    for qb in range(n_blocks):
        i = tile * n_blocks + qb
        start = pl.multiple_of(jnp.clip(i * Q_BLOCK - Q_BLOCK, 0, n_lat - BAND), Q_BLOCK)
        valid = jnp.abs(key_minus_qry + (start - i * Q_BLOCK)) <= WINDOW
        v_band = vat_ref[pl.ds(start // LANES, BAND // LANES)]
        units += [make(qb, kv, half, start, valid, v_band) for kv in range(A_KV_HEADS) for half in range(2)]
    return units


def _probs_keys_major(scores_t, sink=None):
    m = None
    for s in scores_t:
        mx = jnp.max(s, axis=0, keepdims=True)
        m = mx if m is None else jnp.maximum(m, mx)
    if sink is not None:
        m = jnp.maximum(m, sink)
    probs = [jnp.exp2(s - m).astype(BF16) for s in scores_t]
    return probs, (None if sink is None else jnp.exp2(sink - m))


def _mla_units(qb_ref, kbl_ref, vtl_ref, kbc_ref, vtc_ref, ob_ref):
    kb = MLA_KEY_BLOCK
    ones = jnp.ones((SUM_ROWS, kb), BF16)
    blocks = [(kbc_ref, vtc_ref, j) for j in range(kbc_ref.shape[0] // kb)]
    blocks += [(kbl_ref, vtl_ref, j) for j in range(kbl_ref.shape[0] // kb)]
    pending = {}

    def make(h):
        lanes = slice(h * LANES, (h + 1) * LANES)
        feat = slice(h * B_V_DIM, (h + 1) * B_V_DIM)

        def scores():
            return [_dot_nt(k_ref[j * kb:(j + 1) * kb, lanes], qb_ref[:, lanes]) for k_ref, _, j in blocks]

        def values(p_blocks):
            acc = None
            for (_, v_ref, j), p in zip(blocks, p_blocks):
                r = _dot(jnp.concatenate([v_ref[feat, j * kb:(j + 1) * kb], ones], axis=0), p)
                acc = r if acc is None else acc + r
            return acc

        def finish(o):
            if h % 2 == 0:
                pending[h] = o
                return
            c = h // 2
            pair = jnp.concatenate([pending.pop(h - 1), o], axis=0)
            ob_ref[:, c * LANES:(c + 1) * LANES] = pair.T.astype(BF16)

        return _Unit(scores, lambda: None, values, finish)

    return [make(h) for h in range(B_HEADS)]


def _att_lat_kernel(cast_groups, n_lat, sink_ref, qa_ref, kap_ref, vat_ref, kapc_ref, vatc_ref,
                    qb_ref, kbl_ref, vtl_ref, kbc_ref, vtc_ref, *rest):
    n_src = sum(cast_groups)
    oa_ref, ob_ref = rest[n_src:n_src + 2]
    _cast_blocks(rest[:n_src], rest[n_src + 2:], cast_groups)

    mla = _mla_units(qb_ref, kbl_ref, vtl_ref, kbc_ref, vtc_ref, ob_ref)
    win = _window_units(n_lat, pl.program_id(1), sink_ref, qa_ref, kap_ref, vat_ref, kapc_ref, vatc_ref, oa_ref)
    assert len(win) % len(mla) == 0
    per_head = len(win) // len(mla)
    groups = [[unit] + win[h * per_head:(h + 1) * per_head] for h, unit in enumerate(mla)]
    _attention_pipeline(groups, B_V_DIM)


def _att_lat(sink, qa, kap, vat, kapc, vatc, qb, kbl, vtl, kbc, vtc, cast_weights):
    b, n, _ = qb.shape
    past = kbc.shape[1]
    tq = LAT_Q_TILE
    vdim = B_HEADS * B_V_DIM
    q_tiles = n // tq
    steps = b * q_tiles
    w_in_specs, w_args, w_out_specs, w_out_shapes, groups = _cast_windows(
        cast_weights, steps, lambda bi, i: bi * q_tiles + i)
    q_blk = lambda wd: pl.BlockSpec((None, tq, wd), lambda bi, i: (bi, i, 0))
    rows_blk = lambda r, wd: pl.BlockSpec((None, r, wd), lambda bi, i: (bi, 0, 0))
    feat_blk = lambda feat, r: pl.BlockSpec((feat, r), lambda bi, i: (0, bi))
    out = pl.pallas_call(
        functools.partial(_att_lat_kernel, groups, n),
        grid=(b, q_tiles),
        in_specs=[pl.BlockSpec(memory_space=pltpu.SMEM),
                  q_blk(A_Q_W), rows_blk(n, kap.shape[2]),
                  pl.BlockSpec((n // LANES, A_KV_W, LANES), lambda bi, i: (bi, 0, 0)),
                  rows_blk(past, kapc.shape[2]), feat_blk(A_KV_W, past),
                  q_blk(qb.shape[2]), rows_blk(n, kbl.shape[2]), feat_blk(vdim, n),
                  rows_blk(past, kbc.shape[2]), feat_blk(vdim, past)] + w_in_specs,
        out_specs=[q_blk(A_Q_W), q_blk(vdim)] + w_out_specs,
        out_shape=[jax.ShapeDtypeStruct((b, n, A_Q_W), BF16), jax.ShapeDtypeStruct((b, n, vdim), BF16)]
        + w_out_shapes,
        compiler_params=pltpu.CompilerParams(dimension_semantics=("parallel", "parallel"),
                                             vmem_limit_bytes=VMEM_LIMIT),
        name="att_latent",
    )(sink, qa, kap, vat, kapc, vatc, qb, kbl, vtl, kbc, vtc, *w_args)
    return out[0], out[1], out[2:]


def _post_kernel(x1_ref, mods_ref, oa_ref, ob_ref, nm_ref, wing_ref, woa_ref, wob_ref, wout_ref,
                 n2_ref, wgu_ref, wd_ref, nf_ref, y_ref):
    def half_tile(rows):
        x1 = x1_ref[rows, :]
        h2 = _mod_norm(x1, nm_ref[...], mods_ref, 1).astype(BF16)
        g = _dot_nt(h2, wing_ref[...])
        ma = _dot(oa_ref[rows, :], woa_ref[...])
        mb = _dot(ob_ref[rows, :], wob_ref[...])
        yield
        m = jax.nn.sigmoid(g[:, :D_MODEL]) * ma + jax.nn.sigmoid(g[:, D_MODEL:]) * mb
        x2 = x1 + _gate(mods_ref, 1) * _dot(m.astype(BF16), wout_ref[...])
        yield
        h3 = _mod_norm(x2, n2_ref[...], mods_ref, 2).astype(BF16)
        x3 = x2 + 0.5 * _gate(mods_ref, 2) * (yield from _ffn_stages(h3, wgu_ref, wd_ref))
        y_ref[rows, :] = _rms(x3, nf_ref[...])

    half = x1_ref.shape[0] // 2
    _interleave([half_tile(slice(0, half)), half_tile(slice(half, 2 * half))])


def _post(x1, mods3, oa, ob, w, latent, tiles_per_batch):
    t = x1.shape[0]
    tm = POST_TILE
    if latent:
        mod_row = lambda i: (1 + i // tiles_per_batch, 0, 0)
    else:
        mod_row = lambda i: (0, 0, 0)
    row_blk = lambda i: (i, 0)
    weights = [w["nm"], w["win_g"], w["woa"], w["wob"], w["wout"], w["n2"], w["wgu2"], w["wd2"], w["nf"]]
    in_specs = [pl.BlockSpec((tm, D_MODEL), row_blk),
                pl.BlockSpec((None, 1, N_MOD * D_MODEL), mod_row),
                pl.BlockSpec((tm, A_Q_W), row_blk),
                pl.BlockSpec((tm, B_HEADS * B_V_DIM), row_blk)]
    in_specs += [_const_spec(a.shape) for a in weights]
    return pl.pallas_call(
        _post_kernel,
        grid=(t // tm,),
        in_specs=in_specs,
        out_specs=pl.BlockSpec((tm, D_MODEL), row_blk),
        out_shape=jax.ShapeDtypeStruct((t, D_MODEL), F32),
        compiler_params=pltpu.CompilerParams(dimension_semantics=("parallel",), vmem_limit_bytes=VMEM_LIMIT),
        name="post_latent" if latent else "post_context",
    )(x1, mods3, oa, ob, *weights)


def _rope_tables(n):
    f32 = np.float32
    rows = n // GRID_W
    t_row = np.repeat(np.arange(rows, dtype=f32), GRID_W)
    t_col = np.tile(np.arange(GRID_W, dtype=f32), rows)

    def angles(d_rot):
        d_half = d_rot // 2
        inv = (f32(1.0) / np.power(f32(ROPE_THETA), np.arange(0, d_half, 2, dtype=f32) / f32(d_half))).astype(f32)
        ar = t_row[:, None] * inv[None, :]
        ac = t_col[:, None] * inv[None, :]
        return np.concatenate([ar, ar, ac, ac], axis=-1).astype(f32)

    def signed(sin, d_rot):
        q = d_rot // 4
        sign = np.where((np.arange(d_rot) % (2 * q)) < q, f32(-1.0), f32(1.0)).astype(f32)
        return sin * sign[None, :]

    ang_a = angles(A_HEAD_DIM)
    cos_a = np.tile(np.cos(ang_a), (1, LANES // A_HEAD_DIM))
    sin_a = np.tile(signed(np.sin(ang_a), A_HEAD_DIM), (1, LANES // A_HEAD_DIM))
    ang_b = angles(B_ROPE_DIM)
    pad = ((0, 0), (B_NOPE_DIM, LANES - B_NOPE_DIM - B_ROPE_DIM))
    cos_b = np.pad(np.cos(ang_b), pad, constant_values=1.0)
    sin_b = np.pad(signed(np.sin(ang_b), B_ROPE_DIM), pad)
    return tuple(jnp.asarray(t, dtype=F32) for t in (cos_a, sin_a, cos_b, sin_b))


def _prep_weights(ffn1_norm, ffn1_w_gu, ffn1_w_down, mix_norm, w_in, q_lat_norm, kv_lat_norm, w_uq, w_ukv,
                  w_o_a, w_o_b, w_out, ffn2_norm, ffn2_w_gu, ffn2_w_down, final_norm):
    win_t = jnp.swapaxes(w_in[0], 0, 1)
    n_attn = A_Q_W + 2 * A_KV_W + Q_LORA + KV_LORA
    win_kr = jnp.pad(win_t[n_attn:n_attn + B_ROPE_DIM],
                     ((B_NOPE_DIM, LANES - B_NOPE_DIM - B_ROPE_DIM), (0, 0))).astype(BF16)
    n_gate = n_attn + B_ROPE_DIM
    whole = lambda wm: (wm, 0, wm.shape[0])

    wuq = jnp.pad(w_uq[0].reshape(Q_LORA, B_HEADS, B_QK_DIM),
                  ((0, 0), (0, 0), (0, LANES - B_QK_DIM))).reshape(Q_LORA, B_HEADS * LANES).astype(BF16)
    wukv = w_ukv[0].reshape(KV_LORA, B_HEADS, B_NOPE_DIM + B_V_DIM)
    wuk = jnp.pad(wukv[:, :, :B_NOPE_DIM],
                  ((0, 0), (0, 0), (0, LANES - B_NOPE_DIM))).reshape(KV_LORA, B_HEADS * LANES).astype(BF16)
    wuv = wukv[:, :, B_NOPE_DIM:].reshape(KV_LORA, B_HEADS * B_V_DIM).astype(BF16)

    return {
        "n1": ffn1_norm, "nm": mix_norm, "win_kr": win_kr,
        "qn": q_lat_norm, "kvn": kv_lat_norm, "wuq": wuq, "wuk": wuk, "wuv_t": wuv.T,
        "n2": ffn2_norm, "nf": final_norm.reshape(1, D_MODEL),
        "pre_f32": [whole(ffn1_w_gu[0]), whole(ffn1_w_down[0]), (win_t, 0, IN_A_W)],
        "post_f32": [(win_t, n_gate, 2 * D_MODEL), whole(w_o_a[0]), whole(w_o_b[0]), whole(w_out[0]),
                     whole(ffn2_w_gu[0]), whole(ffn2_w_down[0])],
    }


def kernel(x_prompt, x_sample, cache_attn_k, cache_attn_v, cache_mla_ckv, cache_mla_krope, c, c_ctx, ada_w, ada_b, ffn1_norm, ffn1_w_gu, ffn1_w_down, mix_norm, w_in, attn_sink, q_lat_norm, kv_lat_norm, w_uq, w_ukv, w_o_a, w_o_b, w_out, ffn2_norm, ffn2_w_gu, ffn2_w_down, final_norm):
    assert ada_w.shape[0] == 1, "single trunk layer"
    bp, sp, d = x_prompt.shape
    bs, ns, _ = x_sample.shape
    past = cache_attn_k.shape[2]
    assert d == D_MODEL and bs + 1 <= MOD_ROWS
    assert PRE_TILE_CTX % sp == 0 and (bp * sp) % PRE_TILE_CTX == 0 and (bp * sp) % POST_TILE == 0
    assert ns % PRE_TILE_LAT == 0 and ns % POST_TILE == 0
    assert ns % LAT_Q_TILE == 0 and LAT_Q_TILE % Q_BLOCK == 0 and ns >= BAND

    w = _prep_weights(ffn1_norm, ffn1_w_gu, ffn1_w_down, mix_norm, w_in, q_lat_norm, kv_lat_norm, w_uq, w_ukv,
                      w_o_a, w_o_b, w_out, ffn2_norm, ffn2_w_gu, ffn2_w_down, final_norm)
    sink = attn_sink[0]

    cvec = jnp.concatenate([c_ctx[None, :], c, jnp.zeros((MOD_ROWS - 1 - bs, d), F32)], axis=0)
    mods, (w["wgu1"], w["wd1"], w["win_a"]) = _ada_mods(cvec, ada_w[0], ada_b, w["pre_f32"])
    mods3 = mods.reshape(MOD_ROWS, 1, N_MOD * D_MODEL)

    xp = x_prompt.reshape(bp * sp, d)
    (x1p, qa, kap, qb, kb, vb_t, k_t, v_t, ckv_n, krope_t) = _pre(xp, mods3, w, False, sp, None, PRE_TILE_CTX)
    oa_p, ob_p = _att_ctx(sink, qa, kap, v_t, qb, kb, vb_t, sp)

    xs = x_sample.reshape(bs * ns, d)
    (x1s, qa, kap, vad, qb, kb, vb_t) = _pre(xs, mods3, w, True, ns, _rope_tables(ns), PRE_TILE_LAT)
    feat_major = lambda a: a[:, 0].transpose(0, 2, 3, 1).reshape(bs, A_KV_W, past)
    kbc, vbc, kapc, vadc = _cache_kv(
        cache_mla_ckv[:, 0], jnp.swapaxes(cache_mla_krope[:, 0], 1, 2),
        feat_major(cache_attn_k), feat_major(cache_attn_v), w["wuk"], w["wuv_t"])
    r3 = lambda a: a.reshape(bs, ns, a.shape[1])
    oa, ob, (w["win_g"], w["woa"], w["wob"], w["wout"], w["wgu2"], w["wd2"]) = _att_lat(
        sink, r3(qa), r3(kap), vad, kapc, vadc, r3(qb), r3(kb), vb_t, kbc, vbc, w["post_f32"])

    y_prompt = _post(x1p, mods3, oa_p, ob_p, w, False, 1).reshape(bp, sp, d)
    y_sample = _post(x1s, mods3, oa.reshape(bs * ns, A_Q_W), ob.reshape(bs * ns, B_HEADS * B_V_DIM),
                     w, True, ns // POST_TILE).reshape(bs, ns, d)

    new_attn_k = k_t.reshape(bp, 1, A_KV_HEADS, A_HEAD_DIM, sp).transpose(0, 1, 4, 2, 3)
    new_attn_v = v_t.reshape(bp, 1, A_KV_HEADS, A_HEAD_DIM, sp).transpose(0, 1, 4, 2, 3)
    new_mla_ckv = ckv_n.reshape(bp, 1, sp, KV_LORA)
    new_mla_krope = krope_t.reshape(bp, 1, B_ROPE_DIM, sp).transpose(0, 1, 3, 2)
    return (y_prompt, y_sample, new_attn_k, new_attn_v, new_mla_ckv, new_mla_krope)
```

```python
import functools
import math

import jax
import jax.numpy as jnp
import numpy as np
from jax import lax
from jax.experimental import pallas as pl
from jax.experimental.pallas import tpu as pltpu

F32 = jnp.float32
BF16 = jnp.bfloat16

D_MODEL = 1024
N_MOD = 9
GRID_W = 64
WINDOW = 128
A_HEADS = 8
A_KV_HEADS = 2
A_HEAD_DIM = 64
A_Q_W = A_HEADS * A_HEAD_DIM
A_KV_W = A_KV_HEADS * A_HEAD_DIM
B_HEADS = 8
B_NOPE_DIM = 64
B_ROPE_DIM = 32
B_V_DIM = 64
B_QK_DIM = B_NOPE_DIM + B_ROPE_DIM
Q_LORA = 256
KV_LORA = 256
D_FF = 2816
ROPE_THETA = 10000.0
EPS = 1e-6
NEG = -1e30
A_SCALE = A_HEAD_DIM ** -0.5
B_SCALE = B_QK_DIM ** -0.5
LOG2E = 1.4426950408889634

LANES = 128
HALF = LANES // 2
FF_CHUNK = 256
N_FF_CHUNKS = D_FF // FF_CHUNK
PRE_TILE_CTX = 512
PRE_TILE_LAT = 256
POST_TILE = 1024
POST_STREAM_ROWS = 256
ADA_TILE = 2304
Q_BLOCK = 128
BAND = 3 * Q_BLOCK
CTX_BATCHES = 4
CTX_GROUP = 4
BF16_TILE_ROWS = 16
SUM_ROWS = BF16_TILE_ROWS
LAT_Q_TILE = 512
MLA_KEY_BLOCK = 256
MOD_ROWS = 8
VMEM_LIMIT = 60 * 1024 * 1024

IN_A_W = A_Q_W + 2 * A_KV_W + Q_LORA + KV_LORA
OFF_KA = A_Q_W
OFF_VA = OFF_KA + A_KV_W
OFF_QLAT = OFF_VA + A_KV_W
OFF_CKV = OFF_QLAT + Q_LORA


def _dot(a, b):
    return jnp.dot(a, b, preferred_element_type=F32)


def _dot_nt(a, b):
    return lax.dot_general(a, b, (((1,), (1,)), ((), ())), preferred_element_type=F32)


def _rms(x, g):
    ms = jnp.mean(x * x, axis=-1, keepdims=True)
    return x * lax.rsqrt(ms + EPS) * g


def _mod_norm(x, g, mods_ref, k):
    shift = mods_ref[:, (3 * k) * D_MODEL:(3 * k + 1) * D_MODEL]
    scale = mods_ref[:, (3 * k + 1) * D_MODEL:(3 * k + 2) * D_MODEL]
    return _rms(x, g) * (1.0 + scale) + shift


def _gate(mods_ref, k):
    return mods_ref[:, (3 * k + 2) * D_MODEL:(3 * k + 3) * D_MODEL]


def _ffn_stages(h, wgu_ref, wd_ref, hooks=None):
    def gate_up(c):
        a = _dot(h, wgu_ref[:, c * FF_CHUNK:(c + 1) * FF_CHUNK])
        u = _dot(h, wgu_ref[:, D_FF + c * FF_CHUNK:D_FF + (c + 1) * FF_CHUNK])
        return a, u

    acc = None
    nxt = gate_up(0)
    yield
    for c in range(N_FF_CHUNKS):
        a, u = nxt
        if c + 1 < N_FF_CHUNKS:
            nxt = gate_up(c + 1)
        yield
        act = (a * jax.nn.sigmoid(a) * u).astype(BF16)
        d = _dot(act, wd_ref[c * FF_CHUNK:(c + 1) * FF_CHUNK, :])
        acc = d if acc is None else acc + d
        if hooks and c in hooks:
            hooks[c]()
    return acc


def _ffn(h, wgu_ref, wd_ref, hooks=None):
    stages = _ffn_stages(h, wgu_ref, wd_ref, hooks)
    while True:
        try:
            next(stages)
        except StopIteration as done:
            return done.value


def _interleave(streams, width=2):
    pending, live = list(streams), []
    while pending or live:
        if pending and len(live) < width:
            live.append(pending.pop(0))
        for s in list(live):
            try:
                next(s)
            except StopIteration:
                live.remove(s)


def _rope(x, cos, sin_signed, dist):
    lane = lax.broadcasted_iota(jnp.int32, x.shape, 1)
    first = (lane & (2 * dist - 1)) < dist
    partner = jnp.where(first, pltpu.roll(x, LANES - dist, axis=1), pltpu.roll(x, dist, axis=1))
    return x * cos + partner * sin_signed


def _store_gqa_k(k, low, kap_ref):
    k_sw = pltpu.roll(k, HALF, axis=1)
    zero = jnp.zeros_like(k)
    kap_ref[:, 0 * LANES:1 * LANES] = jnp.where(low, k, zero).astype(BF16)
    kap_ref[:, 1 * LANES:2 * LANES] = jnp.where(low, zero, k_sw).astype(BF16)
    kap_ref[:, 2 * LANES:3 * LANES] = jnp.where(low, k_sw, zero).astype(BF16)
    kap_ref[:, 3 * LANES:4 * LANES] = jnp.where(low, zero, k).astype(BF16)


def _store_gqa_v_blocks(v, vt_ref):
    for r in range(vt_ref.shape[0]):
        vt_ref[r] = v[r * LANES:(r + 1) * LANES, :].T.astype(BF16)


def _store_seq_minor(out_ref, val):
    nb, feat, seq = out_ref.shape
    for bi in range(nb):
        out_ref[bi] = val[bi * seq:(bi + 1) * seq, :].T[:feat, :]


def _low_lanes(rows):
    return lax.broadcasted_iota(jnp.int32, (rows, LANES), 1) < HALF


def _cast_windows(windows, steps, step_of):
    in_specs, args, out_specs, out_shapes, groups = [], [], [], [], []
    for wm, row0, rows in windows:
        cols = wm.shape[1]
        assert rows % (steps * BF16_TILE_ROWS) == 0, "row blocks must be whole packed-bf16 sublane tiles"
        per_step = rows // steps
        sub = math.gcd(per_step, row0) if row0 else per_step
        assert sub % BF16_TILE_ROWS == 0
        k = per_step // sub
        for j in range(k):
            in_specs.append(pl.BlockSpec(
                (sub, cols), lambda *g, j=j, k=k, base=row0 // sub: (base + step_of(*g) * k + j, 0)))
            args.append(wm)
        out_specs.append(pl.BlockSpec((per_step, cols), lambda *g: (step_of(*g), 0)))
        out_shapes.append(jax.ShapeDtypeStruct((rows, cols), BF16))
        groups.append(k)
    return in_specs, args, out_specs, out_shapes, tuple(groups)


def _cast_blocks(w32_refs, w16_refs, groups):
    first = 0
    for dst, k in zip(w16_refs, groups):
        sub = dst.shape[0] // k
        for j in range(k):
            dst[j * sub:(j + 1) * sub, :] = w32_refs[first + j][...].astype(BF16)
        first += k


def _ada_kernel(cast_groups, c_ref, w_ref, b_ref, *rest):
    n_src = sum(cast_groups)
    _cast_blocks(rest[:n_src], rest[n_src + 1:], cast_groups)
    o_ref = rest[n_src]
    c = c_ref[...]
    s = (c * jax.nn.sigmoid(c)).astype(BF16)
    o_ref[...] = _dot(s, w_ref[...].astype(BF16)) + b_ref[...]


def _ada_mods(cvec, ada_w, ada_b, cast_weights):
    n = ada_w.shape[1]
    steps = n // ADA_TILE
    w_in_specs, w_args, w_out_specs, w_out_shapes, groups = _cast_windows(cast_weights, steps, lambda j: j)
    out = pl.pallas_call(
        functools.partial(_ada_kernel, groups),
        grid=(steps,),
        in_specs=[
            pl.BlockSpec((MOD_ROWS, D_MODEL), lambda j: (0, 0)),
            pl.BlockSpec((D_MODEL, ADA_TILE), lambda j: (0, j)),
            pl.BlockSpec((1, ADA_TILE), lambda j: (0, j)),
        ] + w_in_specs,
        out_specs=[pl.BlockSpec((MOD_ROWS, ADA_TILE), lambda j: (0, j))] + w_out_specs,
        out_shape=[jax.ShapeDtypeStruct((MOD_ROWS, n), F32)] + w_out_shapes,
        compiler_params=pltpu.CompilerParams(dimension_semantics=("parallel",), vmem_limit_bytes=VMEM_LIMIT),
        name="ada_mods",
    )(cvec, ada_w, ada_b, *w_args)
    return out[0], out[1:]


PRE_HOOK_IN_PROJ = 1
PRE_HOOK_UP_PROJ = 6


def _pre_kernel(latent, n_tiles, *refs):
    x1s_ref = refs[-1]
    if latent:
        (x_ref, mods_ref, modsp_ref, n1_ref, wgu_ref, wd_ref, nm_ref, win_ref, wkr_ref, qn_ref, kvn_ref, wuq_ref, wuk_ref,
         wuv_ref, cosa_ref, sina_ref, cosb_ref, sinb_ref,
         x1_ref, qa_ref, kap_ref, vad_ref, qb_ref, kb_ref, vb_ref) = refs[:-1]
    else:
        (x_ref, mods_ref, modsp_ref, n1_ref, wgu_ref, wd_ref, nm_ref, win_ref, wkr_ref, qn_ref, kvn_ref, wuq_ref, wuk_ref,
         wuv_ref,
         x1_ref, qa_ref, kap_ref, qb_ref, kb_ref, vb_ref,
         knat_ref, vnat_ref, ckvn_ref, krope_ref) = refs[:-1]
    i = pl.program_id(0)
    carry = {}

    def in_proj():
        h2 = _mod_norm(x1s_ref[...], nm_ref[...], modsp_ref, 1).astype(BF16)
        z = _dot_nt(h2, win_ref[...])
        low = _low_lanes(z.shape[0])

        for c in range(A_Q_W // LANES):
            q = z[:, c * LANES:(c + 1) * LANES]
            if latent:
                q = _rope(q, cosa_ref[...], sina_ref[...], A_HEAD_DIM // 4)
            qa_ref[:, c * LANES:(c + 1) * LANES] = (q * (A_SCALE * LOG2E)).astype(BF16)
        k = z[:, OFF_KA:OFF_KA + LANES]
        v = z[:, OFF_VA:OFF_VA + LANES]
        if latent:
            k = _rope(k, cosa_ref[...], sina_ref[...], A_HEAD_DIM // 4)
            _store_gqa_v_blocks(v, vad_ref)
        else:
            _store_seq_minor(knat_ref, k)
            _store_seq_minor(vnat_ref, v)
        _store_gqa_k(k, low, kap_ref)

        carry["q_lat"] = _rms(z[:, OFF_QLAT:OFF_QLAT + Q_LORA], qn_ref[...]).astype(BF16)
        ckv_n = _rms(z[:, OFF_CKV:OFF_CKV + KV_LORA], kvn_ref[...])
        krp = _dot_nt(h2, wkr_ref[...])
        if latent:
            krp = _rope(krp, cosb_ref[...], sinb_ref[...], B_ROPE_DIM // 4)
        else:
            ckvn_ref[...] = ckv_n
            _store_seq_minor(krope_ref, pltpu.roll(krp, HALF, axis=1))
        carry["ckv_b"] = ckv_n.astype(BF16)
        carry["krp"] = krp

    def up_proj():
        q_lat, ckv_b, krp = carry["q_lat"], carry["ckv_b"], carry["krp"]
        qb = _dot(q_lat, wuq_ref[...])
        kn = _dot(ckv_b, wuk_ref[...])
        vb_ref[...] = _dot_nt(wuv_ref[...], ckv_b).astype(BF16)
        for h in range(B_HEADS):
            qh = qb[:, h * LANES:(h + 1) * LANES]
            if latent:
                qh = _rope(qh, cosb_ref[...], sinb_ref[...], B_ROPE_DIM // 4)
            qb_ref[:, h * LANES:(h + 1) * LANES] = (qh * (B_SCALE * LOG2E)).astype(BF16)
            kb_ref[:, h * LANES:(h + 1) * LANES] = (kn[:, h * LANES:(h + 1) * LANES] + krp).astype(BF16)

    def ffn(hooks):
        x = x_ref[...]
        h1 = _mod_norm(x, n1_ref[...], mods_ref, 0).astype(BF16)
        x1 = x + 0.5 * _gate(mods_ref, 0) * _ffn(h1, wgu_ref, wd_ref, hooks)
        x1_ref[...] = x1
        x1s_ref[...] = x1

    @pl.when(i == 0)
    def _():
        ffn(None)

    @pl.when(jnp.logical_and(i > 0, i < n_tiles))
    def _():
        ffn({PRE_HOOK_IN_PROJ: in_proj, PRE_HOOK_UP_PROJ: up_proj})

    @pl.when(i == n_tiles)
    def _():
        in_proj()
        up_proj()


def _const_spec(shape):
    nd = len(shape)
    return pl.BlockSpec(shape, lambda i: (0,) * nd, pipeline_mode=pl.Buffered(1))


def _pre(x, mods3, w, latent, seq, rope, tm):
    t = x.shape[0]
    tiles_per_batch = max(seq // tm, 1)
    n_tiles = t // tm
    cur = lambda i: jnp.minimum(i, n_tiles - 1)
    prev = lambda i: jnp.maximum(i - 1, 0)
    mod_of = (lambda j: 1 + j // tiles_per_batch) if latent else (lambda j: 0)
    cur_rows = lambda i: (cur(i), 0)
    prev_rows = lambda i: (prev(i), 0)

    weights = [w["n1"], w["wgu1"], w["wd1"], w["nm"], w["win_a"], w["win_kr"], w["qn"], w["kvn"], w["wuq"], w["wuk"],
               w["wuv_t"]]
    in_specs = [pl.BlockSpec((tm, D_MODEL), cur_rows),
                pl.BlockSpec((None, 1, N_MOD * D_MODEL), lambda i: (mod_of(cur(i)), 0, 0)),
                pl.BlockSpec((None, 1, N_MOD * D_MODEL), lambda i: (mod_of(prev(i)), 0, 0))]
    in_specs += [_const_spec(a.shape) for a in weights]
    args = [x, mods3, mods3] + weights
    if latent:
        in_specs += [pl.BlockSpec((tm, LANES), lambda i: (prev(i) % tiles_per_batch, 0))] * 4
        args += list(rope)

    out_specs = [pl.BlockSpec((tm, D_MODEL), cur_rows)]
    out_shape = [jax.ShapeDtypeStruct((t, D_MODEL), F32)]
    for wd in (A_Q_W, 4 * LANES):
        out_specs.append(pl.BlockSpec((tm, wd), prev_rows))
        out_shape.append(jax.ShapeDtypeStruct((t, wd), BF16))
    if latent:
        out_specs.append(pl.BlockSpec((tm // LANES, A_KV_W, LANES), lambda i: (prev(i), 0, 0)))
        out_shape.append(jax.ShapeDtypeStruct((t // LANES, A_KV_W, LANES), BF16))
    for wd in (B_HEADS * LANES, B_HEADS * LANES):
        out_specs.append(pl.BlockSpec((tm, wd), prev_rows))
        out_shape.append(jax.ShapeDtypeStruct((t, wd), BF16))
    out_specs.append(pl.BlockSpec((B_HEADS * B_V_DIM, tm), lambda i: (0, prev(i))))
    out_shape.append(jax.ShapeDtypeStruct((B_HEADS * B_V_DIM, t), BF16))
    if not latent:
        nb = tm // seq
        for feat in (A_KV_W, A_KV_W):
            out_specs.append(pl.BlockSpec((nb, feat, seq), lambda i: (prev(i), 0, 0)))
            out_shape.append(jax.ShapeDtypeStruct((t // seq, feat, seq), F32))
        out_specs.append(pl.BlockSpec((tm, KV_LORA), prev_rows))
        out_shape.append(jax.ShapeDtypeStruct((t, KV_LORA), F32))
        out_specs.append(pl.BlockSpec((nb, B_ROPE_DIM, seq), lambda i: (prev(i), 0, 0)))
        out_shape.append(jax.ShapeDtypeStruct((t // seq, B_ROPE_DIM, seq), F32))

    return pl.pallas_call(
        functools.partial(_pre_kernel, latent, n_tiles),
        grid=(n_tiles + 1,),
        in_specs=in_specs,
        out_specs=out_specs,
        out_shape=out_shape,
        scratch_shapes=[pltpu.VMEM((tm, D_MODEL), F32)],
        compiler_params=pltpu.CompilerParams(dimension_semantics=("arbitrary",), vmem_limit_bytes=VMEM_LIMIT),
        name="pre_latent" if latent else "pre_context",
    )(*args)


def _cache_kernel(ckv_ref, kropet_ref, ckt_ref, cvt_ref, wuk_ref, wuv_ref, kb_ref, vb_ref, kap_ref, vat_ref):
    p = ckv_ref.shape[0]
    c = ckv_ref[...].astype(BF16)
    kn = _dot(c, wuk_ref[...])
    krp = jnp.concatenate([jnp.zeros((B_NOPE_DIM, p), F32), kropet_ref[...],
                           jnp.zeros((LANES - B_QK_DIM, p), F32)], axis=0).T
    for h in range(B_HEADS):
        kb_ref[:, h * LANES:(h + 1) * LANES] = (kn[:, h * LANES:(h + 1) * LANES] + krp).astype(BF16)
    vb_ref[...] = _dot_nt(wuv_ref[...], c).astype(BF16)
    _store_gqa_k(ckt_ref[...].T, _low_lanes(p), kap_ref)
    vat_ref[...] = cvt_ref[...].astype(BF16)


def _cache_kv(ckv, krope_t, ck_t, cv_t, wuk, wuv_t):
    b, p, _ = ckv.shape
    blk = lambda wd: pl.BlockSpec((None, p, wd), lambda i: (i, 0, 0))
    blk_t = lambda feat: pl.BlockSpec((None, feat, p), lambda i: (i, 0, 0))
    vdim = B_HEADS * B_V_DIM
    return pl.pallas_call(
        _cache_kernel,
        grid=(b,),
        in_specs=[blk(KV_LORA), blk_t(B_ROPE_DIM), blk_t(A_KV_W), blk_t(A_KV_W),
                  pl.BlockSpec(wuk.shape, lambda i: (0, 0)),
                  pl.BlockSpec(wuv_t.shape, lambda i: (0, 0))],
        out_specs=[blk(B_HEADS * LANES), pl.BlockSpec((vdim, p), lambda i: (0, i)), blk(4 * LANES),
                   pl.BlockSpec((A_KV_W, p), lambda i: (0, i))],
        out_shape=[jax.ShapeDtypeStruct((b, p, B_HEADS * LANES), BF16),
                   jax.ShapeDtypeStruct((vdim, b * p), BF16),
                   jax.ShapeDtypeStruct((b, p, 4 * LANES), BF16),
                   jax.ShapeDtypeStruct((A_KV_W, b * p), BF16)],
        compiler_params=pltpu.CompilerParams(dimension_semantics=("parallel",), vmem_limit_bytes=VMEM_LIMIT),
        name="cache_kv",
    )(ckv, krope_t, ck_t, cv_t, wuk, wuv_t)


def _att_ctx_kernel(seq, sink_ref, qa_ref, kap_ref, vt_ref, qb_ref, kb_ref, vbt_ref, oa_ref, ob_ref):
    nb = qa_ref.shape[0] // seq
    ones = jnp.ones((SUM_ROWS, seq), BF16)
    upper = lax.broadcasted_iota(jnp.int32, (1, 2 * seq), 1) >= seq
    pending = {}

    def gqa_unit(bi, kv, half):
        rows = slice(bi * seq, (bi + 1) * seq)
        u = 2 * kv + half

        def scores():
            q2 = jnp.concatenate([qa_ref[rows, (2 * kv) * LANES:(2 * kv + 1) * LANES],
                                  qa_ref[rows, (2 * kv + 1) * LANES:(2 * kv + 2) * LANES]], axis=0)
            return [_dot_nt(kap_ref[rows, u * LANES:(u + 1) * LANES], q2)]

        def sink():
            return jnp.where(upper, sink_ref[4 * kv + 2 + half], sink_ref[4 * kv + half]) * LOG2E

        def values(p_blocks):
            v = vt_ref[bi, kv * A_HEAD_DIM:(kv + 1) * A_HEAD_DIM, :].astype(BF16)
            return _dot(jnp.concatenate([v, ones], axis=0), p_blocks[0])

        def finish(o):
            if half == 0:
                pending[("a", bi, kv)] = o
                return
            first = pending.pop(("a", bi, kv))
            for pr in range(2):
                cols = slice(pr * seq, (pr + 1) * seq)
                pair = jnp.concatenate([first[:, cols], o[:, cols]], axis=0)
                c = 2 * kv + pr
                oa_ref[rows, c * LANES:(c + 1) * LANES] = pair.T.astype(BF16)

        return _Unit(scores, sink, values, finish)

    def mla_unit(bi, h):
        rows = slice(bi * seq, (bi + 1) * seq)
        lanes = slice(h * LANES, (h + 1) * LANES)

        def scores():
            return [_dot_nt(kb_ref[rows, lanes], qb_ref[rows, lanes])]

        def values(p_blocks):
            v = vbt_ref[h * B_V_DIM:(h + 1) * B_V_DIM, rows]
            return _dot(jnp.concatenate([v, ones], axis=0), p_blocks[0])

        def finish(o):
            if h % 2 == 0:
                pending[("b", bi, h)] = o
                return
            c = h // 2
            pair = jnp.concatenate([pending.pop(("b", bi, h - 1)), o], axis=0)
            ob_ref[rows, c * LANES:(c + 1) * LANES] = pair.T.astype(BF16)

        return _Unit(scores, lambda: None, values, finish)

    units = []
    for bi in range(nb):
        units += [gqa_unit(bi, kv, half) for kv in range(A_KV_HEADS) for half in range(2)]
        units += [mla_unit(bi, h) for h in range(B_HEADS)]
    assert len(units) % CTX_GROUP == 0
    _attention_pipeline([units[g:g + CTX_GROUP] for g in range(0, len(units), CTX_GROUP)], A_HEAD_DIM)


def _att_ctx(sink, qa, kap, v_t, qb, kb, vb_t, seq):
    t = qa.shape[0]
    rows = CTX_BATCHES * seq
    blk = lambda wd: pl.BlockSpec((rows, wd), lambda b: (b, 0))
    return pl.pallas_call(
        functools.partial(_att_ctx_kernel, seq),
        grid=(t // rows,),
        in_specs=[pl.BlockSpec(memory_space=pltpu.SMEM),
                  blk(qa.shape[1]), blk(kap.shape[1]),
                  pl.BlockSpec((CTX_BATCHES, A_KV_W, seq), lambda b: (b, 0, 0)),
                  blk(qb.shape[1]), blk(kb.shape[1]),
                  pl.BlockSpec((vb_t.shape[0], rows), lambda b: (0, b))],
        out_specs=[blk(A_Q_W), blk(B_HEADS * B_V_DIM)],
        out_shape=[jax.ShapeDtypeStruct((t, A_Q_W), BF16), jax.ShapeDtypeStruct((t, B_HEADS * B_V_DIM), BF16)],
        compiler_params=pltpu.CompilerParams(dimension_semantics=("parallel",), vmem_limit_bytes=VMEM_LIMIT),
        name="att_context",
    )(sink, qa, kap, v_t, qb, kb, vb_t)


class _Unit:
    def __init__(self, scores, sink, values, finish):
        self.scores, self.sink, self.values, self.finish = scores, sink, values, finish


def _attention_pipeline(groups, dv):
    n = len(groups)
    scores, probs = {}, {}
    for t in range(n + 2):
        if t < n:
            scores[t] = [unit.scores() for unit in groups[t]]
        if 0 <= t - 2 < n:
            for unit, (p_blocks, p_sink) in zip(groups[t - 2], probs.pop(t - 2)):
                acc = unit.values(p_blocks)
                den = acc[dv:dv + 1]
                unit.finish(acc[:dv] / (den if p_sink is None else den + p_sink))
        if 0 <= t - 1 < n:
            probs[t - 1] = [_probs_keys_major(s, unit.sink()) for unit, s in zip(groups[t - 1], scores.pop(t - 1))]


def _window_units(n_lat, tile, sink_ref, qa_ref, kap_ref, vat_ref, kapc_ref, vatc_ref, oa_ref):
    n_blocks = qa_ref.shape[0] // Q_BLOCK
    nq = 2 * Q_BLOCK
    past = kapc_ref.shape[0]
    upper = lax.broadcasted_iota(jnp.int32, (1, nq), 1) >= Q_BLOCK
    ones_band = jnp.ones((SUM_ROWS, BAND), BF16)
    ones_ctx = jnp.ones((SUM_ROWS, past), BF16)
    key = lax.broadcasted_iota(jnp.int32, (BAND, nq), 0)
    qry = lax.broadcasted_iota(jnp.int32, (BAND, nq), 1) & (Q_BLOCK - 1)
    key_minus_qry = key - qry
    pending = {}

    def make(qb, kv, half, start, valid, v_band):
        u = 2 * kv + half
        rows = slice(qb * Q_BLOCK, (qb + 1) * Q_BLOCK)
        feat = slice(kv * A_HEAD_DIM, (kv + 1) * A_HEAD_DIM)

        def scores():
            q2 = jnp.concatenate([qa_ref[rows, (2 * kv) * LANES:(2 * kv + 1) * LANES],
                                  qa_ref[rows, (2 * kv + 1) * LANES:(2 * kv + 2) * LANES]], axis=0)
            s_band = _dot_nt(kap_ref[pl.ds(start, BAND), u * LANES:(u + 1) * LANES], q2)
            s_ctx = _dot_nt(kapc_ref[:, u * LANES:(u + 1) * LANES], q2)
            return [jnp.where(valid, s_band, NEG), s_ctx]

        def sink():
            return jnp.where(upper, sink_ref[4 * kv + 2 + half], sink_ref[4 * kv + half]) * LOG2E

        def values(p_blocks):
            vb = jnp.concatenate([v_band[r][feat] for r in range(BAND // LANES)], axis=1)
            r_band = _dot(jnp.concatenate([vb, ones_band], axis=0), p_blocks[0])
            r_ctx = _dot(jnp.concatenate([vatc_ref[feat, :], ones_ctx], axis=0), p_blocks[1])
            return r_band + r_ctx

        def finish(o):
            if half == 0:
                pending[(qb, kv)] = o
                return
            first = pending.pop((qb, kv))
            for pr in range(2):
                cols = slice(pr * Q_BLOCK, (pr + 1) * Q_BLOCK)
                pair = jnp.concatenate([first[:, cols], o[:, cols]], axis=0)
                c = 2 * kv + pr
                oa_ref[rows, c * LANES:(c + 1) * LANES] = pair.T.astype(BF16)

        return _Unit(scores, sink, values, finish)

    units = []
    for qb in range(n_blocks):
        i = tile * n_blocks + qb
        start = pl.multiple_of(jnp.clip(i * Q_BLOCK - Q_BLOCK, 0, n_lat - BAND), Q_BLOCK)
        valid = jnp.abs(key_minus_qry + (start - i * Q_BLOCK)) <= WINDOW
        v_band = vat_ref[pl.ds(start // LANES, BAND // LANES)]
        units += [make(qb, kv, half, start, valid, v_band) for kv in range(A_KV_HEADS) for half in range(2)]
    return units


def _probs_keys_major(scores_t, sink=None):
    m = None
    for s in scores_t:
        mx = jnp.max(s, axis=0, keepdims=True)
        m = mx if m is None else jnp.maximum(m, mx)
    if sink is not None:
        m = jnp.maximum(m, sink)
    probs = [jnp.exp2(s - m).astype(BF16) for s in scores_t]
    return probs, (None if sink is None else jnp.exp2(sink - m))


def _mla_units(qb_ref, kbl_ref, vtl_ref, kbc_ref, vtc_ref, ob_ref):
    kb = MLA_KEY_BLOCK
    ones = jnp.ones((SUM_ROWS, kb), BF16)
    blocks = [(kbc_ref, vtc_ref, j) for j in range(kbc_ref.shape[0] // kb)]
    blocks += [(kbl_ref, vtl_ref, j) for j in range(kbl_ref.shape[0] // kb)]
    pending = {}

    def make(h):
        lanes = slice(h * LANES, (h + 1) * LANES)
        feat = slice(h * B_V_DIM, (h + 1) * B_V_DIM)

        def scores():
            return [_dot_nt(k_ref[j * kb:(j + 1) * kb, lanes], qb_ref[:, lanes]) for k_ref, _, j in blocks]

        def values(p_blocks):
            acc = None
            for (_, v_ref, j), p in zip(blocks, p_blocks):
                r = _dot(jnp.concatenate([v_ref[feat, j * kb:(j + 1) * kb], ones], axis=0), p)
                acc = r if acc is None else acc + r
            return acc

        def finish(o):
            if h % 2 == 0:
                pending[h] = o
                return
            c = h // 2
            pair = jnp.concatenate([pending.pop(h - 1), o], axis=0)
            ob_ref[:, c * LANES:(c + 1) * LANES] = pair.T.astype(BF16)

        return _Unit(scores, lambda: None, values, finish)

    return [make(h) for h in range(B_HEADS)]


def _att_lat_kernel(cast_groups, n_lat, sink_ref, qa_ref, kap_ref, vat_ref, kapc_ref, vatc_ref,
                    qb_ref, kbl_ref, vtl_ref, kbc_ref, vtc_ref, *rest):
    n_src = sum(cast_groups)
    oa_ref, ob_ref = rest[n_src:n_src + 2]
    _cast_blocks(rest[:n_src], rest[n_src + 2:], cast_groups)

    mla = _mla_units(qb_ref, kbl_ref, vtl_ref, kbc_ref, vtc_ref, ob_ref)
    win = _window_units(n_lat, pl.program_id(1), sink_ref, qa_ref, kap_ref, vat_ref, kapc_ref, vatc_ref, oa_ref)
    assert len(win) % len(mla) == 0
    per_head = len(win) // len(mla)
    groups = [[unit] + win[h * per_head:(h + 1) * per_head] for h, unit in enumerate(mla)]
    _attention_pipeline(groups, B_V_DIM)


def _att_lat(sink, qa, kap, vat, kapc, vatc, qb, kbl, vtl, kbc, vtc, cast_weights):
    b, n, _ = qb.shape
    past = kbc.shape[1]
    tq = LAT_Q_TILE
    vdim = B_HEADS * B_V_DIM
    q_tiles = n // tq
    steps = b * q_tiles
    w_in_specs, w_args, w_out_specs, w_out_shapes, groups = _cast_windows(
        cast_weights, steps, lambda bi, i: bi * q_tiles + i)
    q_blk = lambda wd: pl.BlockSpec((None, tq, wd), lambda bi, i: (bi, i, 0))
    rows_blk = lambda r, wd: pl.BlockSpec((None, r, wd), lambda bi, i: (bi, 0, 0))
    feat_blk = lambda feat, r: pl.BlockSpec((feat, r), lambda bi, i: (0, bi))
    out = pl.pallas_call(
        functools.partial(_att_lat_kernel, groups, n),
        grid=(b, q_tiles),
        in_specs=[pl.BlockSpec(memory_space=pltpu.SMEM),
                  q_blk(A_Q_W), rows_blk(n, kap.shape[2]),
                  pl.BlockSpec((n // LANES, A_KV_W, LANES), lambda bi, i: (bi, 0, 0)),
                  rows_blk(past, kapc.shape[2]), feat_blk(A_KV_W, past),
                  q_blk(qb.shape[2]), rows_blk(n, kbl.shape[2]), feat_blk(vdim, n),
                  rows_blk(past, kbc.shape[2]), feat_blk(vdim, past)] + w_in_specs,
        out_specs=[q_blk(A_Q_W), q_blk(vdim)] + w_out_specs,
        out_shape=[jax.ShapeDtypeStruct((b, n, A_Q_W), BF16), jax.ShapeDtypeStruct((b, n, vdim), BF16)]
        + w_out_shapes,
        compiler_params=pltpu.CompilerParams(dimension_semantics=("parallel", "parallel"),
                                             vmem_limit_bytes=VMEM_LIMIT),
        name="att_latent",
    )(sink, qa, kap, vat, kapc, vatc, qb, kbl, vtl, kbc, vtc, *w_args)
    return out[0], out[1], out[2:]


def _post_kernel(x1_ref, mods_ref, oa_ref, ob_ref, nm_ref, wing_ref, woa_ref, wob_ref, wout_ref,
                 n2_ref, wgu_ref, wd_ref, nf_ref, y_ref):
    def half_tile(rows):
        x1 = x1_ref[rows, :]
        h2 = _mod_norm(x1, nm_ref[...], mods_ref, 1).astype(BF16)
        g = _dot_nt(h2, wing_ref[...])
        ma = _dot(oa_ref[rows, :], woa_ref[...])
        mb = _dot(ob_ref[rows, :], wob_ref[...])
        yield
        m = jax.nn.sigmoid(g[:, :D_MODEL]) * ma + jax.nn.sigmoid(g[:, D_MODEL:]) * mb
        x2 = x1 + _gate(mods_ref, 1) * _dot(m.astype(BF16), wout_ref[...])
        yield
        h3 = _mod_norm(x2, n2_ref[...], mods_ref, 2).astype(BF16)
        x3 = x2 + 0.5 * _gate(mods_ref, 2) * (yield from _ffn_stages(h3, wgu_ref, wd_ref))
        y_ref[rows, :] = _rms(x3, nf_ref[...])

    n = x1_ref.shape[0] // POST_STREAM_ROWS
    _interleave([half_tile(slice(r * POST_STREAM_ROWS, (r + 1) * POST_STREAM_ROWS)) for r in range(n)])


def _post(x1, mods3, oa, ob, w, latent, tiles_per_batch):
    t = x1.shape[0]
    tm = POST_TILE
    if latent:
        mod_row = lambda i: (1 + i // tiles_per_batch, 0, 0)
    else:
        mod_row = lambda i: (0, 0, 0)
    row_blk = lambda i: (i, 0)
    weights = [w["nm"], w["win_g"], w["woa"], w["wob"], w["wout"], w["n2"], w["wgu2"], w["wd2"], w["nf"]]
    in_specs = [pl.BlockSpec((tm, D_MODEL), row_blk),
                pl.BlockSpec((None, 1, N_MOD * D_MODEL), mod_row),
                pl.BlockSpec((tm, A_Q_W), row_blk),
                pl.BlockSpec((tm, B_HEADS * B_V_DIM), row_blk)]
    in_specs += [_const_spec(a.shape) for a in weights]
    return pl.pallas_call(
        _post_kernel,
        grid=(t // tm,),
        in_specs=in_specs,
        out_specs=pl.BlockSpec((tm, D_MODEL), row_blk),
        out_shape=jax.ShapeDtypeStruct((t, D_MODEL), F32),
        compiler_params=pltpu.CompilerParams(dimension_semantics=("parallel",), vmem_limit_bytes=VMEM_LIMIT),
        name="post_latent" if latent else "post_context",
    )(x1, mods3, oa, ob, *weights)


def _rope_tables(n):
    f32 = np.float32
    rows = n // GRID_W
    t_row = np.repeat(np.arange(rows, dtype=f32), GRID_W)
    t_col = np.tile(np.arange(GRID_W, dtype=f32), rows)

    def angles(d_rot):
        d_half = d_rot // 2
        inv = (f32(1.0) / np.power(f32(ROPE_THETA), np.arange(0, d_half, 2, dtype=f32) / f32(d_half))).astype(f32)
        ar = t_row[:, None] * inv[None, :]
        ac = t_col[:, None] * inv[None, :]
        return np.concatenate([ar, ar, ac, ac], axis=-1).astype(f32)

    def signed(sin, d_rot):
        q = d_rot // 4
        sign = np.where((np.arange(d_rot) % (2 * q)) < q, f32(-1.0), f32(1.0)).astype(f32)
        return sin * sign[None, :]

    ang_a = angles(A_HEAD_DIM)
    cos_a = np.tile(np.cos(ang_a), (1, LANES // A_HEAD_DIM))
    sin_a = np.tile(signed(np.sin(ang_a), A_HEAD_DIM), (1, LANES // A_HEAD_DIM))
    ang_b = angles(B_ROPE_DIM)
    pad = ((0, 0), (B_NOPE_DIM, LANES - B_NOPE_DIM - B_ROPE_DIM))
    cos_b = np.pad(np.cos(ang_b), pad, constant_values=1.0)
    sin_b = np.pad(signed(np.sin(ang_b), B_ROPE_DIM), pad)
    return tuple(jnp.asarray(t, dtype=F32) for t in (cos_a, sin_a, cos_b, sin_b))


def _prep_weights(ffn1_norm, ffn1_w_gu, ffn1_w_down, mix_norm, w_in, q_lat_norm, kv_lat_norm, w_uq, w_ukv,
                  w_o_a, w_o_b, w_out, ffn2_norm, ffn2_w_gu, ffn2_w_down, final_norm):
    win_t = jnp.swapaxes(w_in[0], 0, 1)
    n_attn = A_Q_W + 2 * A_KV_W + Q_LORA + KV_LORA
    win_kr = jnp.pad(win_t[n_attn:n_attn + B_ROPE_DIM],
                     ((B_NOPE_DIM, LANES - B_NOPE_DIM - B_ROPE_DIM), (0, 0))).astype(BF16)
    n_gate = n_attn + B_ROPE_DIM
    whole = lambda wm: (wm, 0, wm.shape[0])

    wuq = jnp.pad(w_uq[0].reshape(Q_LORA, B_HEADS, B_QK_DIM),
                  ((0, 0), (0, 0), (0, LANES - B_QK_DIM))).reshape(Q_LORA, B_HEADS * LANES).astype(BF16)
    wukv = w_ukv[0].reshape(KV_LORA, B_HEADS, B_NOPE_DIM + B_V_DIM)
    wuk = jnp.pad(wukv[:, :, :B_NOPE_DIM],
                  ((0, 0), (0, 0), (0, LANES - B_NOPE_DIM))).reshape(KV_LORA, B_HEADS * LANES).astype(BF16)
    wuv = wukv[:, :, B_NOPE_DIM:].reshape(KV_LORA, B_HEADS * B_V_DIM).astype(BF16)

    return {
        "n1": ffn1_norm, "nm": mix_norm, "win_kr": win_kr,
        "qn": q_lat_norm, "kvn": kv_lat_norm, "wuq": wuq, "wuk": wuk, "wuv_t": wuv.T,
        "n2": ffn2_norm, "nf": final_norm.reshape(1, D_MODEL),
        "pre_f32": [whole(ffn1_w_gu[0]), whole(ffn1_w_down[0]), (win_t, 0, IN_A_W)],
        "post_f32": [(win_t, n_gate, 2 * D_MODEL), whole(w_o_a[0]), whole(w_o_b[0]), whole(w_out[0]),
                     whole(ffn2_w_gu[0]), whole(ffn2_w_down[0])],
    }


def kernel(x_prompt, x_sample, cache_attn_k, cache_attn_v, cache_mla_ckv, cache_mla_krope, c, c_ctx, ada_w, ada_b, ffn1_norm, ffn1_w_gu, ffn1_w_down, mix_norm, w_in, attn_sink, q_lat_norm, kv_lat_norm, w_uq, w_ukv, w_o_a, w_o_b, w_out, ffn2_norm, ffn2_w_gu, ffn2_w_down, final_norm):
    assert ada_w.shape[0] == 1, "single trunk layer"
    bp, sp, d = x_prompt.shape
    bs, ns, _ = x_sample.shape
    past = cache_attn_k.shape[2]
    assert d == D_MODEL and bs + 1 <= MOD_ROWS
    assert PRE_TILE_CTX % sp == 0 and (bp * sp) % PRE_TILE_CTX == 0 and (bp * sp) % POST_TILE == 0
    assert ns % PRE_TILE_LAT == 0 and ns % POST_TILE == 0
    assert ns % LAT_Q_TILE == 0 and LAT_Q_TILE % Q_BLOCK == 0 and ns >= BAND

    w = _prep_weights(ffn1_norm, ffn1_w_gu, ffn1_w_down, mix_norm, w_in, q_lat_norm, kv_lat_norm, w_uq, w_ukv,
                      w_o_a, w_o_b, w_out, ffn2_norm, ffn2_w_gu, ffn2_w_down, final_norm)
    sink = attn_sink[0]

    cvec = jnp.concatenate([c_ctx[None, :], c, jnp.zeros((MOD_ROWS - 1 - bs, d), F32)], axis=0)
    mods, (w["wgu1"], w["wd1"], w["win_a"]) = _ada_mods(cvec, ada_w[0], ada_b, w["pre_f32"])
    mods3 = mods.reshape(MOD_ROWS, 1, N_MOD * D_MODEL)

    xp = x_prompt.reshape(bp * sp, d)
    (x1p, qa, kap, qb, kb, vb_t, k_t, v_t, ckv_n, krope_t) = _pre(xp, mods3, w, False, sp, None, PRE_TILE_CTX)
    oa_p, ob_p = _att_ctx(sink, qa, kap, v_t, qb, kb, vb_t, sp)

    xs = x_sample.reshape(bs * ns, d)
    (x1s, qa, kap, vad, qb, kb, vb_t) = _pre(xs, mods3, w, True, ns, _rope_tables(ns), PRE_TILE_LAT)
    feat_major = lambda a: a[:, 0].transpose(0, 2, 3, 1).reshape(bs, A_KV_W, past)
    kbc, vbc, kapc, vadc = _cache_kv(
        cache_mla_ckv[:, 0], jnp.swapaxes(cache_mla_krope[:, 0], 1, 2),
        feat_major(cache_attn_k), feat_major(cache_attn_v), w["wuk"], w["wuv_t"])
    r3 = lambda a: a.reshape(bs, ns, a.shape[1])
    oa, ob, (w["win_g"], w["woa"], w["wob"], w["wout"], w["wgu2"], w["wd2"]) = _att_lat(
        sink, r3(qa), r3(kap), vad, kapc, vadc, r3(qb), r3(kb), vb_t, kbc, vbc, w["post_f32"])

    y_prompt = _post(x1p, mods3, oa_p, ob_p, w, False, 1).reshape(bp, sp, d)
    y_sample = _post(x1s, mods3, oa.reshape(bs * ns, A_Q_W), ob.reshape(bs * ns, B_HEADS * B_V_DIM),
                     w, True, ns // POST_TILE).reshape(bs, ns, d)

    new_attn_k = k_t.reshape(bp, 1, A_KV_HEADS, A_HEAD_DIM, sp).transpose(0, 1, 4, 2, 3)
    new_attn_v = v_t.reshape(bp, 1, A_KV_HEADS, A_HEAD_DIM, sp).transpose(0, 1, 4, 2, 3)
    new_mla_ckv = ckv_n.reshape(bp, 1, sp, KV_LORA)
    new_mla_krope = krope_t.reshape(bp, 1, B_ROPE_DIM, sp).transpose(0, 1, 3, 2)
    return (y_prompt, y_sample, new_attn_k, new_attn_v, new_mla_ckv, new_mla_krope)
```

```python
import functools
import math

import jax
import jax.numpy as jnp
import numpy as np
from jax import lax
from jax.experimental import pallas as pl
from jax.experimental.pallas import tpu as pltpu

F32 = jnp.float32
BF16 = jnp.bfloat16

D_MODEL = 1024
N_MOD = 9
GRID_W = 64
WINDOW = 128
A_HEADS = 8
A_KV_HEADS = 2
A_HEAD_DIM = 64
A_Q_W = A_HEADS * A_HEAD_DIM
A_KV_W = A_KV_HEADS * A_HEAD_DIM
B_HEADS = 8
B_NOPE_DIM = 64
B_ROPE_DIM = 32
B_V_DIM = 64
B_QK_DIM = B_NOPE_DIM + B_ROPE_DIM
Q_LORA = 256
KV_LORA = 256
D_FF = 2816
ROPE_THETA = 10000.0
EPS = 1e-6
NEG = -1e30
A_SCALE = A_HEAD_DIM ** -0.5
B_SCALE = B_QK_DIM ** -0.5
LOG2E = 1.4426950408889634

LANES = 128
HALF = LANES // 2
FF_CHUNK = 256
N_FF_CHUNKS = D_FF // FF_CHUNK
PRE_TILE_CTX = 512
PRE_TILE_LAT = 256
POST_TILE = 512
ADA_TILE = 2304
Q_BLOCK = 128
BAND = 3 * Q_BLOCK
CTX_BATCHES = 4
CTX_GROUP = 4
BF16_TILE_ROWS = 16
SUM_ROWS = BF16_TILE_ROWS
LAT_Q_TILE = 512
MLA_KEY_BLOCK = 256
MOD_ROWS = 8
VMEM_LIMIT = 56 * 1024 * 1024

IN_A_W = A_Q_W + 2 * A_KV_W + Q_LORA + KV_LORA
OFF_KA = A_Q_W
OFF_VA = OFF_KA + A_KV_W
OFF_QLAT = OFF_VA + A_KV_W
OFF_CKV = OFF_QLAT + Q_LORA


def _dot(a, b):
    return jnp.dot(a, b, preferred_element_type=F32)


def _dot_nt(a, b):
    return lax.dot_general(a, b, (((1,), (1,)), ((), ())), preferred_element_type=F32)


def _rms(x, g):
    ms = jnp.mean(x * x, axis=-1, keepdims=True)
    return x * lax.rsqrt(ms + EPS) * g


def _mod_norm(x, g, mods_ref, k):
    shift = mods_ref[:, (3 * k) * D_MODEL:(3 * k + 1) * D_MODEL]
    scale = mods_ref[:, (3 * k + 1) * D_MODEL:(3 * k + 2) * D_MODEL]
    return _rms(x, g) * (1.0 + scale) + shift


def _gate(mods_ref, k):
    return mods_ref[:, (3 * k + 2) * D_MODEL:(3 * k + 3) * D_MODEL]


def _ffn_stages(h, wgu_ref, wd_ref, hooks=None):
    def gate_up(c):
        a = _dot(h, wgu_ref[:, c * FF_CHUNK:(c + 1) * FF_CHUNK])
        u = _dot(h, wgu_ref[:, D_FF + c * FF_CHUNK:D_FF + (c + 1) * FF_CHUNK])
        return a, u

    acc = None
    nxt = gate_up(0)
    yield
    for c in range(N_FF_CHUNKS):
        a, u = nxt
        if c + 1 < N_FF_CHUNKS:
            nxt = gate_up(c + 1)
        yield
        act = (a * jax.nn.sigmoid(a) * u).astype(BF16)
        d = _dot(act, wd_ref[c * FF_CHUNK:(c + 1) * FF_CHUNK, :])
        acc = d if acc is None else acc + d
        if hooks and c in hooks:
            hooks[c]()
    return acc


def _ffn(h, wgu_ref, wd_ref, hooks=None):
    stages = _ffn_stages(h, wgu_ref, wd_ref, hooks)
    while True:
        try:
            next(stages)
        except StopIteration as done:
            return done.value


def _interleave(streams):
    live = list(streams)
    next(live[0])
    while live:
        for s in list(live):
            try:
                next(s)
            except StopIteration:
                live.remove(s)


def _rope(x, cos, sin_signed, dist):
    lane = lax.broadcasted_iota(jnp.int32, x.shape, 1)
    first = (lane & (2 * dist - 1)) < dist
    partner = jnp.where(first, pltpu.roll(x, LANES - dist, axis=1), pltpu.roll(x, dist, axis=1))
    return x * cos + partner * sin_signed


def _store_gqa_k(k, low, kap_ref):
    k_sw = pltpu.roll(k, HALF, axis=1)
    zero = jnp.zeros_like(k)
    kap_ref[:, 0 * LANES:1 * LANES] = jnp.where(low, k, zero).astype(BF16)
    kap_ref[:, 1 * LANES:2 * LANES] = jnp.where(low, zero, k_sw).astype(BF16)
    kap_ref[:, 2 * LANES:3 * LANES] = jnp.where(low, k_sw, zero).astype(BF16)
    kap_ref[:, 3 * LANES:4 * LANES] = jnp.where(low, zero, k).astype(BF16)


def _store_gqa_v_blocks(v, vt_ref):
    for r in range(vt_ref.shape[0]):
        vt_ref[r] = v[r * LANES:(r + 1) * LANES, :].T.astype(BF16)


def _store_seq_minor(out_ref, val):
    nb, feat, seq = out_ref.shape
    for bi in range(nb):
        out_ref[bi] = val[bi * seq:(bi + 1) * seq, :].T[:feat, :]


def _low_lanes(rows):
    return lax.broadcasted_iota(jnp.int32, (rows, LANES), 1) < HALF


def _cast_windows(windows, steps, step_of):
    in_specs, args, out_specs, out_shapes, groups = [], [], [], [], []
    for wm, row0, rows in windows:
        cols = wm.shape[1]
        assert rows % (steps * BF16_TILE_ROWS) == 0, "row blocks must be whole packed-bf16 sublane tiles"
        per_step = rows // steps
        sub = math.gcd(per_step, row0) if row0 else per_step
        assert sub % BF16_TILE_ROWS == 0
        k = per_step // sub
        for j in range(k):
            in_specs.append(pl.BlockSpec(
                (sub, cols), lambda *g, j=j, k=k, base=row0 // sub: (base + step_of(*g) * k + j, 0)))
            args.append(wm)
        out_specs.append(pl.BlockSpec((per_step, cols), lambda *g: (step_of(*g), 0)))
        out_shapes.append(jax.ShapeDtypeStruct((rows, cols), BF16))
        groups.append(k)
    return in_specs, args, out_specs, out_shapes, tuple(groups)


def _cast_blocks(w32_refs, w16_refs, groups):
    first = 0
    for dst, k in zip(w16_refs, groups):
        sub = dst.shape[0] // k
        for j in range(k):
            dst[j * sub:(j + 1) * sub, :] = w32_refs[first + j][...].astype(BF16)
        first += k


def _ada_kernel(cast_groups, c_ref, w_ref, b_ref, *rest):
    n_src = sum(cast_groups)
    _cast_blocks(rest[:n_src], rest[n_src + 1:], cast_groups)
    o_ref = rest[n_src]
    c = c_ref[...]
    s = (c * jax.nn.sigmoid(c)).astype(BF16)
    o_ref[...] = _dot(s, w_ref[...].astype(BF16)) + b_ref[...]


def _ada_mods(cvec, ada_w, ada_b, cast_weights):
    n = ada_w.shape[1]
    steps = n // ADA_TILE
    w_in_specs, w_args, w_out_specs, w_out_shapes, groups = _cast_windows(cast_weights, steps, lambda j: j)
    out = pl.pallas_call(
        functools.partial(_ada_kernel, groups),
        grid=(steps,),
        in_specs=[
            pl.BlockSpec((MOD_ROWS, D_MODEL), lambda j: (0, 0)),
            pl.BlockSpec((D_MODEL, ADA_TILE), lambda j: (0, j)),
            pl.BlockSpec((1, ADA_TILE), lambda j: (0, j)),
        ] + w_in_specs,
        out_specs=[pl.BlockSpec((MOD_ROWS, ADA_TILE), lambda j: (0, j))] + w_out_specs,
        out_shape=[jax.ShapeDtypeStruct((MOD_ROWS, n), F32)] + w_out_shapes,
        compiler_params=pltpu.CompilerParams(dimension_semantics=("parallel",), vmem_limit_bytes=VMEM_LIMIT),
        name="ada_mods",
    )(cvec, ada_w, ada_b, *w_args)
    return out[0], out[1:]


PRE_HOOK_IN_PROJ = 1
PRE_HOOK_UP_PROJ = 6


def _pre_kernel(latent, n_tiles, *refs):
    x1s_ref = refs[-1]
    if latent:
        (x_ref, mods_ref, modsp_ref, n1_ref, wgu_ref, wd_ref, nm_ref, win_ref, wkr_ref, qn_ref, kvn_ref, wuq_ref, wuk_ref,
         wuv_ref, cosa_ref, sina_ref, cosb_ref, sinb_ref,
         x1_ref, qa_ref, kap_ref, vad_ref, qb_ref, kb_ref, vb_ref) = refs[:-1]
    else:
        (x_ref, mods_ref, modsp_ref, n1_ref, wgu_ref, wd_ref, nm_ref, win_ref, wkr_ref, qn_ref, kvn_ref, wuq_ref, wuk_ref,
         wuv_ref,
         x1_ref, qa_ref, kap_ref, qb_ref, kb_ref, vb_ref,
         knat_ref, vnat_ref, ckvn_ref, krope_ref) = refs[:-1]
    i = pl.program_id(0)
    carry = {}

    def in_proj():
        h2 = _mod_norm(x1s_ref[...], nm_ref[...], modsp_ref, 1).astype(BF16)
        z = _dot_nt(h2, win_ref[...])
        low = _low_lanes(z.shape[0])

        for c in range(A_Q_W // LANES):
            q = z[:, c * LANES:(c + 1) * LANES]
            if latent:
                q = _rope(q, cosa_ref[...], sina_ref[...], A_HEAD_DIM // 4)
            qa_ref[:, c * LANES:(c + 1) * LANES] = (q * (A_SCALE * LOG2E)).astype(BF16)
        k = z[:, OFF_KA:OFF_KA + LANES]
        v = z[:, OFF_VA:OFF_VA + LANES]
        if latent:
            k = _rope(k, cosa_ref[...], sina_ref[...], A_HEAD_DIM // 4)
            _store_gqa_v_blocks(v, vad_ref)
        else:
            _store_seq_minor(knat_ref, k)
            _store_seq_minor(vnat_ref, v)
        _store_gqa_k(k, low, kap_ref)

        carry["q_lat"] = _rms(z[:, OFF_QLAT:OFF_QLAT + Q_LORA], qn_ref[...]).astype(BF16)
        ckv_n = _rms(z[:, OFF_CKV:OFF_CKV + KV_LORA], kvn_ref[...])
        krp = _dot_nt(h2, wkr_ref[...])
        if latent:
            krp = _rope(krp, cosb_ref[...], sinb_ref[...], B_ROPE_DIM // 4)
        else:
            ckvn_ref[...] = ckv_n
            _store_seq_minor(krope_ref, pltpu.roll(krp, HALF, axis=1))
        carry["ckv_b"] = ckv_n.astype(BF16)
        carry["krp"] = krp

    def up_proj():
        q_lat, ckv_b, krp = carry["q_lat"], carry["ckv_b"], carry["krp"]
        qb = _dot(q_lat, wuq_ref[...])
        kn = _dot(ckv_b, wuk_ref[...])
        vb_ref[...] = _dot_nt(wuv_ref[...], ckv_b).astype(BF16)
        for h in range(B_HEADS):
            qh = qb[:, h * LANES:(h + 1) * LANES]
            if latent:
                qh = _rope(qh, cosb_ref[...], sinb_ref[...], B_ROPE_DIM // 4)
            qb_ref[:, h * LANES:(h + 1) * LANES] = (qh * (B_SCALE * LOG2E)).astype(BF16)
            kb_ref[:, h * LANES:(h + 1) * LANES] = (kn[:, h * LANES:(h + 1) * LANES] + krp).astype(BF16)

    def ffn(hooks):
        x = x_ref[...]
        h1 = _mod_norm(x, n1_ref[...], mods_ref, 0).astype(BF16)
        x1 = x + 0.5 * _gate(mods_ref, 0) * _ffn(h1, wgu_ref, wd_ref, hooks)
        x1_ref[...] = x1
        x1s_ref[...] = x1

    @pl.when(i == 0)
    def _():
        ffn(None)

    @pl.when(jnp.logical_and(i > 0, i < n_tiles))
    def _():
        ffn({PRE_HOOK_IN_PROJ: in_proj, PRE_HOOK_UP_PROJ: up_proj})

    @pl.when(i == n_tiles)
    def _():
        in_proj()
        up_proj()


def _const_spec(shape):
    nd = len(shape)
    return pl.BlockSpec(shape, lambda i: (0,) * nd, pipeline_mode=pl.Buffered(1))


def _pre(x, mods3, w, latent, seq, rope, tm):
    t = x.shape[0]
    tiles_per_batch = max(seq // tm, 1)
    n_tiles = t // tm
    cur = lambda i: jnp.minimum(i, n_tiles - 1)
    prev = lambda i: jnp.maximum(i - 1, 0)
    mod_of = (lambda j: 1 + j // tiles_per_batch) if latent else (lambda j: 0)
    cur_rows = lambda i: (cur(i), 0)
    prev_rows = lambda i: (prev(i), 0)

    weights = [w["n1"], w["wgu1"], w["wd1"], w["nm"], w["win_a"], w["win_kr"], w["qn"], w["kvn"], w["wuq"], w["wuk"],
               w["wuv_t"]]
    in_specs = [pl.BlockSpec((tm, D_MODEL), cur_rows),
                pl.BlockSpec((None, 1, N_MOD * D_MODEL), lambda i: (mod_of(cur(i)), 0, 0)),
                pl.BlockSpec((None, 1, N_MOD * D_MODEL), lambda i: (mod_of(prev(i)), 0, 0))]
    in_specs += [_const_spec(a.shape) for a in weights]
    args = [x, mods3, mods3] + weights
    if latent:
        in_specs += [pl.BlockSpec((tm, LANES), lambda i: (prev(i) % tiles_per_batch, 0))] * 4
        args += list(rope)

    out_specs = [pl.BlockSpec((tm, D_MODEL), cur_rows)]
    out_shape = [jax.ShapeDtypeStruct((t, D_MODEL), F32)]
    for wd in (A_Q_W, 4 * LANES):
        out_specs.append(pl.BlockSpec((tm, wd), prev_rows))
        out_shape.append(jax.ShapeDtypeStruct((t, wd), BF16))
    if latent:
        out_specs.append(pl.BlockSpec((tm // LANES, A_KV_W, LANES), lambda i: (prev(i), 0, 0)))
        out_shape.append(jax.ShapeDtypeStruct((t // LANES, A_KV_W, LANES), BF16))
    for wd in (B_HEADS * LANES, B_HEADS * LANES):
        out_specs.append(pl.BlockSpec((tm, wd), prev_rows))
        out_shape.append(jax.ShapeDtypeStruct((t, wd), BF16))
    out_specs.append(pl.BlockSpec((B_HEADS * B_V_DIM, tm), lambda i: (0, prev(i))))
    out_shape.append(jax.ShapeDtypeStruct((B_HEADS * B_V_DIM, t), BF16))
    if not latent:
        nb = tm // seq
        for feat in (A_KV_W, A_KV_W):
            out_specs.append(pl.BlockSpec((nb, feat, seq), lambda i: (prev(i), 0, 0)))
            out_shape.append(jax.ShapeDtypeStruct((t // seq, feat, seq), F32))
        out_specs.append(pl.BlockSpec((tm, KV_LORA), prev_rows))
        out_shape.append(jax.ShapeDtypeStruct((t, KV_LORA), F32))
        out_specs.append(pl.BlockSpec((nb, B_ROPE_DIM, seq), lambda i: (prev(i), 0, 0)))
        out_shape.append(jax.ShapeDtypeStruct((t // seq, B_ROPE_DIM, seq), F32))

    return pl.pallas_call(
        functools.partial(_pre_kernel, latent, n_tiles),
        grid=(n_tiles + 1,),
        in_specs=in_specs,
        out_specs=out_specs,
        out_shape=out_shape,
        scratch_shapes=[pltpu.VMEM((tm, D_MODEL), F32)],
        compiler_params=pltpu.CompilerParams(dimension_semantics=("arbitrary",), vmem_limit_bytes=VMEM_LIMIT),
        name="pre_latent" if latent else "pre_context",
    )(*args)


def _cache_kernel(cast_groups, ckv_ref, kropet_ref, ckt_ref, cvt_ref, wuk_ref, wuv_ref, *rest):
    n_src = sum(cast_groups)
    _cast_blocks(rest[:n_src], rest[n_src + 4:], cast_groups)
    kb_ref, vb_ref, kap_ref, vat_ref = rest[n_src:n_src + 4]
    p = ckv_ref.shape[0]
    c = ckv_ref[...].astype(BF16)
    kn = _dot(c, wuk_ref[...])
    krp = jnp.concatenate([jnp.zeros((B_NOPE_DIM, p), F32), kropet_ref[...],
                           jnp.zeros((LANES - B_QK_DIM, p), F32)], axis=0).T
    for h in range(B_HEADS):
        kb_ref[:, h * LANES:(h + 1) * LANES] = (kn[:, h * LANES:(h + 1) * LANES] + krp).astype(BF16)
    vb_ref[...] = _dot_nt(wuv_ref[...], c).astype(BF16)
    _store_gqa_k(ckt_ref[...].T, _low_lanes(p), kap_ref)
    vat_ref[...] = cvt_ref[...].astype(BF16)


def _cache_kv(ckv, krope_t, ck_t, cv_t, wuk, wuv_t, cast_weights):
    b, p, _ = ckv.shape
    blk = lambda wd: pl.BlockSpec((None, p, wd), lambda i: (i, 0, 0))
    blk_t = lambda feat: pl.BlockSpec((None, feat, p), lambda i: (i, 0, 0))
    vdim = B_HEADS * B_V_DIM
    w_in_specs, w_args, w_out_specs, w_out_shapes, groups = _cast_windows(cast_weights, b, lambda i: i)
    out = pl.pallas_call(
        functools.partial(_cache_kernel, groups),
        grid=(b,),
        in_specs=[blk(KV_LORA), blk_t(B_ROPE_DIM), blk_t(A_KV_W), blk_t(A_KV_W),
                  pl.BlockSpec(wuk.shape, lambda i: (0, 0)),
                  pl.BlockSpec(wuv_t.shape, lambda i: (0, 0))] + w_in_specs,
        out_specs=[blk(B_HEADS * LANES), pl.BlockSpec((vdim, p), lambda i: (0, i)), blk(4 * LANES),
                   pl.BlockSpec((A_KV_W, p), lambda i: (0, i))] + w_out_specs,
        out_shape=[jax.ShapeDtypeStruct((b, p, B_HEADS * LANES), BF16),
                   jax.ShapeDtypeStruct((vdim, b * p), BF16),
                   jax.ShapeDtypeStruct((b, p, 4 * LANES), BF16),
                   jax.ShapeDtypeStruct((A_KV_W, b * p), BF16)] + w_out_shapes,
        compiler_params=pltpu.CompilerParams(dimension_semantics=("parallel",), vmem_limit_bytes=VMEM_LIMIT),
        name="cache_kv",
    )(ckv, krope_t, ck_t, cv_t, wuk, wuv_t, *w_args)
    return out[:4], out[4:]


def _att_ctx_kernel(seq, sink_ref, qa_ref, kap_ref, vt_ref, qb_ref, kb_ref, vbt_ref, oa_ref, ob_ref):
    nb = qa_ref.shape[0] // seq
    ones = jnp.ones((SUM_ROWS, seq), BF16)
    upper = lax.broadcasted_iota(jnp.int32, (1, 2 * seq), 1) >= seq
    pending = {}

    def gqa_unit(bi, kv, half):
        rows = slice(bi * seq, (bi + 1) * seq)
        u = 2 * kv + half

        def scores():
            q2 = jnp.concatenate([qa_ref[rows, (2 * kv) * LANES:(2 * kv + 1) * LANES],
                                  qa_ref[rows, (2 * kv + 1) * LANES:(2 * kv + 2) * LANES]], axis=0)
            return [_dot_nt(kap_ref[rows, u * LANES:(u + 1) * LANES], q2)]

        def sink():
            return jnp.where(upper, sink_ref[4 * kv + 2 + half], sink_ref[4 * kv + half]) * LOG2E

        def values(p_blocks):
            v = vt_ref[bi, kv * A_HEAD_DIM:(kv + 1) * A_HEAD_DIM, :].astype(BF16)
            return _dot(jnp.concatenate([v, ones], axis=0), p_blocks[0])

        def finish(o):
            if half == 0:
                pending[("a", bi, kv)] = o
                return
            first = pending.pop(("a", bi, kv))
            for pr in range(2):
                cols = slice(pr * seq, (pr + 1) * seq)
                pair = jnp.concatenate([first[:, cols], o[:, cols]], axis=0)
                c = 2 * kv + pr
                oa_ref[rows, c * LANES:(c + 1) * LANES] = pair.T.astype(BF16)

        return _Unit(scores, sink, values, finish)

    def mla_unit(bi, h):
        rows = slice(bi * seq, (bi + 1) * seq)
        lanes = slice(h * LANES, (h + 1) * LANES)

        def scores():
            return [_dot_nt(kb_ref[rows, lanes], qb_ref[rows, lanes])]

        def values(p_blocks):
            v = vbt_ref[h * B_V_DIM:(h + 1) * B_V_DIM, rows]
            return _dot(jnp.concatenate([v, ones], axis=0), p_blocks[0])

        def finish(o):
            if h % 2 == 0:
                pending[("b", bi, h)] = o
                return
            c = h // 2
            pair = jnp.concatenate([pending.pop(("b", bi, h - 1)), o], axis=0)
            ob_ref[rows, c * LANES:(c + 1) * LANES] = pair.T.astype(BF16)

        return _Unit(scores, lambda: None, values, finish)

    units = []
    for bi in range(nb):
        units += [gqa_unit(bi, kv, half) for kv in range(A_KV_HEADS) for half in range(2)]
        units += [mla_unit(bi, h) for h in range(B_HEADS)]
    assert len(units) % CTX_GROUP == 0
    _attention_pipeline([units[g:g + CTX_GROUP] for g in range(0, len(units), CTX_GROUP)], A_HEAD_DIM)


def _att_ctx(sink, qa, kap, v_t, qb, kb, vb_t, seq):
    t = qa.shape[0]
    rows = CTX_BATCHES * seq
    blk = lambda wd: pl.BlockSpec((rows, wd), lambda b: (b, 0))
    return pl.pallas_call(
        functools.partial(_att_ctx_kernel, seq),
        grid=(t // rows,),
        in_specs=[pl.BlockSpec(memory_space=pltpu.SMEM),
                  blk(qa.shape[1]), blk(kap.shape[1]),
                  pl.BlockSpec((CTX_BATCHES, A_KV_W, seq), lambda b: (b, 0, 0)),
                  blk(qb.shape[1]), blk(kb.shape[1]),
                  pl.BlockSpec((vb_t.shape[0], rows), lambda b: (0, b))],
        out_specs=[blk(A_Q_W), blk(B_HEADS * B_V_DIM)],
        out_shape=[jax.ShapeDtypeStruct((t, A_Q_W), BF16), jax.ShapeDtypeStruct((t, B_HEADS * B_V_DIM), BF16)],
        compiler_params=pltpu.CompilerParams(dimension_semantics=("parallel",), vmem_limit_bytes=VMEM_LIMIT),
        name="att_context",
    )(sink, qa, kap, v_t, qb, kb, vb_t)


class _Unit:
    def __init__(self, scores, sink, values, finish):
        self.scores, self.sink, self.values, self.finish = scores, sink, values, finish


def _attention_pipeline(groups, dv):
    n = len(groups)
    scores, probs = {}, {}
    for t in range(n + 2):
        if t < n:
            scores[t] = [unit.scores() for unit in groups[t]]
        if 0 <= t - 2 < n:
            for unit, (p_blocks, p_sink) in zip(groups[t - 2], probs.pop(t - 2)):
                acc = unit.values(p_blocks)
                den = acc[dv:dv + 1]
                unit.finish(acc[:dv] / (den if p_sink is None else den + p_sink))
        if 0 <= t - 1 < n:
            probs[t - 1] = [_probs_keys_major(s, unit.sink()) for unit, s in zip(groups[t - 1], scores.pop(t - 1))]


def _window_units(n_lat, tile, sink_ref, qa_ref, kap_ref, vat_ref, kapc_ref, vatc_ref, oa_ref):
    n_blocks = qa_ref.shape[0] // Q_BLOCK
    nq = 2 * Q_BLOCK
    past = kapc_ref.shape[0]
    upper = lax.broadcasted_iota(jnp.int32, (1, nq), 1) >= Q_BLOCK
    ones_band = jnp.ones((SUM_ROWS, BAND), BF16)
    ones_ctx = jnp.ones((SUM_ROWS, past), BF16)
    key = lax.broadcasted_iota(jnp.int32, (BAND, nq), 0)
    qry = lax.broadcasted_iota(jnp.int32, (BAND, nq), 1) & (Q_BLOCK - 1)
    key_minus_qry = key - qry
    pending = {}

    def make(qb, kv, half, start, valid, v_band):
        u = 2 * kv + half
        rows = slice(qb * Q_BLOCK, (qb + 1) * Q_BLOCK)
        feat = slice(kv * A_HEAD_DIM, (kv + 1) * A_HEAD_DIM)

        def scores():
            q2 = jnp.concatenate([qa_ref[rows, (2 * kv) * LANES:(2 * kv + 1) * LANES],
                                  qa_ref[rows, (2 * kv + 1) * LANES:(2 * kv + 2) * LANES]], axis=0)
            s_band = _dot_nt(kap_ref[pl.ds(start, BAND), u * LANES:(u + 1) * LANES], q2)
            s_ctx = _dot_nt(kapc_ref[:, u * LANES:(u + 1) * LANES], q2)
            return [jnp.where(valid, s_band, NEG), s_ctx]

        def sink():
            return jnp.where(upper, sink_ref[4 * kv + 2 + half], sink_ref[4 * kv + half]) * LOG2E

        def values(p_blocks):
            vb = jnp.concatenate([v_band[r][feat] for r in range(BAND // LANES)], axis=1)
            r_band = _dot(jnp.concatenate([vb, ones_band], axis=0), p_blocks[0])
            r_ctx = _dot(jnp.concatenate([vatc_ref[feat, :], ones_ctx], axis=0), p_blocks[1])
            return r_band + r_ctx

        def finish(o):
            if half == 0:
                pending[(qb, kv)] = o
                return
            first = pending.pop((qb, kv))
            for pr in range(2):
                cols = slice(pr * Q_BLOCK, (pr + 1) * Q_BLOCK)
                pair = jnp.concatenate([first[:, cols], o[:, cols]], axis=0)
                c = 2 * kv + pr
                oa_ref[rows, c * LANES:(c + 1) * LANES] = pair.T.astype(BF16)

        return _Unit(scores, sink, values, finish)

    units = []
    for qb in range(n_blocks):
        i = tile * n_blocks + qb
        start = pl.multiple_of(jnp.clip(i * Q_BLOCK - Q_BLOCK, 0, n_lat - BAND), Q_BLOCK)
        valid = jnp.abs(key_minus_qry + (start - i * Q_BLOCK)) <= WINDOW
        v_band = vat_ref[pl.ds(start // LANES, BAND // LANES)]
        units += [make(qb, kv, half, start, valid, v_band) for kv in range(A_KV_HEADS) for half in range(2)]
    return units


def _probs_keys_major(scores_t, sink=None):
    m = None
    for s in scores_t:
        mx = jnp.max(s, axis=0, keepdims=True)
        m = mx if m is None else jnp.maximum(m, mx)
    if sink is not None:
        m = jnp.maximum(m, sink)
    probs = [jnp.exp2(s - m).astype(BF16) for s in scores_t]
    return probs, (None if sink is None else jnp.exp2(sink - m))


def _mla_units(qb_ref, kbl_ref, vtl_ref, kbc_ref, vtc_ref, ob_ref):
    kb = MLA_KEY_BLOCK
    ones = jnp.ones((SUM_ROWS, kb), BF16)
    blocks = [(kbc_ref, vtc_ref, j) for j in range(kbc_ref.shape[0] // kb)]
    blocks += [(kbl_ref, vtl_ref, j) for j in range(kbl_ref.shape[0] // kb)]
    pending = {}

    def make(h):
        lanes = slice(h * LANES, (h + 1) * LANES)
        feat = slice(h * B_V_DIM, (h + 1) * B_V_DIM)

        def scores():
            return [_dot_nt(k_ref[j * kb:(j + 1) * kb, lanes], qb_ref[:, lanes]) for k_ref, _, j in blocks]

        def values(p_blocks):
            acc = None
            for (_, v_ref, j), p in zip(blocks, p_blocks):
                r = _dot(jnp.concatenate([v_ref[feat, j * kb:(j + 1) * kb], ones], axis=0), p)
                acc = r if acc is None else acc + r
            return acc

        def finish(o):
            if h % 2 == 0:
                pending[h] = o
                return
            c = h // 2
            pair = jnp.concatenate([pending.pop(h - 1), o], axis=0)
            ob_ref[:, c * LANES:(c + 1) * LANES] = pair.T.astype(BF16)

        return _Unit(scores, lambda: None, values, finish)

    return [make(h) for h in range(B_HEADS)]


def _att_lat_kernel(cast_groups, n_lat, sink_ref, qa_ref, kap_ref, vat_ref, kapc_ref, vatc_ref,
                    qb_ref, kbl_ref, vtl_ref, kbc_ref, vtc_ref, *rest):
    n_src = sum(cast_groups)
    oa_ref, ob_ref = rest[n_src:n_src + 2]
    _cast_blocks(rest[:n_src], rest[n_src + 2:], cast_groups)

    mla = _mla_units(qb_ref, kbl_ref, vtl_ref, kbc_ref, vtc_ref, ob_ref)
    win = _window_units(n_lat, pl.program_id(1), sink_ref, qa_ref, kap_ref, vat_ref, kapc_ref, vatc_ref, oa_ref)
    assert len(win) % len(mla) == 0
    per_head = len(win) // len(mla)
    groups = [[unit] + win[h * per_head:(h + 1) * per_head] for h, unit in enumerate(mla)]
    _attention_pipeline(groups, B_V_DIM)


def _att_lat(sink, qa, kap, vat, kapc, vatc, qb, kbl, vtl, kbc, vtc, cast_weights):
    b, n, _ = qb.shape
    past = kbc.shape[1]
    tq = LAT_Q_TILE
    vdim = B_HEADS * B_V_DIM
    q_tiles = n // tq
    steps = b * q_tiles
    w_in_specs, w_args, w_out_specs, w_out_shapes, groups = _cast_windows(
        cast_weights, steps, lambda bi, i: bi * q_tiles + i)
    q_blk = lambda wd: pl.BlockSpec((None, tq, wd), lambda bi, i: (bi, i, 0))
    rows_blk = lambda r, wd: pl.BlockSpec((None, r, wd), lambda bi, i: (bi, 0, 0))
    feat_blk = lambda feat, r: pl.BlockSpec((feat, r), lambda bi, i: (0, bi))
    out = pl.pallas_call(
        functools.partial(_att_lat_kernel, groups, n),
        grid=(b, q_tiles),
        in_specs=[pl.BlockSpec(memory_space=pltpu.SMEM),
                  q_blk(A_Q_W), rows_blk(n, kap.shape[2]),
                  pl.BlockSpec((n // LANES, A_KV_W, LANES), lambda bi, i: (bi, 0, 0)),
                  rows_blk(past, kapc.shape[2]), feat_blk(A_KV_W, past),
                  q_blk(qb.shape[2]), rows_blk(n, kbl.shape[2]), feat_blk(vdim, n),
                  rows_blk(past, kbc.shape[2]), feat_blk(vdim, past)] + w_in_specs,
        out_specs=[q_blk(A_Q_W), q_blk(vdim)] + w_out_specs,
        out_shape=[jax.ShapeDtypeStruct((b, n, A_Q_W), BF16), jax.ShapeDtypeStruct((b, n, vdim), BF16)]
        + w_out_shapes,
        compiler_params=pltpu.CompilerParams(dimension_semantics=("parallel", "parallel"),
                                             vmem_limit_bytes=VMEM_LIMIT),
        name="att_latent",
    )(sink, qa, kap, vat, kapc, vatc, qb, kbl, vtl, kbc, vtc, *w_args)
    return out[0], out[1], out[2:]


def _post_kernel(x1_ref, mods_ref, oa_ref, ob_ref, nm_ref, wing_ref, woa_ref, wob_ref, wout_ref,
                 n2_ref, wgu_ref, wd_ref, nf_ref, y_ref):
    def half_tile(rows):
        x1 = x1_ref[rows, :]
        h2 = _mod_norm(x1, nm_ref[...], mods_ref, 1).astype(BF16)
        g = _dot_nt(h2, wing_ref[...])
        ma = _dot(oa_ref[rows, :], woa_ref[...])
        mb = _dot(ob_ref[rows, :], wob_ref[...])
        yield
        m = jax.nn.sigmoid(g[:, :D_MODEL]) * ma + jax.nn.sigmoid(g[:, D_MODEL:]) * mb
        x2 = x1 + _gate(mods_ref, 1) * _dot(m.astype(BF16), wout_ref[...])
        yield
        h3 = _mod_norm(x2, n2_ref[...], mods_ref, 2).astype(BF16)
        x3 = x2 + 0.5 * _gate(mods_ref, 2) * (yield from _ffn_stages(h3, wgu_ref, wd_ref))
        y_ref[rows, :] = _rms(x3, nf_ref[...])

    half = x1_ref.shape[0] // 2
    _interleave([half_tile(slice(0, half)), half_tile(slice(half, 2 * half))])


def _post(x1, mods3, oa, ob, w, latent, tiles_per_batch):
    t = x1.shape[0]
    tm = POST_TILE
    if latent:
        mod_row = lambda i: (1 + i // tiles_per_batch, 0, 0)
    else:
        mod_row = lambda i: (0, 0, 0)
    row_blk = lambda i: (i, 0)
    weights = [w["nm"], w["win_g"], w["woa"], w["wob"], w["wout"], w["n2"], w["wgu2"], w["wd2"], w["nf"]]
    in_specs = [pl.BlockSpec((tm, D_MODEL), row_blk),
                pl.BlockSpec((None, 1, N_MOD * D_MODEL), mod_row),
                pl.BlockSpec((tm, A_Q_W), row_blk),
                pl.BlockSpec((tm, B_HEADS * B_V_DIM), row_blk)]
    in_specs += [_const_spec(a.shape) for a in weights]
    return pl.pallas_call(
        _post_kernel,
        grid=(t // tm,),
        in_specs=in_specs,
        out_specs=pl.BlockSpec((tm, D_MODEL), row_blk),
        out_shape=jax.ShapeDtypeStruct((t, D_MODEL), F32),
        compiler_params=pltpu.CompilerParams(dimension_semantics=("parallel",), vmem_limit_bytes=VMEM_LIMIT),
        name="post_latent" if latent else "post_context",
    )(x1, mods3, oa, ob, *weights)


def _rope_tables(n):
    f32 = np.float32
    rows = n // GRID_W
    t_row = np.repeat(np.arange(rows, dtype=f32), GRID_W)
    t_col = np.tile(np.arange(GRID_W, dtype=f32), rows)

    def angles(d_rot):
        d_half = d_rot // 2
        inv = (f32(1.0) / np.power(f32(ROPE_THETA), np.arange(0, d_half, 2, dtype=f32) / f32(d_half))).astype(f32)
        ar = t_row[:, None] * inv[None, :]
        ac = t_col[:, None] * inv[None, :]
        return np.concatenate([ar, ar, ac, ac], axis=-1).astype(f32)

    def signed(sin, d_rot):
        q = d_rot // 4
        sign = np.where((np.arange(d_rot) % (2 * q)) < q, f32(-1.0), f32(1.0)).astype(f32)
        return sin * sign[None, :]

    ang_a = angles(A_HEAD_DIM)
    cos_a = np.tile(np.cos(ang_a), (1, LANES // A_HEAD_DIM))
    sin_a = np.tile(signed(np.sin(ang_a), A_HEAD_DIM), (1, LANES // A_HEAD_DIM))
    ang_b = angles(B_ROPE_DIM)
    pad = ((0, 0), (B_NOPE_DIM, LANES - B_NOPE_DIM - B_ROPE_DIM))
    cos_b = np.pad(np.cos(ang_b), pad, constant_values=1.0)
    sin_b = np.pad(signed(np.sin(ang_b), B_ROPE_DIM), pad)
    return tuple(jnp.asarray(t, dtype=F32) for t in (cos_a, sin_a, cos_b, sin_b))


def _prep_weights(ffn1_norm, ffn1_w_gu, ffn1_w_down, mix_norm, w_in, q_lat_norm, kv_lat_norm, w_uq, w_ukv,
                  w_o_a, w_o_b, w_out, ffn2_norm, ffn2_w_gu, ffn2_w_down, final_norm):
    win_t = jnp.swapaxes(w_in[0], 0, 1)
    n_attn = A_Q_W + 2 * A_KV_W + Q_LORA + KV_LORA
    win_kr = jnp.pad(win_t[n_attn:n_attn + B_ROPE_DIM],
                     ((B_NOPE_DIM, LANES - B_NOPE_DIM - B_ROPE_DIM), (0, 0))).astype(BF16)
    n_gate = n_attn + B_ROPE_DIM
    whole = lambda wm: (wm, 0, wm.shape[0])

    wuq = jnp.pad(w_uq[0].reshape(Q_LORA, B_HEADS, B_QK_DIM),
                  ((0, 0), (0, 0), (0, LANES - B_QK_DIM))).reshape(Q_LORA, B_HEADS * LANES).astype(BF16)
    wukv = w_ukv[0].reshape(KV_LORA, B_HEADS, B_NOPE_DIM + B_V_DIM)
    wuk = jnp.pad(wukv[:, :, :B_NOPE_DIM],
                  ((0, 0), (0, 0), (0, LANES - B_NOPE_DIM))).reshape(KV_LORA, B_HEADS * LANES).astype(BF16)
    wuv = wukv[:, :, B_NOPE_DIM:].reshape(KV_LORA, B_HEADS * B_V_DIM).astype(BF16)

    return {
        "n1": ffn1_norm, "nm": mix_norm, "win_kr": win_kr,
        "qn": q_lat_norm, "kvn": kv_lat_norm, "wuq": wuq, "wuk": wuk, "wuv_t": wuv.T,
        "n2": ffn2_norm, "nf": final_norm.reshape(1, D_MODEL),
        "pre_f32": [whole(ffn1_w_gu[0]), (win_t, 0, IN_A_W)],
        "cache_f32": [whole(ffn1_w_down[0])],
        "post_f32": [(win_t, n_gate, 2 * D_MODEL), whole(w_o_a[0]), whole(w_o_b[0]), whole(w_out[0]),
                     whole(ffn2_w_gu[0]), whole(ffn2_w_down[0])],
    }


def kernel(x_prompt, x_sample, cache_attn_k, cache_attn_v, cache_mla_ckv, cache_mla_krope, c, c_ctx, ada_w, ada_b, ffn1_norm, ffn1_w_gu, ffn1_w_down, mix_norm, w_in, attn_sink, q_lat_norm, kv_lat_norm, w_uq, w_ukv, w_o_a, w_o_b, w_out, ffn2_norm, ffn2_w_gu, ffn2_w_down, final_norm):
    assert ada_w.shape[0] == 1, "single trunk layer"
    bp, sp, d = x_prompt.shape
    bs, ns, _ = x_sample.shape
    past = cache_attn_k.shape[2]
    assert d == D_MODEL and bs + 1 <= MOD_ROWS
    assert PRE_TILE_CTX % sp == 0 and (bp * sp) % PRE_TILE_CTX == 0 and (bp * sp) % POST_TILE == 0
    assert ns % PRE_TILE_LAT == 0 and ns % POST_TILE == 0
    assert ns % LAT_Q_TILE == 0 and LAT_Q_TILE % Q_BLOCK == 0 and ns >= BAND

    w = _prep_weights(ffn1_norm, ffn1_w_gu, ffn1_w_down, mix_norm, w_in, q_lat_norm, kv_lat_norm, w_uq, w_ukv,
                      w_o_a, w_o_b, w_out, ffn2_norm, ffn2_w_gu, ffn2_w_down, final_norm)
    sink = attn_sink[0]

    cvec = jnp.concatenate([c_ctx[None, :], c, jnp.zeros((MOD_ROWS - 1 - bs, d), F32)], axis=0)
    mods, (w["wgu1"], w["win_a"]) = _ada_mods(cvec, ada_w[0], ada_b, w["pre_f32"])
    mods3 = mods.reshape(MOD_ROWS, 1, N_MOD * D_MODEL)
    feat_major = lambda a: a[:, 0].transpose(0, 2, 3, 1).reshape(bs, A_KV_W, past)
    (kbc, vbc, kapc, vadc), (w["wd1"],) = _cache_kv(
        cache_mla_ckv[:, 0], jnp.swapaxes(cache_mla_krope[:, 0], 1, 2),
        feat_major(cache_attn_k), feat_major(cache_attn_v), w["wuk"], w["wuv_t"], w["cache_f32"])

    xp = x_prompt.reshape(bp * sp, d)
    (x1p, qa, kap, qb, kb, vb_t, k_t, v_t, ckv_n, krope_t) = _pre(xp, mods3, w, False, sp, None, PRE_TILE_CTX)
    oa_p, ob_p = _att_ctx(sink, qa, kap, v_t, qb, kb, vb_t, sp)

    xs = x_sample.reshape(bs * ns, d)
    (x1s, qa, kap, vad, qb, kb, vb_t) = _pre(xs, mods3, w, True, ns, _rope_tables(ns), PRE_TILE_LAT)
    r3 = lambda a: a.reshape(bs, ns, a.shape[1])
    oa, ob, (w["win_g"], w["woa"], w["wob"], w["wout"], w["wgu2"], w["wd2"]) = _att_lat(
        sink, r3(qa), r3(kap), vad, kapc, vadc, r3(qb), r3(kb), vb_t, kbc, vbc, w["post_f32"])

    y_prompt = _post(x1p, mods3, oa_p, ob_p, w, False, 1).reshape(bp, sp, d)
    y_sample = _post(x1s, mods3, oa.reshape(bs * ns, A_Q_W), ob.reshape(bs * ns, B_HEADS * B_V_DIM),
                     w, True, ns // POST_TILE).reshape(bs, ns, d)

    new_attn_k = k_t.reshape(bp, 1, A_KV_HEADS, A_HEAD_DIM, sp).transpose(0, 1, 4, 2, 3)
    new_attn_v = v_t.reshape(bp, 1, A_KV_HEADS, A_HEAD_DIM, sp).transpose(0, 1, 4, 2, 3)
    new_mla_ckv = ckv_n.reshape(bp, 1, sp, KV_LORA)
    new_mla_krope = krope_t.reshape(bp, 1, B_ROPE_DIM, sp).transpose(0, 1, 3, 2)
    return (y_prompt, y_sample, new_attn_k, new_attn_v, new_mla_ckv, new_mla_krope)
```

```python
import functools
import math

import jax
import jax.numpy as jnp
import numpy as np
from jax import lax
from jax.experimental import pallas as pl
from jax.experimental.pallas import tpu as pltpu

F32 = jnp.float32
BF16 = jnp.bfloat16

D_MODEL = 1024
N_MOD = 9
GRID_W = 64
WINDOW = 128
A_HEADS = 8
A_KV_HEADS = 2
A_HEAD_DIM = 64
A_Q_W = A_HEADS * A_HEAD_DIM
A_KV_W = A_KV_HEADS * A_HEAD_DIM
B_HEADS = 8
B_NOPE_DIM = 64
B_ROPE_DIM = 32
B_V_DIM = 64
B_QK_DIM = B_NOPE_DIM + B_ROPE_DIM
Q_LORA = 256
KV_LORA = 256
D_FF = 2816
ROPE_THETA = 10000.0
EPS = 1e-6
NEG = -1e30
A_SCALE = A_HEAD_DIM ** -0.5
B_SCALE = B_QK_DIM ** -0.5
LOG2E = 1.4426950408889634

LANES = 128
HALF = LANES // 2
FF_CHUNK = 256
N_FF_CHUNKS = D_FF // FF_CHUNK
PRE_TILE_CTX = 512
PRE_TILE_LAT = 256
POST_TILE = 512
ADA_TILE = 2304
Q_BLOCK = 128
BAND = 3 * Q_BLOCK
CTX_BATCHES = 4
CTX_GROUP = 4
BF16_TILE_ROWS = 16
SUM_ROWS = BF16_TILE_ROWS
LAT_Q_TILE = 512
MLA_KEY_BLOCK = 256
MOD_ROWS = 8
VMEM_LIMIT = 56 * 1024 * 1024

IN_A_W = A_Q_W + 2 * A_KV_W + Q_LORA + KV_LORA
OFF_KA = A_Q_W
OFF_VA = OFF_KA + A_KV_W
OFF_QLAT = OFF_VA + A_KV_W
OFF_CKV = OFF_QLAT + Q_LORA


def _dot(a, b):
    return jnp.dot(a, b, preferred_element_type=F32)


def _dot_nt(a, b):
    return lax.dot_general(a, b, (((1,), (1,)), ((), ())), preferred_element_type=F32)


def _rms(x, g):
    ms = jnp.mean(x * x, axis=-1, keepdims=True)
    return x * lax.rsqrt(ms + EPS) * g


def _mod_norm(x, g, mods_ref, k):
    shift = mods_ref[:, (3 * k) * D_MODEL:(3 * k + 1) * D_MODEL]
    scale = mods_ref[:, (3 * k + 1) * D_MODEL:(3 * k + 2) * D_MODEL]
    return _rms(x, g) * (1.0 + scale) + shift


def _gate(mods_ref, k):
    return mods_ref[:, (3 * k + 2) * D_MODEL:(3 * k + 3) * D_MODEL]


def _ffn_stages(h, wgu_ref, wd_ref, hooks=None):
    def gate_up(c):
        a = _dot(h, wgu_ref[:, c * FF_CHUNK:(c + 1) * FF_CHUNK])
        u = _dot(h, wgu_ref[:, D_FF + c * FF_CHUNK:D_FF + (c + 1) * FF_CHUNK])
        return a, u

    acc = None
    nxt = gate_up(0)
    yield
    for c in range(N_FF_CHUNKS):
        a, u = nxt
        if c + 1 < N_FF_CHUNKS:
            nxt = gate_up(c + 1)
        yield
        act = (a * jax.nn.sigmoid(a) * u).astype(BF16)
        d = _dot(act, wd_ref[c * FF_CHUNK:(c + 1) * FF_CHUNK, :])
        acc = d if acc is None else acc + d
        if hooks and c in hooks:
            hooks[c]()
    return acc


def _ffn(h, wgu_ref, wd_ref, hooks=None):
    stages = _ffn_stages(h, wgu_ref, wd_ref, hooks)
    while True:
        try:
            next(stages)
        except StopIteration as done:
            return done.value


def _interleave(streams):
    live = list(streams)
    next(live[0])
    while live:
        for s in list(live):
            try:
                next(s)
            except StopIteration:
                live.remove(s)


def _rope(x, cos, sin_signed, dist):
    lane = lax.broadcasted_iota(jnp.int32, x.shape, 1)
    first = (lane & (2 * dist - 1)) < dist
    partner = jnp.where(first, pltpu.roll(x, LANES - dist, axis=1), pltpu.roll(x, dist, axis=1))
    return x * cos + partner * sin_signed


def _store_gqa_k(k, low, kap_ref):
    k_sw = pltpu.roll(k, HALF, axis=1)
    zero = jnp.zeros_like(k)
    kap_ref[:, 0 * LANES:1 * LANES] = jnp.where(low, k, zero).astype(BF16)
    kap_ref[:, 1 * LANES:2 * LANES] = jnp.where(low, zero, k_sw).astype(BF16)
    kap_ref[:, 2 * LANES:3 * LANES] = jnp.where(low, k_sw, zero).astype(BF16)
    kap_ref[:, 3 * LANES:4 * LANES] = jnp.where(low, zero, k).astype(BF16)


def _store_gqa_v_blocks(v, vt_ref):
    for r in range(vt_ref.shape[0]):
        vt_ref[r] = v[r * LANES:(r + 1) * LANES, :].T.astype(BF16)


def _store_seq_minor(out_ref, val):
    nb, feat, seq = out_ref.shape
    for bi in range(nb):
        out_ref[bi] = val[bi * seq:(bi + 1) * seq, :].T[:feat, :]


def _low_lanes(rows):
    return lax.broadcasted_iota(jnp.int32, (rows, LANES), 1) < HALF


def _cast_windows(windows, steps, step_of):
    in_specs, args, out_specs, out_shapes, groups = [], [], [], [], []
    for wm, row0, rows in windows:
        cols = wm.shape[1]
        assert rows % (steps * BF16_TILE_ROWS) == 0, "row blocks must be whole packed-bf16 sublane tiles"
        per_step = rows // steps
        sub = math.gcd(per_step, row0) if row0 else per_step
        assert sub % BF16_TILE_ROWS == 0
        k = per_step // sub
        for j in range(k):
            in_specs.append(pl.BlockSpec(
                (sub, cols), lambda *g, j=j, k=k, base=row0 // sub: (base + step_of(*g) * k + j, 0)))
            args.append(wm)
        out_specs.append(pl.BlockSpec((per_step, cols), lambda *g: (step_of(*g), 0)))
        out_shapes.append(jax.ShapeDtypeStruct((rows, cols), BF16))
        groups.append(k)
    return in_specs, args, out_specs, out_shapes, tuple(groups)


def _cast_blocks(w32_refs, w16_refs, groups):
    first = 0
    for dst, k in zip(w16_refs, groups):
        sub = dst.shape[0] // k
        for j in range(k):
            dst[j * sub:(j + 1) * sub, :] = w32_refs[first + j][...].astype(BF16)
        first += k


def _ada_kernel(cast_groups, c_ref, w_ref, b_ref, *rest):
    n_src = sum(cast_groups)
    _cast_blocks(rest[:n_src], rest[n_src + 1:], cast_groups)
    o_ref = rest[n_src]
    c = c_ref[...]
    s = (c * jax.nn.sigmoid(c)).astype(BF16)
    mods = _dot(s, w_ref[...].astype(BF16)) + b_ref[...]
    for r in range(MOD_ROWS):
        o_ref[r] = mods[r:r + 1, :]


def _ada_mods(cvec, ada_w, ada_b, cast_weights):
    n = ada_w.shape[1]
    steps = n // ADA_TILE
    w_in_specs, w_args, w_out_specs, w_out_shapes, groups = _cast_windows(cast_weights, steps, lambda j: j)
    out = pl.pallas_call(
        functools.partial(_ada_kernel, groups),
        grid=(steps,),
        in_specs=[
            pl.BlockSpec((MOD_ROWS, D_MODEL), lambda j: (0, 0)),
            pl.BlockSpec((D_MODEL, ADA_TILE), lambda j: (0, j)),
            pl.BlockSpec((1, ADA_TILE), lambda j: (0, j)),
        ] + w_in_specs,
        out_specs=[pl.BlockSpec((MOD_ROWS, 1, ADA_TILE), lambda j: (0, 0, j))] + w_out_specs,
        out_shape=[jax.ShapeDtypeStruct((MOD_ROWS, 1, n), F32)] + w_out_shapes,
        compiler_params=pltpu.CompilerParams(dimension_semantics=("parallel",), vmem_limit_bytes=VMEM_LIMIT),
        name="ada_mods",
    )(cvec, ada_w, ada_b, *w_args)
    return out[0], out[1:]


PRE_HOOK_IN_PROJ = 1
PRE_HOOK_UP_PROJ = 6


def _pre_kernel(latent, n_tiles, *refs):
    x1s_ref = refs[-1]
    if latent:
        (x_ref, mods_ref, modsp_ref, n1_ref, wgu_ref, wd_ref, nm_ref, win_ref, wkr_ref, qn_ref, kvn_ref, wuq_ref, wuk_ref,
         wuv_ref, cosa_ref, sina_ref, cosb_ref, sinb_ref,
         x1_ref, qa_ref, kap_ref, vad_ref, qb_ref, kb_ref, vb_ref) = refs[:-1]
    else:
        (x_ref, mods_ref, modsp_ref, n1_ref, wgu_ref, wd_ref, nm_ref, win_ref, wkr_ref, qn_ref, kvn_ref, wuq_ref, wuk_ref,
         wuv_ref,
         x1_ref, qa_ref, kap_ref, qb_ref, kb_ref, vb_ref,
         knat_ref, vnat_ref, ckvn_ref, krope_ref) = refs[:-1]
    i = pl.program_id(0)
    carry = {}

    def in_proj():
        h2 = _mod_norm(x1s_ref[...], nm_ref[...], modsp_ref, 1).astype(BF16)
        z = _dot_nt(h2, win_ref[...])
        low = _low_lanes(z.shape[0])

        for c in range(A_Q_W // LANES):
            q = z[:, c * LANES:(c + 1) * LANES]
            if latent:
                q = _rope(q, cosa_ref[...], sina_ref[...], A_HEAD_DIM // 4)
            qa_ref[:, c * LANES:(c + 1) * LANES] = (q * (A_SCALE * LOG2E)).astype(BF16)
        k = z[:, OFF_KA:OFF_KA + LANES]
        v = z[:, OFF_VA:OFF_VA + LANES]
        if latent:
            k = _rope(k, cosa_ref[...], sina_ref[...], A_HEAD_DIM // 4)
            _store_gqa_v_blocks(v, vad_ref)
        else:
            _store_seq_minor(knat_ref, k)
            _store_seq_minor(vnat_ref, v)
        _store_gqa_k(k, low, kap_ref)

        carry["q_lat"] = _rms(z[:, OFF_QLAT:OFF_QLAT + Q_LORA], qn_ref[...]).astype(BF16)
        ckv_n = _rms(z[:, OFF_CKV:OFF_CKV + KV_LORA], kvn_ref[...])
        krp = _dot_nt(h2, wkr_ref[...])
        if latent:
            krp = _rope(krp, cosb_ref[...], sinb_ref[...], B_ROPE_DIM // 4)
        else:
            ckvn_ref[...] = ckv_n
            _store_seq_minor(krope_ref, pltpu.roll(krp, HALF, axis=1))
        carry["ckv_b"] = ckv_n.astype(BF16)
        carry["krp"] = krp

    def up_proj():
        q_lat, ckv_b, krp = carry["q_lat"], carry["ckv_b"], carry["krp"]
        qb = _dot(q_lat, wuq_ref[...])
        kn = _dot(ckv_b, wuk_ref[...])
        vb_ref[...] = _dot_nt(wuv_ref[...], ckv_b).astype(BF16)
        for h in range(B_HEADS):
            qh = qb[:, h * LANES:(h + 1) * LANES]
            if latent:
                qh = _rope(qh, cosb_ref[...], sinb_ref[...], B_ROPE_DIM // 4)
            qb_ref[:, h * LANES:(h + 1) * LANES] = (qh * (B_SCALE * LOG2E)).astype(BF16)
            kb_ref[:, h * LANES:(h + 1) * LANES] = (kn[:, h * LANES:(h + 1) * LANES] + krp).astype(BF16)

    def ffn(hooks):
        x = x_ref[...]
        h1 = _mod_norm(x, n1_ref[...], mods_ref, 0).astype(BF16)
        x1 = x + 0.5 * _gate(mods_ref, 0) * _ffn(h1, wgu_ref, wd_ref, hooks)
        x1_ref[...] = x1
        x1s_ref[...] = x1

    @pl.when(i == 0)
    def _():
        ffn(None)

    @pl.when(jnp.logical_and(i > 0, i < n_tiles))
    def _():
        ffn({PRE_HOOK_IN_PROJ: in_proj, PRE_HOOK_UP_PROJ: up_proj})

    @pl.when(i == n_tiles)
    def _():
        in_proj()
        up_proj()


def _const_spec(shape):
    nd = len(shape)
    return pl.BlockSpec(shape, lambda i: (0,) * nd, pipeline_mode=pl.Buffered(1))


def _pre(x, mods3, w, latent, seq, rope, tm):
    t = x.shape[0]
    tiles_per_batch = max(seq // tm, 1)
    n_tiles = t // tm
    cur = lambda i: jnp.minimum(i, n_tiles - 1)
    prev = lambda i: jnp.maximum(i - 1, 0)
    mod_of = (lambda j: 1 + j // tiles_per_batch) if latent else (lambda j: 0)
    cur_rows = lambda i: (cur(i), 0)
    prev_rows = lambda i: (prev(i), 0)

    weights = [w["n1"], w["wgu1"], w["wd1"], w["nm"], w["win_a"], w["win_kr"], w["qn"], w["kvn"], w["wuq"], w["wuk"],
               w["wuv_t"]]
    in_specs = [pl.BlockSpec((tm, D_MODEL), cur_rows),
                pl.BlockSpec((None, 1, N_MOD * D_MODEL), lambda i: (mod_of(cur(i)), 0, 0)),
                pl.BlockSpec((None, 1, N_MOD * D_MODEL), lambda i: (mod_of(prev(i)), 0, 0))]
    in_specs += [_const_spec(a.shape) for a in weights]
    args = [x, mods3, mods3] + weights
    if latent:
        in_specs += [pl.BlockSpec((tm, LANES), lambda i: (prev(i) % tiles_per_batch, 0))] * 4
        args += list(rope)

    out_specs = [pl.BlockSpec((tm, D_MODEL), cur_rows)]
    out_shape = [jax.ShapeDtypeStruct((t, D_MODEL), F32)]
    for wd in (A_Q_W, 4 * LANES):
        out_specs.append(pl.BlockSpec((tm, wd), prev_rows))
        out_shape.append(jax.ShapeDtypeStruct((t, wd), BF16))
    if latent:
        out_specs.append(pl.BlockSpec((tm // LANES, A_KV_W, LANES), lambda i: (prev(i), 0, 0)))
        out_shape.append(jax.ShapeDtypeStruct((t // LANES, A_KV_W, LANES), BF16))
    for wd in (B_HEADS * LANES, B_HEADS * LANES):
        out_specs.append(pl.BlockSpec((tm, wd), prev_rows))
        out_shape.append(jax.ShapeDtypeStruct((t, wd), BF16))
    out_specs.append(pl.BlockSpec((B_HEADS * B_V_DIM, tm), lambda i: (0, prev(i))))
    out_shape.append(jax.ShapeDtypeStruct((B_HEADS * B_V_DIM, t), BF16))
    if not latent:
        nb = tm // seq
        for feat in (A_KV_W, A_KV_W):
            out_specs.append(pl.BlockSpec((nb, feat, seq), lambda i: (prev(i), 0, 0)))
            out_shape.append(jax.ShapeDtypeStruct((t // seq, feat, seq), F32))
        out_specs.append(pl.BlockSpec((tm, KV_LORA), prev_rows))
        out_shape.append(jax.ShapeDtypeStruct((t, KV_LORA), F32))
        out_specs.append(pl.BlockSpec((nb, B_ROPE_DIM, seq), lambda i: (prev(i), 0, 0)))
        out_shape.append(jax.ShapeDtypeStruct((t // seq, B_ROPE_DIM, seq), F32))

    return pl.pallas_call(
        functools.partial(_pre_kernel, latent, n_tiles),
        grid=(n_tiles + 1,),
        in_specs=in_specs,
        out_specs=out_specs,
        out_shape=out_shape,
        scratch_shapes=[pltpu.VMEM((tm, D_MODEL), F32)],
        compiler_params=pltpu.CompilerParams(dimension_semantics=("arbitrary",), vmem_limit_bytes=VMEM_LIMIT),
        name="pre_latent" if latent else "pre_context",
    )(*args)


def _cache_kernel(ckv_ref, kropet_ref, ckt_ref, cvt_ref, wuk_ref, wuv_ref, kb_ref, vb_ref, kap_ref, vat_ref):
    p = ckv_ref.shape[0]
    c = ckv_ref[...].astype(BF16)
    kn = _dot(c, wuk_ref[...])
    krp = jnp.concatenate([jnp.zeros((B_NOPE_DIM, p), F32), kropet_ref[...],
                           jnp.zeros((LANES - B_QK_DIM, p), F32)], axis=0).T
    for h in range(B_HEADS):
        kb_ref[:, h * LANES:(h + 1) * LANES] = (kn[:, h * LANES:(h + 1) * LANES] + krp).astype(BF16)
    vb_ref[...] = _dot_nt(wuv_ref[...], c).astype(BF16)
    _store_gqa_k(ckt_ref[...].T, _low_lanes(p), kap_ref)
    vat_ref[...] = cvt_ref[...].astype(BF16)


def _cache_kv(ckv, krope_t, ck_t, cv_t, wuk, wuv_t):
    b, p, _ = ckv.shape
    blk = lambda wd: pl.BlockSpec((None, p, wd), lambda i: (i, 0, 0))
    blk_t = lambda feat: pl.BlockSpec((None, feat, p), lambda i: (i, 0, 0))
    vdim = B_HEADS * B_V_DIM
    return pl.pallas_call(
        _cache_kernel,
        grid=(b,),
        in_specs=[blk(KV_LORA), blk_t(B_ROPE_DIM), blk_t(A_KV_W), blk_t(A_KV_W),
                  pl.BlockSpec(wuk.shape, lambda i: (0, 0)),
                  pl.BlockSpec(wuv_t.shape, lambda i: (0, 0))],
        out_specs=[blk(B_HEADS * LANES), pl.BlockSpec((vdim, p), lambda i: (0, i)), blk(4 * LANES),
                   pl.BlockSpec((A_KV_W, p), lambda i: (0, i))],
        out_shape=[jax.ShapeDtypeStruct((b, p, B_HEADS * LANES), BF16),
                   jax.ShapeDtypeStruct((vdim, b * p), BF16),
                   jax.ShapeDtypeStruct((b, p, 4 * LANES), BF16),
                   jax.ShapeDtypeStruct((A_KV_W, b * p), BF16)],
        compiler_params=pltpu.CompilerParams(dimension_semantics=("parallel",), vmem_limit_bytes=VMEM_LIMIT),
        name="cache_kv",
    )(ckv, krope_t, ck_t, cv_t, wuk, wuv_t)


def _att_ctx_kernel(seq, sink_ref, qa_ref, kap_ref, vt_ref, qb_ref, kb_ref, vbt_ref, oa_ref, ob_ref):
    nb = qa_ref.shape[0] // seq
    ones = jnp.ones((SUM_ROWS, seq), BF16)
    upper = lax.broadcasted_iota(jnp.int32, (1, 2 * seq), 1) >= seq
    pending = {}

    def gqa_unit(bi, kv, half):
        rows = slice(bi * seq, (bi + 1) * seq)
        u = 2 * kv + half

        def scores():
            q2 = jnp.concatenate([qa_ref[rows, (2 * kv) * LANES:(2 * kv + 1) * LANES],
                                  qa_ref[rows, (2 * kv + 1) * LANES:(2 * kv + 2) * LANES]], axis=0)
            return [_dot_nt(kap_ref[rows, u * LANES:(u + 1) * LANES], q2)]

        def sink():
            return jnp.where(upper, sink_ref[4 * kv + 2 + half], sink_ref[4 * kv + half]) * LOG2E

        def values(p_blocks):
            v = vt_ref[bi, kv * A_HEAD_DIM:(kv + 1) * A_HEAD_DIM, :].astype(BF16)
            return _dot(jnp.concatenate([v, ones], axis=0), p_blocks[0])

        def finish(o):
            if half == 0:
                pending[("a", bi, kv)] = o
                return
            first = pending.pop(("a", bi, kv))
            for pr in range(2):
                cols = slice(pr * seq, (pr + 1) * seq)
                pair = jnp.concatenate([first[:, cols], o[:, cols]], axis=0)
                c = 2 * kv + pr
                oa_ref[rows, c * LANES:(c + 1) * LANES] = pair.T.astype(BF16)

        return _Unit(scores, sink, values, finish)

    def mla_unit(bi, h):
        rows = slice(bi * seq, (bi + 1) * seq)
        lanes = slice(h * LANES, (h + 1) * LANES)

        def scores():
            return [_dot_nt(kb_ref[rows, lanes], qb_ref[rows, lanes])]

        def values(p_blocks):
            v = vbt_ref[h * B_V_DIM:(h + 1) * B_V_DIM, rows]
            return _dot(jnp.concatenate([v, ones], axis=0), p_blocks[0])

        def finish(o):
            if h % 2 == 0:
                pending[("b", bi, h)] = o
                return
            c = h // 2
            pair = jnp.concatenate([pending.pop(("b", bi, h - 1)), o], axis=0)
            ob_ref[rows, c * LANES:(c + 1) * LANES] = pair.T.astype(BF16)

        return _Unit(scores, lambda: None, values, finish)

    units = []
    for bi in range(nb):
        units += [gqa_unit(bi, kv, half) for kv in range(A_KV_HEADS) for half in range(2)]
        units += [mla_unit(bi, h) for h in range(B_HEADS)]
    assert len(units) % CTX_GROUP == 0
    _attention_pipeline([units[g:g + CTX_GROUP] for g in range(0, len(units), CTX_GROUP)], A_HEAD_DIM)


def _att_ctx(sink, qa, kap, v_t, qb, kb, vb_t, seq):
    t = qa.shape[0]
    rows = CTX_BATCHES * seq
    blk = lambda wd: pl.BlockSpec((rows, wd), lambda b: (b, 0))
    return pl.pallas_call(
        functools.partial(_att_ctx_kernel, seq),
        grid=(t // rows,),
        in_specs=[pl.BlockSpec(memory_space=pltpu.SMEM),
                  blk(qa.shape[1]), blk(kap.shape[1]),
                  pl.BlockSpec((CTX_BATCHES, A_KV_W, seq), lambda b: (b, 0, 0)),
                  blk(qb.shape[1]), blk(kb.shape[1]),
                  pl.BlockSpec((vb_t.shape[0], rows), lambda b: (0, b))],
        out_specs=[blk(A_Q_W), blk(B_HEADS * B_V_DIM)],
        out_shape=[jax.ShapeDtypeStruct((t, A_Q_W), BF16), jax.ShapeDtypeStruct((t, B_HEADS * B_V_DIM), BF16)],
        compiler_params=pltpu.CompilerParams(dimension_semantics=("parallel",), vmem_limit_bytes=VMEM_LIMIT),
        name="att_context",
    )(sink, qa, kap, v_t, qb, kb, vb_t)


class _Unit:
    def __init__(self, scores, sink, values, finish):
        self.scores, self.sink, self.values, self.finish = scores, sink, values, finish


def _attention_pipeline(groups, dv):
    n = len(groups)
    scores, probs = {}, {}
    for t in range(n + 2):
        if t < n:
            scores[t] = [unit.scores() for unit in groups[t]]
        if 0 <= t - 2 < n:
            for unit, (p_blocks, p_sink) in zip(groups[t - 2], probs.pop(t - 2)):
                acc = unit.values(p_blocks)
                den = acc[dv:dv + 1]
                unit.finish(acc[:dv] / (den if p_sink is None else den + p_sink))
        if 0 <= t - 1 < n:
            probs[t - 1] = [_probs_keys_major(s, unit.sink()) for unit, s in zip(groups[t - 1], scores.pop(t - 1))]


def _window_units(n_lat, tile, sink_ref, qa_ref, kap_ref, vat_ref, kapc_ref, vatc_ref, oa_ref):
    n_blocks = qa_ref.shape[0] // Q_BLOCK
    nq = 2 * Q_BLOCK
    past = kapc_ref.shape[0]
    upper = lax.broadcasted_iota(jnp.int32, (1, nq), 1) >= Q_BLOCK
    ones_band = jnp.ones((SUM_ROWS, BAND), BF16)
    ones_ctx = jnp.ones((SUM_ROWS, past), BF16)
    key = lax.broadcasted_iota(jnp.int32, (BAND, nq), 0)
    qry = lax.broadcasted_iota(jnp.int32, (BAND, nq), 1) & (Q_BLOCK - 1)
    key_minus_qry = key - qry
    pending = {}

    def make(qb, kv, half, start, valid, v_band):
        u = 2 * kv + half
        rows = slice(qb * Q_BLOCK, (qb + 1) * Q_BLOCK)
        feat = slice(kv * A_HEAD_DIM, (kv + 1) * A_HEAD_DIM)

        def scores():
            q2 = jnp.concatenate([qa_ref[rows, (2 * kv) * LANES:(2 * kv + 1) * LANES],
                                  qa_ref[rows, (2 * kv + 1) * LANES:(2 * kv + 2) * LANES]], axis=0)
            s_band = _dot_nt(kap_ref[pl.ds(start, BAND), u * LANES:(u + 1) * LANES], q2)
            s_ctx = _dot_nt(kapc_ref[:, u * LANES:(u + 1) * LANES], q2)
            return [jnp.where(valid, s_band, NEG), s_ctx]

        def sink():
            return jnp.where(upper, sink_ref[4 * kv + 2 + half], sink_ref[4 * kv + half]) * LOG2E

        def values(p_blocks):
            vb = jnp.concatenate([v_band[r][feat] for r in range(BAND // LANES)], axis=1)
            r_band = _dot(jnp.concatenate([vb, ones_band], axis=0), p_blocks[0])
            r_ctx = _dot(jnp.concatenate([vatc_ref[feat, :], ones_ctx], axis=0), p_blocks[1])
            return r_band + r_ctx

        def finish(o):
            if half == 0:
                pending[(qb, kv)] = o
                return
            first = pending.pop((qb, kv))
            for pr in range(2):
                cols = slice(pr * Q_BLOCK, (pr + 1) * Q_BLOCK)
                pair = jnp.concatenate([first[:, cols], o[:, cols]], axis=0)
                c = 2 * kv + pr
                oa_ref[rows, c * LANES:(c + 1) * LANES] = pair.T.astype(BF16)

        return _Unit(scores, sink, values, finish)

    units = []
    for qb in range(n_blocks):
        i = tile * n_blocks + qb
        start = pl.multiple_of(jnp.clip(i * Q_BLOCK - Q_BLOCK, 0, n_lat - BAND), Q_BLOCK)
        valid = jnp.abs(key_minus_qry + (start - i * Q_BLOCK)) <= WINDOW
        v_band = vat_ref[pl.ds(start // LANES, BAND // LANES)]
        units += [make(qb, kv, half, start, valid, v_band) for kv in range(A_KV_HEADS) for half in range(2)]
    return units


def _probs_keys_major(scores_t, sink=None):
    m = None
    for s in scores_t:
        mx = jnp.max(s, axis=0, keepdims=True)
        m = mx if m is None else jnp.maximum(m, mx)
    if sink is not None:
        m = jnp.maximum(m, sink)
    probs = [jnp.exp2(s - m).astype(BF16) for s in scores_t]
    return probs, (None if sink is None else jnp.exp2(sink - m))


def _mla_units(qb_ref, kbl_ref, vtl_ref, kbc_ref, vtc_ref, ob_ref):
    kb = MLA_KEY_BLOCK
    ones = jnp.ones((SUM_ROWS, kb), BF16)
    blocks = [(kbc_ref, vtc_ref, j) for j in range(kbc_ref.shape[0] // kb)]
    blocks += [(kbl_ref, vtl_ref, j) for j in range(kbl_ref.shape[0] // kb)]
    pending = {}

    def make(h):
        lanes = slice(h * LANES, (h + 1) * LANES)
        feat = slice(h * B_V_DIM, (h + 1) * B_V_DIM)

        def scores():
            return [_dot_nt(k_ref[j * kb:(j + 1) * kb, lanes], qb_ref[:, lanes]) for k_ref, _, j in blocks]

        def values(p_blocks):
            acc = None
            for (_, v_ref, j), p in zip(blocks, p_blocks):
                r = _dot(jnp.concatenate([v_ref[feat, j * kb:(j + 1) * kb], ones], axis=0), p)
                acc = r if acc is None else acc + r
            return acc

        def finish(o):
            if h % 2 == 0:
                pending[h] = o
                return
            c = h // 2
            pair = jnp.concatenate([pending.pop(h - 1), o], axis=0)
            ob_ref[:, c * LANES:(c + 1) * LANES] = pair.T.astype(BF16)

        return _Unit(scores, lambda: None, values, finish)

    return [make(h) for h in range(B_HEADS)]


def _att_lat_kernel(cast_groups, n_lat, sink_ref, qa_ref, kap_ref, vat_ref, kapc_ref, vatc_ref,
                    qb_ref, kbl_ref, vtl_ref, kbc_ref, vtc_ref, *rest):
    n_src = sum(cast_groups)
    oa_ref, ob_ref = rest[n_src:n_src + 2]
    _cast_blocks(rest[:n_src], rest[n_src + 2:], cast_groups)

    mla = _mla_units(qb_ref, kbl_ref, vtl_ref, kbc_ref, vtc_ref, ob_ref)
    win = _window_units(n_lat, pl.program_id(1), sink_ref, qa_ref, kap_ref, vat_ref, kapc_ref, vatc_ref, oa_ref)
    assert len(win) % len(mla) == 0
    per_head = len(win) // len(mla)
    groups = [[unit] + win[h * per_head:(h + 1) * per_head] for h, unit in enumerate(mla)]
    _attention_pipeline(groups, B_V_DIM)


def _att_lat(sink, qa, kap, vat, kapc, vatc, qb, kbl, vtl, kbc, vtc, cast_weights):
    b, n, _ = qb.shape
    past = kbc.shape[1]
    tq = LAT_Q_TILE
    vdim = B_HEADS * B_V_DIM
    q_tiles = n // tq
    steps = b * q_tiles
    w_in_specs, w_args, w_out_specs, w_out_shapes, groups = _cast_windows(
        cast_weights, steps, lambda bi, i: bi * q_tiles + i)
    q_blk = lambda wd: pl.BlockSpec((None, tq, wd), lambda bi, i: (bi, i, 0))
    rows_blk = lambda r, wd: pl.BlockSpec((None, r, wd), lambda bi, i: (bi, 0, 0))
    feat_blk = lambda feat, r: pl.BlockSpec((feat, r), lambda bi, i: (0, bi))
    out = pl.pallas_call(
        functools.partial(_att_lat_kernel, groups, n),
        grid=(b, q_tiles),
        in_specs=[pl.BlockSpec(memory_space=pltpu.SMEM),
                  q_blk(A_Q_W), rows_blk(n, kap.shape[2]),
                  pl.BlockSpec((n // LANES, A_KV_W, LANES), lambda bi, i: (bi, 0, 0)),
                  rows_blk(past, kapc.shape[2]), feat_blk(A_KV_W, past),
                  q_blk(qb.shape[2]), rows_blk(n, kbl.shape[2]), feat_blk(vdim, n),
                  rows_blk(past, kbc.shape[2]), feat_blk(vdim, past)] + w_in_specs,
        out_specs=[q_blk(A_Q_W), q_blk(vdim)] + w_out_specs,
        out_shape=[jax.ShapeDtypeStruct((b, n, A_Q_W), BF16), jax.ShapeDtypeStruct((b, n, vdim), BF16)]
        + w_out_shapes,
        compiler_params=pltpu.CompilerParams(dimension_semantics=("parallel", "parallel"),
                                             vmem_limit_bytes=VMEM_LIMIT),
        name="att_latent",
    )(sink, qa, kap, vat, kapc, vatc, qb, kbl, vtl, kbc, vtc, *w_args)
    return out[0], out[1], out[2:]


def _post_kernel(x1_ref, mods_ref, oa_ref, ob_ref, nm_ref, wing_ref, woa_ref, wob_ref, wout_ref,
                 n2_ref, wgu_ref, wd_ref, nf_ref, y_ref):
    def half_tile(rows):
        x1 = x1_ref[rows, :]
        h2 = _mod_norm(x1, nm_ref[...], mods_ref, 1).astype(BF16)
        g = _dot_nt(h2, wing_ref[...])
        ma = _dot(oa_ref[rows, :], woa_ref[...])
        mb = _dot(ob_ref[rows, :], wob_ref[...])
        yield
        m = jax.nn.sigmoid(g[:, :D_MODEL]) * ma + jax.nn.sigmoid(g[:, D_MODEL:]) * mb
        x2 = x1 + _gate(mods_ref, 1) * _dot(m.astype(BF16), wout_ref[...])
        yield
        h3 = _mod_norm(x2, n2_ref[...], mods_ref, 2).astype(BF16)
        x3 = x2 + 0.5 * _gate(mods_ref, 2) * (yield from _ffn_stages(h3, wgu_ref, wd_ref))
        y_ref[rows, :] = _rms(x3, nf_ref[...])

    half = x1_ref.shape[0] // 2
    _interleave([half_tile(slice(0, half)), half_tile(slice(half, 2 * half))])


def _post(x1, mods3, oa, ob, w, latent, tiles_per_batch):
    t = x1.shape[0]
    tm = POST_TILE
    if latent:
        mod_row = lambda i: (1 + i // tiles_per_batch, 0, 0)
    else:
        mod_row = lambda i: (0, 0, 0)
    row_blk = lambda i: (i, 0)
    weights = [w["nm"], w["win_g"], w["woa"], w["wob"], w["wout"], w["n2"], w["wgu2"], w["wd2"], w["nf"]]
    in_specs = [pl.BlockSpec((tm, D_MODEL), row_blk),
                pl.BlockSpec((None, 1, N_MOD * D_MODEL), mod_row),
                pl.BlockSpec((tm, A_Q_W), row_blk),
                pl.BlockSpec((tm, B_HEADS * B_V_DIM), row_blk)]
    in_specs += [_const_spec(a.shape) for a in weights]
    return pl.pallas_call(
        _post_kernel,
        grid=(t // tm,),
        in_specs=in_specs,
        out_specs=pl.BlockSpec((tm, D_MODEL), row_blk),
        out_shape=jax.ShapeDtypeStruct((t, D_MODEL), F32),
        compiler_params=pltpu.CompilerParams(dimension_semantics=("parallel",), vmem_limit_bytes=VMEM_LIMIT),
        name="post_latent" if latent else "post_context",
    )(x1, mods3, oa, ob, *weights)


def _rope_tables(n):
    f32 = np.float32
    rows = n // GRID_W
    t_row = np.repeat(np.arange(rows, dtype=f32), GRID_W)
    t_col = np.tile(np.arange(GRID_W, dtype=f32), rows)

    def angles(d_rot):
        d_half = d_rot // 2
        inv = (f32(1.0) / np.power(f32(ROPE_THETA), np.arange(0, d_half, 2, dtype=f32) / f32(d_half))).astype(f32)
        ar = t_row[:, None] * inv[None, :]
        ac = t_col[:, None] * inv[None, :]
        return np.concatenate([ar, ar, ac, ac], axis=-1).astype(f32)

    def signed(sin, d_rot):
        q = d_rot // 4
        sign = np.where((np.arange(d_rot) % (2 * q)) < q, f32(-1.0), f32(1.0)).astype(f32)
        return sin * sign[None, :]

    ang_a = angles(A_HEAD_DIM)
    cos_a = np.tile(np.cos(ang_a), (1, LANES // A_HEAD_DIM))
    sin_a = np.tile(signed(np.sin(ang_a), A_HEAD_DIM), (1, LANES // A_HEAD_DIM))
    ang_b = angles(B_ROPE_DIM)
    pad = ((0, 0), (B_NOPE_DIM, LANES - B_NOPE_DIM - B_ROPE_DIM))
    cos_b = np.pad(np.cos(ang_b), pad, constant_values=1.0)
    sin_b = np.pad(signed(np.sin(ang_b), B_ROPE_DIM), pad)
    return tuple(jnp.asarray(t, dtype=F32) for t in (cos_a, sin_a, cos_b, sin_b))


def _prep_weights(ffn1_norm, ffn1_w_gu, ffn1_w_down, mix_norm, w_in, q_lat_norm, kv_lat_norm, w_uq, w_ukv,
                  w_o_a, w_o_b, w_out, ffn2_norm, ffn2_w_gu, ffn2_w_down, final_norm):
    win_t = jnp.swapaxes(w_in[0], 0, 1)
    n_attn = A_Q_W + 2 * A_KV_W + Q_LORA + KV_LORA
    win_kr = jnp.pad(win_t[n_attn:n_attn + B_ROPE_DIM],
                     ((B_NOPE_DIM, LANES - B_NOPE_DIM - B_ROPE_DIM), (0, 0))).astype(BF16)
    n_gate = n_attn + B_ROPE_DIM
    whole = lambda wm: (wm, 0, wm.shape[0])

    wuq = jnp.pad(w_uq[0].reshape(Q_LORA, B_HEADS, B_QK_DIM),
                  ((0, 0), (0, 0), (0, LANES - B_QK_DIM))).reshape(Q_LORA, B_HEADS * LANES).astype(BF16)
    wukv = w_ukv[0].reshape(KV_LORA, B_HEADS, B_NOPE_DIM + B_V_DIM)
    wuk = jnp.pad(wukv[:, :, :B_NOPE_DIM],
                  ((0, 0), (0, 0), (0, LANES - B_NOPE_DIM))).reshape(KV_LORA, B_HEADS * LANES).astype(BF16)
    wuv = wukv[:, :, B_NOPE_DIM:].reshape(KV_LORA, B_HEADS * B_V_DIM).astype(BF16)

    return {
        "n1": ffn1_norm, "nm": mix_norm, "win_kr": win_kr,
        "qn": q_lat_norm, "kvn": kv_lat_norm, "wuq": wuq, "wuk": wuk, "wuv_t": wuv.T,
        "n2": ffn2_norm, "nf": final_norm.reshape(1, D_MODEL),
        "pre_f32": [whole(ffn1_w_gu[0]), whole(ffn1_w_down[0]), (win_t, 0, IN_A_W)],
        "post_f32": [(win_t, n_gate, 2 * D_MODEL), whole(w_o_a[0]), whole(w_o_b[0]), whole(w_out[0]),
                     whole(ffn2_w_gu[0]), whole(ffn2_w_down[0])],
    }


def kernel(x_prompt, x_sample, cache_attn_k, cache_attn_v, cache_mla_ckv, cache_mla_krope, c, c_ctx, ada_w, ada_b, ffn1_norm, ffn1_w_gu, ffn1_w_down, mix_norm, w_in, attn_sink, q_lat_norm, kv_lat_norm, w_uq, w_ukv, w_o_a, w_o_b, w_out, ffn2_norm, ffn2_w_gu, ffn2_w_down, final_norm):
    assert ada_w.shape[0] == 1, "single trunk layer"
    bp, sp, d = x_prompt.shape
    bs, ns, _ = x_sample.shape
    past = cache_attn_k.shape[2]
    assert d == D_MODEL and bs + 1 <= MOD_ROWS
    assert PRE_TILE_CTX % sp == 0 and (bp * sp) % PRE_TILE_CTX == 0 and (bp * sp) % POST_TILE == 0
    assert ns % PRE_TILE_LAT == 0 and ns % POST_TILE == 0
    assert ns % LAT_Q_TILE == 0 and LAT_Q_TILE % Q_BLOCK == 0 and ns >= BAND

    w = _prep_weights(ffn1_norm, ffn1_w_gu, ffn1_w_down, mix_norm, w_in, q_lat_norm, kv_lat_norm, w_uq, w_ukv,
                      w_o_a, w_o_b, w_out, ffn2_norm, ffn2_w_gu, ffn2_w_down, final_norm)
    sink = attn_sink[0]

    cvec = jnp.concatenate([c_ctx[None, :], c, jnp.zeros((MOD_ROWS - 1 - bs, d), F32)], axis=0)
    mods3, (w["wgu1"], w["wd1"], w["win_a"]) = _ada_mods(cvec, ada_w[0], ada_b, w["pre_f32"])

    xp = x_prompt.reshape(bp * sp, d)
    (x1p, qa, kap, qb, kb, vb_t, k_t, v_t, ckv_n, krope_t) = _pre(xp, mods3, w, False, sp, None, PRE_TILE_CTX)
    oa_p, ob_p = _att_ctx(sink, qa, kap, v_t, qb, kb, vb_t, sp)

    xs = x_sample.reshape(bs * ns, d)
    (x1s, qa, kap, vad, qb, kb, vb_t) = _pre(xs, mods3, w, True, ns, _rope_tables(ns), PRE_TILE_LAT)
    feat_major = lambda a: a[:, 0].transpose(0, 2, 3, 1).reshape(bs, A_KV_W, past)
    kbc, vbc, kapc, vadc = _cache_kv(
        cache_mla_ckv[:, 0], jnp.swapaxes(cache_mla_krope[:, 0], 1, 2),
        feat_major(cache_attn_k), feat_major(cache_attn_v), w["wuk"], w["wuv_t"])
    r3 = lambda a: a.reshape(bs, ns, a.shape[1])
    oa, ob, (w["win_g"], w["woa"], w["wob"], w["wout"], w["wgu2"], w["wd2"]) = _att_lat(
        sink, r3(qa), r3(kap), vad, kapc, vadc, r3(qb), r3(kb), vb_t, kbc, vbc, w["post_f32"])

    y_prompt = _post(x1p, mods3, oa_p, ob_p, w, False, 1).reshape(bp, sp, d)
    y_sample = _post(x1s, mods3, oa.reshape(bs * ns, A_Q_W), ob.reshape(bs * ns, B_HEADS * B_V_DIM),
                     w, True, ns // POST_TILE).reshape(bs, ns, d)

    new_attn_k = k_t.reshape(bp, 1, A_KV_HEADS, A_HEAD_DIM, sp).transpose(0, 1, 4, 2, 3)
    new_attn_v = v_t.reshape(bp, 1, A_KV_HEADS, A_HEAD_DIM, sp).transpose(0, 1, 4, 2, 3)
    new_mla_ckv = ckv_n.reshape(bp, 1, sp, KV_LORA)
    new_mla_krope = krope_t.reshape(bp, 1, B_ROPE_DIM, sp).transpose(0, 1, 3, 2)
    return (y_prompt, y_sample, new_attn_k, new_attn_v, new_mla_ckv, new_mla_krope)
```

```python
import functools
import math

import jax
import jax.numpy as jnp
import numpy as np
from jax import lax
from jax.experimental import pallas as pl
from jax.experimental.pallas import tpu as pltpu

F32 = jnp.float32
BF16 = jnp.bfloat16

D_MODEL = 1024
N_MOD = 9
GRID_W = 64
WINDOW = 128
A_HEADS = 8
A_KV_HEADS = 2
A_HEAD_DIM = 64
A_Q_W = A_HEADS * A_HEAD_DIM
A_KV_W = A_KV_HEADS * A_HEAD_DIM
B_HEADS = 8
B_NOPE_DIM = 64
B_ROPE_DIM = 32
B_V_DIM = 64
B_QK_DIM = B_NOPE_DIM + B_ROPE_DIM
Q_LORA = 256
KV_LORA = 256
D_FF = 2816
ROPE_THETA = 10000.0
EPS = 1e-6
NEG = -1e30
A_SCALE = A_HEAD_DIM ** -0.5
B_SCALE = B_QK_DIM ** -0.5
LOG2E = 1.4426950408889634

LANES = 128
HALF = LANES // 2
FF_CHUNK = 256
N_FF_CHUNKS = D_FF // FF_CHUNK
PRE_TILE_CTX = 512
PRE_TILE_LAT = 256
POST_TILE = 512
ADA_TILE = 2304
Q_BLOCK = 128
BAND = 3 * Q_BLOCK
CTX_BATCHES = 4
CTX_GROUP = 4
BF16_TILE_ROWS = 16
SUM_ROWS = BF16_TILE_ROWS
LAT_Q_TILE = 512
MLA_KEY_BLOCK = 256
MOD_ROWS = 8
VMEM_LIMIT = 56 * 1024 * 1024

IN_A_W = A_Q_W + 2 * A_KV_W + Q_LORA + KV_LORA
OFF_KA = A_Q_W
OFF_VA = OFF_KA + A_KV_W
OFF_QLAT = OFF_VA + A_KV_W
OFF_CKV = OFF_QLAT + Q_LORA


def _dot(a, b):
    return jnp.dot(a, b, preferred_element_type=F32)


def _dot_nt(a, b):
    return lax.dot_general(a, b, (((1,), (1,)), ((), ())), preferred_element_type=F32)


def _rms(x, g):
    ms = jnp.mean(x * x, axis=-1, keepdims=True)
    return x * lax.rsqrt(ms + EPS) * g


def _mod_norm(x, g, mods_ref, k):
    shift = mods_ref[:, (3 * k) * D_MODEL:(3 * k + 1) * D_MODEL]
    scale = mods_ref[:, (3 * k + 1) * D_MODEL:(3 * k + 2) * D_MODEL]
    return _rms(x, g) * (1.0 + scale) + shift


def _gate(mods_ref, k):
    return mods_ref[:, (3 * k + 2) * D_MODEL:(3 * k + 3) * D_MODEL]


def _ffn_stages(h, wgu_ref, wd_ref, hooks=None):
    def gate_up(c):
        a = _dot(h, wgu_ref[:, c * FF_CHUNK:(c + 1) * FF_CHUNK])
        u = _dot(h, wgu_ref[:, D_FF + c * FF_CHUNK:D_FF + (c + 1) * FF_CHUNK])
        return a, u

    acc = None
    nxt = gate_up(0)
    yield
    for c in range(N_FF_CHUNKS):
        a, u = nxt
        if c + 1 < N_FF_CHUNKS:
            nxt = gate_up(c + 1)
        yield
        act = (a * jax.nn.sigmoid(a) * u).astype(BF16)
        d = _dot(act, wd_ref[c * FF_CHUNK:(c + 1) * FF_CHUNK, :])
        acc = d if acc is None else acc + d
        if hooks and c in hooks:
            hooks[c]()
    return acc


def _ffn(h, wgu_ref, wd_ref, hooks=None):
    stages = _ffn_stages(h, wgu_ref, wd_ref, hooks)
    while True:
        try:
            next(stages)
        except StopIteration as done:
            return done.value


def _interleave(streams):
    live = list(streams)
    while live:
        for s in list(live):
            try:
                next(s)
            except StopIteration:
                live.remove(s)


def _rope(x, cos, sin_signed, dist):
    lane = lax.broadcasted_iota(jnp.int32, x.shape, 1)
    first = (lane & (2 * dist - 1)) < dist
    partner = jnp.where(first, pltpu.roll(x, LANES - dist, axis=1), pltpu.roll(x, dist, axis=1))
    return x * cos + partner * sin_signed


def _store_gqa_k(k, low, kap_ref):
    k_sw = pltpu.roll(k, HALF, axis=1)
    zero = jnp.zeros_like(k)
    kap_ref[:, 0 * LANES:1 * LANES] = jnp.where(low, k, zero).astype(BF16)
    kap_ref[:, 1 * LANES:2 * LANES] = jnp.where(low, zero, k_sw).astype(BF16)
    kap_ref[:, 2 * LANES:3 * LANES] = jnp.where(low, k_sw, zero).astype(BF16)
    kap_ref[:, 3 * LANES:4 * LANES] = jnp.where(low, zero, k).astype(BF16)


def _store_gqa_v_blocks(v, vt_ref):
    for r in range(vt_ref.shape[0]):
        vt_ref[r] = v[r * LANES:(r + 1) * LANES, :].T.astype(BF16)


def _store_seq_minor(out_ref, val):
    nb, feat, seq = out_ref.shape
    for bi in range(nb):
        out_ref[bi] = val[bi * seq:(bi + 1) * seq, :].T[:feat, :]


def _low_lanes(rows):
    return lax.broadcasted_iota(jnp.int32, (rows, LANES), 1) < HALF


def _cast_windows(windows, steps, step_of):
    in_specs, args, out_specs, out_shapes, groups = [], [], [], [], []
    for wm, row0, rows in windows:
        cols = wm.shape[1]
        assert rows % (steps * BF16_TILE_ROWS) == 0, "row blocks must be whole packed-bf16 sublane tiles"
        per_step = rows // steps
        sub = math.gcd(per_step, row0) if row0 else per_step
        assert sub % BF16_TILE_ROWS == 0
        k = per_step // sub
        for j in range(k):
            in_specs.append(pl.BlockSpec(
                (sub, cols), lambda *g, j=j, k=k, base=row0 // sub: (base + step_of(*g) * k + j, 0)))
            args.append(wm)
        out_specs.append(pl.BlockSpec((per_step, cols), lambda *g: (step_of(*g), 0)))
        out_shapes.append(jax.ShapeDtypeStruct((rows, cols), BF16))
        groups.append(k)
    return in_specs, args, out_specs, out_shapes, tuple(groups)


def _cast_blocks(w32_refs, w16_refs, groups):
    first = 0
    for dst, k in zip(w16_refs, groups):
        sub = dst.shape[0] // k
        for j in range(k):
            dst[j * sub:(j + 1) * sub, :] = w32_refs[first + j][...].astype(BF16)
        first += k


def _ada_kernel(cast_groups, c_ref, w_ref, b_ref, *rest):
    n_src = sum(cast_groups)
    _cast_blocks(rest[:n_src], rest[n_src + 1:], cast_groups)
    o_ref = rest[n_src]
    c = c_ref[...]
    s = (c * jax.nn.sigmoid(c)).astype(BF16)
    mods = _dot(s, w_ref[...].astype(BF16)) + b_ref[...]
    for r in range(MOD_ROWS):
        o_ref[r] = mods[r:r + 1, :]


def _ada_mods(cvec, ada_w, ada_b, cast_weights):
    n = ada_w.shape[1]
    steps = n // ADA_TILE
    w_in_specs, w_args, w_out_specs, w_out_shapes, groups = _cast_windows(cast_weights, steps, lambda j: j)
    out = pl.pallas_call(
        functools.partial(_ada_kernel, groups),
        grid=(steps,),
        in_specs=[
            pl.BlockSpec((MOD_ROWS, D_MODEL), lambda j: (0, 0)),
            pl.BlockSpec((D_MODEL, ADA_TILE), lambda j: (0, j)),
            pl.BlockSpec((1, ADA_TILE), lambda j: (0, j)),
        ] + w_in_specs,
        out_specs=[pl.BlockSpec((MOD_ROWS, 1, ADA_TILE), lambda j: (0, 0, j))] + w_out_specs,
        out_shape=[jax.ShapeDtypeStruct((MOD_ROWS, 1, n), F32)] + w_out_shapes,
        compiler_params=pltpu.CompilerParams(dimension_semantics=("parallel",), vmem_limit_bytes=VMEM_LIMIT),
        name="ada_mods",
    )(cvec, ada_w, ada_b, *w_args)
    return out[0], out[1:]


PRE_HOOK_IN_PROJ = 1
PRE_HOOK_UP_PROJ = 6


def _pre_kernel(latent, n_tiles, *refs):
    x1s_ref = refs[-1]
    if latent:
        (x_ref, mods_ref, modsp_ref, n1_ref, wgu_ref, wd_ref, nm_ref, win_ref, wkr_ref, qn_ref, kvn_ref, wuq_ref, wuk_ref,
         wuv_ref, cosa_ref, sina_ref, cosb_ref, sinb_ref,
         x1_ref, qa_ref, kap_ref, vad_ref, qb_ref, kb_ref, vb_ref) = refs[:-1]
    else:
        (x_ref, mods_ref, modsp_ref, n1_ref, wgu_ref, wd_ref, nm_ref, win_ref, wkr_ref, qn_ref, kvn_ref, wuq_ref, wuk_ref,
         wuv_ref,
         x1_ref, qa_ref, kap_ref, qb_ref, kb_ref, vb_ref,
         knat_ref, vnat_ref, ckvn_ref, krope_ref) = refs[:-1]
    i = pl.program_id(0)
    carry = {}

    def in_proj():
        h2 = _mod_norm(x1s_ref[...], nm_ref[...], modsp_ref, 1).astype(BF16)
        z = _dot_nt(h2, win_ref[...])
        low = _low_lanes(z.shape[0])

        for c in range(A_Q_W // LANES):
            q = z[:, c * LANES:(c + 1) * LANES]
            if latent:
                q = _rope(q, cosa_ref[...], sina_ref[...], A_HEAD_DIM // 4)
            qa_ref[:, c * LANES:(c + 1) * LANES] = (q * (A_SCALE * LOG2E)).astype(BF16)
        k = z[:, OFF_KA:OFF_KA + LANES]
        v = z[:, OFF_VA:OFF_VA + LANES]
        if latent:
            k = _rope(k, cosa_ref[...], sina_ref[...], A_HEAD_DIM // 4)
            _store_gqa_v_blocks(v, vad_ref)
        else:
            _store_seq_minor(knat_ref, k)
            _store_seq_minor(vnat_ref, v)
        _store_gqa_k(k, low, kap_ref)

        carry["q_lat"] = _rms(z[:, OFF_QLAT:OFF_QLAT + Q_LORA], qn_ref[...]).astype(BF16)
        ckv_n = _rms(z[:, OFF_CKV:OFF_CKV + KV_LORA], kvn_ref[...])
        krp = _dot_nt(h2, wkr_ref[...])
        if latent:
            krp = _rope(krp, cosb_ref[...], sinb_ref[...], B_ROPE_DIM // 4)
        else:
            ckvn_ref[...] = ckv_n
            _store_seq_minor(krope_ref, pltpu.roll(krp, HALF, axis=1))
        carry["ckv_b"] = ckv_n.astype(BF16)
        carry["krp"] = krp

    def up_proj():
        q_lat, ckv_b, krp = carry["q_lat"], carry["ckv_b"], carry["krp"]
        qb = _dot(q_lat, wuq_ref[...])
        kn = _dot(ckv_b, wuk_ref[...])
        vb_ref[...] = _dot_nt(wuv_ref[...], ckv_b).astype(BF16)
        for h in range(B_HEADS):
            qh = qb[:, h * LANES:(h + 1) * LANES]
            if latent:
                qh = _rope(qh, cosb_ref[...], sinb_ref[...], B_ROPE_DIM // 4)
            qb_ref[:, h * LANES:(h + 1) * LANES] = (qh * (B_SCALE * LOG2E)).astype(BF16)
            kb_ref[:, h * LANES:(h + 1) * LANES] = (kn[:, h * LANES:(h + 1) * LANES] + krp).astype(BF16)

    def ffn(hooks):
        x = x_ref[...]
        h1 = _mod_norm(x, n1_ref[...], mods_ref, 0).astype(BF16)
        x1 = x + 0.5 * _gate(mods_ref, 0) * _ffn(h1, wgu_ref, wd_ref, hooks)
        x1_ref[...] = x1
        x1s_ref[...] = x1

    @pl.when(i == 0)
    def _():
        ffn(None)

    @pl.when(jnp.logical_and(i > 0, i < n_tiles))
    def _():
        ffn({PRE_HOOK_IN_PROJ: in_proj, PRE_HOOK_UP_PROJ: up_proj})

    @pl.when(i == n_tiles)
    def _():
        in_proj()
        up_proj()


def _const_spec(shape):
    nd = len(shape)
    return pl.BlockSpec(shape, lambda i: (0,) * nd, pipeline_mode=pl.Buffered(1))


def _pre(x, mods3, w, latent, seq, rope, tm):
    t = x.shape[0]
    tiles_per_batch = max(seq // tm, 1)
    n_tiles = t // tm
    cur = lambda i: jnp.minimum(i, n_tiles - 1)
    prev = lambda i: jnp.maximum(i - 1, 0)
    mod_of = (lambda j: 1 + j // tiles_per_batch) if latent else (lambda j: 0)
    cur_rows = lambda i: (cur(i), 0)
    prev_rows = lambda i: (prev(i), 0)

    weights = [w["n1"], w["wgu1"], w["wd1"], w["nm"], w["win_a"], w["win_kr"], w["qn"], w["kvn"], w["wuq"], w["wuk"],
               w["wuv_t"]]
    in_specs = [pl.BlockSpec((tm, D_MODEL), cur_rows),
                pl.BlockSpec((None, 1, N_MOD * D_MODEL), lambda i: (mod_of(cur(i)), 0, 0)),
                pl.BlockSpec((None, 1, N_MOD * D_MODEL), lambda i: (mod_of(prev(i)), 0, 0))]
    in_specs += [_const_spec(a.shape) for a in weights]
    args = [x, mods3, mods3] + weights
    if latent:
        in_specs += [pl.BlockSpec((tm, LANES), lambda i: (prev(i) % tiles_per_batch, 0))] * 4
        args += list(rope)

    out_specs = [pl.BlockSpec((tm, D_MODEL), cur_rows)]
    out_shape = [jax.ShapeDtypeStruct((t, D_MODEL), F32)]
    for wd in (A_Q_W, 4 * LANES):
        out_specs.append(pl.BlockSpec((tm, wd), prev_rows))
        out_shape.append(jax.ShapeDtypeStruct((t, wd), BF16))
    if latent:
        out_specs.append(pl.BlockSpec((tm // LANES, A_KV_W, LANES), lambda i: (prev(i), 0, 0)))
        out_shape.append(jax.ShapeDtypeStruct((t // LANES, A_KV_W, LANES), BF16))
    for wd in (B_HEADS * LANES, B_HEADS * LANES):
        out_specs.append(pl.BlockSpec((tm, wd), prev_rows))
        out_shape.append(jax.ShapeDtypeStruct((t, wd), BF16))
    out_specs.append(pl.BlockSpec((B_HEADS * B_V_DIM, tm), lambda i: (0, prev(i))))
    out_shape.append(jax.ShapeDtypeStruct((B_HEADS * B_V_DIM, t), BF16))
    if not latent:
        nb = tm // seq
        for feat in (A_KV_W, A_KV_W):
            out_specs.append(pl.BlockSpec((nb, feat, seq), lambda i: (prev(i), 0, 0)))
            out_shape.append(jax.ShapeDtypeStruct((t // seq, feat, seq), F32))
        out_specs.append(pl.BlockSpec((tm, KV_LORA), prev_rows))
        out_shape.append(jax.ShapeDtypeStruct((t, KV_LORA), F32))
        out_specs.append(pl.BlockSpec((nb, B_ROPE_DIM, seq), lambda i: (prev(i), 0, 0)))
        out_shape.append(jax.ShapeDtypeStruct((t // seq, B_ROPE_DIM, seq), F32))

    return pl.pallas_call(
        functools.partial(_pre_kernel, latent, n_tiles),
        grid=(n_tiles + 1,),
        in_specs=in_specs,
        out_specs=out_specs,
        out_shape=out_shape,
        scratch_shapes=[pltpu.VMEM((tm, D_MODEL), F32)],
        compiler_params=pltpu.CompilerParams(dimension_semantics=("arbitrary",), vmem_limit_bytes=VMEM_LIMIT),
        name="pre_latent" if latent else "pre_context",
    )(*args)


def _cache_kernel(ckv_ref, kropet_ref, ckt_ref, cvt_ref, wuk_ref, wuv_ref, kb_ref, vb_ref, kap_ref, vat_ref):
    p = ckv_ref.shape[0]
    c = ckv_ref[...].astype(BF16)
    kn = _dot(c, wuk_ref[...])
    krp = jnp.concatenate([jnp.zeros((B_NOPE_DIM, p), F32), kropet_ref[...],
                           jnp.zeros((LANES - B_QK_DIM, p), F32)], axis=0).T
    for h in range(B_HEADS):
        kb_ref[:, h * LANES:(h + 1) * LANES] = (kn[:, h * LANES:(h + 1) * LANES] + krp).astype(BF16)
    vb_ref[...] = _dot_nt(wuv_ref[...], c).astype(BF16)
    _store_gqa_k(ckt_ref[...].T, _low_lanes(p), kap_ref)
    vat_ref[...] = cvt_ref[...].astype(BF16)


def _cache_kv(ckv, krope_t, ck_t, cv_t, wuk, wuv_t):
    b, p, _ = ckv.shape
    blk = lambda wd: pl.BlockSpec((None, p, wd), lambda i: (i, 0, 0))
    blk_t = lambda feat: pl.BlockSpec((None, feat, p), lambda i: (i, 0, 0))
    vdim = B_HEADS * B_V_DIM
    return pl.pallas_call(
        _cache_kernel,
        grid=(b,),
        in_specs=[blk(KV_LORA), blk_t(B_ROPE_DIM), blk_t(A_KV_W), blk_t(A_KV_W),
                  pl.BlockSpec(wuk.shape, lambda i: (0, 0)),
                  pl.BlockSpec(wuv_t.shape, lambda i: (0, 0))],
        out_specs=[blk(B_HEADS * LANES), pl.BlockSpec((vdim, p), lambda i: (0, i)), blk(4 * LANES),
                   pl.BlockSpec((A_KV_W, p), lambda i: (0, i))],
        out_shape=[jax.ShapeDtypeStruct((b, p, B_HEADS * LANES), BF16),
                   jax.ShapeDtypeStruct((vdim, b * p), BF16),
                   jax.ShapeDtypeStruct((b, p, 4 * LANES), BF16),
                   jax.ShapeDtypeStruct((A_KV_W, b * p), BF16)],
        compiler_params=pltpu.CompilerParams(dimension_semantics=("parallel",), vmem_limit_bytes=VMEM_LIMIT),
        name="cache_kv",
    )(ckv, krope_t, ck_t, cv_t, wuk, wuv_t)


def _att_ctx_kernel(seq, sink_ref, qa_ref, kap_ref, vt_ref, qb_ref, kb_ref, vbt_ref, oa_ref, ob_ref):
    nb = qa_ref.shape[0] // seq
    ones = jnp.ones((SUM_ROWS, seq), BF16)
    upper = lax.broadcasted_iota(jnp.int32, (1, 2 * seq), 1) >= seq
    pending = {}

    def gqa_unit(bi, kv, half):
        rows = slice(bi * seq, (bi + 1) * seq)
        u = 2 * kv + half

        def scores():
            q2 = jnp.concatenate([qa_ref[rows, (2 * kv) * LANES:(2 * kv + 1) * LANES],
                                  qa_ref[rows, (2 * kv + 1) * LANES:(2 * kv + 2) * LANES]], axis=0)
            return [_dot_nt(kap_ref[rows, u * LANES:(u + 1) * LANES], q2)]

        def sink():
            return jnp.where(upper, sink_ref[4 * kv + 2 + half], sink_ref[4 * kv + half]) * LOG2E

        def values(p_blocks):
            v = vt_ref[bi, kv * A_HEAD_DIM:(kv + 1) * A_HEAD_DIM, :].astype(BF16)
            return _dot(jnp.concatenate([v, ones], axis=0), p_blocks[0])

        def finish(o):
            if half == 0:
                pending[("a", bi, kv)] = o
                return
            first = pending.pop(("a", bi, kv))
            for pr in range(2):
                cols = slice(pr * seq, (pr + 1) * seq)
                pair = jnp.concatenate([first[:, cols], o[:, cols]], axis=0)
                c = 2 * kv + pr
                oa_ref[rows, c * LANES:(c + 1) * LANES] = pair.T.astype(BF16)

        return _Unit(scores, sink, values, finish)

    def mla_unit(bi, h):
        rows = slice(bi * seq, (bi + 1) * seq)
        lanes = slice(h * LANES, (h + 1) * LANES)

        def scores():
            return [_dot_nt(kb_ref[rows, lanes], qb_ref[rows, lanes])]

        def values(p_blocks):
            v = vbt_ref[h * B_V_DIM:(h + 1) * B_V_DIM, rows]
            return _dot(jnp.concatenate([v, ones], axis=0), p_blocks[0])

        def finish(o):
            if h % 2 == 0:
                pending[("b", bi, h)] = o
                return
            c = h // 2
            pair = jnp.concatenate([pending.pop(("b", bi, h - 1)), o], axis=0)
            ob_ref[rows, c * LANES:(c + 1) * LANES] = pair.T.astype(BF16)

        return _Unit(scores, lambda: None, values, finish)

    units = []
    for bi in range(nb):
        units += [gqa_unit(bi, kv, half) for kv in range(A_KV_HEADS) for half in range(2)]
        units += [mla_unit(bi, h) for h in range(B_HEADS)]
    assert len(units) % CTX_GROUP == 0
    _attention_pipeline([units[g:g + CTX_GROUP] for g in range(0, len(units), CTX_GROUP)], A_HEAD_DIM)


def _att_ctx(sink, qa, kap, v_t, qb, kb, vb_t, seq):
    t = qa.shape[0]
    rows = CTX_BATCHES * seq
    blk = lambda wd: pl.BlockSpec((rows, wd), lambda b: (b, 0))
    return pl.pallas_call(
        functools.partial(_att_ctx_kernel, seq),
        grid=(t // rows,),
        in_specs=[pl.BlockSpec(memory_space=pltpu.SMEM),
                  blk(qa.shape[1]), blk(kap.shape[1]),
                  pl.BlockSpec((CTX_BATCHES, A_KV_W, seq), lambda b: (b, 0, 0)),
                  blk(qb.shape[1]), blk(kb.shape[1]),
                  pl.BlockSpec((vb_t.shape[0], rows), lambda b: (0, b))],
        out_specs=[blk(A_Q_W), blk(B_HEADS * B_V_DIM)],
        out_shape=[jax.ShapeDtypeStruct((t, A_Q_W), BF16), jax.ShapeDtypeStruct((t, B_HEADS * B_V_DIM), BF16)],
        compiler_params=pltpu.CompilerParams(dimension_semantics=("parallel",), vmem_limit_bytes=VMEM_LIMIT),
        name="att_context",
    )(sink, qa, kap, v_t, qb, kb, vb_t)


class _Unit:
    def __init__(self, scores, sink, values, finish):
        self.scores, self.sink, self.values, self.finish = scores, sink, values, finish


def _attention_pipeline(groups, dv):
    n = len(groups)
    scores, probs = {}, {}
    for t in range(n + 2):
        if t < n:
            scores[t] = [unit.scores() for unit in groups[t]]
        if 0 <= t - 2 < n:
            for unit, (p_blocks, p_sink) in zip(groups[t - 2], probs.pop(t - 2)):
                acc = unit.values(p_blocks)
                den = acc[dv:dv + 1]
                unit.finish(acc[:dv] / (den if p_sink is None else den + p_sink))
        if 0 <= t - 1 < n:
            probs[t - 1] = [_probs_keys_major(s, unit.sink()) for unit, s in zip(groups[t - 1], scores.pop(t - 1))]


def _window_units(n_lat, tile, sink_ref, qa_ref, kap_ref, vat_ref, kapc_ref, vatc_ref, oa_ref):
    n_blocks = qa_ref.shape[0] // Q_BLOCK
    nq = 2 * Q_BLOCK
    past = kapc_ref.shape[0]
    upper = lax.broadcasted_iota(jnp.int32, (1, nq), 1) >= Q_BLOCK
    ones_band = jnp.ones((SUM_ROWS, BAND), BF16)
    ones_ctx = jnp.ones((SUM_ROWS, past), BF16)
    key = lax.broadcasted_iota(jnp.int32, (BAND, nq), 0)
    qry = lax.broadcasted_iota(jnp.int32, (BAND, nq), 1) & (Q_BLOCK - 1)
    key_minus_qry = key - qry
    pending = {}

    def make(qb, kv, half, start, valid, v_band):
        u = 2 * kv + half
        rows = slice(qb * Q_BLOCK, (qb + 1) * Q_BLOCK)
        feat = slice(kv * A_HEAD_DIM, (kv + 1) * A_HEAD_DIM)

        def scores():
            q2 = jnp.concatenate([qa_ref[rows, (2 * kv) * LANES:(2 * kv + 1) * LANES],
                                  qa_ref[rows, (2 * kv + 1) * LANES:(2 * kv + 2) * LANES]], axis=0)
            s_band = _dot_nt(kap_ref[pl.ds(start, BAND), u * LANES:(u + 1) * LANES], q2)
            s_ctx = _dot_nt(kapc_ref[:, u * LANES:(u + 1) * LANES], q2)
            return [jnp.where(valid, s_band, NEG), s_ctx]

        def sink():
            return jnp.where(upper, sink_ref[4 * kv + 2 + half], sink_ref[4 * kv + half]) * LOG2E

        def values(p_blocks):
            vb = jnp.concatenate([v_band[r][feat] for r in range(BAND // LANES)], axis=1)
            r_band = _dot(jnp.concatenate([vb, ones_band], axis=0), p_blocks[0])
            r_ctx = _dot(jnp.concatenate([vatc_ref[feat, :], ones_ctx], axis=0), p_blocks[1])
            return r_band + r_ctx

        def finish(o):
            if half == 0:
                pending[(qb, kv)] = o
                return
            first = pending.pop((qb, kv))
            for pr in range(2):
                cols = slice(pr * Q_BLOCK, (pr + 1) * Q_BLOCK)
                pair = jnp.concatenate([first[:, cols], o[:, cols]], axis=0)
                c = 2 * kv + pr
                oa_ref[rows, c * LANES:(c + 1) * LANES] = pair.T.astype(BF16)

        return _Unit(scores, sink, values, finish)

    units = []
    for qb in range(n_blocks):
        i = tile * n_blocks + qb
        start = pl.multiple_of(jnp.clip(i * Q_BLOCK - Q_BLOCK, 0, n_lat - BAND), Q_BLOCK)
        valid = jnp.abs(key_minus_qry + (start - i * Q_BLOCK)) <= WINDOW
        v_band = vat_ref[pl.ds(start // LANES, BAND // LANES)]
        units += [make(qb, kv, half, start, valid, v_band) for kv in range(A_KV_HEADS) for half in range(2)]
    return units


def _probs_keys_major(scores_t, sink=None):
    m = None
    for s in scores_t:
        mx = jnp.max(s, axis=0, keepdims=True)
        m = mx if m is None else jnp.maximum(m, mx)
    if sink is not None:
        m = jnp.maximum(m, sink)
    probs = [jnp.exp2(s - m).astype(BF16) for s in scores_t]
    return probs, (None if sink is None else jnp.exp2(sink - m))


def _mla_units(qb_ref, kbl_ref, vtl_ref, kbc_ref, vtc_ref, ob_ref):
    kb = MLA_KEY_BLOCK
    ones = jnp.ones((SUM_ROWS, kb), BF16)
    blocks = [(kbc_ref, vtc_ref, j) for j in range(kbc_ref.shape[0] // kb)]
    blocks += [(kbl_ref, vtl_ref, j) for j in range(kbl_ref.shape[0] // kb)]
    pending = {}

    def make(h):
        lanes = slice(h * LANES, (h + 1) * LANES)
        feat = slice(h * B_V_DIM, (h + 1) * B_V_DIM)

        def scores():
            return [_dot_nt(k_ref[j * kb:(j + 1) * kb, lanes], qb_ref[:, lanes]) for k_ref, _, j in blocks]

        def values(p_blocks):
            acc = None
            for (_, v_ref, j), p in zip(blocks, p_blocks):
                r = _dot(jnp.concatenate([v_ref[feat, j * kb:(j + 1) * kb], ones], axis=0), p)
                acc = r if acc is None else acc + r
            return acc

        def finish(o):
            if h % 2 == 0:
                pending[h] = o
                return
            c = h // 2
            pair = jnp.concatenate([pending.pop(h - 1), o], axis=0)
            ob_ref[:, c * LANES:(c + 1) * LANES] = pair.T.astype(BF16)

        return _Unit(scores, lambda: None, values, finish)

    return [make(h) for h in range(B_HEADS)]


def _att_lat_kernel(cast_groups, n_lat, sink_ref, qa_ref, kap_ref, vat_ref, kapc_ref, vatc_ref,
                    qb_ref, kbl_ref, vtl_ref, kbc_ref, vtc_ref, *rest):
    n_src = sum(cast_groups)
    oa_ref, ob_ref = rest[n_src:n_src + 2]
    _cast_blocks(rest[:n_src], rest[n_src + 2:], cast_groups)

    mla = _mla_units(qb_ref, kbl_ref, vtl_ref, kbc_ref, vtc_ref, ob_ref)
    win = _window_units(n_lat, pl.program_id(1), sink_ref, qa_ref, kap_ref, vat_ref, kapc_ref, vatc_ref, oa_ref)
    assert len(win) % len(mla) == 0
    per_head = len(win) // len(mla)
    groups = [[unit] + win[h * per_head:(h + 1) * per_head] for h, unit in enumerate(mla)]
    _attention_pipeline(groups, B_V_DIM)


def _att_lat(sink, qa, kap, vat, kapc, vatc, qb, kbl, vtl, kbc, vtc, cast_weights):
    b, n, _ = qb.shape
    past = kbc.shape[1]
    tq = LAT_Q_TILE
    vdim = B_HEADS * B_V_DIM
    q_tiles = n // tq
    steps = b * q_tiles
    w_in_specs, w_args, w_out_specs, w_out_shapes, groups = _cast_windows(
        cast_weights, steps, lambda bi, i: bi * q_tiles + i)
    q_blk = lambda wd: pl.BlockSpec((None, tq, wd), lambda bi, i: (bi, i, 0))
    rows_blk = lambda r, wd: pl.BlockSpec((None, r, wd), lambda bi, i: (bi, 0, 0))
    feat_blk = lambda feat, r: pl.BlockSpec((feat, r), lambda bi, i: (0, bi))
    out = pl.pallas_call(
        functools.partial(_att_lat_kernel, groups, n),
        grid=(b, q_tiles),
        in_specs=[pl.BlockSpec(memory_space=pltpu.SMEM),
                  q_blk(A_Q_W), rows_blk(n, kap.shape[2]),
                  pl.BlockSpec((n // LANES, A_KV_W, LANES), lambda bi, i: (bi, 0, 0)),
                  rows_blk(past, kapc.shape[2]), feat_blk(A_KV_W, past),
                  q_blk(qb.shape[2]), rows_blk(n, kbl.shape[2]), feat_blk(vdim, n),
                  rows_blk(past, kbc.shape[2]), feat_blk(vdim, past)] + w_in_specs,
        out_specs=[q_blk(A_Q_W), q_blk(vdim)] + w_out_specs,
        out_shape=[jax.ShapeDtypeStruct((b, n, A_Q_W), BF16), jax.ShapeDtypeStruct((b, n, vdim), BF16)]
        + w_out_shapes,
        compiler_params=pltpu.CompilerParams(dimension_semantics=("parallel", "parallel"),
                                             vmem_limit_bytes=VMEM_LIMIT),
        name="att_latent",
    )(sink, qa, kap, vat, kapc, vatc, qb, kbl, vtl, kbc, vtc, *w_args)
    return out[0], out[1], out[2:]


def _post_kernel(x1_ref, mods_ref, oa_ref, ob_ref, nm_ref, wing_ref, woa_ref, wob_ref, wout_ref,
                 n2_ref, wgu_ref, wd_ref, nf_ref, y_ref):
    def half_tile(rows):
        x1 = x1_ref[rows, :]
        h2 = _mod_norm(x1, nm_ref[...], mods_ref, 1).astype(BF16)
        g = _dot_nt(h2, wing_ref[...])
        ma = _dot(oa_ref[rows, :], woa_ref[...])
        mb = _dot(ob_ref[rows, :], wob_ref[...])
        yield
        m = jax.nn.sigmoid(g[:, :D_MODEL]) * ma + jax.nn.sigmoid(g[:, D_MODEL:]) * mb
        x2 = x1 + _gate(mods_ref, 1) * _dot(m.astype(BF16), wout_ref[...])
        yield
        h3 = _mod_norm(x2, n2_ref[...], mods_ref, 2).astype(BF16)
        x3 = x2 + 0.5 * _gate(mods_ref, 2) * (yield from _ffn_stages(h3, wgu_ref, wd_ref))
        y_ref[rows, :] = _rms(x3, nf_ref[...])

    half = x1_ref.shape[0] // 2
    _interleave([half_tile(slice(0, half)), half_tile(slice(half, 2 * half))])


def _post(x1, mods3, oa, ob, w, latent, tiles_per_batch):
    t = x1.shape[0]
    tm = POST_TILE
    if latent:
        mod_row = lambda i: (1 + i // tiles_per_batch, 0, 0)
    else:
        mod_row = lambda i: (0, 0, 0)
    row_blk = lambda i: (i, 0)
    weights = [w["nm"], w["win_g"], w["woa"], w["wob"], w["wout"], w["n2"], w["wgu2"], w["wd2"], w["nf"]]
    in_specs = [pl.BlockSpec((tm, D_MODEL), row_blk),
                pl.BlockSpec((None, 1, N_MOD * D_MODEL), mod_row),
                pl.BlockSpec((tm, A_Q_W), row_blk),
                pl.BlockSpec((tm, B_HEADS * B_V_DIM), row_blk)]
    in_specs += [_const_spec(a.shape) for a in weights]
    return pl.pallas_call(
        _post_kernel,
        grid=(t // tm,),
        in_specs=in_specs,
        out_specs=pl.BlockSpec((tm, D_MODEL), row_blk),
        out_shape=jax.ShapeDtypeStruct((t, D_MODEL), F32),
        compiler_params=pltpu.CompilerParams(dimension_semantics=("parallel",), vmem_limit_bytes=VMEM_LIMIT),
        name="post_latent" if latent else "post_context",
    )(x1, mods3, oa, ob, *weights)


def _rope_tables(n):
    f32 = np.float32
    rows = n // GRID_W
    t_row = np.repeat(np.arange(rows, dtype=f32), GRID_W)
    t_col = np.tile(np.arange(GRID_W, dtype=f32), rows)

    def angles(d_rot):
        d_half = d_rot // 2
        inv = (f32(1.0) / np.power(f32(ROPE_THETA), np.arange(0, d_half, 2, dtype=f32) / f32(d_half))).astype(f32)
        ar = t_row[:, None] * inv[None, :]
        ac = t_col[:, None] * inv[None, :]
        return np.concatenate([ar, ar, ac, ac], axis=-1).astype(f32)

    def signed(sin, d_rot):
        q = d_rot // 4
        sign = np.where((np.arange(d_rot) % (2 * q)) < q, f32(-1.0), f32(1.0)).astype(f32)
        return sin * sign[None, :]

    ang_a = angles(A_HEAD_DIM)
    cos_a = np.tile(np.cos(ang_a), (1, LANES // A_HEAD_DIM))
    sin_a = np.tile(signed(np.sin(ang_a), A_HEAD_DIM), (1, LANES // A_HEAD_DIM))
    ang_b = angles(B_ROPE_DIM)
    pad = ((0, 0), (B_NOPE_DIM, LANES - B_NOPE_DIM - B_ROPE_DIM))
    cos_b = np.pad(np.cos(ang_b), pad, constant_values=1.0)
    sin_b = np.pad(signed(np.sin(ang_b), B_ROPE_DIM), pad)
    return tuple(jnp.asarray(t, dtype=F32) for t in (cos_a, sin_a, cos_b, sin_b))


def _prep_weights(ffn1_norm, ffn1_w_gu, ffn1_w_down, mix_norm, w_in, q_lat_norm, kv_lat_norm, w_uq, w_ukv,
                  w_o_a, w_o_b, w_out, ffn2_norm, ffn2_w_gu, ffn2_w_down, final_norm):
    win_t = jnp.swapaxes(w_in[0], 0, 1)
    n_attn = A_Q_W + 2 * A_KV_W + Q_LORA + KV_LORA
    win_kr = jnp.pad(win_t[n_attn:n_attn + B_ROPE_DIM],
                     ((B_NOPE_DIM, LANES - B_NOPE_DIM - B_ROPE_DIM), (0, 0))).astype(BF16)
    n_gate = n_attn + B_ROPE_DIM
    whole = lambda wm: (wm, 0, wm.shape[0])

    wuq = jnp.pad(w_uq[0].reshape(Q_LORA, B_HEADS, B_QK_DIM),
                  ((0, 0), (0, 0), (0, LANES - B_QK_DIM))).reshape(Q_LORA, B_HEADS * LANES).astype(BF16)
    wukv = w_ukv[0].reshape(KV_LORA, B_HEADS, B_NOPE_DIM + B_V_DIM)
    wuk = jnp.pad(wukv[:, :, :B_NOPE_DIM],
                  ((0, 0), (0, 0), (0, LANES - B_NOPE_DIM))).reshape(KV_LORA, B_HEADS * LANES).astype(BF16)
    wuv = wukv[:, :, B_NOPE_DIM:].reshape(KV_LORA, B_HEADS * B_V_DIM).astype(BF16)

    return {
        "n1": ffn1_norm, "nm": mix_norm, "win_kr": win_kr,
        "qn": q_lat_norm, "kvn": kv_lat_norm, "wuq": wuq, "wuk": wuk, "wuv_t": wuv.T,
        "n2": ffn2_norm, "nf": final_norm.reshape(1, D_MODEL),
        "pre_f32": [whole(ffn1_w_gu[0]), whole(ffn1_w_down[0]), (win_t, 0, IN_A_W)],
        "post_f32": [(win_t, n_gate, 2 * D_MODEL), whole(w_o_a[0]), whole(w_o_b[0]), whole(w_out[0]),
                     whole(ffn2_w_gu[0]), whole(ffn2_w_down[0])],
    }


def kernel(x_prompt, x_sample, cache_attn_k, cache_attn_v, cache_mla_ckv, cache_mla_krope, c, c_ctx, ada_w, ada_b, ffn1_norm, ffn1_w_gu, ffn1_w_down, mix_norm, w_in, attn_sink, q_lat_norm, kv_lat_norm, w_uq, w_ukv, w_o_a, w_o_b, w_out, ffn2_norm, ffn2_w_gu, ffn2_w_down, final_norm):
    assert ada_w.shape[0] == 1, "single trunk layer"
    bp, sp, d = x_prompt.shape
    bs, ns, _ = x_sample.shape
    past = cache_attn_k.shape[2]
    assert d == D_MODEL and bs + 1 <= MOD_ROWS
    assert PRE_TILE_CTX % sp == 0 and (bp * sp) % PRE_TILE_CTX == 0 and (bp * sp) % POST_TILE == 0
    assert ns % PRE_TILE_LAT == 0 and ns % POST_TILE == 0
    assert ns % LAT_Q_TILE == 0 and LAT_Q_TILE % Q_BLOCK == 0 and ns >= BAND

    w = _prep_weights(ffn1_norm, ffn1_w_gu, ffn1_w_down, mix_norm, w_in, q_lat_norm, kv_lat_norm, w_uq, w_ukv,
                      w_o_a, w_o_b, w_out, ffn2_norm, ffn2_w_gu, ffn2_w_down, final_norm)
    sink = attn_sink[0]

    cvec = jnp.concatenate([c_ctx[None, :], c, jnp.zeros((MOD_ROWS - 1 - bs, d), F32)], axis=0)
    mods3, (w["wgu1"], w["wd1"], w["win_a"]) = _ada_mods(cvec, ada_w[0], ada_b, w["pre_f32"])

    xp = x_prompt.reshape(bp * sp, d)
    (x1p, qa, kap, qb, kb, vb_t, k_t, v_t, ckv_n, krope_t) = _pre(xp, mods3, w, False, sp, None, PRE_TILE_CTX)
    oa_p, ob_p = _att_ctx(sink, qa, kap, v_t, qb, kb, vb_t, sp)

    xs = x_sample.reshape(bs * ns, d)
    (x1s, qa, kap, vad, qb, kb, vb_t) = _pre(xs, mods3, w, True, ns, _rope_tables(ns), PRE_TILE_LAT)
    feat_major = lambda a: a[:, 0].transpose(0, 2, 3, 1).reshape(bs, A_KV_W, past)
    kbc, vbc, kapc, vadc = _cache_kv(
        cache_mla_ckv[:, 0], jnp.swapaxes(cache_mla_krope[:, 0], 1, 2),
        feat_major(cache_attn_k), feat_major(cache_attn_v), w["wuk"], w["wuv_t"])
    r3 = lambda a: a.reshape(bs, ns, a.shape[1])
    oa, ob, (w["win_g"], w["woa"], w["wob"], w["wout"], w["wgu2"], w["wd2"]) = _att_lat(
        sink, r3(qa), r3(kap), vad, kapc, vadc, r3(qb), r3(kb), vb_t, kbc, vbc, w["post_f32"])

    y_prompt = _post(x1p, mods3, oa_p, ob_p, w, False, 1).reshape(bp, sp, d)
    y_sample = _post(x1s, mods3, oa.reshape(bs * ns, A_Q_W), ob.reshape(bs * ns, B_HEADS * B_V_DIM),
                     w, True, ns // POST_TILE).reshape(bs, ns, d)

    new_attn_k = k_t.reshape(bp, 1, A_KV_HEADS, A_HEAD_DIM, sp).transpose(0, 1, 4, 2, 3)
    new_attn_v = v_t.reshape(bp, 1, A_KV_HEADS, A_HEAD_DIM, sp).transpose(0, 1, 4, 2, 3)
    new_mla_ckv = ckv_n.reshape(bp, 1, sp, KV_LORA)
    new_mla_krope = krope_t.reshape(bp, 1, B_ROPE_DIM, sp).transpose(0, 1, 3, 2)
    return (y_prompt, y_sample, new_attn_k, new_attn_v, new_mla_ckv, new_mla_krope)
```

```python
import functools
import math

import jax
import jax.numpy as jnp
import numpy as np
from jax import lax
from jax.experimental import pallas as pl
from jax.experimental.pallas import tpu as pltpu

F32 = jnp.float32
BF16 = jnp.bfloat16

D_MODEL = 1024
N_MOD = 9
GRID_W = 64
WINDOW = 128
A_HEADS = 8
A_KV_HEADS = 2
A_HEAD_DIM = 64
A_Q_W = A_HEADS * A_HEAD_DIM
A_KV_W = A_KV_HEADS * A_HEAD_DIM
B_HEADS = 8
B_NOPE_DIM = 64
B_ROPE_DIM = 32
B_V_DIM = 64
B_QK_DIM = B_NOPE_DIM + B_ROPE_DIM
Q_LORA = 256
KV_LORA = 256
D_FF = 2816
ROPE_THETA = 10000.0
EPS = 1e-6
NEG = -1e30
A_SCALE = A_HEAD_DIM ** -0.5
B_SCALE = B_QK_DIM ** -0.5
LOG2E = 1.4426950408889634

LANES = 128
HALF = LANES // 2
FF_CHUNK = 256
N_FF_CHUNKS = D_FF // FF_CHUNK
PRE_TILE_CTX = 512
PRE_TILE_LAT = 256
POST_TILE = 512
ADA_TILE = 2304
Q_BLOCK = 128
BAND = 3 * Q_BLOCK
CTX_BATCHES = 4
CTX_GROUP = 4
BF16_TILE_ROWS = 16
SUM_ROWS = BF16_TILE_ROWS
LAT_Q_TILE = 512
MLA_KEY_BLOCK = 256
MOD_ROWS = 8
VMEM_LIMIT = 56 * 1024 * 1024

IN_A_W = A_Q_W + 2 * A_KV_W + Q_LORA + KV_LORA
OFF_KA = A_Q_W
OFF_VA = OFF_KA + A_KV_W
OFF_QLAT = OFF_VA + A_KV_W
OFF_CKV = OFF_QLAT + Q_LORA


def _dot(a, b):
    return jnp.dot(a, b, preferred_element_type=F32)


def _dot_nt(a, b):
    return lax.dot_general(a, b, (((1,), (1,)), ((), ())), preferred_element_type=F32)


def _rms(x, g):
    ms = jnp.mean(x * x, axis=-1, keepdims=True)
    return x * lax.rsqrt(ms + EPS) * g


def _mod_norm(x, g, mods_ref, k):
    shift = mods_ref[:, (3 * k) * D_MODEL:(3 * k + 1) * D_MODEL]
    scale = mods_ref[:, (3 * k + 1) * D_MODEL:(3 * k + 2) * D_MODEL]
    return _rms(x, g) * (1.0 + scale) + shift


def _gate(mods_ref, k):
    return mods_ref[:, (3 * k + 2) * D_MODEL:(3 * k + 3) * D_MODEL]


def _ffn_stages(h, wgu_ref, wd_ref, hooks=None):
    def gate_up(c):
        a = _dot(h, wgu_ref[:, c * FF_CHUNK:(c + 1) * FF_CHUNK])
        u = _dot(h, wgu_ref[:, D_FF + c * FF_CHUNK:D_FF + (c + 1) * FF_CHUNK])
        return a, u

    acc = None
    nxt = gate_up(0)
    yield
    for c in range(N_FF_CHUNKS):
        a, u = nxt
        if c + 1 < N_FF_CHUNKS:
            nxt = gate_up(c + 1)
        yield
        act = (a * jax.nn.sigmoid(a) * u).astype(BF16)
        d = _dot(act, wd_ref[c * FF_CHUNK:(c + 1) * FF_CHUNK, :])
        acc = d if acc is None else acc + d
        if hooks and c in hooks:
            hooks[c]()
    return acc


def _ffn(h, wgu_ref, wd_ref, hooks=None):
    stages = _ffn_stages(h, wgu_ref, wd_ref, hooks)
    while True:
        try:
            next(stages)
        except StopIteration as done:
            return done.value


def _interleave(streams):
    live = list(streams)
    while live:
        for s in list(live):
            try:
                next(s)
            except StopIteration:
                live.remove(s)


def _rope(x, cos, sin_signed, dist):
    lane = lax.broadcasted_iota(jnp.int32, x.shape, 1)
    first = (lane & (2 * dist - 1)) < dist
    partner = jnp.where(first, pltpu.roll(x, LANES - dist, axis=1), pltpu.roll(x, dist, axis=1))
    return x * cos + partner * sin_signed


def _store_gqa_k(k, low, kap_ref):
    k_sw = pltpu.roll(k, HALF, axis=1)
    zero = jnp.zeros_like(k)
    kap_ref[:, 0 * LANES:1 * LANES] = jnp.where(low, k, zero).astype(BF16)
    kap_ref[:, 1 * LANES:2 * LANES] = jnp.where(low, zero, k_sw).astype(BF16)
    kap_ref[:, 2 * LANES:3 * LANES] = jnp.where(low, k_sw, zero).astype(BF16)
    kap_ref[:, 3 * LANES:4 * LANES] = jnp.where(low, zero, k).astype(BF16)


def _store_gqa_v_blocks(v, vt_ref):
    for r in range(vt_ref.shape[0]):
        vt_ref[r] = v[r * LANES:(r + 1) * LANES, :].T.astype(BF16)


def _store_seq_minor(out_ref, val):
    nb, feat, seq = out_ref.shape
    for bi in range(nb):
        out_ref[bi] = val[bi * seq:(bi + 1) * seq, :].T[:feat, :]


def _low_lanes(rows):
    return lax.broadcasted_iota(jnp.int32, (rows, LANES), 1) < HALF


def _cast_windows(windows, steps, step_of):
    in_specs, args, out_specs, out_shapes, groups = [], [], [], [], []
    for wm, row0, rows in windows:
        cols = wm.shape[1]
        assert rows % (steps * BF16_TILE_ROWS) == 0, "row blocks must be whole packed-bf16 sublane tiles"
        per_step = rows // steps
        sub = math.gcd(per_step, row0) if row0 else per_step
        assert sub % BF16_TILE_ROWS == 0
        k = per_step // sub
        for j in range(k):
            in_specs.append(pl.BlockSpec(
                (sub, cols), lambda *g, j=j, k=k, base=row0 // sub: (base + step_of(*g) * k + j, 0)))
            args.append(wm)
        out_specs.append(pl.BlockSpec((per_step, cols), lambda *g: (step_of(*g), 0)))
        out_shapes.append(jax.ShapeDtypeStruct((rows, cols), BF16))
        groups.append(k)
    return in_specs, args, out_specs, out_shapes, tuple(groups)


def _cast_blocks(w32_refs, w16_refs, groups):
    first = 0
    for dst, k in zip(w16_refs, groups):
        sub = dst.shape[0] // k
        for j in range(k):
            dst[j * sub:(j + 1) * sub, :] = w32_refs[first + j][...].astype(BF16)
        first += k


def _ada_kernel(cast_groups, c_ref, w_ref, b_ref, *rest):
    n_src = sum(cast_groups)
    _cast_blocks(rest[:n_src], rest[n_src + 1:], cast_groups)
    o_ref = rest[n_src]
    c = c_ref[...]
    s = (c * jax.nn.sigmoid(c)).astype(BF16)
    mods = _dot(s, w_ref[...].astype(BF16)) + b_ref[...]
    for r in range(MOD_ROWS):
        o_ref[r] = mods[r:r + 1, :]


def _ada_mods(cvec, ada_w, ada_b, cast_weights):
    n = ada_w.shape[1]
    steps = n // ADA_TILE
    w_in_specs, w_args, w_out_specs, w_out_shapes, groups = _cast_windows(cast_weights, steps, lambda j: j)
    out = pl.pallas_call(
        functools.partial(_ada_kernel, groups),
        grid=(steps,),
        in_specs=[
            pl.BlockSpec((MOD_ROWS, D_MODEL), lambda j: (0, 0)),
            pl.BlockSpec((D_MODEL, ADA_TILE), lambda j: (0, j)),
            pl.BlockSpec((1, ADA_TILE), lambda j: (0, j)),
        ] + w_in_specs,
        out_specs=[pl.BlockSpec((MOD_ROWS, 1, ADA_TILE), lambda j: (0, 0, j))] + w_out_specs,
        out_shape=[jax.ShapeDtypeStruct((MOD_ROWS, 1, n), F32)] + w_out_shapes,
        compiler_params=pltpu.CompilerParams(dimension_semantics=("parallel",), vmem_limit_bytes=VMEM_LIMIT),
        name="ada_mods",
    )(cvec, ada_w, ada_b, *w_args)
    return out[0], out[1:]


PRE_HOOK_IN_PROJ = 1
PRE_HOOK_UP_PROJ = 6


def _pre_kernel(latent, n_tiles, *refs):
    x1s_ref = refs[-1]
    if latent:
        (x_ref, mods_ref, modsp_ref, n1_ref, wgu_ref, wd_ref, nm_ref, win_ref, wkr_ref, qn_ref, kvn_ref, wuq_ref, wuk_ref,
         wuv_ref, cosa_ref, sina_ref, cosb_ref, sinb_ref,
         x1_ref, qa_ref, kap_ref, vad_ref, qb_ref, kb_ref, vb_ref) = refs[:-1]
    else:
        (x_ref, mods_ref, modsp_ref, n1_ref, wgu_ref, wd_ref, nm_ref, win_ref, wkr_ref, qn_ref, kvn_ref, wuq_ref, wuk_ref,
         wuv_ref,
         x1_ref, qa_ref, kap_ref, qb_ref, kb_ref, vb_ref,
         knat_ref, vnat_ref, ckvn_ref, krope_ref) = refs[:-1]
    i = pl.program_id(0)
    carry = {}

    def in_proj():
        h2 = _mod_norm(x1s_ref[...], nm_ref[...], modsp_ref, 1).astype(BF16)
        z = _dot_nt(h2, win_ref[...])
        low = _low_lanes(z.shape[0])

        for c in range(A_Q_W // LANES):
            q = z[:, c * LANES:(c + 1) * LANES]
            if latent:
                q = _rope(q, cosa_ref[...], sina_ref[...], A_HEAD_DIM // 4)
            qa_ref[:, c * LANES:(c + 1) * LANES] = (q * (A_SCALE * LOG2E)).astype(BF16)
        k = z[:, OFF_KA:OFF_KA + LANES]
        v = z[:, OFF_VA:OFF_VA + LANES]
        if latent:
            k = _rope(k, cosa_ref[...], sina_ref[...], A_HEAD_DIM // 4)
            _store_gqa_v_blocks(v, vad_ref)
        else:
            _store_seq_minor(knat_ref, k)
            _store_seq_minor(vnat_ref, v)
        _store_gqa_k(k, low, kap_ref)

        carry["q_lat"] = _rms(z[:, OFF_QLAT:OFF_QLAT + Q_LORA], qn_ref[...]).astype(BF16)
        ckv_n = _rms(z[:, OFF_CKV:OFF_CKV + KV_LORA], kvn_ref[...])
        krp = _dot_nt(h2, wkr_ref[...])
        if latent:
            krp = _rope(krp, cosb_ref[...], sinb_ref[...], B_ROPE_DIM // 4)
        else:
            ckvn_ref[...] = ckv_n
            _store_seq_minor(krope_ref, pltpu.roll(krp, HALF, axis=1))
        carry["ckv_b"] = ckv_n.astype(BF16)
        carry["krp"] = krp

    def up_proj():
        q_lat, ckv_b, krp = carry["q_lat"], carry["ckv_b"], carry["krp"]
        qb = _dot(q_lat, wuq_ref[...])
        kn = _dot(ckv_b, wuk_ref[...])
        vb_ref[...] = _dot_nt(wuv_ref[...], ckv_b).astype(BF16)
        for h in range(B_HEADS):
            qh = qb[:, h * LANES:(h + 1) * LANES]
            if latent:
                qh = _rope(qh, cosb_ref[...], sinb_ref[...], B_ROPE_DIM // 4)
            qb_ref[:, h * LANES:(h + 1) * LANES] = (qh * (B_SCALE * LOG2E)).astype(BF16)
            kb_ref[:, h * LANES:(h + 1) * LANES] = (kn[:, h * LANES:(h + 1) * LANES] + krp).astype(BF16)

    def ffn(hooks):
        x = x_ref[...]
        h1 = _mod_norm(x, n1_ref[...], mods_ref, 0).astype(BF16)
        x1 = x + 0.5 * _gate(mods_ref, 0) * _ffn(h1, wgu_ref, wd_ref, hooks)
        x1_ref[...] = x1
        x1s_ref[...] = x1

    @pl.when(i == 0)
    def _():
        ffn(None)

    @pl.when(jnp.logical_and(i > 0, i < n_tiles))
    def _():
        ffn({PRE_HOOK_IN_PROJ: in_proj, PRE_HOOK_UP_PROJ: up_proj})

    @pl.when(i == n_tiles)
    def _():
        in_proj()
        up_proj()


def _const_spec(shape):
    nd = len(shape)
    return pl.BlockSpec(shape, lambda i: (0,) * nd, pipeline_mode=pl.Buffered(1))


def _pre(x, mods3, w, latent, seq, rope, tm):
    t = x.shape[0]
    tiles_per_batch = max(seq // tm, 1)
    n_tiles = t // tm
    cur = lambda i: jnp.minimum(i, n_tiles - 1)
    prev = lambda i: jnp.maximum(i - 1, 0)
    mod_of = (lambda j: 1 + j // tiles_per_batch) if latent else (lambda j: 0)
    cur_rows = lambda i: (cur(i), 0)
    prev_rows = lambda i: (prev(i), 0)

    weights = [w["n1"], w["wgu1"], w["wd1"], w["nm"], w["win_a"], w["win_kr"], w["qn"], w["kvn"], w["wuq"], w["wuk"],
               w["wuv_t"]]
    in_specs = [pl.BlockSpec((tm, D_MODEL), cur_rows),
                pl.BlockSpec((None, 1, N_MOD * D_MODEL), lambda i: (mod_of(cur(i)), 0, 0)),
                pl.BlockSpec((None, 1, N_MOD * D_MODEL), lambda i: (mod_of(prev(i)), 0, 0))]
    in_specs += [_const_spec(a.shape) for a in weights]
    args = [x, mods3, mods3] + weights
    if latent:
        in_specs += [pl.BlockSpec((tm, LANES), lambda i: (prev(i) % tiles_per_batch, 0))] * 4
        args += list(rope)

    out_specs = [pl.BlockSpec((tm, D_MODEL), cur_rows)]
    out_shape = [jax.ShapeDtypeStruct((t, D_MODEL), F32)]
    for wd in (A_Q_W, 4 * LANES):
        out_specs.append(pl.BlockSpec((tm, wd), prev_rows))
        out_shape.append(jax.ShapeDtypeStruct((t, wd), BF16))
    if latent:
        out_specs.append(pl.BlockSpec((tm // LANES, A_KV_W, LANES), lambda i: (prev(i), 0, 0)))
        out_shape.append(jax.ShapeDtypeStruct((t // LANES, A_KV_W, LANES), BF16))
    for wd in (B_HEADS * LANES, B_HEADS * LANES):
        out_specs.append(pl.BlockSpec((tm, wd), prev_rows))
        out_shape.append(jax.ShapeDtypeStruct((t, wd), BF16))
    out_specs.append(pl.BlockSpec((B_HEADS * B_V_DIM, tm), lambda i: (0, prev(i))))
    out_shape.append(jax.ShapeDtypeStruct((B_HEADS * B_V_DIM, t), BF16))
    if not latent:
        nb = tm // seq
        for feat in (A_KV_W, A_KV_W):
            out_specs.append(pl.BlockSpec((nb, feat, seq), lambda i: (prev(i), 0, 0)))
            out_shape.append(jax.ShapeDtypeStruct((t // seq, feat, seq), F32))
        out_specs.append(pl.BlockSpec((tm, KV_LORA), prev_rows))
        out_shape.append(jax.ShapeDtypeStruct((t, KV_LORA), F32))
        out_specs.append(pl.BlockSpec((nb, B_ROPE_DIM, seq), lambda i: (prev(i), 0, 0)))
        out_shape.append(jax.ShapeDtypeStruct((t // seq, B_ROPE_DIM, seq), F32))

    return pl.pallas_call(
        functools.partial(_pre_kernel, latent, n_tiles),
        grid=(n_tiles + 1,),
        in_specs=in_specs,
        out_specs=out_specs,
        out_shape=out_shape,
        scratch_shapes=[pltpu.VMEM((tm, D_MODEL), F32)],
        compiler_params=pltpu.CompilerParams(dimension_semantics=("arbitrary",), vmem_limit_bytes=VMEM_LIMIT),
        name="pre_latent" if latent else "pre_context",
    )(*args)


def _cache_kernel(ckv_ref, kropet_ref, ckt_ref, cvt_ref, wuk_ref, wuv_ref, kb_ref, vb_ref, kap_ref, vat_ref):
    p = ckv_ref.shape[0]
    c = ckv_ref[...].astype(BF16)
    kn = _dot(c, wuk_ref[...])
    krp = jnp.concatenate([jnp.zeros((B_NOPE_DIM, p), F32), kropet_ref[...],
                           jnp.zeros((LANES - B_QK_DIM, p), F32)], axis=0).T
    for h in range(B_HEADS):
        kb_ref[:, h * LANES:(h + 1) * LANES] = (kn[:, h * LANES:(h + 1) * LANES] + krp).astype(BF16)
    vb_ref[...] = _dot_nt(wuv_ref[...], c).astype(BF16)
    _store_gqa_k(ckt_ref[...].T, _low_lanes(p), kap_ref)
    vat_ref[...] = cvt_ref[...].astype(BF16)


def _cache_kv(ckv, krope_t, ck_t, cv_t, wuk, wuv_t):
    b, p, _ = ckv.shape
    blk = lambda wd: pl.BlockSpec((None, p, wd), lambda i: (i, 0, 0))
    blk_t = lambda feat: pl.BlockSpec((None, feat, p), lambda i: (i, 0, 0))
    vdim = B_HEADS * B_V_DIM
    return pl.pallas_call(
        _cache_kernel,
        grid=(b,),
        in_specs=[blk(KV_LORA), blk_t(B_ROPE_DIM), blk_t(A_KV_W), blk_t(A_KV_W),
                  pl.BlockSpec(wuk.shape, lambda i: (0, 0)),
                  pl.BlockSpec(wuv_t.shape, lambda i: (0, 0))],
        out_specs=[blk(B_HEADS * LANES), pl.BlockSpec((vdim, p), lambda i: (0, i)), blk(4 * LANES),
                   pl.BlockSpec((A_KV_W, p), lambda i: (0, i))],
        out_shape=[jax.ShapeDtypeStruct((b, p, B_HEADS * LANES), BF16),
                   jax.ShapeDtypeStruct((vdim, b * p), BF16),
                   jax.ShapeDtypeStruct((b, p, 4 * LANES), BF16),
                   jax.ShapeDtypeStruct((A_KV_W, b * p), BF16)],
        compiler_params=pltpu.CompilerParams(dimension_semantics=("parallel",), vmem_limit_bytes=VMEM_LIMIT),
        name="cache_kv",
    )(ckv, krope_t, ck_t, cv_t, wuk, wuv_t)


def _att_ctx_kernel(seq, sink_ref, qa_ref, kap_ref, vt_ref, qb_ref, kb_ref, vbt_ref, oa_ref, ob_ref):
    nb = qa_ref.shape[0] // seq
    ones = jnp.ones((SUM_ROWS, seq), BF16)
    upper = lax.broadcasted_iota(jnp.int32, (1, 2 * seq), 1) >= seq
    pending = {}

    def gqa_unit(bi, kv, half):
        rows = slice(bi * seq, (bi + 1) * seq)
        u = 2 * kv + half

        def scores():
            q2 = jnp.concatenate([qa_ref[rows, (2 * kv) * LANES:(2 * kv + 1) * LANES],
                                  qa_ref[rows, (2 * kv + 1) * LANES:(2 * kv + 2) * LANES]], axis=0)
            return [_dot_nt(kap_ref[rows, u * LANES:(u + 1) * LANES], q2)]

        def sink():
            return jnp.where(upper, sink_ref[4 * kv + 2 + half], sink_ref[4 * kv + half]) * LOG2E

        def values(p_blocks):
            v = vt_ref[bi, kv * A_HEAD_DIM:(kv + 1) * A_HEAD_DIM, :].astype(BF16)
            return _dot(jnp.concatenate([v, ones], axis=0), p_blocks[0])

        def finish(o):
            if half == 0:
                pending[("a", bi, kv)] = o
                return
            first = pending.pop(("a", bi, kv))
            for pr in range(2):
                cols = slice(pr * seq, (pr + 1) * seq)
                pair = jnp.concatenate([first[:, cols], o[:, cols]], axis=0)
                c = 2 * kv + pr
                oa_ref[rows, c * LANES:(c + 1) * LANES] = pair.T.astype(BF16)

        return _Unit(scores, sink, values, finish)

    def mla_unit(bi, h):
        rows = slice(bi * seq, (bi + 1) * seq)
        lanes = slice(h * LANES, (h + 1) * LANES)

        def scores():
            return [_dot_nt(kb_ref[rows, lanes], qb_ref[rows, lanes])]

        def values(p_blocks):
            v = vbt_ref[h * B_V_DIM:(h + 1) * B_V_DIM, rows]
            return _dot(jnp.concatenate([v, ones], axis=0), p_blocks[0])

        def finish(o):
            if h % 2 == 0:
                pending[("b", bi, h)] = o
                return
            c = h // 2
            pair = jnp.concatenate([pending.pop(("b", bi, h - 1)), o], axis=0)
            ob_ref[rows, c * LANES:(c + 1) * LANES] = pair.T.astype(BF16)

        return _Unit(scores, lambda: None, values, finish)

    units = []
    for bi in range(nb):
        units += [gqa_unit(bi, kv, half) for kv in range(A_KV_HEADS) for half in range(2)]
        units += [mla_unit(bi, h) for h in range(B_HEADS)]
    assert len(units) % CTX_GROUP == 0
    _attention_pipeline([units[g:g + CTX_GROUP] for g in range(0, len(units), CTX_GROUP)], A_HEAD_DIM)


def _att_ctx(sink, qa, kap, v_t, qb, kb, vb_t, seq):
    t = qa.shape[0]
    rows = CTX_BATCHES * seq
    blk = lambda wd: pl.BlockSpec((rows, wd), lambda b: (b, 0))
    return pl.pallas_call(
        functools.partial(_att_ctx_kernel, seq),
        grid=(t // rows,),
        in_specs=[pl.BlockSpec(memory_space=pltpu.SMEM),
                  blk(qa.shape[1]), blk(kap.shape[1]),
                  pl.BlockSpec((CTX_BATCHES, A_KV_W, seq), lambda b: (b, 0, 0)),
                  blk(qb.shape[1]), blk(kb.shape[1]),
                  pl.BlockSpec((vb_t.shape[0], rows), lambda b: (0, b))],
        out_specs=[blk(A_Q_W), blk(B_HEADS * B_V_DIM)],
        out_shape=[jax.ShapeDtypeStruct((t, A_Q_W), BF16), jax.ShapeDtypeStruct((t, B_HEADS * B_V_DIM), BF16)],
        compiler_params=pltpu.CompilerParams(dimension_semantics=("parallel",), vmem_limit_bytes=VMEM_LIMIT),
        name="att_context",
    )(sink, qa, kap, v_t, qb, kb, vb_t)


class _Unit:
    def __init__(self, scores, sink, values, finish):
        self.scores, self.sink, self.values, self.finish = scores, sink, values, finish


def _attention_pipeline(groups, dv):
    n = len(groups)
    scores, probs = {}, {}
    for t in range(n + 2):
        if t < n:
            scores[t] = [unit.scores() for unit in groups[t]]
        if 0 <= t - 2 < n:
            for unit, (p_blocks, p_sink) in zip(groups[t - 2], probs.pop(t - 2)):
                acc = unit.values(p_blocks)
                den = acc[dv:dv + 1]
                unit.finish(acc[:dv] / (den if p_sink is None else den + p_sink))
        if 0 <= t - 1 < n:
            probs[t - 1] = [_probs_keys_major(s, unit.sink()) for unit, s in zip(groups[t - 1], scores.pop(t - 1))]


def _window_units(n_lat, tile, sink_ref, qa_ref, kap_ref, vat_ref, kapc_ref, vatc_ref, oa_ref):
    n_blocks = qa_ref.shape[0] // Q_BLOCK
    nq = 2 * Q_BLOCK
    past = kapc_ref.shape[0]
    upper = lax.broadcasted_iota(jnp.int32, (1, nq), 1) >= Q_BLOCK
    ones_band = jnp.ones((SUM_ROWS, BAND), BF16)
    ones_ctx = jnp.ones((SUM_ROWS, past), BF16)
    key = lax.broadcasted_iota(jnp.int32, (BAND, nq), 0)
    qry = lax.broadcasted_iota(jnp.int32, (BAND, nq), 1) & (Q_BLOCK - 1)
    key_minus_qry = key - qry
    pending = {}

    def make(qb, kv, half, start, valid, v_band):
        u = 2 * kv + half
        rows = slice(qb * Q_BLOCK, (qb + 1) * Q_BLOCK)
        feat = slice(kv * A_HEAD_DIM, (kv + 1) * A_HEAD_DIM)

        def scores():
            q2 = jnp.concatenate([qa_ref[rows, (2 * kv) * LANES:(2 * kv + 1) * LANES],
                                  qa_ref[rows, (2 * kv + 1) * LANES:(2 * kv + 2) * LANES]], axis=0)
            s_band = _dot_nt(kap_ref[pl.ds(start, BAND), u * LANES:(u + 1) * LANES], q2)
            s_ctx = _dot_nt(kapc_ref[:, u * LANES:(u + 1) * LANES], q2)
            return [jnp.where(valid, s_band, NEG), s_ctx]

        def sink():
            return jnp.where(upper, sink_ref[4 * kv + 2 + half], sink_ref[4 * kv + half]) * LOG2E

        def values(p_blocks):
            vb = jnp.concatenate([v_band[r][feat] for r in range(BAND // LANES)], axis=1)
            r_band = _dot(jnp.concatenate([vb, ones_band], axis=0), p_blocks[0])
            r_ctx = _dot(jnp.concatenate([vatc_ref[feat, :], ones_ctx], axis=0), p_blocks[1])
            return r_band + r_ctx

        def finish(o):
            if half == 0:
                pending[(qb, kv)] = o
                return
            first = pending.pop((qb, kv))
            for pr in range(2):
                cols = slice(pr * Q_BLOCK, (pr + 1) * Q_BLOCK)
                pair = jnp.concatenate([first[:, cols], o[:, cols]], axis=0)
                c = 2 * kv + pr
                oa_ref[rows, c * LANES:(c + 1) * LANES] = pair.T.astype(BF16)

        return _Unit(scores, sink, values, finish)

    units = []
    for qb in range(n_blocks):
        i = tile * n_blocks + qb
        start = pl.multiple_of(jnp.clip(i * Q_BLOCK - Q_BLOCK, 0, n_lat - BAND), Q_BLOCK)
        valid = jnp.abs(key_minus_qry + (start - i * Q_BLOCK)) <= WINDOW
        v_band = vat_ref[pl.ds(start // LANES, BAND // LANES)]
        units += [make(qb, kv, half, start, valid, v_band) for kv in range(A_KV_HEADS) for half in range(2)]
    return units


def _probs_keys_major(scores_t, sink=None):
    m = None
    for s in scores_t:
        mx = jnp.max(s, axis=0, keepdims=True)
        m = mx if m is None else jnp.maximum(m, mx)
    if sink is not None:
        m = jnp.maximum(m, sink)
    probs = [jnp.exp2(s - m).astype(BF16) for s in scores_t]
    return probs, (None if sink is None else jnp.exp2(sink - m))


def _mla_units(qb_ref, kbl_ref, vtl_ref, kbc_ref, vtc_ref, ob_ref):
    kb = MLA_KEY_BLOCK
    ones = jnp.ones((SUM_ROWS, kb), BF16)
    blocks = [(kbc_ref, vtc_ref, j) for j in range(kbc_ref.shape[0] // kb)]
    blocks += [(kbl_ref, vtl_ref, j) for j in range(kbl_ref.shape[0] // kb)]
    pending = {}

    def make(h):
        lanes = slice(h * LANES, (h + 1) * LANES)
        feat = slice(h * B_V_DIM, (h + 1) * B_V_DIM)

        def scores():
            return [_dot_nt(k_ref[j * kb:(j + 1) * kb, lanes], qb_ref[:, lanes]) for k_ref, _, j in blocks]

        def values(p_blocks):
            acc = None
            for (_, v_ref, j), p in zip(blocks, p_blocks):
                r = _dot(jnp.concatenate([v_ref[feat, j * kb:(j + 1) * kb], ones], axis=0), p)
                acc = r if acc is None else acc + r
            return acc

        def finish(o):
            if h % 2 == 0:
                pending[h] = o
                return
            c = h // 2
            pair = jnp.concatenate([pending.pop(h - 1), o], axis=0)
            ob_ref[:, c * LANES:(c + 1) * LANES] = pair.T.astype(BF16)

        return _Unit(scores, lambda: None, values, finish)

    return [make(h) for h in range(B_HEADS)]


def _att_lat_kernel(cast_groups, n_lat, sink_ref, qa_ref, kap_ref, vat_ref, kapc_ref, vatc_ref,
                    qb_ref, kbl_ref, vtl_ref, kbc_ref, vtc_ref, *rest):
    n_src = sum(cast_groups)
    oa_ref, ob_ref = rest[n_src:n_src + 2]
    _cast_blocks(rest[:n_src], rest[n_src + 2:], cast_groups)

    mla = _mla_units(qb_ref, kbl_ref, vtl_ref, kbc_ref, vtc_ref, ob_ref)
    win = _window_units(n_lat, pl.program_id(1), sink_ref, qa_ref, kap_ref, vat_ref, kapc_ref, vatc_ref, oa_ref)
    assert len(win) % len(mla) == 0
    per_head = len(win) // len(mla)
    groups = [[unit] + win[h * per_head:(h + 1) * per_head] for h, unit in enumerate(mla)]
    _attention_pipeline(groups, B_V_DIM)


def _att_lat(sink, qa, kap, vat, kapc, vatc, qb, kbl, vtl, kbc, vtc, cast_weights):
    b, n, _ = qb.shape
    past = kbc.shape[1]
    tq = LAT_Q_TILE
    vdim = B_HEADS * B_V_DIM
    q_tiles = n // tq
    steps = b * q_tiles
    w_in_specs, w_args, w_out_specs, w_out_shapes, groups = _cast_windows(
        cast_weights, steps, lambda bi, i: bi * q_tiles + i)
    q_blk = lambda wd: pl.BlockSpec((None, tq, wd), lambda bi, i: (bi, i, 0))
    rows_blk = lambda r, wd: pl.BlockSpec((None, r, wd), lambda bi, i: (bi, 0, 0))
    feat_blk = lambda feat, r: pl.BlockSpec((feat, r), lambda bi, i: (0, bi))
    out = pl.pallas_call(
        functools.partial(_att_lat_kernel, groups, n),
        grid=(b, q_tiles),
        in_specs=[pl.BlockSpec(memory_space=pltpu.SMEM),
                  q_blk(A_Q_W), rows_blk(n, kap.shape[2]),
                  pl.BlockSpec((n // LANES, A_KV_W, LANES), lambda bi, i: (bi, 0, 0)),
                  rows_blk(past, kapc.shape[2]), feat_blk(A_KV_W, past),
                  q_blk(qb.shape[2]), rows_blk(n, kbl.shape[2]), feat_blk(vdim, n),
                  rows_blk(past, kbc.shape[2]), feat_blk(vdim, past)] + w_in_specs,
        out_specs=[q_blk(A_Q_W), q_blk(vdim)] + w_out_specs,
        out_shape=[jax.ShapeDtypeStruct((b, n, A_Q_W), BF16), jax.ShapeDtypeStruct((b, n, vdim), BF16)]
        + w_out_shapes,
        compiler_params=pltpu.CompilerParams(dimension_semantics=("parallel", "parallel"),
                                             vmem_limit_bytes=VMEM_LIMIT),
        name="att_latent",
    )(sink, qa, kap, vat, kapc, vatc, qb, kbl, vtl, kbc, vtc, *w_args)
    return out[0], out[1], out[2:]


def _post_kernel(x1_ref, mods_ref, oa_ref, ob_ref, nm_ref, wing_ref, woa_ref, wob_ref, wout_ref,
                 n2_ref, wgu_ref, wd_ref, nf_ref, y_ref):
    def half_tile(rows):
        x1 = x1_ref[rows, :]
        h2 = _mod_norm(x1, nm_ref[...], mods_ref, 1).astype(BF16)
        g = _dot_nt(h2, wing_ref[...])
        ma = _dot(oa_ref[rows, :], woa_ref[...])
        mb = _dot(ob_ref[rows, :], wob_ref[...])
        yield
        m = jax.nn.sigmoid(g[:, :D_MODEL]) * ma + jax.nn.sigmoid(g[:, D_MODEL:]) * mb
        x2 = x1 + _gate(mods_ref, 1) * _dot(m.astype(BF16), wout_ref[...])
        yield
        h3 = _mod_norm(x2, n2_ref[...], mods_ref, 2).astype(BF16)
        x3 = x2 + 0.5 * _gate(mods_ref, 2) * (yield from _ffn_stages(h3, wgu_ref, wd_ref))
        y_ref[rows, :] = _rms(x3, nf_ref[...])

    half = x1_ref.shape[0] // 2
    _interleave([half_tile(slice(0, half)), half_tile(slice(half, 2 * half))])


def _post_pipelined(n_tiles, mod_row, x1_hbm, mods_hbm, oa_hbm, ob_hbm, *rest):
    weights, y_hbm = rest[:-1], rest[-1]
    tm = POST_TILE
    row_blk = lambda i: (i, 0)

    def body(x1_ref, mods_ref, oa_ref, ob_ref, y_ref):
        _post_kernel(x1_ref, mods_ref.at[0], oa_ref, ob_ref, *weights, y_ref)

    pltpu.emit_pipeline(
        body,
        grid=(n_tiles,),
        in_specs=[pl.BlockSpec((tm, D_MODEL), row_blk),
                  pl.BlockSpec((1, 1, N_MOD * D_MODEL), mod_row),
                  pl.BlockSpec((tm, A_Q_W), row_blk),
                  pl.BlockSpec((tm, B_HEADS * B_V_DIM), row_blk)],
        out_specs=[pl.BlockSpec((tm, D_MODEL), row_blk)],
    )(x1_hbm, mods_hbm, oa_hbm, ob_hbm, y_hbm)


def _post(x1, mods3, oa, ob, w, latent, tiles_per_batch):
    t = x1.shape[0]
    if latent:
        mod_row = lambda i: (1 + i // tiles_per_batch, 0, 0)
    else:
        mod_row = lambda i: (0, 0, 0)
    weights = [w["nm"], w["win_g"], w["woa"], w["wob"], w["wout"], w["n2"], w["wgu2"], w["wd2"], w["nf"]]
    streamed = pl.BlockSpec(memory_space=pl.ANY)
    return pl.pallas_call(
        functools.partial(_post_pipelined, t // POST_TILE, mod_row),
        in_specs=[streamed] * 4 + [pl.BlockSpec(memory_space=pltpu.VMEM)] * len(weights),
        out_specs=streamed,
        out_shape=jax.ShapeDtypeStruct((t, D_MODEL), F32),
        compiler_params=pltpu.CompilerParams(vmem_limit_bytes=VMEM_LIMIT),
        name="post_latent" if latent else "post_context",
    )(x1, mods3, oa, ob, *weights)


def _rope_tables(n):
    f32 = np.float32
    rows = n // GRID_W
    t_row = np.repeat(np.arange(rows, dtype=f32), GRID_W)
    t_col = np.tile(np.arange(GRID_W, dtype=f32), rows)

    def angles(d_rot):
        d_half = d_rot // 2
        inv = (f32(1.0) / np.power(f32(ROPE_THETA), np.arange(0, d_half, 2, dtype=f32) / f32(d_half))).astype(f32)
        ar = t_row[:, None] * inv[None, :]
        ac = t_col[:, None] * inv[None, :]
        return np.concatenate([ar, ar, ac, ac], axis=-1).astype(f32)

    def signed(sin, d_rot):
        q = d_rot // 4
        sign = np.where((np.arange(d_rot) % (2 * q)) < q, f32(-1.0), f32(1.0)).astype(f32)
        return sin * sign[None, :]

    ang_a = angles(A_HEAD_DIM)
    cos_a = np.tile(np.cos(ang_a), (1, LANES // A_HEAD_DIM))
    sin_a = np.tile(signed(np.sin(ang_a), A_HEAD_DIM), (1, LANES // A_HEAD_DIM))
    ang_b = angles(B_ROPE_DIM)
    pad = ((0, 0), (B_NOPE_DIM, LANES - B_NOPE_DIM - B_ROPE_DIM))
    cos_b = np.pad(np.cos(ang_b), pad, constant_values=1.0)
    sin_b = np.pad(signed(np.sin(ang_b), B_ROPE_DIM), pad)
    return tuple(jnp.asarray(t, dtype=F32) for t in (cos_a, sin_a, cos_b, sin_b))


def _prep_weights(ffn1_norm, ffn1_w_gu, ffn1_w_down, mix_norm, w_in, q_lat_norm, kv_lat_norm, w_uq, w_ukv,
                  w_o_a, w_o_b, w_out, ffn2_norm, ffn2_w_gu, ffn2_w_down, final_norm):
    win_t = jnp.swapaxes(w_in[0], 0, 1)
    n_attn = A_Q_W + 2 * A_KV_W + Q_LORA + KV_LORA
    win_kr = jnp.pad(win_t[n_attn:n_attn + B_ROPE_DIM],
                     ((B_NOPE_DIM, LANES - B_NOPE_DIM - B_ROPE_DIM), (0, 0))).astype(BF16)
    n_gate = n_attn + B_ROPE_DIM
    whole = lambda wm: (wm, 0, wm.shape[0])

    wuq = jnp.pad(w_uq[0].reshape(Q_LORA, B_HEADS, B_QK_DIM),
                  ((0, 0), (0, 0), (0, LANES - B_QK_DIM))).reshape(Q_LORA, B_HEADS * LANES).astype(BF16)
    wukv = w_ukv[0].reshape(KV_LORA, B_HEADS, B_NOPE_DIM + B_V_DIM)
    wuk = jnp.pad(wukv[:, :, :B_NOPE_DIM],
                  ((0, 0), (0, 0), (0, LANES - B_NOPE_DIM))).reshape(KV_LORA, B_HEADS * LANES).astype(BF16)
    wuv = wukv[:, :, B_NOPE_DIM:].reshape(KV_LORA, B_HEADS * B_V_DIM).astype(BF16)

    return {
        "n1": ffn1_norm, "nm": mix_norm, "win_kr": win_kr,
        "qn": q_lat_norm, "kvn": kv_lat_norm, "wuq": wuq, "wuk": wuk, "wuv_t": wuv.T,
        "n2": ffn2_norm, "nf": final_norm.reshape(1, D_MODEL),
        "pre_f32": [whole(ffn1_w_gu[0]), whole(ffn1_w_down[0]), (win_t, 0, IN_A_W)],
        "post_f32": [(win_t, n_gate, 2 * D_MODEL), whole(w_o_a[0]), whole(w_o_b[0]), whole(w_out[0]),
                     whole(ffn2_w_gu[0]), whole(ffn2_w_down[0])],
    }


def kernel(x_prompt, x_sample, cache_attn_k, cache_attn_v, cache_mla_ckv, cache_mla_krope, c, c_ctx, ada_w, ada_b, ffn1_norm, ffn1_w_gu, ffn1_w_down, mix_norm, w_in, attn_sink, q_lat_norm, kv_lat_norm, w_uq, w_ukv, w_o_a, w_o_b, w_out, ffn2_norm, ffn2_w_gu, ffn2_w_down, final_norm):
    assert ada_w.shape[0] == 1, "single trunk layer"
    bp, sp, d = x_prompt.shape
    bs, ns, _ = x_sample.shape
    past = cache_attn_k.shape[2]
    assert d == D_MODEL and bs + 1 <= MOD_ROWS
    assert PRE_TILE_CTX % sp == 0 and (bp * sp) % PRE_TILE_CTX == 0 and (bp * sp) % POST_TILE == 0
    assert ns % PRE_TILE_LAT == 0 and ns % POST_TILE == 0
    assert ns % LAT_Q_TILE == 0 and LAT_Q_TILE % Q_BLOCK == 0 and ns >= BAND

    w = _prep_weights(ffn1_norm, ffn1_w_gu, ffn1_w_down, mix_norm, w_in, q_lat_norm, kv_lat_norm, w_uq, w_ukv,
                      w_o_a, w_o_b, w_out, ffn2_norm, ffn2_w_gu, ffn2_w_down, final_norm)
    sink = attn_sink[0]

    cvec = jnp.concatenate([c_ctx[None, :], c, jnp.zeros((MOD_ROWS - 1 - bs, d), F32)], axis=0)
    mods3, (w["wgu1"], w["wd1"], w["win_a"]) = _ada_mods(cvec, ada_w[0], ada_b, w["pre_f32"])

    xp = x_prompt.reshape(bp * sp, d)
    (x1p, qa, kap, qb, kb, vb_t, k_t, v_t, ckv_n, krope_t) = _pre(xp, mods3, w, False, sp, None, PRE_TILE_CTX)
    oa_p, ob_p = _att_ctx(sink, qa, kap, v_t, qb, kb, vb_t, sp)

    xs = x_sample.reshape(bs * ns, d)
    (x1s, qa, kap, vad, qb, kb, vb_t) = _pre(xs, mods3, w, True, ns, _rope_tables(ns), PRE_TILE_LAT)
    feat_major = lambda a: a[:, 0].transpose(0, 2, 3, 1).reshape(bs, A_KV_W, past)
    kbc, vbc, kapc, vadc = _cache_kv(
        cache_mla_ckv[:, 0], jnp.swapaxes(cache_mla_krope[:, 0], 1, 2),
        feat_major(cache_attn_k), feat_major(cache_attn_v), w["wuk"], w["wuv_t"])
    r3 = lambda a: a.reshape(bs, ns, a.shape[1])
    oa, ob, (w["win_g"], w["woa"], w["wob"], w["wout"], w["wgu2"], w["wd2"]) = _att_lat(
        sink, r3(qa), r3(kap), vad, kapc, vadc, r3(qb), r3(kb), vb_t, kbc, vbc, w["post_f32"])

    y_prompt = _post(x1p, mods3, oa_p, ob_p, w, False, 1).reshape(bp, sp, d)
    y_sample = _post(x1s, mods3, oa.reshape(bs * ns, A_Q_W), ob.reshape(bs * ns, B_HEADS * B_V_DIM),
                     w, True, ns // POST_TILE).reshape(bs, ns, d)

    new_attn_k = k_t.reshape(bp, 1, A_KV_HEADS, A_HEAD_DIM, sp).transpose(0, 1, 4, 2, 3)
    new_attn_v = v_t.reshape(bp, 1, A_KV_HEADS, A_HEAD_DIM, sp).transpose(0, 1, 4, 2, 3)
    new_mla_ckv = ckv_n.reshape(bp, 1, sp, KV_LORA)
    new_mla_krope = krope_t.reshape(bp, 1, B_ROPE_DIM, sp).transpose(0, 1, 3, 2)
    return (y_prompt, y_sample, new_attn_k, new_attn_v, new_mla_ckv, new_mla_krope)
```
